```python
import math
import jax
import jax.numpy as jnp
from jax import lax
import numpy as np

D_MODEL = 1024
BATCH = 16
SEQ = 4096
DEPTH = 2

MEM_LEN = 256
EPS = 1e-6
N_HEADS = 8
N_KV_HEADS = 2
HEAD_DIM = 64
ATTN_WIDTH = N_HEADS * HEAD_DIM
KV_WIDTH = N_KV_HEADS * HEAD_DIM
WINDOW = 128
BLOCK = 128
ROPE_THETA = 10000.0
HY_WIDTH = D_MODEL - ATTN_WIDTH
HY_ORDER = 2
HY_SHORT = 3
HY_EMB = 33
HY_BANDS = (HY_EMB - 1) // 2
HY_FILTER_HIDDEN = 64
HY_TARGET = 1e-2
HY_FAST_DECAY = 0.3
HY_SLOW_DECAY = 1.5
HY_MIN_DECAY = math.log(HY_TARGET) / HY_SLOW_DECAY
HY_MAX_DECAY = math.log(HY_TARGET) / HY_FAST_DECAY
MIX_WIDTH = ATTN_WIDTH + HY_WIDTH
Q_END = ATTN_WIDTH
K_END = Q_END + KV_WIDTH
V_END = K_END + KV_WIDTH
IN_COLS = V_END + (HY_ORDER + 1) * HY_WIDTH
X_HEADS = 4
X_HEAD_DIM = 128
X_WIDTH = X_HEADS * X_HEAD_DIM
D_FF = 2816
N_EXPERTS = 8
TOP_K = 2
D_FF_EXPERT = 3584
MOE_BLOCK = 512
N_DENSE = (DEPTH + 1) // 2
N_MOE = DEPTH // 2

kernel_name = 'hymba_attn_hyena_moe_encoder'


def rms_norm(x, g):
    xf = x.astype(jnp.float32)
    y = xf * lax.rsqrt(jnp.mean(xf * xf, axis=-1, keepdims=True) + EPS)
    return (y * g.astype(jnp.float32)).astype(x.dtype)


def rope_tables(seq):
    pos = jnp.arange(seq, dtype=jnp.float32)
    inv = ROPE_THETA ** (-jnp.arange(0, HEAD_DIM, 2, dtype=jnp.float32) / HEAD_DIM)
    ang = pos[:, None] * inv[None, :]
    return jnp.cos(ang), jnp.sin(ang)


def apply_rope(x, cos, sin):
    x1, x2 = jnp.split(x.astype(jnp.float32), 2, axis=-1)
    c = cos[None, :, None, :]
    s = sin[None, :, None, :]
    return jnp.concatenate([x1 * c - x2 * s, x2 * c + x1 * s], axis=-1).astype(x.dtype)


def band_attention(q, k, v, sink):
    B, S, H, Dh = q.shape
    nb = S // BLOCK
    G = H // N_KV_HEADS
    scale = Dh ** -0.5
    qb = q.reshape(B, nb, BLOCK, N_KV_HEADS, G, Dh).transpose(1, 0, 2, 3, 4, 5)
    kp = jnp.pad(k, ((0, 0), (BLOCK, BLOCK), (0, 0), (0, 0)))
    vp = jnp.pad(v, ((0, 0), (BLOCK, BLOCK), (0, 0), (0, 0)))
    sink_l = jnp.broadcast_to(sink.astype(jnp.float32).reshape(1, N_KV_HEADS, G, 1, 1),
                              (B, N_KV_HEADS, G, BLOCK, 1))
    neg = jnp.finfo(jnp.float32).min

    def one_block(args):
        i, qi = args
        kk = lax.dynamic_slice_in_dim(kp, i * BLOCK, 3 * BLOCK, axis=1)
        vv = lax.dynamic_slice_in_dim(vp, i * BLOCK, 3 * BLOCK, axis=1)
        s = jnp.einsum('bqkgd,bskd->bkgqs', qi, kk, preferred_element_type=jnp.float32) * scale
        qpos = i * BLOCK + jnp.arange(BLOCK)
        kpos = (i - 1) * BLOCK + jnp.arange(3 * BLOCK)
        mask = (jnp.abs(kpos[None, :] - qpos[:, None]) <= WINDOW) & (kpos[None, :] >= 0) & (kpos[None, :] < S)
        s = jnp.where(mask[None, None, None], s, neg)
        p = jax.nn.softmax(jnp.concatenate([s, sink_l], axis=-1), axis=-1)[..., :-1]
        return jnp.einsum('bkgqs,bskd->bqkgd', p.astype(vv.dtype), vv)

    o = lax.map(one_block, (jnp.arange(nb), qb))
    return o.transpose(1, 0, 2, 3, 4, 5).reshape(B, S, H * Dh)


def hyena_pos_features(L):
    t = jnp.linspace(0.0, 1.0, L, dtype=jnp.float32)[:, None]
    w = 2.0 * math.pi * jnp.arange(L, dtype=jnp.float32)[:, None] / L
    f = jnp.linspace(1e-4, HY_BANDS - 1, HY_BANDS, dtype=jnp.float32)[None, :]
    z = jnp.concatenate([t, jnp.cos(f * w), -jnp.sin(f * w)], axis=-1)
    return z, t


def hyena_filters(z, t, f_w1, f_b1, f_freq1, f_w2, f_b2, f_freq2, f_w3):
    L = z.shape[0]
    h = jnp.sin(f_freq1 * (z @ f_w1 + f_b1))
    h = jnp.sin(f_freq2 * (h @ f_w2 + f_b2))
    h = (h @ f_w3).astype(jnp.float32).reshape(L, 2, HY_ORDER, HY_WIDTH)
    deltas = jnp.linspace(HY_MIN_DECAY, HY_MAX_DECAY, HY_ORDER * HY_WIDTH,
                          dtype=jnp.float32).reshape(HY_ORDER, HY_WIDTH)
    h = h * jnp.exp(-t[:, :, None, None] * jnp.abs(deltas))
    fwd = h[:, 0]
    bwd = h[1:, 1]
    return jnp.concatenate([fwd, jnp.zeros((1,) + fwd.shape[1:], jnp.float32), bwd[::-1]], axis=0)


def hyena_mixer(u, conv_w, conv_b, k_long, skip):
    L = u.shape[1]
    up = jnp.pad(u, ((0, 0), (1, 1), (0, 0)))
    u = up[:, :-2] * conv_w[0] + up[:, 1:-1] * conv_w[1] + up[:, 2:] * conv_w[2] + conv_b
    x1, x2, v = jnp.split(u, HY_ORDER + 1, axis=-1)
    K = jnp.fft.rfft(k_long, axis=0)

    def long_conv(zin, Ko, d):
        zf32 = zin.astype(jnp.float32)
        zf = jnp.fft.rfft(zf32, n=2 * L, axis=1)
        y = jnp.fft.irfft(zf * Ko[None], n=2 * L, axis=1)[:, :L]
        return (y + zf32 * d.astype(jnp.float32)).astype(zin.dtype)

    zz = x1 * long_conv(v, K[:, 0], skip[0])
    return x2 * long_conv(zz, K[:, 1], skip[1])


def memory_attention(h, m, w_q, w_k, w_v, w_o):
    B, S, _ = h.shape
    q = (h @ w_q).reshape(B, S, X_HEADS, X_HEAD_DIM)
    k = (m @ w_k).reshape(B, -1, X_HEADS, X_HEAD_DIM)
    v = (m @ w_v).reshape(B, -1, X_HEADS, X_HEAD_DIM)
    s = jnp.einsum('bqhd,bkhd->bhqk', q, k, preferred_element_type=jnp.float32) * (X_HEAD_DIM ** -0.5)
    p = jax.nn.softmax(s, axis=-1).astype(v.dtype)
    o = jnp.einsum('bhqk,bkhd->bqhd', p, v).reshape(B, S, X_WIDTH)
    return o @ w_o


def swiglu(h, w_gate, w_up, w_down):
    return (jax.nn.silu(h @ w_gate) * (h @ w_up)) @ w_down


def moe_swiglu(h, w_router, w_gate, w_up, w_down):
    B, S, D = h.shape
    t = h.reshape(B * S, D)
    n_tok = B * S
    n_asg = n_tok * TOP_K
    logits = jnp.dot(t.astype(jnp.float32), w_router.astype(jnp.float32))
    top_v, top_i = lax.top_k(logits, TOP_K)
    top_w = jax.nn.softmax(top_v, axis=-1)
    flat_e = top_i.reshape(-1)
    flat_tok = jnp.repeat(jnp.arange(n_tok, dtype=jnp.int32), TOP_K)
    flat_w = top_w.reshape(-1)
    order = jnp.argsort(flat_e)
    sorted_e = flat_e[order]
    counts = jnp.bincount(flat_e, length=N_EXPERTS)
    padded = (counts + MOE_BLOCK - 1) // MOE_BLOCK * MOE_BLOCK
    pad_end = jnp.cumsum(padded)
    pad_start = pad_end - padded
    seg_start = jnp.cumsum(counts) - counts
    dest = pad_start[sorted_e] + jnp.arange(n_asg, dtype=jnp.int32) - seg_start[sorted_e]
    n_blk = -(-(n_asg + N_EXPERTS * (MOE_BLOCK - 1)) // MOE_BLOCK)
    buf_tok = jnp.zeros((n_blk * MOE_BLOCK,), jnp.int32).at[dest].set(flat_tok[order])
    buf_w = jnp.zeros((n_blk * MOE_BLOCK,), jnp.float32).at[dest].set(flat_w[order])
    blk_e = jnp.minimum(jnp.searchsorted(pad_end, jnp.arange(n_blk, dtype=jnp.int32) * MOE_BLOCK, side='right'),
                        N_EXPERTS - 1)

    def run_block(args):
        e, tok, wt = args
        xb = t[tok]
        yb = swiglu(xb, w_gate[e], w_up[e], w_down[e])
        return yb * wt[:, None].astype(yb.dtype)

    y = lax.map(run_block, (blk_e, buf_tok.reshape(n_blk, MOE_BLOCK), buf_w.reshape(n_blk, MOE_BLOCK)))
    out = jnp.zeros_like(t).at[buf_tok].add(y.reshape(-1, D))
    return out.reshape(B, S, D)


def setup_inputs(seed: int = 0) -> dict:
    key = jax.random.key(seed)
    ks = iter(jax.random.split(key, 40))

    def nrm(shape, scale):
        return scale * jax.random.normal(next(ks), shape, jnp.float32)

    def gain(shape):
        return 1.0 + 0.05 * jax.random.normal(next(ks), shape, jnp.float32)

    D = D_MODEL
    C3 = (HY_ORDER + 1) * HY_WIDTH
    return {
        'x': nrm((BATCH, SEQ, D), 1.0),
        'mem': nrm((BATCH, MEM_LEN, D), 1.0),
        'mem_norm': gain((D,)),
        'mix_norm': gain((DEPTH, D)),
        'w_in': nrm((DEPTH, D, IN_COLS), D ** -0.5),
        'attn_sink': nrm((DEPTH, N_HEADS), 0.5),
        'hy_conv_w': nrm((DEPTH, HY_SHORT, C3), HY_SHORT ** -0.5),
        'hy_conv_b': nrm((DEPTH, C3), 0.02),
        'hy_f_w1': nrm((DEPTH, HY_EMB, HY_FILTER_HIDDEN), HY_EMB ** -0.5),
        'hy_f_b1': nrm((DEPTH, HY_FILTER_HIDDEN), 0.1),
        'hy_f_freq1': gain((DEPTH, HY_FILTER_HIDDEN)),
        'hy_f_w2': nrm((DEPTH, HY_FILTER_HIDDEN, HY_FILTER_HIDDEN), HY_FILTER_HIDDEN ** -0.5),
        'hy_f_b2': nrm((DEPTH, HY_FILTER_HIDDEN), 0.1),
        'hy_f_freq2': gain((DEPTH, HY_FILTER_HIDDEN)),
        'hy_f_w3': nrm((DEPTH, HY_FILTER_HIDDEN, 2 * HY_ORDER * HY_WIDTH), 0.05 * HY_FILTER_HIDDEN ** -0.5),
        'hy_skip': nrm((DEPTH, HY_ORDER, HY_WIDTH), 0.5),
        'attn_out_norm': gain((DEPTH, ATTN_WIDTH)),
        'hy_out_norm': gain((DEPTH, HY_WIDTH)),
        'w_out': nrm((DEPTH, MIX_WIDTH, D), MIX_WIDTH ** -0.5),
        'xattn_norm': gain((DEPTH, D)),
        'xw_q': nrm((DEPTH, D, X_WIDTH), D ** -0.5),
        'xw_k': nrm((DEPTH, D, X_WIDTH), D ** -0.5),
        'xw_v': nrm((DEPTH, D, X_WIDTH), D ** -0.5),
        'xw_o': nrm((DEPTH, X_WIDTH, D), X_WIDTH ** -0.5),
        'ffn_norm': gain((DEPTH, D)),
        'ffn_w_gate': nrm((N_DENSE, D, D_FF), D ** -0.5),
        'ffn_w_up': nrm((N_DENSE, D, D_FF), D ** -0.5),
        'ffn_w_down': nrm((N_DENSE, D_FF, D), D_FF ** -0.5),
        'moe_router': nrm((N_MOE, D, N_EXPERTS), D ** -0.5),
        'moe_w_gate': nrm((N_MOE, N_EXPERTS, D, D_FF_EXPERT), D ** -0.5),
        'moe_w_up': nrm((N_MOE, N_EXPERTS, D, D_FF_EXPERT), D ** -0.5),
        'moe_w_down': nrm((N_MOE, N_EXPERTS, D_FF_EXPERT, D), D_FF_EXPERT ** -0.5),
        'final_norm': gain((D,)),
    }


def reference(x, mem, mem_norm, mix_norm, w_in, attn_sink, hy_conv_w, hy_conv_b,
              hy_f_w1, hy_f_b1, hy_f_freq1, hy_f_w2, hy_f_b2, hy_f_freq2, hy_f_w3, hy_skip,
              attn_out_norm, hy_out_norm, w_out, xattn_norm, xw_q, xw_k, xw_v, xw_o,
              ffn_norm, ffn_w_gate, ffn_w_up, ffn_w_down,
              moe_router, moe_w_gate, moe_w_up, moe_w_down, final_norm):
    B, S, _ = x.shape
    cos, sin = rope_tables(S)
    z_pos, t_pos = hyena_pos_features(S)
    m = rms_norm(mem, mem_norm)
    for l in range(DEPTH):
        h = rms_norm(x, mix_norm[l])
        proj = h @ w_in[l]
        q, k, v, u = jnp.split(proj, [Q_END, K_END, V_END], axis=-1)
        q = apply_rope(q.reshape(B, S, N_HEADS, HEAD_DIM), cos, sin)
        k = apply_rope(k.reshape(B, S, N_KV_HEADS, HEAD_DIM), cos, sin)
        v = v.reshape(B, S, N_KV_HEADS, HEAD_DIM)
        a = band_attention(q, k, v, attn_sink[l])
        k_long = hyena_filters(z_pos, t_pos, hy_f_w1[l], hy_f_b1[l], hy_f_freq1[l],
                               hy_f_w2[l], hy_f_b2[l], hy_f_freq2[l], hy_f_w3[l])
        y = hyena_mixer(u, hy_conv_w[l], hy_conv_b[l], k_long, hy_skip[l])
        mixed = jnp.concatenate([rms_norm(a, attn_out_norm[l]), rms_norm(y, hy_out_norm[l])], axis=-1)
        x = x + mixed @ w_out[l]
        x = x + memory_attention(rms_norm(x, xattn_norm[l]), m, xw_q[l], xw_k[l], xw_v[l], xw_o[l])
        h = rms_norm(x, ffn_norm[l])
        if l % 2 == 0:
            j = l // 2
            x = x + swiglu(h, ffn_w_gate[j], ffn_w_up[j], ffn_w_down[j])
        else:
            j = l // 2
            x = x + moe_swiglu(h, moe_router[j], moe_w_gate[j], moe_w_up[j], moe_w_down[j])
    return rms_norm(x, final_norm)
```

```python
import functools
import math

import jax
import jax.numpy as jnp
from jax import lax
from jax.experimental import pallas as pl
from jax.experimental.pallas import tpu as pltpu

F32 = jnp.float32
BF16 = jnp.bfloat16

EPS = 1e-6
N_HEADS = 8
N_KV_HEADS = 2
HEAD_DIM = 64
ATTN_WIDTH = N_HEADS * HEAD_DIM
KV_WIDTH = N_KV_HEADS * HEAD_DIM
WINDOW = 128
ROPE_THETA = 10000.0
HY_WIDTH = 512
HY_ORDER = 2
HY_EMB = 33
HY_BANDS = (HY_EMB - 1) // 2
HY_FILTER_HIDDEN = 64
HY_TARGET = 1e-2
HY_FAST_DECAY = 0.3
HY_SLOW_DECAY = 1.5
HY_MIN_DECAY = math.log(HY_TARGET) / HY_SLOW_DECAY
HY_MAX_DECAY = math.log(HY_TARGET) / HY_FAST_DECAY
Q_END = ATTN_WIDTH
K_END = Q_END + KV_WIDTH
V_END = K_END + KV_WIDTH
X_HEADS = 4
X_HEAD_DIM = 128
X_WIDTH = X_HEADS * X_HEAD_DIM
N_EXPERTS = 8
TOP_K = 2

LANES = 128
SUBLANES = 8
MXU_DIM = 256
VMEM_LIMIT = 56 * 1024 * 1024

FFT_N1 = 256
FFT_N2 = 32
FFT_N = FFT_N1 * FFT_N2
FFT_J = SUBLANES
FFT_GROUPS = FFT_N1 // FFT_J
FFT_CHUNK = 64
FFT_SUB = FFT_CHUNK // FFT_J

MOE_ROWS = 1024
NEG = float(jnp.finfo(jnp.float32).min)


def _params(*sem):
    return pltpu.CompilerParams(dimension_semantics=sem, vmem_limit_bytes=VMEM_LIMIT)


def _dot(a, b):
    return jnp.dot(a, b, preferred_element_type=F32)


def _dot_t(a, b):
    return lax.dot_general(a, b, (((1,), (1,)), ((), ())), preferred_element_type=F32)


def _rms(xf, g):
    ms = jnp.mean(xf * xf, axis=-1, keepdims=True)
    return xf * lax.rsqrt(ms + EPS) * g


def _inproj_kernel(x_ref, g_ref, w_ref, cos_ref, sin_ref, q_ref, kv_ref, u_ref):
    h = _rms(x_ref[...], g_ref[...]).astype(BF16)
    cosf = cos_ref[...]
    sins = sin_ref[...]
    lane = lax.broadcasted_iota(jnp.int32, cosf.shape, 1)
    low = (lane % HEAD_DIM) < (HEAD_DIM // 2)

    def rope(c):
        rot = jnp.where(low, pltpu.roll(c, LANES - HEAD_DIM // 2, 1), pltpu.roll(c, HEAD_DIM // 2, 1))
        return c * cosf + rot * sins

    scale = HEAD_DIM ** -0.5
    for j in range(ATTN_WIDTH // MXU_DIM):
        qc = _dot(h, w_ref[:, j * MXU_DIM:(j + 1) * MXU_DIM])
        for t in range(MXU_DIM // LANES):
            c0 = j * MXU_DIM + t * LANES
            q_ref[:, c0:c0 + LANES] = (rope(qc[:, t * LANES:(t + 1) * LANES]) * scale).astype(BF16)
    kvc = _dot(h, w_ref[:, Q_END:V_END])
    kv_ref[:, :KV_WIDTH] = rope(kvc[:, :KV_WIDTH]).astype(BF16)
    kv_ref[:, KV_WIDTH:] = kvc[:, KV_WIDTH:].astype(BF16)
    n_u = u_ref.shape[1]
    for j in range(n_u // 512):
        u_ref[:, j * 512:(j + 1) * 512] = _dot(h, w_ref[:, V_END + j * 512:V_END + (j + 1) * 512]).astype(BF16)


def _inproj(x2d, g, w, cosf, sins, seq):
    n, d = x2d.shape
    tm = 512
    n_u = w.shape[1] - V_END
    spb = seq // tm
    return pl.pallas_call(
        _inproj_kernel,
        grid=(n // tm,),
        in_specs=[
            pl.BlockSpec((tm, d), lambda i: (i, 0)),
            pl.BlockSpec((1, d), lambda i: (0, 0)),
            pl.BlockSpec(w.shape, lambda i: (0, 0)),
            pl.BlockSpec((tm, LANES), lambda i: (i % spb, 0)),
            pl.BlockSpec((tm, LANES), lambda i: (i % spb, 0)),
        ],
        out_specs=[
            pl.BlockSpec((tm, ATTN_WIDTH), lambda i: (i, 0)),
            pl.BlockSpec((tm, 2 * KV_WIDTH), lambda i: (i, 0)),
            pl.BlockSpec((tm, n_u), lambda i: (i, 0)),
        ],
        out_shape=[
            jax.ShapeDtypeStruct((n, ATTN_WIDTH), BF16),
            jax.ShapeDtypeStruct((n, 2 * KV_WIDTH), BF16),
            jax.ShapeDtypeStruct((n, n_u), BF16),
        ],
        compiler_params=_params("parallel"),
        name="inproj",
    )(x2d, g, w, cosf, sins)


def _battn_kernel(sink_ref, q_ref, kv_ref, o_ref, *, seq, tq):
    i = pl.program_id(1)
    blk = WINDOW
    row = lax.broadcasted_iota(jnp.int32, (blk, 3 * blk), 0)
    col = lax.broadcasted_iota(jnp.int32, (blk, 3 * blk), 1)
    rel = col - blk - row
    group = N_HEADS // N_KV_HEADS
    for jb in range(tq // blk):
        base = i * tq + jb * blk
        kpos = base - blk + col
        valid = (jnp.abs(rel) <= WINDOW) & (kpos >= 0) & (kpos < seq)
        start = pl.multiple_of(base, blk)
        for g in range(N_KV_HEADS):
            kk = kv_ref[0, pl.ds(start, 3 * blk), g * HEAD_DIM:(g + 1) * HEAD_DIM]
            vv = kv_ref[0, pl.ds(start, 3 * blk), KV_WIDTH + g * HEAD_DIM:KV_WIDTH + (g + 1) * HEAD_DIM]
            for hh in range(group):
                h = g * group + hh
                qh = q_ref[0, jb * blk:(jb + 1) * blk, h * HEAD_DIM:(h + 1) * HEAD_DIM]
                s = jnp.where(valid, _dot_t(qh, kk), NEG)
                sk = sink_ref[h]
                m = jnp.maximum(jnp.max(s, axis=-1, keepdims=True), sk)
                p = jnp.exp(s - m)
                den = jnp.sum(p, axis=-1, keepdims=True) + jnp.exp(sk - m)
                o = _dot(p.astype(BF16), vv) / den
                o_ref[0, jb * blk:(jb + 1) * blk, h * HEAD_DIM:(h + 1) * HEAD_DIM] = o.astype(BF16)


def _band_attention(q, kv_pad, sink, seq):
    b = q.shape[0]
    tq = 512
    return pl.pallas_call(
        functools.partial(_battn_kernel, seq=seq, tq=tq),
        grid=(b, seq // tq),
        in_specs=[
            pl.BlockSpec(memory_space=pltpu.SMEM),
            pl.BlockSpec((1, tq, ATTN_WIDTH), lambda bi, i: (bi, i, 0)),
            pl.BlockSpec((1, seq + 2 * WINDOW, 2 * KV_WIDTH), lambda bi, i: (bi, 0, 0)),
        ],
        out_specs=pl.BlockSpec((1, tq, ATTN_WIDTH), lambda bi, i: (bi, i, 0)),
        out_shape=jax.ShapeDtypeStruct((b, seq, ATTN_WIDTH), BF16),
        compiler_params=_params("parallel", "arbitrary"),
        name="band_attention",
    )(sink, q, kv_pad)


def _filter_kernel(z_ref, t_ref, w1_ref, b1_ref, f1_ref, w2_ref, b2_ref, f2_ref, w3_ref, ad_ref, o_ref, *, tl):
    hp = lax.Precision.HIGHEST
    h = jnp.sin(f1_ref[...] * (jnp.dot(z_ref[...], w1_ref[...], precision=hp, preferred_element_type=F32)
                               + b1_ref[...]))
    h = jnp.sin(f2_ref[...] * (jnp.dot(h, w2_ref[...], precision=hp, preferred_element_type=F32) + b2_ref[...]))
    t = t_ref[...]
    rowid = pl.program_id(0) * tl + lax.broadcasted_iota(jnp.int32, (tl, HY_WIDTH), 0)
    for d in range(2):
        for o in range(HY_ORDER):
            c0 = (d * HY_ORDER + o) * HY_WIDTH
            v = jnp.dot(h, w3_ref[:, c0:c0 + HY_WIDTH], precision=hp, preferred_element_type=F32)
            v = v * jnp.exp(-t * ad_ref[:, o * HY_WIDTH:(o + 1) * HY_WIDTH])
            if d == 1:
                v = jnp.where(rowid == 0, 0.0, v)
            o_ref[d * HY_ORDER + o] = v


def _hyena_filters(z_pos, t_pos, w1, b1, f1, w2, b2, f2, w3, absd):
    L, e = z_pos.shape
    tl = 512
    full = lambda a: pl.BlockSpec(a.shape, lambda i: (0,) * a.ndim)
    return pl.pallas_call(
        functools.partial(_filter_kernel, tl=tl),
        grid=(L // tl,),
        in_specs=[
            pl.BlockSpec((tl, e), lambda i: (i, 0)),
            pl.BlockSpec((tl, 1), lambda i: (i, 0)),
            full(w1), full(b1), full(f1), full(w2), full(b2), full(f2), full(w3), full(absd),
        ],
        out_specs=pl.BlockSpec((2 * HY_ORDER, tl, HY_WIDTH), lambda i: (0, i, 0)),
        out_shape=jax.ShapeDtypeStruct((2 * HY_ORDER, L, HY_WIDTH), F32),
        compiler_params=_params("parallel"),
        name="hyena_filters",
    )(z_pos, t_pos, w1, b1, f1, w2, b2, f2, w3, absd)


def _shortconv_kernel(u_ref, w_ref, b_ref, o_ref):
    u = u_ref[0].astype(F32)
    L = u.shape[0]
    row = lax.broadcasted_iota(jnp.int32, u.shape, 0)
    prev = jnp.where(row == 0, 0.0, pltpu.roll(u, 1, 0))
    nxt = jnp.where(row == L - 1, 0.0, pltpu.roll(u, L - 1, 0))
    o_ref[0] = prev * w_ref[0:1, :] + u * w_ref[1:2, :] + nxt * w_ref[2:3, :] + b_ref[...]


def _shortconv(u, w, bias):
    b, L, c3 = u.shape
    ct = 256
    return pl.pallas_call(
        _shortconv_kernel,
        grid=(b, c3 // ct),
        in_specs=[
            pl.BlockSpec((1, L, ct), lambda bi, ci: (bi, 0, ci)),
            pl.BlockSpec((3, ct), lambda bi, ci: (0, ci)),
            pl.BlockSpec((1, ct), lambda bi, ci: (0, ci)),
        ],
        out_specs=pl.BlockSpec((1, L, ct), lambda bi, ci: (bi, 0, ci)),
        out_shape=jax.ShapeDtypeStruct((b, L, c3), F32),
        compiler_params=_params("parallel", "parallel"),
        name="shortconv",
    )(u, w, bias)


def _cis(m, n):
    ang = (2.0 * math.pi / n) * (m % n).astype(F32)
    return jnp.cos(ang), jnp.sin(ang)


def _complex_blocks(re, im):
    return jnp.stack([jnp.stack([re, -im]), jnp.stack([im, re])])


def _fft_tables():
    ar = lambda n: jnp.arange(n, dtype=jnp.int32)
    nh = FFT_N2 // 2
    g = ar(FFT_GROUPS)[:, None, None, None]
    k2 = ar(FFT_N2)[None, :, None, None]
    j = ar(FFT_J)[None, None, :, None]
    n2 = ar(nh)[None, None, None, :]
    m = (FFT_N1 * n2 * k2 + (FFT_J * g + j) * k2) % FFT_N
    c, s = _cis(m, FFT_N)
    eye = jnp.eye(FFT_J, dtype=F32)
    fwd = _complex_blocks(c, -s)
    g_fwd = jnp.einsum("opgkjn,jJ->gkojpnJ", fwd, eye).reshape(FFT_GROUPS, FFT_N2 * 2 * FFT_J, 2 * nh * FFT_J)
    inv = _complex_blocks(c, s) * (1.0 / FFT_N)
    g_inv = jnp.einsum("opgkjn,jJ->gonjkpJ", inv, eye).reshape(FFT_GROUPS, 2 * nh * FFT_J, FFT_N2 * 2 * FFT_J)
    mk = (ar(FFT_N1)[:, None] * ar(FFT_N1)[None, :]) % FFT_N1
    c1, s1 = _cis(mk, FFT_N1)
    fb = jnp.transpose(_complex_blocks(c1, -s1), (0, 2, 1, 3)).reshape(2 * FFT_N1, 2 * FFT_N1)
    fbi = jnp.transpose(_complex_blocks(c1, s1), (0, 2, 1, 3)).reshape(2 * FFT_N1, 2 * FFT_N1)
    return g_fwd.astype(BF16), g_inv.astype(BF16), fb.astype(BF16), fbi.astype(BF16)


def _fs_kernel(*refs, paired):
    if paired:
        zre_ref, zim_ref, g_ref, o_ref = refs
    else:
        zre_ref, g_ref, o_ref = refs
    nh = FFT_N2 // 2
    ct = o_ref.shape[-1]
    for s in range(FFT_SUB):
        rows = slice(FFT_J * s, FFT_J * (s + 1))
        tre = zre_ref[0, :, rows, :].reshape(nh * FFT_J, ct)
        if paired:
            tim = zim_ref[0, :, rows, :].reshape(nh * FFT_J, ct)
            r = _dot(g_ref[s], jnp.concatenate([tre, tim], axis=0).astype(BF16))
        else:
            r = _dot(g_ref[s, :, :nh * FFT_J], tre.astype(BF16))
        o_ref[0, :, :, rows, :] = r.reshape(FFT_N2, 2, FFT_J, ct)


def _fft_small_fwd(z4, g_fwd, *, paired, c_off, n_c):
    nb, nh, n1, _ = z4.shape
    ct = MXU_DIM
    ncb = nb // 2 if paired else nb
    cblk = c_off // ct
    zspec = lambda off: pl.BlockSpec((1, nh, FFT_CHUNK, ct), lambda b, ci, q: (b + off, 0, q, cblk + ci))
    gspec = pl.BlockSpec((FFT_SUB,) + g_fwd.shape[1:], lambda b, ci, q: (q, 0, 0))
    ins = [zspec(0), zspec(ncb), gspec] if paired else [zspec(0), gspec]
    args = (z4, z4, g_fwd) if paired else (z4, g_fwd)
    return pl.pallas_call(
        functools.partial(_fs_kernel, paired=paired),
        grid=(ncb, n_c // ct, n1 // FFT_CHUNK),
        in_specs=ins,
        out_specs=pl.BlockSpec((1, FFT_N2, 2, FFT_CHUNK, ct), lambda b, ci, q: (b, 0, 0, q, ci)),
        out_shape=jax.ShapeDtypeStruct((ncb, FFT_N2, 2, n1, n_c), F32),
        compiler_params=_params("parallel", "parallel", "arbitrary"),
        name="fft_small_fwd",
    )(*args)


def _big_kernel(x_ref, h_ref, fb_ref, fbi_ref, o_ref):
    ct = o_ref.shape[-1]
    hre = h_ref[0, 0]
    him = h_ref[0, 1]
    for b in range(x_ref.shape[0]):
        x = x_ref[b, 0].reshape(2 * FFT_N1, ct).astype(BF16)
        xf = _dot(fb_ref[...], x)
        xr = xf[:FFT_N1]
        xi = xf[FFT_N1:]
        y = jnp.concatenate([xr * hre - xi * him, xr * him + xi * hre], axis=0).astype(BF16)
        o_ref[b, 0] = _dot(fbi_ref[...], y).reshape(2, FFT_N1, ct)


def _fft_big(xs, hspec, fb, fbi):
    ncb, n2, _, n1, c = xs.shape
    ct = MXU_DIM
    nbb = 4 if ncb % 4 == 0 else 1
    return pl.pallas_call(
        _big_kernel,
        grid=(c // ct, n2, ncb // nbb),
        in_specs=[
            pl.BlockSpec((nbb, 1, 2, n1, ct), lambda ci, k, b: (b, k, 0, 0, ci)),
            pl.BlockSpec((1, 2, n1, ct), lambda ci, k, b: (k, 0, 0, ci)),
            pl.BlockSpec(fb.shape, lambda ci, k, b: (0, 0)),
            pl.BlockSpec(fbi.shape, lambda ci, k, b: (0, 0)),
        ],
        out_specs=pl.BlockSpec((nbb, 1, 2, n1, ct), lambda ci, k, b: (b, k, 0, 0, ci)),
        out_shape=jax.ShapeDtypeStruct(xs.shape, F32),
        compiler_params=_params("parallel", "parallel", "arbitrary"),
        name="fft_big",
    )(xs, hspec, fb, fbi)


def _spectrum_kernel(x_ref, fb_ref, o_ref):
    ct = o_ref.shape[-1]
    xf = [_dot(fb_ref[...], x_ref[i, 0].reshape(2 * FFT_N1, ct).astype(BF16)) for i in range(2 * HY_ORDER)]
    for o in range(HY_ORDER):
        a = xf[o]
        r = xf[HY_ORDER + o]
        o_ref[o, 0, 0] = a[:FFT_N1] + r[:FFT_N1]
        o_ref[o, 0, 1] = a[FFT_N1:] - r[FFT_N1:]


def _filter_spectrum(xs, fb):
    nf, n2, _, n1, c = xs.shape
    ct = MXU_DIM
    return pl.pallas_call(
        _spectrum_kernel,
        grid=(c // ct, n2),
        in_specs=[
            pl.BlockSpec((nf, 1, 2, n1, ct), lambda ci, k: (0, k, 0, 0, ci)),
            pl.BlockSpec(fb.shape, lambda ci, k: (0, 0)),
        ],
        out_specs=pl.BlockSpec((HY_ORDER, 1, 2, n1, ct), lambda ci, k: (0, k, 0, 0, ci)),
        out_shape=jax.ShapeDtypeStruct((HY_ORDER, n2, 2, n1, c), F32),
        compiler_params=_params("parallel", "parallel"),
        name="filter_spectrum",
    )(xs, fb)


def _is_kernel(c_ref, gi_ref, z_ref, gate_ref, skip_ref, o_ref):
    nh = FFT_N2 // 2
    ct = o_ref.shape[-1]
    skip = skip_ref[...]
    for s in range(FFT_SUB):
        rows = slice(FFT_J * s, FFT_J * (s + 1))
        tile = c_ref[0, :, :, rows, :].reshape(FFT_N2 * 2 * FFT_J, ct).astype(BF16)
        r = _dot(gi_ref[s], tile).reshape(2, nh, FFT_J, ct)
        z = z_ref[:, 0, :, rows, :]
        o_ref[:, 0, :, rows, :] = gate_ref[:, 0, :, rows, :] * (r + z * skip)


def _fft_small_inv(cs, g_inv, z5, z_off, gate5, gate_off, skip):
    ncb, n2, _, n1, c = cs.shape
    nh = n2 // 2
    ct = MXU_DIM
    zb = z_off // ct
    gb = gate_off // ct
    pair = lambda off: pl.BlockSpec((2, 1, nh, FFT_CHUNK, ct), lambda b, ci, q: (0, b, 0, q, off + ci))
    return pl.pallas_call(
        _is_kernel,
        grid=(ncb, c // ct, n1 // FFT_CHUNK),
        in_specs=[
            pl.BlockSpec((1, n2, 2, FFT_CHUNK, ct), lambda b, ci, q: (b, 0, 0, q, ci)),
            pl.BlockSpec((FFT_SUB,) + g_inv.shape[1:], lambda b, ci, q: (q, 0, 0)),
            pair(zb), pair(gb),
            pl.BlockSpec((1, ct), lambda b, ci, q: (0, ci)),
        ],
        out_specs=pair(0),
        out_shape=jax.ShapeDtypeStruct((2, ncb, nh, n1, c), F32),
        compiler_params=_params("parallel", "parallel", "arbitrary"),
        name="fft_small_inv",
    )(cs, g_inv, z5, gate5, skip)


def _norm_mm_kernel(x_ref, g_ref, w_ref, o_ref):
    o_ref[...] = _dot(_rms(x_ref[...], g_ref[...]).astype(BF16), w_ref[...]).astype(o_ref.dtype)


def _norm_mm(x2d, g, w):
    n, d = x2d.shape
    tm = 512
    return pl.pallas_call(
        _norm_mm_kernel,
        grid=(n // tm,),
        in_specs=[
            pl.BlockSpec((tm, d), lambda i: (i, 0)),
            pl.BlockSpec((1, d), lambda i: (0, 0)),
            pl.BlockSpec(w.shape, lambda i: (0, 0)),
        ],
        out_specs=pl.BlockSpec((tm, w.shape[1]), lambda i: (i, 0)),
        out_shape=jax.ShapeDtypeStruct((n, w.shape[1]), BF16),
        compiler_params=_params("parallel"),
        name="memory_kv",
    )(x2d, g, w)


def _mix_xattn_kernel(x_ref, a_ref, y_ref, ga_ref, gy_ref, wo_ref, gx_ref, wq_ref, kv_ref, wxo_ref, o_ref):
    an = _rms(a_ref[0].astype(F32), ga_ref[...]).astype(BF16)
    yn = _rms(y_ref[0], gy_ref[...]).astype(BF16)
    x1 = x_ref[0] + _dot(an, wo_ref[:ATTN_WIDTH, :]) + _dot(yn, wo_ref[ATTN_WIDTH:, :])
    h = _rms(x1, gx_ref[...]).astype(BF16)
    q = (_dot(h, wq_ref[...]) * (X_HEAD_DIM ** -0.5)).astype(BF16)
    outs = []
    for hh in range(X_HEADS):
        cols = slice(hh * X_HEAD_DIM, (hh + 1) * X_HEAD_DIM)
        kh = kv_ref[0, :, cols]
        vh = kv_ref[0, :, X_WIDTH + hh * X_HEAD_DIM:X_WIDTH + (hh + 1) * X_HEAD_DIM]
        s = _dot_t(q[:, cols], kh)
        p = jnp.exp(s - jnp.max(s, axis=-1, keepdims=True))
        den = jnp.sum(p, axis=-1, keepdims=True)
        outs.append((_dot(p.astype(BF16), vh) / den).astype(BF16))
    o_ref[0] = x1 + _dot(jnp.concatenate(outs, axis=1), wxo_ref[...])


def _mix_xattn(x, a, y, ga, gy, wo, gx, wq, kv, wxo):
    b, s, d = x.shape
    tq = 512
    m = kv.shape[1]
    full = lambda arr: pl.BlockSpec(arr.shape, lambda bi, i: (0,) * arr.ndim)
    tok = lambda w: pl.BlockSpec((1, tq, w), lambda bi, i: (bi, i, 0))
    return pl.pallas_call(
        _mix_xattn_kernel,
        grid=(b, s // tq),
        in_specs=[tok(d), tok(ATTN_WIDTH), tok(HY_WIDTH), full(ga), full(gy), full(wo), full(gx), full(wq),
                  pl.BlockSpec((1, m, 2 * X_WIDTH), lambda bi, i: (bi, 0, 0)), full(wxo)],
        out_specs=tok(d),
        out_shape=jax.ShapeDtypeStruct((b, s, d), F32),
        compiler_params=_params("parallel", "arbitrary"),
        name="mix_xattn",
    )(x, a, y, ga, gy, wo, gx, wq, kv, wxo)


def _swiglu_step(h, wg, wu, wd):
    g = _dot(h, wg)
    u = _dot(h, wu)
    a = (g * (1.0 / (1.0 + jnp.exp(-g))) * u).astype(BF16)
    return _dot(a, wd)


def _ffn_kernel(x_ref, g_ref, wg_ref, wu_ref, wd_ref, o_ref, h_ref, acc_ref):
    f = pl.program_id(1)

    @pl.when(f == 0)
    def _():
        h_ref[...] = _rms(x_ref[...], g_ref[...]).astype(BF16)
        acc_ref[...] = x_ref[...]

    acc_ref[...] += _swiglu_step(h_ref[...], wg_ref[...], wu_ref[...], wd_ref[...])

    @pl.when(f == pl.num_programs(1) - 1)
    def _():
        o_ref[...] = acc_ref[...]


def _dense_ffn(x2d, g, wg, wu, wd):
    n, d = x2d.shape
    ff = wg.shape[1]
    tm = 512
    tf = ff // 2
    return pl.pallas_call(
        _ffn_kernel,
        grid=(n // tm, ff // tf),
        in_specs=[
            pl.BlockSpec((tm, d), lambda i, f: (i, 0)),
            pl.BlockSpec((1, d), lambda i, f: (0, 0)),
            pl.BlockSpec((d, tf), lambda i, f: (0, f)),
            pl.BlockSpec((d, tf), lambda i, f: (0, f)),
            pl.BlockSpec((tf, d), lambda i, f: (f, 0)),
        ],
        out_specs=pl.BlockSpec((tm, d), lambda i, f: (i, 0)),
        out_shape=jax.ShapeDtypeStruct((n, d), F32),
        scratch_shapes=[pltpu.VMEM((tm, d), BF16), pltpu.VMEM((tm, d), F32)],
        compiler_params=_params("parallel", "arbitrary"),
        name="dense_ffn",
    )(x2d, g, wg, wu, wd)


def _router_kernel(x_ref, g_ref, wr_ref, h_ref, r_ref):
    hf = _rms(x_ref[...], g_ref[...])
    h_ref[...] = hf.astype(BF16)
    logits = jnp.dot(hf, wr_ref[...], precision=lax.Precision.HIGHEST, preferred_element_type=F32)
    lane = lax.broadcasted_iota(jnp.int32, logits.shape, 1)
    logits = jnp.where(lane < N_EXPERTS, logits, NEG)
    lanef = lane.astype(F32)
    big = float(LANES)
    m1 = jnp.max(logits, axis=-1, keepdims=True)
    i1 = jnp.min(jnp.where(logits == m1, lanef, big), axis=-1, keepdims=True)
    rest = jnp.where(lanef == i1, NEG, logits)
    m2 = jnp.max(rest, axis=-1, keepdims=True)
    i2 = jnp.min(jnp.where(rest == m2, lanef, big), axis=-1, keepdims=True)
    e2 = jnp.exp(m2 - m1)
    w1 = 1.0 / (1.0 + e2)
    w2 = e2 / (1.0 + e2)
    r_ref[...] = jnp.where(lane == 0, i1,
                           jnp.where(lane == 1, i2, jnp.where(lane == 2, w1, jnp.where(lane == 3, w2, 0.0))))


def _router(x2d, g, wr_pad):
    n, d = x2d.shape
    tm = 512
    return pl.pallas_call(
        _router_kernel,
        grid=(n // tm,),
        in_specs=[
            pl.BlockSpec((tm, d), lambda i: (i, 0)),
            pl.BlockSpec((1, d), lambda i: (0, 0)),
            pl.BlockSpec(wr_pad.shape, lambda i: (0, 0)),
        ],
        out_specs=[pl.BlockSpec((tm, d), lambda i: (i, 0)), pl.BlockSpec((tm, LANES), lambda i: (i, 0))],
        out_shape=[jax.ShapeDtypeStruct((n, d), BF16), jax.ShapeDtypeStruct((n, LANES), F32)],
        compiler_params=_params("parallel"),
        name="router",
    )(x2d, g, wr_pad)


def _moe_kernel(be_ref, x_ref, wg_ref, wu_ref, wd_ref, o_ref, acc_ref):
    f = pl.program_id(1)

    @pl.when(f == 0)
    def _():
        acc_ref[...] = jnp.zeros_like(acc_ref)

    acc_ref[...] += _swiglu_step(x_ref[...], wg_ref[0], wu_ref[0], wd_ref[0])

    @pl.when(f == pl.num_programs(1) - 1)
    def _():
        o_ref[...] = acc_ref[...].astype(BF16)


def _moe_experts(blk_e, xb, wg, wu, wd):
    rows, d = xb.shape
    ff = wg.shape[2]
    tf = ff // 4
    n_blk = rows // MOE_ROWS
    grid_spec = pltpu.PrefetchScalarGridSpec(
        num_scalar_prefetch=1,
        grid=(n_blk, ff // tf),
        in_specs=[
            pl.BlockSpec((MOE_ROWS, d), lambda i, f, be: (i, 0)),
            pl.BlockSpec((1, d, tf), lambda i, f, be: (be[i], 0, f)),
            pl.BlockSpec((1, d, tf), lambda i, f, be: (be[i], 0, f)),
            pl.BlockSpec((1, tf, d), lambda i, f, be: (be[i], f, 0)),
        ],
        out_specs=pl.BlockSpec((MOE_ROWS, d), lambda i, f, be: (i, 0)),
        scratch_shapes=[pltpu.VMEM((MOE_ROWS, d), F32)],
    )
    return pl.pallas_call(
        _moe_kernel,
        grid_spec=grid_spec,
        out_shape=jax.ShapeDtypeStruct((rows, d), BF16),
        compiler_params=_params("parallel", "arbitrary"),
        name="moe_experts",
    )(blk_e, xb, wg, wu, wd)


def _combine_kernel(*refs, normed):
    if normed:
        x_ref, y1_ref, y2_ref, r_ref, g_ref, o_ref = refs
    else:
        x_ref, y1_ref, y2_ref, r_ref, o_ref = refs
    w1 = r_ref[:, 2:3]
    w2 = r_ref[:, 3:4]
    x = x_ref[...] + y1_ref[...].astype(F32) * w1 + y2_ref[...].astype(F32) * w2
    o_ref[...] = _rms(x, g_ref[...]) if normed else x


def _combine(x2d, y1, y2, route, gain):
    n, d = x2d.shape
    tm = 512
    normed = gain is not None
    tok = lambda w: pl.BlockSpec((tm, w), lambda i: (i, 0))
    ins = [tok(d), tok(d), tok(d), tok(LANES)]
    args = [x2d, y1, y2, route]
    if normed:
        ins.append(pl.BlockSpec((1, d), lambda i: (0, 0)))
        args.append(gain)
    return pl.pallas_call(
        functools.partial(_combine_kernel, normed=normed),
        grid=(n // tm,),
        in_specs=ins,
        out_specs=tok(d),
        out_shape=jax.ShapeDtypeStruct((n, d), F32),
        compiler_params=_params("parallel"),
        name="moe_combine",
    )(*args)


def _norm_kernel(x_ref, g_ref, o_ref):
    o_ref[...] = _rms(x_ref[...], g_ref[...])


def _final_norm(x2d, g):
    n, d = x2d.shape
    tm = 512
    return pl.pallas_call(
        _norm_kernel,
        grid=(n // tm,),
        in_specs=[pl.BlockSpec((tm, d), lambda i: (i, 0)), pl.BlockSpec((1, d), lambda i: (0, 0))],
        out_specs=pl.BlockSpec((tm, d), lambda i: (i, 0)),
        out_shape=jax.ShapeDtypeStruct((n, d), F32),
        compiler_params=_params("parallel"),
        name="final_norm",
    )(x2d, g)


def _hyena(u, conv_w, conv_b, filt, skip, tables):
    g_fwd, g_inv, fb, fbi = tables
    b, L, c3 = u.shape
    c = c3 // (HY_ORDER + 1)
    nh = FFT_N2 // 2
    uc = _shortconv(u, conv_w, conv_b.reshape(1, c3))
    uc4 = uc.reshape(b, nh, FFT_N1, c3)
    uc5 = uc.reshape(2, b // 2, nh, FFT_N1, c3)
    spec = _filter_spectrum(
        _fft_small_fwd(filt.reshape(2 * HY_ORDER, nh, FFT_N1, c), g_fwd, paired=False, c_off=0, n_c=c), fb)
    t = _fft_small_fwd(uc4, g_fwd, paired=True, c_off=2 * c, n_c=c)
    t = _fft_big(t, spec[0], fb, fbi)
    zz5 = _fft_small_inv(t, g_inv, uc5, 2 * c, uc5, 0, skip[0:1])
    t = _fft_small_fwd(zz5.reshape(b, nh, FFT_N1, c), g_fwd, paired=True, c_off=0, n_c=c)
    t = _fft_big(t, spec[1], fb, fbi)
    y5 = _fft_small_inv(t, g_inv, zz5, 0, uc5, c, skip[1:2])
    return y5.reshape(b, L, c)


def _moe_dispatch(route, n_tok):
    n_asg = n_tok * TOP_K
    flat_e = route[:, :TOP_K].astype(jnp.int32).reshape(-1)
    onehot = (flat_e[:, None] == jnp.arange(N_EXPERTS, dtype=jnp.int32)[None, :]).astype(jnp.int32)
    csum = jnp.cumsum(onehot, axis=0)
    counts = csum[-1]
    rank = jnp.sum((csum - 1) * onehot, axis=1)
    padded = (counts + MOE_ROWS - 1) // MOE_ROWS * MOE_ROWS
    pad_end = jnp.cumsum(padded)
    pad_start = pad_end - padded
    dest = pad_start[flat_e] + rank
    n_blk = -(-(n_asg + N_EXPERTS * (MOE_ROWS - 1)) // MOE_ROWS)
    flat_tok = jnp.repeat(jnp.arange(n_tok, dtype=jnp.int32), TOP_K)
    buf_tok = jnp.zeros((n_blk * MOE_ROWS,), jnp.int32).at[dest].set(flat_tok)
    blk_e = jnp.minimum(
        jnp.searchsorted(pad_end, jnp.arange(n_blk, dtype=jnp.int32) * MOE_ROWS, side="right"), N_EXPERTS - 1)
    return buf_tok, blk_e.astype(jnp.int32), dest.reshape(n_tok, TOP_K)


def kernel(x, mem, mem_norm, mix_norm, w_in, attn_sink, hy_conv_w, hy_conv_b, hy_f_w1, hy_f_b1, hy_f_freq1,
           hy_f_w2, hy_f_b2, hy_f_freq2, hy_f_w3, hy_skip, attn_out_norm, hy_out_norm, w_out, xattn_norm,
           xw_q, xw_k, xw_v, xw_o, ffn_norm, ffn_w_gate, ffn_w_up, ffn_w_down,
           moe_router, moe_w_gate, moe_w_up, moe_w_down, final_norm):
    b, s, d = x.shape
    depth = w_in.shape[0]
    n_tok = b * s
    assert s == FFT_N // 2 and b % 2 == 0
    row = lambda v: v.reshape(1, -1).astype(F32)

    pos = jnp.arange(s, dtype=F32)
    inv = ROPE_THETA ** (-jnp.arange(0, HEAD_DIM, 2, dtype=F32) / HEAD_DIM)
    ang = pos[:, None] * inv[None, :]
    cosf = jnp.tile(jnp.cos(ang), (1, LANES // (HEAD_DIM // 2)))
    sins = jnp.tile(jnp.concatenate([-jnp.sin(ang), jnp.sin(ang)], axis=1), (1, LANES // HEAD_DIM))
    t_pos = jnp.linspace(0.0, 1.0, s, dtype=F32)[:, None]
    wv = 2.0 * math.pi * jnp.arange(s, dtype=F32)[:, None] / s
    fr = jnp.linspace(1e-4, HY_BANDS - 1, HY_BANDS, dtype=F32)[None, :]
    z_pos = jnp.concatenate([t_pos, jnp.cos(fr * wv), -jnp.sin(fr * wv)], axis=-1)
    emb_pad = 64
    z_pos = jnp.pad(z_pos, ((0, 0), (0, emb_pad - HY_EMB)))
    absd = jnp.abs(jnp.linspace(HY_MIN_DECAY, HY_MAX_DECAY, HY_ORDER * HY_WIDTH, dtype=F32)).reshape(1, -1)
    tables = _fft_tables()

    mem2d = mem.reshape(-1, d)
    x2d = x.reshape(n_tok, d)
    out = None
    for l in range(depth):
        q, kv, u = _inproj(x2d, row(mix_norm[l]), w_in[l].astype(BF16), cosf, sins, s)
        kv_pad = jnp.pad(kv.reshape(b, s, 2 * KV_WIDTH), ((0, 0), (WINDOW, WINDOW), (0, 0)))
        a = _band_attention(q.reshape(b, s, ATTN_WIDTH), kv_pad, attn_sink[l].astype(F32), s)
        filt = _hyena_filters(
            z_pos, t_pos, jnp.pad(hy_f_w1[l], ((0, emb_pad - HY_EMB), (0, 0))), row(hy_f_b1[l]),
            row(hy_f_freq1[l]), hy_f_w2[l], row(hy_f_b2[l]), row(hy_f_freq2[l]), hy_f_w3[l], absd)
        y = _hyena(u.reshape(b, s, -1), hy_conv_w[l], hy_conv_b[l], filt, hy_skip[l], tables)
        mkv = _norm_mm(mem2d, row(mem_norm), jnp.concatenate([xw_k[l], xw_v[l]], axis=1).astype(BF16))
        x3 = _mix_xattn(x2d.reshape(b, s, d), a, y, row(attn_out_norm[l]), row(hy_out_norm[l]),
                        w_out[l].astype(BF16), row(xattn_norm[l]), xw_q[l].astype(BF16),
                        mkv.reshape(b, -1, 2 * X_WIDTH), xw_o[l].astype(BF16))
        x2d = x3.reshape(n_tok, d)
        j = l // 2
        last = l == depth - 1
        if l % 2 == 0:
            x2d = _dense_ffn(x2d, row(ffn_norm[l]), ffn_w_gate[j].astype(BF16), ffn_w_up[j].astype(BF16),
                             ffn_w_down[j].astype(BF16))
            if last:
                out = _final_norm(x2d, row(final_norm))
        else:
            wr = jnp.pad(moe_router[j].astype(F32), ((0, 0), (0, LANES - N_EXPERTS)))
            hb, route = _router(x2d, row(ffn_norm[l]), wr)
            buf_tok, blk_e, dest = _moe_dispatch(route, n_tok)
            yb = _moe_experts(blk_e, hb[buf_tok], moe_w_gate[j].astype(BF16), moe_w_up[j].astype(BF16),
                              moe_w_down[j].astype(BF16))
            res = _combine(x2d, yb[dest[:, 0]], yb[dest[:, 1]], route, row(final_norm) if last else None)
            if last:
                out = res
            else:
                x2d = res
    return out.reshape(b, s, d)
```

```python
import functools
import math

import jax
import jax.numpy as jnp
from jax import lax
from jax.experimental import pallas as pl
from jax.experimental.pallas import tpu as pltpu

F32 = jnp.float32
BF16 = jnp.bfloat16

EPS = 1e-6
N_HEADS = 8
N_KV_HEADS = 2
HEAD_DIM = 64
ATTN_WIDTH = N_HEADS * HEAD_DIM
KV_WIDTH = N_KV_HEADS * HEAD_DIM
WINDOW = 128
ROPE_THETA = 10000.0
HY_WIDTH = 512
HY_ORDER = 2
HY_EMB = 33
HY_BANDS = (HY_EMB - 1) // 2
HY_FILTER_HIDDEN = 64
HY_TARGET = 1e-2
HY_FAST_DECAY = 0.3
HY_SLOW_DECAY = 1.5
HY_MIN_DECAY = math.log(HY_TARGET) / HY_SLOW_DECAY
HY_MAX_DECAY = math.log(HY_TARGET) / HY_FAST_DECAY
Q_END = ATTN_WIDTH
K_END = Q_END + KV_WIDTH
V_END = K_END + KV_WIDTH
X_HEADS = 4
X_HEAD_DIM = 128
X_WIDTH = X_HEADS * X_HEAD_DIM
N_EXPERTS = 8
TOP_K = 2

LANES = 128
SUBLANES = 8
MXU_DIM = 256
VMEM_LIMIT = 56 * 1024 * 1024

FFT_N1 = 256
FFT_N2 = 32
FFT_N = FFT_N1 * FFT_N2
FFT_J = SUBLANES
FFT_GROUPS = FFT_N1 // FFT_J
FFT_CHUNK = 64
FFT_SUB = FFT_CHUNK // FFT_J

MOE_ROWS = 1024
NEG = float(jnp.finfo(jnp.float32).min)
MASKED = -1e30
LOG2E = math.log2(math.e)


def _params(*sem):
    return pltpu.CompilerParams(dimension_semantics=sem, vmem_limit_bytes=VMEM_LIMIT)


def _dot(a, b):
    return jnp.dot(a, b, preferred_element_type=F32)


def _dot_t(a, b):
    return lax.dot_general(a, b, (((1,), (1,)), ((), ())), preferred_element_type=F32)


def _rms(xf, g):
    ms = jnp.mean(xf * xf, axis=-1, keepdims=True)
    return xf * lax.rsqrt(ms + EPS) * g


def _inproj_kernel(x_ref, g_ref, w_ref, cos_ref, sin_ref, q_ref, kv_ref, u_ref):
    h = _rms(x_ref[...], g_ref[...]).astype(BF16)
    cosf = cos_ref[...]
    sins = sin_ref[...]
    lane = lax.broadcasted_iota(jnp.int32, cosf.shape, 1)
    low = (lane % HEAD_DIM) < (HEAD_DIM // 2)

    def rope(c):
        rot = jnp.where(low, pltpu.roll(c, LANES - HEAD_DIM // 2, 1), pltpu.roll(c, HEAD_DIM // 2, 1))
        return c * cosf + rot * sins

    scale = HEAD_DIM ** -0.5 * LOG2E
    for j in range(ATTN_WIDTH // MXU_DIM):
        qc = _dot(h, w_ref[:, j * MXU_DIM:(j + 1) * MXU_DIM])
        for t in range(MXU_DIM // LANES):
            c0 = j * MXU_DIM + t * LANES
            q_ref[:, c0:c0 + LANES] = (rope(qc[:, t * LANES:(t + 1) * LANES]) * scale).astype(BF16)
    kvc = _dot(h, w_ref[:, Q_END:V_END])
    kv_ref[:, :KV_WIDTH] = rope(kvc[:, :KV_WIDTH]).astype(BF16)
    kv_ref[:, KV_WIDTH:] = kvc[:, KV_WIDTH:].astype(BF16)
    n_u = u_ref.shape[1]
    for j in range(n_u // 512):
        u_ref[:, j * 512:(j + 1) * 512] = _dot(h, w_ref[:, V_END + j * 512:V_END + (j + 1) * 512]).astype(BF16)


def _inproj(x2d, g, w, cosf, sins, seq):
    n, d = x2d.shape
    tm = 512
    n_u = w.shape[1] - V_END
    spb = seq // tm
    return pl.pallas_call(
        _inproj_kernel,
        grid=(n // tm,),
        in_specs=[
            pl.BlockSpec((tm, d), lambda i: (i, 0)),
            pl.BlockSpec((1, d), lambda i: (0, 0)),
            pl.BlockSpec(w.shape, lambda i: (0, 0)),
            pl.BlockSpec((tm, LANES), lambda i: (i % spb, 0)),
            pl.BlockSpec((tm, LANES), lambda i: (i % spb, 0)),
        ],
        out_specs=[
            pl.BlockSpec((tm, ATTN_WIDTH), lambda i: (i, 0)),
            pl.BlockSpec((tm, 2 * KV_WIDTH), lambda i: (i, 0)),
            pl.BlockSpec((tm, n_u), lambda i: (i, 0)),
        ],
        out_shape=[
            jax.ShapeDtypeStruct((n, ATTN_WIDTH), BF16),
            jax.ShapeDtypeStruct((n, 2 * KV_WIDTH), BF16),
            jax.ShapeDtypeStruct((n, n_u), BF16),
        ],
        compiler_params=_params("parallel"),
        name="inproj",
    )(x2d, g, w, cosf, sins)


def _battn_kernel(sink_ref, q_ref, kv_ref, o_ref, *, seq, tq):
    i = pl.program_id(1)
    blk = WINDOW
    row = lax.broadcasted_iota(jnp.int32, (blk, blk), 0)
    col = lax.broadcasted_iota(jnp.int32, (blk, blk), 1)
    tri_prev = jnp.where(col >= row, 0.0, MASKED)
    tri_next = jnp.where(col <= row, 0.0, MASKED)
    group = N_HEADS // N_KV_HEADS
    heads_per_store = LANES // HEAD_DIM
    units = [(jb, hp) for jb in range(tq // blk) for hp in range(N_HEADS // heads_per_store)]

    def scores(jb, hp):
        base = i * tq + jb * blk
        start = pl.multiple_of(base, blk)
        g = hp * heads_per_store // group
        kk = kv_ref[0, pl.ds(start, 3 * blk), g * HEAD_DIM:(g + 1) * HEAD_DIM]
        out = []
        for t in range(heads_per_store):
            h = hp * heads_per_store + t
            out.append(_dot_t(q_ref[0, jb * blk:(jb + 1) * blk, h * HEAD_DIM:(h + 1) * HEAD_DIM], kk))
        return out

    def finish(jb, hp, s_pair):
        base = i * tq + jb * blk
        start = pl.multiple_of(base, blk)
        g = hp * heads_per_store // group
        vv = kv_ref[0, pl.ds(start, 3 * blk), KV_WIDTH + g * HEAD_DIM:KV_WIDTH + (g + 1) * HEAD_DIM]
        b_prev = tri_prev + jnp.where(base == 0, MASKED, 0.0)
        b_next = tri_next + jnp.where(base == seq - blk, MASKED, 0.0)
        outs = []
        for t in range(heads_per_store):
            s = s_pair[t]
            s0 = s[:, :blk] + b_prev
            s1 = s[:, blk:2 * blk]
            s2 = s[:, 2 * blk:] + b_next
            sk = sink_ref[hp * heads_per_store + t] * LOG2E
            m = jnp.maximum(jnp.max(jnp.maximum(jnp.maximum(s0, s1), s2), axis=-1, keepdims=True), sk)
            p0 = jnp.exp2(s0 - m)
            p1 = jnp.exp2(s1 - m)
            p2 = jnp.exp2(s2 - m)
            den = jnp.sum(p0 + p1 + p2, axis=-1, keepdims=True) + jnp.exp2(sk - m)
            p = jnp.concatenate([p0, p1, p2], axis=1).astype(BF16)
            outs.append(_dot(p, vv) * (1.0 / den))
        c0 = hp * LANES
        o_ref[0, jb * blk:(jb + 1) * blk, c0:c0 + LANES] = jnp.concatenate(outs, axis=1).astype(BF16)

    ahead = 2
    pending = [scores(*u) for u in units[:ahead]]
    for n, u in enumerate(units):
        if n + ahead < len(units):
            pending.append(scores(*units[n + ahead]))
        finish(*u, pending.pop(0))


def _band_attention(q, kv_pad, sink, seq):
    b = q.shape[0]
    tq = 512
    return pl.pallas_call(
        functools.partial(_battn_kernel, seq=seq, tq=tq),
        grid=(b, seq // tq),
        in_specs=[
            pl.BlockSpec(memory_space=pltpu.SMEM),
            pl.BlockSpec((1, tq, ATTN_WIDTH), lambda bi, i: (bi, i, 0)),
            pl.BlockSpec((1, seq + 2 * WINDOW, 2 * KV_WIDTH), lambda bi, i: (bi, 0, 0)),
        ],
        out_specs=pl.BlockSpec((1, tq, ATTN_WIDTH), lambda bi, i: (bi, i, 0)),
        out_shape=jax.ShapeDtypeStruct((b, seq, ATTN_WIDTH), BF16),
        compiler_params=_params("parallel", "arbitrary"),
        name="band_attention",
    )(sink, q, kv_pad)


def _filter_kernel(z_ref, t_ref, w1_ref, b1_ref, f1_ref, w2_ref, b2_ref, f2_ref, w3_ref, ad_ref, o_ref, *, tl):
    hp = lax.Precision.HIGHEST
    h = jnp.sin(f1_ref[...] * (jnp.dot(z_ref[...], w1_ref[...], precision=hp, preferred_element_type=F32)
                               + b1_ref[...]))
    h = jnp.sin(f2_ref[...] * (jnp.dot(h, w2_ref[...], precision=hp, preferred_element_type=F32) + b2_ref[...]))
    t = t_ref[...]
    rowid = pl.program_id(0) * tl + lax.broadcasted_iota(jnp.int32, (tl, HY_WIDTH), 0)
    for d in range(2):
        for o in range(HY_ORDER):
            c0 = (d * HY_ORDER + o) * HY_WIDTH
            v = jnp.dot(h, w3_ref[:, c0:c0 + HY_WIDTH], precision=hp, preferred_element_type=F32)
            v = v * jnp.exp(-t * ad_ref[:, o * HY_WIDTH:(o + 1) * HY_WIDTH])
            if d == 1:
                v = jnp.where(rowid == 0, 0.0, v)
            o_ref[d * HY_ORDER + o] = v


def _hyena_filters(z_pos, t_pos, w1, b1, f1, w2, b2, f2, w3, absd):
    L, e = z_pos.shape
    tl = 512
    full = lambda a: pl.BlockSpec(a.shape, lambda i: (0,) * a.ndim)
    return pl.pallas_call(
        functools.partial(_filter_kernel, tl=tl),
        grid=(L // tl,),
        in_specs=[
            pl.BlockSpec((tl, e), lambda i: (i, 0)),
            pl.BlockSpec((tl, 1), lambda i: (i, 0)),
            full(w1), full(b1), full(f1), full(w2), full(b2), full(f2), full(w3), full(absd),
        ],
        out_specs=pl.BlockSpec((2 * HY_ORDER, tl, HY_WIDTH), lambda i: (0, i, 0)),
        out_shape=jax.ShapeDtypeStruct((2 * HY_ORDER, L, HY_WIDTH), F32),
        compiler_params=_params("parallel"),
        name="hyena_filters",
    )(z_pos, t_pos, w1, b1, f1, w2, b2, f2, w3, absd)


def _shortconv_kernel(u_ref, w_ref, b_ref, o_ref):
    u = u_ref[0].astype(F32)
    L = u.shape[0]
    row = lax.broadcasted_iota(jnp.int32, u.shape, 0)
    prev = jnp.where(row == 0, 0.0, pltpu.roll(u, 1, 0))
    nxt = jnp.where(row == L - 1, 0.0, pltpu.roll(u, L - 1, 0))
    o_ref[0] = prev * w_ref[0:1, :] + u * w_ref[1:2, :] + nxt * w_ref[2:3, :] + b_ref[...]


def _shortconv(u, w, bias):
    b, L, c3 = u.shape
    ct = 256
    return pl.pallas_call(
        _shortconv_kernel,
        grid=(b, c3 // ct),
        in_specs=[
            pl.BlockSpec((1, L, ct), lambda bi, ci: (bi, 0, ci)),
            pl.BlockSpec((3, ct), lambda bi, ci: (0, ci)),
            pl.BlockSpec((1, ct), lambda bi, ci: (0, ci)),
        ],
        out_specs=pl.BlockSpec((1, L, ct), lambda bi, ci: (bi, 0, ci)),
        out_shape=jax.ShapeDtypeStruct((b, L, c3), F32),
        compiler_params=_params("parallel", "parallel"),
        name="shortconv",
    )(u, w, bias)


def _cis(m, n):
    ang = (2.0 * math.pi / n) * (m % n).astype(F32)
    return jnp.cos(ang), jnp.sin(ang)


def _complex_blocks(re, im):
    return jnp.stack([jnp.stack([re, -im]), jnp.stack([im, re])])


def _fft_tables():
    ar = lambda n: jnp.arange(n, dtype=jnp.int32)
    nh = FFT_N2 // 2
    g = ar(FFT_GROUPS)[:, None, None, None]
    k2 = ar(FFT_N2)[None, :, None, None]
    j = ar(FFT_J)[None, None, :, None]
    n2 = ar(nh)[None, None, None, :]
    m = (FFT_N1 * n2 * k2 + (FFT_J * g + j) * k2) % FFT_N
    c, s = _cis(m, FFT_N)
    eye = jnp.eye(FFT_J, dtype=F32)
    fwd = _complex_blocks(c, -s)
    g_fwd = jnp.einsum("opgkjn,jJ->gkojpnJ", fwd, eye).reshape(FFT_GROUPS, FFT_N2 * 2 * FFT_J, 2 * nh * FFT_J)
    inv = _complex_blocks(c, s) * (1.0 / FFT_N)
    g_inv = jnp.einsum("opgkjn,jJ->gonjkpJ", inv, eye).reshape(FFT_GROUPS, 2 * nh * FFT_J, FFT_N2 * 2 * FFT_J)
    mk = (ar(FFT_N1)[:, None] * ar(FFT_N1)[None, :]) % FFT_N1
    c1, s1 = _cis(mk, FFT_N1)
    fb = jnp.transpose(_complex_blocks(c1, -s1), (0, 2, 1, 3)).reshape(2 * FFT_N1, 2 * FFT_N1)
    fbi = jnp.transpose(_complex_blocks(c1, s1), (0, 2, 1, 3)).reshape(2 * FFT_N1, 2 * FFT_N1)
    return g_fwd.astype(BF16), g_inv.astype(BF16), fb.astype(BF16), fbi.astype(BF16)


def _fs_kernel(*refs, paired):
    if paired:
        zre_ref, zim_ref, g_ref, o_ref = refs
    else:
        zre_ref, g_ref, o_ref = refs
    nh = FFT_N2 // 2
    ct = o_ref.shape[-1]
    for s in range(FFT_SUB):
        rows = slice(FFT_J * s, FFT_J * (s + 1))
        tre = zre_ref[0, :, rows, :].reshape(nh * FFT_J, ct)
        if paired:
            tim = zim_ref[0, :, rows, :].reshape(nh * FFT_J, ct)
            r = _dot(g_ref[s], jnp.concatenate([tre, tim], axis=0).astype(BF16))
        else:
            r = _dot(g_ref[s, :, :nh * FFT_J], tre.astype(BF16))
        o_ref[0, :, :, rows, :] = r.reshape(FFT_N2, 2, FFT_J, ct)


def _fft_small_fwd(z4, g_fwd, *, paired, c_off, n_c):
    nb, nh, n1, _ = z4.shape
    ct = MXU_DIM
    ncb = nb // 2 if paired else nb
    cblk = c_off // ct
    zspec = lambda off: pl.BlockSpec((1, nh, FFT_CHUNK, ct), lambda b, ci, q: (b + off, 0, q, cblk + ci))
    gspec = pl.BlockSpec((FFT_SUB,) + g_fwd.shape[1:], lambda b, ci, q: (q, 0, 0))
    ins = [zspec(0), zspec(ncb), gspec] if paired else [zspec(0), gspec]
    args = (z4, z4, g_fwd) if paired else (z4, g_fwd)
    return pl.pallas_call(
        functools.partial(_fs_kernel, paired=paired),
        grid=(ncb, n_c // ct, n1 // FFT_CHUNK),
        in_specs=ins,
        out_specs=pl.BlockSpec((1, FFT_N2, 2, FFT_CHUNK, ct), lambda b, ci, q: (b, 0, 0, q, ci)),
        out_shape=jax.ShapeDtypeStruct((ncb, FFT_N2, 2, n1, n_c), F32),
        compiler_params=_params("parallel", "parallel", "arbitrary"),
        name="fft_small_fwd",
    )(*args)


def _big_kernel(x_ref, h_ref, fb_ref, fbi_ref, o_ref):
    ct = o_ref.shape[-1]
    hre = h_ref[0, 0]
    him = h_ref[0, 1]
    for b in range(x_ref.shape[0]):
        x = x_ref[b, 0].reshape(2 * FFT_N1, ct).astype(BF16)
        xf = _dot(fb_ref[...], x)
        xr = xf[:FFT_N1]
        xi = xf[FFT_N1:]
        y = jnp.concatenate([xr * hre - xi * him, xr * him + xi * hre], axis=0).astype(BF16)
        o_ref[b, 0] = _dot(fbi_ref[...], y).reshape(2, FFT_N1, ct)


def _fft_big(xs, hspec, fb, fbi):
    ncb, n2, _, n1, c = xs.shape
    ct = MXU_DIM
    nbb = 4 if ncb % 4 == 0 else 1
    return pl.pallas_call(
        _big_kernel,
        grid=(c // ct, n2, ncb // nbb),
        in_specs=[
            pl.BlockSpec((nbb, 1, 2, n1, ct), lambda ci, k, b: (b, k, 0, 0, ci)),
            pl.BlockSpec((1, 2, n1, ct), lambda ci, k, b: (k, 0, 0, ci)),
            pl.BlockSpec(fb.shape, lambda ci, k, b: (0, 0)),
            pl.BlockSpec(fbi.shape, lambda ci, k, b: (0, 0)),
        ],
        out_specs=pl.BlockSpec((nbb, 1, 2, n1, ct), lambda ci, k, b: (b, k, 0, 0, ci)),
        out_shape=jax.ShapeDtypeStruct(xs.shape, F32),
        compiler_params=_params("parallel", "parallel", "arbitrary"),
        name="fft_big",
    )(xs, hspec, fb, fbi)


def _spectrum_kernel(x_ref, fb_ref, o_ref):
    ct = o_ref.shape[-1]
    xf = [_dot(fb_ref[...], x_ref[i, 0].reshape(2 * FFT_N1, ct).astype(BF16)) for i in range(2 * HY_ORDER)]
    for o in range(HY_ORDER):
        a = xf[o]
        r = xf[HY_ORDER + o]
        o_ref[o, 0, 0] = a[:FFT_N1] + r[:FFT_N1]
        o_ref[o, 0, 1] = a[FFT_N1:] - r[FFT_N1:]


def _filter_spectrum(xs, fb):
    nf, n2, _, n1, c = xs.shape
    ct = MXU_DIM
    return pl.pallas_call(
        _spectrum_kernel,
        grid=(c // ct, n2),
        in_specs=[
            pl.BlockSpec((nf, 1, 2, n1, ct), lambda ci, k: (0, k, 0, 0, ci)),
            pl.BlockSpec(fb.shape, lambda ci, k: (0, 0)),
        ],
        out_specs=pl.BlockSpec((HY_ORDER, 1, 2, n1, ct), lambda ci, k: (0, k, 0, 0, ci)),
        out_shape=jax.ShapeDtypeStruct((HY_ORDER, n2, 2, n1, c), F32),
        compiler_params=_params("parallel", "parallel"),
        name="filter_spectrum",
    )(xs, fb)


def _is_kernel(c_ref, gi_ref, z_ref, gate_ref, skip_ref, o_ref):
    nh = FFT_N2 // 2
    ct = o_ref.shape[-1]
    skip = skip_ref[...]
    for s in range(FFT_SUB):
        rows = slice(FFT_J * s, FFT_J * (s + 1))
        tile = c_ref[0, :, :, rows, :].reshape(FFT_N2 * 2 * FFT_J, ct).astype(BF16)
        r = _dot(gi_ref[s], tile).reshape(2, nh, FFT_J, ct)
        z = z_ref[:, 0, :, rows, :]
        o_ref[:, 0, :, rows, :] = gate_ref[:, 0, :, rows, :] * (r + z * skip)


def _fft_small_inv(cs, g_inv, z5, z_off, gate5, gate_off, skip):
    ncb, n2, _, n1, c = cs.shape
    nh = n2 // 2
    ct = MXU_DIM
    zb = z_off // ct
    gb = gate_off // ct
    pair = lambda off: pl.BlockSpec((2, 1, nh, FFT_CHUNK, ct), lambda b, ci, q: (0, b, 0, q, off + ci))
    return pl.pallas_call(
        _is_kernel,
        grid=(ncb, c // ct, n1 // FFT_CHUNK),
        in_specs=[
            pl.BlockSpec((1, n2, 2, FFT_CHUNK, ct), lambda b, ci, q: (b, 0, 0, q, ci)),
            pl.BlockSpec((FFT_SUB,) + g_inv.shape[1:], lambda b, ci, q: (q, 0, 0)),
            pair(zb), pair(gb),
            pl.BlockSpec((1, ct), lambda b, ci, q: (0, ci)),
        ],
        out_specs=pair(0),
        out_shape=jax.ShapeDtypeStruct((2, ncb, nh, n1, c), F32),
        compiler_params=_params("parallel", "parallel", "arbitrary"),
        name="fft_small_inv",
    )(cs, g_inv, z5, gate5, skip)


def _norm_mm_kernel(x_ref, g_ref, w_ref, o_ref):
    o_ref[...] = _dot(_rms(x_ref[...], g_ref[...]).astype(BF16), w_ref[...]).astype(o_ref.dtype)


def _norm_mm(x2d, g, w):
    n, d = x2d.shape
    tm = 512
    return pl.pallas_call(
        _norm_mm_kernel,
        grid=(n // tm,),
        in_specs=[
            pl.BlockSpec((tm, d), lambda i: (i, 0)),
            pl.BlockSpec((1, d), lambda i: (0, 0)),
            pl.BlockSpec(w.shape, lambda i: (0, 0)),
        ],
        out_specs=pl.BlockSpec((tm, w.shape[1]), lambda i: (i, 0)),
        out_shape=jax.ShapeDtypeStruct((n, w.shape[1]), BF16),
        compiler_params=_params("parallel"),
        name="memory_kv",
    )(x2d, g, w)


def _mix_xattn_kernel(x_ref, a_ref, y_ref, ga_ref, gy_ref, wo_ref, gx_ref, wq_ref, kv_ref, wxo_ref, o_ref):
    an = _rms(a_ref[0].astype(F32), ga_ref[...]).astype(BF16)
    yn = _rms(y_ref[0], gy_ref[...]).astype(BF16)
    x1 = x_ref[0] + _dot(an, wo_ref[:ATTN_WIDTH, :]) + _dot(yn, wo_ref[ATTN_WIDTH:, :])
    h = _rms(x1, gx_ref[...]).astype(BF16)
    q = (_dot(h, wq_ref[...]) * (X_HEAD_DIM ** -0.5)).astype(BF16)
    outs = []
    for hh in range(X_HEADS):
        cols = slice(hh * X_HEAD_DIM, (hh + 1) * X_HEAD_DIM)
        kh = kv_ref[0, :, cols]
        vh = kv_ref[0, :, X_WIDTH + hh * X_HEAD_DIM:X_WIDTH + (hh + 1) * X_HEAD_DIM]
        s = _dot_t(q[:, cols], kh)
        p = jnp.exp(s - jnp.max(s, axis=-1, keepdims=True))
        den = jnp.sum(p, axis=-1, keepdims=True)
        outs.append((_dot(p.astype(BF16), vh) / den).astype(BF16))
    o_ref[0] = x1 + _dot(jnp.concatenate(outs, axis=1), wxo_ref[...])


def _mix_xattn(x, a, y, ga, gy, wo, gx, wq, kv, wxo):
    b, s, d = x.shape
    tq = 512
    m = kv.shape[1]
    full = lambda arr: pl.BlockSpec(arr.shape, lambda bi, i: (0,) * arr.ndim)
    tok = lambda w: pl.BlockSpec((1, tq, w), lambda bi, i: (bi, i, 0))
    return pl.pallas_call(
        _mix_xattn_kernel,
        grid=(b, s // tq),
        in_specs=[tok(d), tok(ATTN_WIDTH), tok(HY_WIDTH), full(ga), full(gy), full(wo), full(gx), full(wq),
                  pl.BlockSpec((1, m, 2 * X_WIDTH), lambda bi, i: (bi, 0, 0)), full(wxo)],
        out_specs=tok(d),
        out_shape=jax.ShapeDtypeStruct((b, s, d), F32),
        compiler_params=_params("parallel", "arbitrary"),
        name="mix_xattn",
    )(x, a, y, ga, gy, wo, gx, wq, kv, wxo)


def _swiglu_step(h, wg, wu, wd):
    g = _dot(h, wg)
    u = _dot(h, wu)
    a = (g * (1.0 / (1.0 + jnp.exp(-g))) * u).astype(BF16)
    return _dot(a, wd)


def _ffn_kernel(x_ref, g_ref, wg_ref, wu_ref, wd_ref, o_ref, h_ref, acc_ref):
    f = pl.program_id(1)

    @pl.when(f == 0)
    def _():
        h_ref[...] = _rms(x_ref[...], g_ref[...]).astype(BF16)
        acc_ref[...] = x_ref[...]

    acc_ref[...] += _swiglu_step(h_ref[...], wg_ref[...], wu_ref[...], wd_ref[...])

    @pl.when(f == pl.num_programs(1) - 1)
    def _():
        o_ref[...] = acc_ref[...]


def _dense_ffn(x2d, g, wg, wu, wd):
    n, d = x2d.shape
    ff = wg.shape[1]
    tm = 512
    tf = ff // 2
    return pl.pallas_call(
        _ffn_kernel,
        grid=(n // tm, ff // tf),
        in_specs=[
            pl.BlockSpec((tm, d), lambda i, f: (i, 0)),
            pl.BlockSpec((1, d), lambda i, f: (0, 0)),
            pl.BlockSpec((d, tf), lambda i, f: (0, f)),
            pl.BlockSpec((d, tf), lambda i, f: (0, f)),
            pl.BlockSpec((tf, d), lambda i, f: (f, 0)),
        ],
        out_specs=pl.BlockSpec((tm, d), lambda i, f: (i, 0)),
        out_shape=jax.ShapeDtypeStruct((n, d), F32),
        scratch_shapes=[pltpu.VMEM((tm, d), BF16), pltpu.VMEM((tm, d), F32)],
        compiler_params=_params("parallel", "arbitrary"),
        name="dense_ffn",
    )(x2d, g, wg, wu, wd)


def _router_kernel(x_ref, g_ref, wr_ref, h_ref, r_ref):
    hf = _rms(x_ref[...], g_ref[...])
    h_ref[...] = hf.astype(BF16)
    logits = jnp.dot(hf, wr_ref[...], precision=lax.Precision.HIGHEST, preferred_element_type=F32)
    lane = lax.broadcasted_iota(jnp.int32, logits.shape, 1)
    logits = jnp.where(lane < N_EXPERTS, logits, NEG)
    lanef = lane.astype(F32)
    big = float(LANES)
    m1 = jnp.max(logits, axis=-1, keepdims=True)
    i1 = jnp.min(jnp.where(logits == m1, lanef, big), axis=-1, keepdims=True)
    rest = jnp.where(lanef == i1, NEG, logits)
    m2 = jnp.max(rest, axis=-1, keepdims=True)
    i2 = jnp.min(jnp.where(rest == m2, lanef, big), axis=-1, keepdims=True)
    e2 = jnp.exp(m2 - m1)
    w1 = 1.0 / (1.0 + e2)
    w2 = e2 / (1.0 + e2)
    r_ref[...] = jnp.where(lane == 0, i1,
                           jnp.where(lane == 1, i2, jnp.where(lane == 2, w1, jnp.where(lane == 3, w2, 0.0))))


def _router(x2d, g, wr_pad):
    n, d = x2d.shape
    tm = 512
    return pl.pallas_call(
        _router_kernel,
        grid=(n // tm,),
        in_specs=[
            pl.BlockSpec((tm, d), lambda i: (i, 0)),
            pl.BlockSpec((1, d), lambda i: (0, 0)),
            pl.BlockSpec(wr_pad.shape, lambda i: (0, 0)),
        ],
        out_specs=[pl.BlockSpec((tm, d), lambda i: (i, 0)), pl.BlockSpec((tm, LANES), lambda i: (i, 0))],
        out_shape=[jax.ShapeDtypeStruct((n, d), BF16), jax.ShapeDtypeStruct((n, LANES), F32)],
        compiler_params=_params("parallel"),
        name="router",
    )(x2d, g, wr_pad)


def _moe_kernel(be_ref, x_ref, wg_ref, wu_ref, wd_ref, o_ref, acc_ref):
    f = pl.program_id(1)

    @pl.when(f == 0)
    def _():
        acc_ref[...] = jnp.zeros_like(acc_ref)

    acc_ref[...] += _swiglu_step(x_ref[...], wg_ref[0], wu_ref[0], wd_ref[0])

    @pl.when(f == pl.num_programs(1) - 1)
    def _():
        o_ref[...] = acc_ref[...].astype(BF16)


def _moe_experts(blk_e, xb, wg, wu, wd):
    rows, d = xb.shape
    ff = wg.shape[2]
    tf = ff // 4
    n_blk = rows // MOE_ROWS
    grid_spec = pltpu.PrefetchScalarGridSpec(
        num_scalar_prefetch=1,
        grid=(n_blk, ff // tf),
        in_specs=[
            pl.BlockSpec((MOE_ROWS, d), lambda i, f, be: (i, 0)),
            pl.BlockSpec((1, d, tf), lambda i, f, be: (be[i], 0, f)),
            pl.BlockSpec((1, d, tf), lambda i, f, be: (be[i], 0, f)),
            pl.BlockSpec((1, tf, d), lambda i, f, be: (be[i], f, 0)),
        ],
        out_specs=pl.BlockSpec((MOE_ROWS, d), lambda i, f, be: (i, 0)),
        scratch_shapes=[pltpu.VMEM((MOE_ROWS, d), F32)],
    )
    return pl.pallas_call(
        _moe_kernel,
        grid_spec=grid_spec,
        out_shape=jax.ShapeDtypeStruct((rows, d), BF16),
        compiler_params=_params("parallel", "arbitrary"),
        name="moe_experts",
    )(blk_e, xb, wg, wu, wd)


def _combine_kernel(*refs, normed):
    if normed:
        x_ref, y1_ref, y2_ref, r_ref, g_ref, o_ref = refs
    else:
        x_ref, y1_ref, y2_ref, r_ref, o_ref = refs
    w1 = r_ref[:, 2:3]
    w2 = r_ref[:, 3:4]
    x = x_ref[...] + y1_ref[...].astype(F32) * w1 + y2_ref[...].astype(F32) * w2
    o_ref[...] = _rms(x, g_ref[...]) if normed else x


def _combine(x2d, y1, y2, route, gain):
    n, d = x2d.shape
    tm = 512
    normed = gain is not None
    tok = lambda w: pl.BlockSpec((tm, w), lambda i: (i, 0))
    ins = [tok(d), tok(d), tok(d), tok(LANES)]
    args = [x2d, y1, y2, route]
    if normed:
        ins.append(pl.BlockSpec((1, d), lambda i: (0, 0)))
        args.append(gain)
    return pl.pallas_call(
        functools.partial(_combine_kernel, normed=normed),
        grid=(n // tm,),
        in_specs=ins,
        out_specs=tok(d),
        out_shape=jax.ShapeDtypeStruct((n, d), F32),
        compiler_params=_params("parallel"),
        name="moe_combine",
    )(*args)


def _norm_kernel(x_ref, g_ref, o_ref):
    o_ref[...] = _rms(x_ref[...], g_ref[...])


def _final_norm(x2d, g):
    n, d = x2d.shape
    tm = 512
    return pl.pallas_call(
        _norm_kernel,
        grid=(n // tm,),
        in_specs=[pl.BlockSpec((tm, d), lambda i: (i, 0)), pl.BlockSpec((1, d), lambda i: (0, 0))],
        out_specs=pl.BlockSpec((tm, d), lambda i: (i, 0)),
        out_shape=jax.ShapeDtypeStruct((n, d), F32),
        compiler_params=_params("parallel"),
        name="final_norm",
    )(x2d, g)


def _hyena(u, conv_w, conv_b, filt, skip, tables):
    g_fwd, g_inv, fb, fbi = tables
    b, L, c3 = u.shape
    c = c3 // (HY_ORDER + 1)
    nh = FFT_N2 // 2
    uc = _shortconv(u, conv_w, conv_b.reshape(1, c3))
    uc4 = uc.reshape(b, nh, FFT_N1, c3)
    uc5 = uc.reshape(2, b // 2, nh, FFT_N1, c3)
    spec = _filter_spectrum(
        _fft_small_fwd(filt.reshape(2 * HY_ORDER, nh, FFT_N1, c), g_fwd, paired=False, c_off=0, n_c=c), fb)
    t = _fft_small_fwd(uc4, g_fwd, paired=True, c_off=2 * c, n_c=c)
    t = _fft_big(t, spec[0], fb, fbi)
    zz5 = _fft_small_inv(t, g_inv, uc5, 2 * c, uc5, 0, skip[0:1])
    t = _fft_small_fwd(zz5.reshape(b, nh, FFT_N1, c), g_fwd, paired=True, c_off=0, n_c=c)
    t = _fft_big(t, spec[1], fb, fbi)
    y5 = _fft_small_inv(t, g_inv, zz5, 0, uc5, c, skip[1:2])
    return y5.reshape(b, L, c)


def _moe_dispatch(route, n_tok):
    n_asg = n_tok * TOP_K
    flat_e = route[:, :TOP_K].astype(jnp.int32).reshape(-1)
    onehot = (flat_e[:, None] == jnp.arange(N_EXPERTS, dtype=jnp.int32)[None, :]).astype(jnp.int32)
    csum = jnp.cumsum(onehot, axis=0)
    counts = csum[-1]
    rank = jnp.sum((csum - 1) * onehot, axis=1)
    padded = (counts + MOE_ROWS - 1) // MOE_ROWS * MOE_ROWS
    pad_end = jnp.cumsum(padded)
    pad_start = pad_end - padded
    dest = pad_start[flat_e] + rank
    seg_start = jnp.cumsum(counts) - counts
    n_blk = -(-(n_asg + N_EXPERTS * (MOE_ROWS - 1)) // MOE_ROWS)
    blk_start = jnp.arange(n_blk, dtype=jnp.int32) * MOE_ROWS
    blk_e = jnp.minimum(jnp.sum((blk_start[:, None] >= pad_end[None, :]).astype(jnp.int32), axis=1), N_EXPERTS - 1)
    order = jnp.argsort(flat_e, stable=True).astype(jnp.int32)
    e_row = jnp.repeat(blk_e, MOE_ROWS)
    r = jnp.arange(n_blk * MOE_ROWS, dtype=jnp.int32) - pad_start[e_row]
    src = order[jnp.clip(seg_start[e_row] + r, 0, n_asg - 1)]
    buf_tok = jnp.where(r < counts[e_row], src // TOP_K, 0)
    return buf_tok, blk_e.astype(jnp.int32), dest.reshape(n_tok, TOP_K)


def kernel(x, mem, mem_norm, mix_norm, w_in, attn_sink, hy_conv_w, hy_conv_b, hy_f_w1, hy_f_b1, hy_f_freq1,
           hy_f_w2, hy_f_b2, hy_f_freq2, hy_f_w3, hy_skip, attn_out_norm, hy_out_norm, w_out, xattn_norm,
           xw_q, xw_k, xw_v, xw_o, ffn_norm, ffn_w_gate, ffn_w_up, ffn_w_down,
           moe_router, moe_w_gate, moe_w_up, moe_w_down, final_norm):
    b, s, d = x.shape
    depth = w_in.shape[0]
    n_tok = b * s
    assert s == FFT_N // 2 and b % 2 == 0
    row = lambda v: v.reshape(1, -1).astype(F32)

    pos = jnp.arange(s, dtype=F32)
    inv = ROPE_THETA ** (-jnp.arange(0, HEAD_DIM, 2, dtype=F32) / HEAD_DIM)
    ang = pos[:, None] * inv[None, :]
    cosf = jnp.tile(jnp.cos(ang), (1, LANES // (HEAD_DIM // 2)))
    sins = jnp.tile(jnp.concatenate([-jnp.sin(ang), jnp.sin(ang)], axis=1), (1, LANES // HEAD_DIM))
    t_pos = jnp.linspace(0.0, 1.0, s, dtype=F32)[:, None]
    wv = 2.0 * math.pi * jnp.arange(s, dtype=F32)[:, None] / s
    fr = jnp.linspace(1e-4, HY_BANDS - 1, HY_BANDS, dtype=F32)[None, :]
    z_pos = jnp.concatenate([t_pos, jnp.cos(fr * wv), -jnp.sin(fr * wv)], axis=-1)
    emb_pad = 64
    z_pos = jnp.pad(z_pos, ((0, 0), (0, emb_pad - HY_EMB)))
    absd = jnp.abs(jnp.linspace(HY_MIN_DECAY, HY_MAX_DECAY, HY_ORDER * HY_WIDTH, dtype=F32)).reshape(1, -1)
    tables = _fft_tables()

    mem2d = mem.reshape(-1, d)
    x2d = x.reshape(n_tok, d)
    out = None
    for l in range(depth):
        q, kv, u = _inproj(x2d, row(mix_norm[l]), w_in[l].astype(BF16), cosf, sins, s)
        kv_pad = jnp.pad(kv.reshape(b, s, 2 * KV_WIDTH), ((0, 0), (WINDOW, WINDOW), (0, 0)))
        a = _band_attention(q.reshape(b, s, ATTN_WIDTH), kv_pad, attn_sink[l].astype(F32), s)
        filt = _hyena_filters(
            z_pos, t_pos, jnp.pad(hy_f_w1[l], ((0, emb_pad - HY_EMB), (0, 0))), row(hy_f_b1[l]),
            row(hy_f_freq1[l]), hy_f_w2[l], row(hy_f_b2[l]), row(hy_f_freq2[l]), hy_f_w3[l], absd)
        y = _hyena(u.reshape(b, s, -1), hy_conv_w[l], hy_conv_b[l], filt, hy_skip[l], tables)
        mkv = _norm_mm(mem2d, row(mem_norm), jnp.concatenate([xw_k[l], xw_v[l]], axis=1).astype(BF16))
        x3 = _mix_xattn(x2d.reshape(b, s, d), a, y, row(attn_out_norm[l]), row(hy_out_norm[l]),
                        w_out[l].astype(BF16), row(xattn_norm[l]), xw_q[l].astype(BF16),
                        mkv.reshape(b, -1, 2 * X_WIDTH), xw_o[l].astype(BF16))
        x2d = x3.reshape(n_tok, d)
        j = l // 2
        last = l == depth - 1
        if l % 2 == 0:
            x2d = _dense_ffn(x2d, row(ffn_norm[l]), ffn_w_gate[j].astype(BF16), ffn_w_up[j].astype(BF16),
                             ffn_w_down[j].astype(BF16))
            if last:
                out = _final_norm(x2d, row(final_norm))
        else:
            wr = jnp.pad(moe_router[j].astype(F32), ((0, 0), (0, LANES - N_EXPERTS)))
            hb, route = _router(x2d, row(ffn_norm[l]), wr)
            buf_tok, blk_e, dest = _moe_dispatch(route, n_tok)
            yb = _moe_experts(blk_e, hb[buf_tok], moe_w_gate[j].astype(BF16), moe_w_up[j].astype(BF16),
                              moe_w_down[j].astype(BF16))
            res = _combine(x2d, yb[dest[:, 0]], yb[dest[:, 1]], route, row(final_norm) if last else None)
            if last:
                out = res
            else:
                x2d = res
    return out.reshape(b, s, d)
```

```python
import functools
import math

import jax
import jax.numpy as jnp
from jax import lax
from jax.experimental import pallas as pl
from jax.experimental.pallas import tpu as pltpu

F32 = jnp.float32
BF16 = jnp.bfloat16

EPS = 1e-6
N_HEADS = 8
N_KV_HEADS = 2
HEAD_DIM = 64
ATTN_WIDTH = N_HEADS * HEAD_DIM
KV_WIDTH = N_KV_HEADS * HEAD_DIM
WINDOW = 128
ROPE_THETA = 10000.0
HY_WIDTH = 512
HY_ORDER = 2
HY_EMB = 33
HY_BANDS = (HY_EMB - 1) // 2
HY_FILTER_HIDDEN = 64
HY_TARGET = 1e-2
HY_FAST_DECAY = 0.3
HY_SLOW_DECAY = 1.5
HY_MIN_DECAY = math.log(HY_TARGET) / HY_SLOW_DECAY
HY_MAX_DECAY = math.log(HY_TARGET) / HY_FAST_DECAY
Q_END = ATTN_WIDTH
K_END = Q_END + KV_WIDTH
V_END = K_END + KV_WIDTH
X_HEADS = 4
X_HEAD_DIM = 128
X_WIDTH = X_HEADS * X_HEAD_DIM
N_EXPERTS = 8
TOP_K = 2

LANES = 128
SUBLANES = 8
MXU_DIM = 256
VMEM_LIMIT = 56 * 1024 * 1024

FFT_N1 = 256
FFT_N2 = 32
FFT_N = FFT_N1 * FFT_N2
FFT_J = SUBLANES
FFT_GROUPS = FFT_N1 // FFT_J
FFT_CHUNK = 64
FFT_SUB = FFT_CHUNK // FFT_J

MOE_ROWS = 1024
NEG = float(jnp.finfo(jnp.float32).min)
MASKED = -1e30
LOG2E = math.log2(math.e)


def _params(*sem):
    return pltpu.CompilerParams(dimension_semantics=sem, vmem_limit_bytes=VMEM_LIMIT)


def _dot(a, b):
    return jnp.dot(a, b, preferred_element_type=F32)


def _dot_t(a, b):
    return lax.dot_general(a, b, (((1,), (1,)), ((), ())), preferred_element_type=F32)


def _rms(xf, g):
    ms = jnp.mean(xf * xf, axis=-1, keepdims=True)
    return xf * lax.rsqrt(ms + EPS) * g


def _inproj_kernel(x_ref, g_ref, w_ref, cos_ref, sin_ref, q_ref, kx_ref, vx_ref, u_ref):
    h = _rms(x_ref[...], g_ref[...]).astype(BF16)
    cosf = cos_ref[...]
    sins = sin_ref[...]
    lane = lax.broadcasted_iota(jnp.int32, cosf.shape, 1)
    low = (lane % HEAD_DIM) < (HEAD_DIM // 2)

    def rope(c):
        rot = jnp.where(low, pltpu.roll(c, LANES - HEAD_DIM // 2, 1), pltpu.roll(c, HEAD_DIM // 2, 1))
        return c * cosf + rot * sins

    scale = HEAD_DIM ** -0.5 * LOG2E
    for j in range(ATTN_WIDTH // MXU_DIM):
        qc = _dot(h, w_ref[:, j * MXU_DIM:(j + 1) * MXU_DIM])
        for t in range(MXU_DIM // LANES):
            c0 = j * MXU_DIM + t * LANES
            q_ref[:, c0:c0 + LANES] = (rope(qc[:, t * LANES:(t + 1) * LANES]) * scale).astype(BF16)
    kvc = _dot(h, w_ref[:, Q_END:V_END])
    kt = rope(kvc[:, :KV_WIDTH]).T
    trow = lax.broadcasted_iota(jnp.int32, kt.shape, 0)
    top = jnp.where(trow < HEAD_DIM, kt, 0.0)
    bot = jnp.where(trow >= HEAD_DIM, kt, 0.0)
    kx_ref[0, 0] = top.astype(BF16)
    kx_ref[0, 1] = pltpu.roll(top, HEAD_DIM, 0).astype(BF16)
    kx_ref[0, 2] = pltpu.roll(bot, HEAD_DIM, 0).astype(BF16)
    kx_ref[0, 3] = bot.astype(BF16)
    vc = kvc[:, KV_WIDTH:]
    vsw = pltpu.roll(vc, HEAD_DIM, 1)
    lo = lane < HEAD_DIM
    vx_ref[:, 0 * LANES:1 * LANES] = jnp.where(lo, vc, 1.0).astype(BF16)
    vx_ref[:, 1 * LANES:2 * LANES] = jnp.where(lo, 1.0, vsw).astype(BF16)
    vx_ref[:, 2 * LANES:3 * LANES] = jnp.where(lo, vsw, 1.0).astype(BF16)
    vx_ref[:, 3 * LANES:4 * LANES] = jnp.where(lo, 1.0, vc).astype(BF16)
    n_u = u_ref.shape[1]
    for j in range(n_u // 512):
        u_ref[:, j * 512:(j + 1) * 512] = _dot(h, w_ref[:, V_END + j * 512:V_END + (j + 1) * 512]).astype(BF16)


def _inproj(x2d, g, w, cosf, sins, seq):
    n, d = x2d.shape
    tm = 512
    n_u = w.shape[1] - V_END
    spb = seq // tm
    return pl.pallas_call(
        _inproj_kernel,
        grid=(n // tm,),
        in_specs=[
            pl.BlockSpec((tm, d), lambda i: (i, 0)),
            pl.BlockSpec((1, d), lambda i: (0, 0)),
            pl.BlockSpec(w.shape, lambda i: (0, 0)),
            pl.BlockSpec((tm, LANES), lambda i: (i % spb, 0)),
            pl.BlockSpec((tm, LANES), lambda i: (i % spb, 0)),
        ],
        out_specs=[
            pl.BlockSpec((tm, ATTN_WIDTH), lambda i: (i, 0)),
            pl.BlockSpec((1, 2 * N_KV_HEADS, LANES, tm), lambda i: (i // spb, 0, 0, i % spb)),
            pl.BlockSpec((tm, 2 * N_KV_HEADS * LANES), lambda i: (i, 0)),
            pl.BlockSpec((tm, n_u), lambda i: (i, 0)),
        ],
        out_shape=[
            jax.ShapeDtypeStruct((n, ATTN_WIDTH), BF16),
            jax.ShapeDtypeStruct((n // seq, 2 * N_KV_HEADS, LANES, seq), BF16),
            jax.ShapeDtypeStruct((n, 2 * N_KV_HEADS * LANES), BF16),
            jax.ShapeDtypeStruct((n, n_u), BF16),
        ],
        compiler_params=_params("parallel"),
        name="inproj",
    )(x2d, g, w, cosf, sins)


def _battn_kernel(sink_ref, q_ref, kx_ref, vx_ref, o_ref, *, seq, tq):
    i = pl.program_id(1)
    blk = WINDOW
    n_blk = seq // blk
    row = lax.broadcasted_iota(jnp.int32, (blk, blk), 0)
    col = lax.broadcasted_iota(jnp.int32, (blk, blk), 1)
    tri_prev = jnp.where(col >= row, 0.0, MASKED)
    tri_next = jnp.where(col <= row, 0.0, MASKED)
    lo = col < HEAD_DIM
    heads_per_pair = LANES // HEAD_DIM
    group_pairs = N_HEADS // N_KV_HEADS // heads_per_pair
    units = [(jb, hp) for jb in range(tq // blk) for hp in range(N_HEADS // heads_per_pair)]

    def window(jb):
        bi = i * (tq // blk) + jb
        starts = [jnp.maximum(bi - 1, 0), bi, jnp.minimum(bi + 1, n_blk - 1)]
        return bi, [pl.multiple_of(s * blk, blk) for s in starts]

    def scores(jb, hp):
        _, starts = window(jb)
        qp = q_ref[0, jb * blk:(jb + 1) * blk, hp * LANES:(hp + 1) * LANES]
        out = []
        for t in range(heads_per_pair):
            var = heads_per_pair * (hp // group_pairs) + t
            kwin = jnp.concatenate([kx_ref[0, var, :, pl.ds(s, blk)] for s in starts], axis=1)
            out.append(_dot(qp, kwin))
        return out

    def shifted(jb, hp, s_pair):
        bi, _ = window(jb)
        b_prev = tri_prev + jnp.where(bi == 0, MASKED, 0.0)
        b_next = tri_next + jnp.where(bi == n_blk - 1, MASKED, 0.0)
        out = []
        for t in range(heads_per_pair):
            s = s_pair[t]
            s0 = s[:, :blk] + b_prev
            s1 = s[:, blk:2 * blk]
            s2 = s[:, 2 * blk:] + b_next
            sk = sink_ref[hp * heads_per_pair + t] * LOG2E
            m = jnp.maximum(jnp.max(jnp.maximum(jnp.maximum(s0, s1), s2), axis=-1, keepdims=True), sk)
            x = jnp.concatenate([s0 - m, s1 - m, s2 - m], axis=1).astype(BF16)
            out.append((x, sk - m))
        return out

    def probs(x_pair):
        return [(jnp.exp2(x), jnp.exp2(d)) for x, d in x_pair]

    def finish(jb, hp, p_pair):
        _, starts = window(jb)
        res = []
        for t in range(heads_per_pair):
            var = heads_per_pair * (hp // group_pairs) + t
            vwin = jnp.concatenate([vx_ref[0, pl.ds(s, blk), var * LANES:(var + 1) * LANES] for s in starts], axis=0)
            res.append(_dot(p_pair[t][0], vwin))
        num = jnp.where(lo, res[0], res[1])
        den = pltpu.roll(jnp.where(lo, res[1], res[0]), HEAD_DIM, 1) + jnp.where(lo, p_pair[0][1], p_pair[1][1])
        o_ref[0, jb * blk:(jb + 1) * blk, hp * LANES:(hp + 1) * LANES] = (num * (1.0 / den)).astype(BF16)

    n_u = len(units)
    st_s, st_x, st_p = {}, {}, {}
    for n in range(n_u + 3):
        if 0 <= n - 3 < n_u:
            finish(*units[n - 3], st_p.pop(n - 3))
        if 0 <= n - 2 < n_u:
            st_p[n - 2] = probs(st_x.pop(n - 2))
        if 0 <= n - 1 < n_u:
            st_x[n - 1] = shifted(*units[n - 1], st_s.pop(n - 1))
        if n < n_u:
            st_s[n] = scores(*units[n])


def _band_attention(q, kx, vx, sink, seq):
    b = q.shape[0]
    tq = 512
    return pl.pallas_call(
        functools.partial(_battn_kernel, seq=seq, tq=tq),
        grid=(b, seq // tq),
        in_specs=[
            pl.BlockSpec(memory_space=pltpu.SMEM),
            pl.BlockSpec((1, tq, ATTN_WIDTH), lambda bi, i: (bi, i, 0)),
            pl.BlockSpec((1,) + kx.shape[1:], lambda bi, i: (bi, 0, 0, 0)),
            pl.BlockSpec((1,) + vx.shape[1:], lambda bi, i: (bi, 0, 0)),
        ],
        out_specs=pl.BlockSpec((1, tq, ATTN_WIDTH), lambda bi, i: (bi, i, 0)),
        out_shape=jax.ShapeDtypeStruct((b, seq, ATTN_WIDTH), BF16),
        compiler_params=_params("parallel", "arbitrary"),
        name="band_attention",
    )(sink, q, kx, vx)


def _filter_kernel(z_ref, t_ref, w1_ref, b1_ref, f1_ref, w2_ref, b2_ref, f2_ref, w3_ref, ad_ref, o_ref, *, tl):
    hp = lax.Precision.HIGHEST
    h = jnp.sin(f1_ref[...] * (jnp.dot(z_ref[...], w1_ref[...], precision=hp, preferred_element_type=F32)
                               + b1_ref[...]))
    h = jnp.sin(f2_ref[...] * (jnp.dot(h, w2_ref[...], precision=hp, preferred_element_type=F32) + b2_ref[...]))
    t = t_ref[...]
    rowid = pl.program_id(0) * tl + lax.broadcasted_iota(jnp.int32, (tl, HY_WIDTH), 0)
    for d in range(2):
        for o in range(HY_ORDER):
            c0 = (d * HY_ORDER + o) * HY_WIDTH
            v = jnp.dot(h, w3_ref[:, c0:c0 + HY_WIDTH], precision=hp, preferred_element_type=F32)
            v = v * jnp.exp(-t * ad_ref[:, o * HY_WIDTH:(o + 1) * HY_WIDTH])
            if d == 1:
                v = jnp.where(rowid == 0, 0.0, v)
            v = v.reshape(tl // FFT_N1, FFT_GROUPS, FFT_J, HY_WIDTH)
            o_ref[d * HY_ORDER + o] = jnp.concatenate([v, jnp.zeros_like(v)], axis=2).astype(BF16)


def _hyena_filters(z_pos, t_pos, w1, b1, f1, w2, b2, f2, w3, absd):
    L, e = z_pos.shape
    tl = 512
    full = lambda a: pl.BlockSpec(a.shape, lambda i: (0,) * a.ndim)
    return pl.pallas_call(
        functools.partial(_filter_kernel, tl=tl),
        grid=(L // tl,),
        in_specs=[
            pl.BlockSpec((tl, e), lambda i: (i, 0)),
            pl.BlockSpec((tl, 1), lambda i: (i, 0)),
            full(w1), full(b1), full(f1), full(w2), full(b2), full(f2), full(w3), full(absd),
        ],
        out_specs=pl.BlockSpec((2 * HY_ORDER, tl // FFT_N1, FFT_GROUPS, 2 * FFT_J, HY_WIDTH),
                               lambda i: (0, i, 0, 0, 0)),
        out_shape=jax.ShapeDtypeStruct((2 * HY_ORDER, L // FFT_N1, FFT_GROUPS, 2 * FFT_J, HY_WIDTH), BF16),
        compiler_params=_params("parallel"),
        name="hyena_filters",
    )(z_pos, t_pos, w1, b1, f1, w2, b2, f2, w3, absd)


def _shortconv_kernel(ua_ref, ub_ref, w_ref, b_ref, o_ref):
    def conv(u_ref):
        u = u_ref[0].astype(F32)
        L, ct = u.shape
        row = lax.broadcasted_iota(jnp.int32, u.shape, 0)
        prev = jnp.where(row == 0, 0.0, pltpu.roll(u, 1, 0))
        nxt = jnp.where(row == L - 1, 0.0, pltpu.roll(u, L - 1, 0))
        r = prev * w_ref[0:1, :] + u * w_ref[1:2, :] + nxt * w_ref[2:3, :] + b_ref[...]
        return r.reshape(L // FFT_N1, FFT_GROUPS, FFT_J, ct)

    o_ref[0] = jnp.concatenate([conv(ua_ref), conv(ub_ref)], axis=2).astype(BF16)


def _shortconv(u, w, bias):
    b, L, c3 = u.shape
    ct = MXU_DIM
    ncb = b // 2
    return pl.pallas_call(
        _shortconv_kernel,
        grid=(ncb, c3 // ct),
        in_specs=[
            pl.BlockSpec((1, L, ct), lambda bi, ci: (bi, 0, ci)),
            pl.BlockSpec((1, L, ct), lambda bi, ci: (bi + ncb, 0, ci)),
            pl.BlockSpec((3, ct), lambda bi, ci: (0, ci)),
            pl.BlockSpec((1, ct), lambda bi, ci: (0, ci)),
        ],
        out_specs=pl.BlockSpec((1, L // FFT_N1, FFT_GROUPS, 2 * FFT_J, ct), lambda bi, ci: (bi, 0, 0, 0, ci)),
        out_shape=jax.ShapeDtypeStruct((ncb, L // FFT_N1, FFT_GROUPS, 2 * FFT_J, c3), BF16),
        compiler_params=_params("parallel", "parallel"),
        name="shortconv",
    )(u, u, w, bias)


def _cis(m, n):
    ang = (2.0 * math.pi / n) * (m % n).astype(F32)
    return jnp.cos(ang), jnp.sin(ang)


def _complex_blocks(re, im):
    return jnp.stack([jnp.stack([re, -im]), jnp.stack([im, re])])


def _fft_tables():
    ar = lambda n: jnp.arange(n, dtype=jnp.int32)
    nh = FFT_N2 // 2
    g = ar(FFT_GROUPS)[:, None, None, None]
    k2 = ar(FFT_N2)[None, :, None, None]
    j = ar(FFT_J)[None, None, :, None]
    n2 = ar(nh)[None, None, None, :]
    m = (FFT_N1 * n2 * k2 + (FFT_J * g + j) * k2) % FFT_N
    c, s = _cis(m, FFT_N)
    eye = jnp.eye(FFT_J, dtype=F32)
    fwd = _complex_blocks(c, -s)
    g_fwd = jnp.einsum("opgkjn,jJ->gkojnpJ", fwd, eye).reshape(FFT_GROUPS, FFT_N2 * 2 * FFT_J, 2 * nh * FFT_J)
    inv = _complex_blocks(c, s) * (1.0 / FFT_N)
    g_inv = jnp.einsum("opgkjn,jJ->gnojkpJ", inv, eye).reshape(FFT_GROUPS, 2 * nh * FFT_J, FFT_N2 * 2 * FFT_J)
    mk = (ar(FFT_N1)[:, None] * ar(FFT_N1)[None, :]) % FFT_N1
    c1, s1 = _cis(mk, FFT_N1)

    def grouped(blocks):
        b6 = blocks.reshape(2, 2, FFT_GROUPS, FFT_J, FFT_GROUPS, FFT_J)
        return jnp.transpose(b6, (2, 0, 3, 4, 1, 5)).reshape(2 * FFT_N1, 2 * FFT_N1)

    fb = grouped(_complex_blocks(c1, -s1))
    fbi = grouped(_complex_blocks(c1, s1))
    return g_fwd.astype(BF16), g_inv.astype(BF16), fb.astype(BF16), fbi.astype(BF16)


def _small_fwd_rows(g_ref, s, tile):
    r = _dot(g_ref[s], tile)
    return r.reshape(FFT_N2, 2 * FFT_J, tile.shape[-1]).astype(BF16)


def _fs_kernel(z_ref, g_ref, o_ref):
    ct = o_ref.shape[-1]
    for s in range(FFT_SUB):
        tile = z_ref[0, :, s, :, :].reshape(FFT_N2 // 2 * 2 * FFT_J, ct)
        o_ref[0, :, s, :, :] = _small_fwd_rows(g_ref, s, tile)


def _fft_small_fwd(zil, g_fwd, *, c_off, n_c):
    ncb, nh, ng, _, _ = zil.shape
    ct = MXU_DIM
    cblk = c_off // ct
    return pl.pallas_call(
        _fs_kernel,
        grid=(ncb, n_c // ct, ng // FFT_SUB),
        in_specs=[
            pl.BlockSpec((1, nh, FFT_SUB, 2 * FFT_J, ct), lambda b, ci, q: (b, 0, q, 0, cblk + ci)),
            pl.BlockSpec((FFT_SUB,) + g_fwd.shape[1:], lambda b, ci, q: (q, 0, 0)),
        ],
        out_specs=pl.BlockSpec((1, FFT_N2, FFT_SUB, 2 * FFT_J, ct), lambda b, ci, q: (b, 0, q, 0, ci)),
        out_shape=jax.ShapeDtypeStruct((ncb, FFT_N2, ng, 2 * FFT_J, n_c), BF16),
        compiler_params=_params("parallel", "parallel", "arbitrary"),
        name="fft_small_fwd",
    )(zil, g_fwd)


def _big_kernel(x_ref, h_ref, fb_ref, fbi_ref, o_ref):
    ct = o_ref.shape[-1]
    h4 = h_ref[0].reshape(FFT_GROUPS, 2, FFT_J, ct)
    hre = h4[:, 0]
    him = h4[:, 1]
    for b in range(x_ref.shape[0]):
        xf = _dot(fb_ref[...], x_ref[b, 0].reshape(2 * FFT_N1, ct)).reshape(FFT_GROUPS, 2, FFT_J, ct)
        xr = xf[:, 0]
        xi = xf[:, 1]
        y = jnp.stack([xr * hre - xi * him, xr * him + xi * hre], axis=1).reshape(2 * FFT_N1, ct).astype(BF16)
        o_ref[b, 0] = _dot(fbi_ref[...], y).reshape(FFT_GROUPS, 2 * FFT_J, ct).astype(BF16)


def _fft_big(xs, hspec, fb, fbi):
    ncb, n2, ng, _, c = xs.shape
    ct = MXU_DIM
    nbb = 4 if ncb % 4 == 0 else 1
    return pl.pallas_call(
        _big_kernel,
        grid=(c // ct, n2, ncb // nbb),
        in_specs=[
            pl.BlockSpec((nbb, 1, ng, 2 * FFT_J, ct), lambda ci, k, b: (b, k, 0, 0, ci)),
            pl.BlockSpec((1, ng, 2 * FFT_J, ct), lambda ci, k, b: (k, 0, 0, ci)),
            pl.BlockSpec(fb.shape, lambda ci, k, b: (0, 0)),
            pl.BlockSpec(fbi.shape, lambda ci, k, b: (0, 0)),
        ],
        out_specs=pl.BlockSpec((nbb, 1, ng, 2 * FFT_J, ct), lambda ci, k, b: (b, k, 0, 0, ci)),
        out_shape=jax.ShapeDtypeStruct(xs.shape, BF16),
        compiler_params=_params("parallel", "parallel", "arbitrary"),
        name="fft_big",
    )(xs, hspec, fb, fbi)


def _spectrum_kernel(x_ref, fb_ref, o_ref):
    ct = o_ref.shape[-1]
    xf = [_dot(fb_ref[...], x_ref[i, 0].reshape(2 * FFT_N1, ct)).reshape(FFT_GROUPS, 2, FFT_J, ct)
          for i in range(2 * HY_ORDER)]
    for o in range(HY_ORDER):
        a = xf[o]
        r = xf[HY_ORDER + o]
        spec = jnp.stack([a[:, 0] + r[:, 0], a[:, 1] - r[:, 1]], axis=1)
        o_ref[o, 0] = spec.reshape(FFT_GROUPS, 2 * FFT_J, ct)


def _filter_spectrum(xs, fb):
    nf, n2, ng, _, c = xs.shape
    ct = MXU_DIM
    return pl.pallas_call(
        _spectrum_kernel,
        grid=(c // ct, n2),
        in_specs=[
            pl.BlockSpec((nf, 1, ng, 2 * FFT_J, ct), lambda ci, k: (0, k, 0, 0, ci)),
            pl.BlockSpec(fb.shape, lambda ci, k: (0, 0)),
        ],
        out_specs=pl.BlockSpec((HY_ORDER, 1, ng, 2 * FFT_J, ct), lambda ci, k: (0, k, 0, 0, ci)),
        out_shape=jax.ShapeDtypeStruct((HY_ORDER, n2, ng, 2 * FFT_J, c), F32),
        compiler_params=_params("parallel", "parallel"),
        name="filter_spectrum",
    )(xs, fb)


def _gated_inverse(c_ref, gi_ref, z_ref, gate_ref, skip, s):
    nh = FFT_N2 // 2
    ct = c_ref.shape[-1]
    rows = nh * 2 * FFT_J
    r = _dot(gi_ref[s], c_ref[0, :, s, :, :].reshape(FFT_N2 * 2 * FFT_J, ct))
    z = z_ref[0, :, s, :, :].astype(F32).reshape(rows, ct)
    gate = gate_ref[0, :, s, :, :].astype(F32).reshape(rows, ct)
    return gate * (r + z * skip)


def _is_fs_kernel(c_ref, gi_ref, z_ref, gate_ref, skip_ref, gf_ref, zz_ref, o_ref):
    nh = FFT_N2 // 2
    ct = o_ref.shape[-1]
    skip = skip_ref[...]
    for s in range(FFT_SUB):
        zz = _gated_inverse(c_ref, gi_ref, z_ref, gate_ref, skip, s).astype(BF16)
        zz_ref[0, :, s, :, :] = zz.reshape(nh, 2 * FFT_J, ct)
        o_ref[0, :, s, :, :] = _small_fwd_rows(gf_ref, s, zz)


def _is_last_kernel(c_ref, gi_ref, z_ref, gate_ref, skip_ref, y_ref):
    nh = FFT_N2 // 2
    ct = y_ref.shape[-1]
    skip = skip_ref[...]
    for s in range(FFT_SUB):
        y = _gated_inverse(c_ref, gi_ref, z_ref, gate_ref, skip, s).reshape(nh, 2, FFT_J, ct)
        for p in range(2):
            y_ref[p, 0, :, FFT_J * s:FFT_J * (s + 1), :] = y[:, p]


def _fft_small_inv(cs, g_inv, zil, z_off, gil, gate_off, skip, g_fwd=None):
    ncb, n2, ng, _, c = cs.shape
    nh = n2 // 2
    ct = MXU_DIM
    zb = z_off // ct
    gb = gate_off // ct
    til = lambda off: pl.BlockSpec((1, nh, FFT_SUB, 2 * FFT_J, ct), lambda b, ci, q: (b, 0, q, 0, off + ci))
    freq = pl.BlockSpec((1, n2, FFT_SUB, 2 * FFT_J, ct), lambda b, ci, q: (b, 0, q, 0, ci))
    mat = lambda m: pl.BlockSpec((FFT_SUB,) + m.shape[1:], lambda b, ci, q: (q, 0, 0))
    ins = [freq, mat(g_inv), til(zb), til(gb), pl.BlockSpec((1, ct), lambda b, ci, q: (0, ci))]
    args = [cs, g_inv, zil, gil, skip]
    if g_fwd is not None:
        body = _is_fs_kernel
        ins.append(mat(g_fwd))
        args.append(g_fwd)
        out_specs = [til(0), freq]
        out_shape = [jax.ShapeDtypeStruct((ncb, nh, ng, 2 * FFT_J, c), BF16), jax.ShapeDtypeStruct(cs.shape, BF16)]
    else:
        body = _is_last_kernel
        out_specs = pl.BlockSpec((2, 1, nh, FFT_CHUNK, ct), lambda b, ci, q: (0, b, 0, q, ci))
        out_shape = jax.ShapeDtypeStruct((2, ncb, nh, ng * FFT_J, c), F32)
    return pl.pallas_call(
        body,
        grid=(ncb, c // ct, ng // FFT_SUB),
        in_specs=ins,
        out_specs=out_specs,
        out_shape=out_shape,
        compiler_params=_params("parallel", "parallel", "arbitrary"),
        name="fft_small_inv",
    )(*args)


def _norm_mm_kernel(x_ref, g_ref, w_ref, o_ref):
    o_ref[...] = _dot(_rms(x_ref[...], g_ref[...]).astype(BF16), w_ref[...]).astype(o_ref.dtype)


def _norm_mm(x2d, g, w):
    n, d = x2d.shape
    tm = 512
    return pl.pallas_call(
        _norm_mm_kernel,
        grid=(n // tm,),
        in_specs=[
            pl.BlockSpec((tm, d), lambda i: (i, 0)),
            pl.BlockSpec((1, d), lambda i: (0, 0)),
            pl.BlockSpec(w.shape, lambda i: (0, 0)),
        ],
        out_specs=pl.BlockSpec((tm, w.shape[1]), lambda i: (i, 0)),
        out_shape=jax.ShapeDtypeStruct((n, w.shape[1]), BF16),
        compiler_params=_params("parallel"),
        name="memory_kv",
    )(x2d, g, w)


def _mix_xattn_kernel(x_ref, a_ref, y_ref, ga_ref, gy_ref, wo_ref, gx_ref, wq_ref, kv_ref, wxo_ref, o_ref):
    an = _rms(a_ref[0].astype(F32), ga_ref[...]).astype(BF16)
    yn = _rms(y_ref[0], gy_ref[...]).astype(BF16)
    x1 = x_ref[0] + _dot(an, wo_ref[:ATTN_WIDTH, :]) + _dot(yn, wo_ref[ATTN_WIDTH:, :])
    h = _rms(x1, gx_ref[...]).astype(BF16)
    q = (_dot(h, wq_ref[...]) * (X_HEAD_DIM ** -0.5)).astype(BF16)
    outs = []
    for hh in range(X_HEADS):
        cols = slice(hh * X_HEAD_DIM, (hh + 1) * X_HEAD_DIM)
        kh = kv_ref[0, :, cols]
        vh = kv_ref[0, :, X_WIDTH + hh * X_HEAD_DIM:X_WIDTH + (hh + 1) * X_HEAD_DIM]
        s = _dot_t(q[:, cols], kh)
        p = jnp.exp(s - jnp.max(s, axis=-1, keepdims=True))
        den = jnp.sum(p, axis=-1, keepdims=True)
        outs.append((_dot(p.astype(BF16), vh) / den).astype(BF16))
    o_ref[0] = x1 + _dot(jnp.concatenate(outs, axis=1), wxo_ref[...])


def _mix_xattn(x, a, y, ga, gy, wo, gx, wq, kv, wxo):
    b, s, d = x.shape
    tq = 512
    m = kv.shape[1]
    full = lambda arr: pl.BlockSpec(arr.shape, lambda bi, i: (0,) * arr.ndim)
    tok = lambda w: pl.BlockSpec((1, tq, w), lambda bi, i: (bi, i, 0))
    return pl.pallas_call(
        _mix_xattn_kernel,
        grid=(b, s // tq),
        in_specs=[tok(d), tok(ATTN_WIDTH), tok(HY_WIDTH), full(ga), full(gy), full(wo), full(gx), full(wq),
                  pl.BlockSpec((1, m, 2 * X_WIDTH), lambda bi, i: (bi, 0, 0)), full(wxo)],
        out_specs=tok(d),
        out_shape=jax.ShapeDtypeStruct((b, s, d), F32),
        compiler_params=_params("parallel", "arbitrary"),
        name="mix_xattn",
    )(x, a, y, ga, gy, wo, gx, wq, kv, wxo)


def _swiglu_step(h, wg, wu, wd):
    g = _dot(h, wg)
    u = _dot(h, wu)
    a = (g * (1.0 / (1.0 + jnp.exp(-g))) * u).astype(BF16)
    return _dot(a, wd)


def _ffn_kernel(x_ref, g_ref, wg_ref, wu_ref, wd_ref, o_ref, h_ref, acc_ref):
    f = pl.program_id(1)

    @pl.when(f == 0)
    def _():
        h_ref[...] = _rms(x_ref[...], g_ref[...]).astype(BF16)
        acc_ref[...] = x_ref[...]

    acc_ref[...] += _swiglu_step(h_ref[...], wg_ref[...], wu_ref[...], wd_ref[...])

    @pl.when(f == pl.num_programs(1) - 1)
    def _():
        o_ref[...] = acc_ref[...]


def _dense_ffn(x2d, g, wg, wu, wd):
    n, d = x2d.shape
    ff = wg.shape[1]
    tm = 512
    tf = ff // 2
    return pl.pallas_call(
        _ffn_kernel,
        grid=(n // tm, ff // tf),
        in_specs=[
            pl.BlockSpec((tm, d), lambda i, f: (i, 0)),
            pl.BlockSpec((1, d), lambda i, f: (0, 0)),
            pl.BlockSpec((d, tf), lambda i, f: (0, f)),
            pl.BlockSpec((d, tf), lambda i, f: (0, f)),
            pl.BlockSpec((tf, d), lambda i, f: (f, 0)),
        ],
        out_specs=pl.BlockSpec((tm, d), lambda i, f: (i, 0)),
        out_shape=jax.ShapeDtypeStruct((n, d), F32),
        scratch_shapes=[pltpu.VMEM((tm, d), BF16), pltpu.VMEM((tm, d), F32)],
        compiler_params=_params("parallel", "arbitrary"),
        name="dense_ffn",
    )(x2d, g, wg, wu, wd)


def _router_kernel(x_ref, g_ref, wr_ref, h_ref, r_ref):
    hf = _rms(x_ref[...], g_ref[...])
    h_ref[...] = hf.astype(BF16)
    logits = jnp.dot(hf, wr_ref[...], precision=lax.Precision.HIGHEST, preferred_element_type=F32)
    lane = lax.broadcasted_iota(jnp.int32, logits.shape, 1)
    logits = jnp.where(lane < N_EXPERTS, logits, NEG)
    lanef = lane.astype(F32)
    big = float(LANES)
    m1 = jnp.max(logits, axis=-1, keepdims=True)
    i1 = jnp.min(jnp.where(logits == m1, lanef, big), axis=-1, keepdims=True)
    rest = jnp.where(lanef == i1, NEG, logits)
    m2 = jnp.max(rest, axis=-1, keepdims=True)
    i2 = jnp.min(jnp.where(rest == m2, lanef, big), axis=-1, keepdims=True)
    e2 = jnp.exp(m2 - m1)
    w1 = 1.0 / (1.0 + e2)
    w2 = e2 / (1.0 + e2)
    r_ref[...] = jnp.where(lane == 0, i1,
                           jnp.where(lane == 1, i2, jnp.where(lane == 2, w1, jnp.where(lane == 3, w2, 0.0))))


def _router(x2d, g, wr_pad):
    n, d = x2d.shape
    tm = 512
    return pl.pallas_call(
        _router_kernel,
        grid=(n // tm,),
        in_specs=[
            pl.BlockSpec((tm, d), lambda i: (i, 0)),
            pl.BlockSpec((1, d), lambda i: (0, 0)),
            pl.BlockSpec(wr_pad.shape, lambda i: (0, 0)),
        ],
        out_specs=[pl.BlockSpec((tm, d), lambda i: (i, 0)), pl.BlockSpec((tm, LANES), lambda i: (i, 0))],
        out_shape=[jax.ShapeDtypeStruct((n, d), BF16), jax.ShapeDtypeStruct((n, LANES), F32)],
        compiler_params=_params("parallel"),
        name="router",
    )(x2d, g, wr_pad)


def _moe_kernel(be_ref, x_ref, wg_ref, wu_ref, wd_ref, o_ref, acc_ref):
    f = pl.program_id(1)

    @pl.when(f == 0)
    def _():
        acc_ref[...] = jnp.zeros_like(acc_ref)

    acc_ref[...] += _swiglu_step(x_ref[...], wg_ref[0], wu_ref[0], wd_ref[0])

    @pl.when(f == pl.num_programs(1) - 1)
    def _():
        o_ref[...] = acc_ref[...].astype(BF16)


def _moe_experts(blk_e, xb, wg, wu, wd):
    rows, d = xb.shape
    ff = wg.shape[2]
    tf = ff // 4
    n_blk = rows // MOE_ROWS
    grid_spec = pltpu.PrefetchScalarGridSpec(
        num_scalar_prefetch=1,
        grid=(n_blk, ff // tf),
        in_specs=[
            pl.BlockSpec((MOE_ROWS, d), lambda i, f, be: (i, 0)),
            pl.BlockSpec((1, d, tf), lambda i, f, be: (be[i], 0, f)),
            pl.BlockSpec((1, d, tf), lambda i, f, be: (be[i], 0, f)),
            pl.BlockSpec((1, tf, d), lambda i, f, be: (be[i], f, 0)),
        ],
        out_specs=pl.BlockSpec((MOE_ROWS, d), lambda i, f, be: (i, 0)),
        scratch_shapes=[pltpu.VMEM((MOE_ROWS, d), F32)],
    )
    return pl.pallas_call(
        _moe_kernel,
        grid_spec=grid_spec,
        out_shape=jax.ShapeDtypeStruct((rows, d), BF16),
        compiler_params=_params("parallel", "arbitrary"),
        name="moe_experts",
    )(blk_e, xb, wg, wu, wd)


def _combine_kernel(*refs, normed):
    if normed:
        x_ref, y1_ref, y2_ref, r_ref, g_ref, o_ref = refs
    else:
        x_ref, y1_ref, y2_ref, r_ref, o_ref = refs
    w1 = r_ref[:, 2:3]
    w2 = r_ref[:, 3:4]
    x = x_ref[...] + y1_ref[...].astype(F32) * w1 + y2_ref[...].astype(F32) * w2
    o_ref[...] = _rms(x, g_ref[...]) if normed else x


def _combine(x2d, y1, y2, route, gain):
    n, d = x2d.shape
    tm = 512
    normed = gain is not None
    tok = lambda w: pl.BlockSpec((tm, w), lambda i: (i, 0))
    ins = [tok(d), tok(d), tok(d), tok(LANES)]
    args = [x2d, y1, y2, route]
    if normed:
        ins.append(pl.BlockSpec((1, d), lambda i: (0, 0)))
        args.append(gain)
    return pl.pallas_call(
        functools.partial(_combine_kernel, normed=normed),
        grid=(n // tm,),
        in_specs=ins,
        out_specs=tok(d),
        out_shape=jax.ShapeDtypeStruct((n, d), F32),
        compiler_params=_params("parallel"),
        name="moe_combine",
    )(*args)


def _norm_kernel(x_ref, g_ref, o_ref):
    o_ref[...] = _rms(x_ref[...], g_ref[...])


def _final_norm(x2d, g):
    n, d = x2d.shape
    tm = 512
    return pl.pallas_call(
        _norm_kernel,
        grid=(n // tm,),
        in_specs=[pl.BlockSpec((tm, d), lambda i: (i, 0)), pl.BlockSpec((1, d), lambda i: (0, 0))],
        out_specs=pl.BlockSpec((tm, d), lambda i: (i, 0)),
        out_shape=jax.ShapeDtypeStruct((n, d), F32),
        compiler_params=_params("parallel"),
        name="final_norm",
    )(x2d, g)


def _hyena(u, conv_w, conv_b, filt, skip, tables):
    g_fwd, g_inv, fb, fbi = tables
    b, L, c3 = u.shape
    c = c3 // (HY_ORDER + 1)
    uc = _shortconv(u, conv_w, conv_b.reshape(1, c3))
    spec = _filter_spectrum(_fft_small_fwd(filt, g_fwd, c_off=0, n_c=c), fb)
    t = _fft_small_fwd(uc, g_fwd, c_off=2 * c, n_c=c)
    t = _fft_big(t, spec[0], fb, fbi)
    zz, t = _fft_small_inv(t, g_inv, uc, 2 * c, uc, 0, skip[0:1], g_fwd)
    t = _fft_big(t, spec[1], fb, fbi)
    y5 = _fft_small_inv(t, g_inv, zz, 0, uc, c, skip[1:2])
    return y5.reshape(b, L, c)


def _moe_dispatch(route, n_tok):
    n_asg = n_tok * TOP_K
    flat_e = route[:, :TOP_K].astype(jnp.int32).reshape(-1)
    onehot = (flat_e[:, None] == jnp.arange(N_EXPERTS, dtype=jnp.int32)[None, :]).astype(jnp.int32)
    csum = jnp.cumsum(onehot, axis=0)
    counts = csum[-1]
    rank = jnp.sum((csum - 1) * onehot, axis=1)
    padded = (counts + MOE_ROWS - 1) // MOE_ROWS * MOE_ROWS
    pad_end = jnp.cumsum(padded)
    pad_start = pad_end - padded
    dest = pad_start[flat_e] + rank
    seg_start = jnp.cumsum(counts) - counts
    n_blk = -(-(n_asg + N_EXPERTS * (MOE_ROWS - 1)) // MOE_ROWS)
    blk_start = jnp.arange(n_blk, dtype=jnp.int32) * MOE_ROWS
    blk_e = jnp.minimum(jnp.sum((blk_start[:, None] >= pad_end[None, :]).astype(jnp.int32), axis=1), N_EXPERTS - 1)
    order = jnp.argsort(flat_e, stable=True).astype(jnp.int32)
    e_row = jnp.repeat(blk_e, MOE_ROWS)
    r = jnp.arange(n_blk * MOE_ROWS, dtype=jnp.int32) - pad_start[e_row]
    src = order[jnp.clip(seg_start[e_row] + r, 0, n_asg - 1)]
    buf_tok = jnp.where(r < counts[e_row], src // TOP_K, 0)
    return buf_tok, blk_e.astype(jnp.int32), dest.reshape(n_tok, TOP_K)


def kernel(x, mem, mem_norm, mix_norm, w_in, attn_sink, hy_conv_w, hy_conv_b, hy_f_w1, hy_f_b1, hy_f_freq1,
           hy_f_w2, hy_f_b2, hy_f_freq2, hy_f_w3, hy_skip, attn_out_norm, hy_out_norm, w_out, xattn_norm,
           xw_q, xw_k, xw_v, xw_o, ffn_norm, ffn_w_gate, ffn_w_up, ffn_w_down,
           moe_router, moe_w_gate, moe_w_up, moe_w_down, final_norm):
    b, s, d = x.shape
    depth = w_in.shape[0]
    n_tok = b * s
    assert s == FFT_N // 2 and b % 2 == 0
    row = lambda v: v.reshape(1, -1).astype(F32)

    pos = jnp.arange(s, dtype=F32)
    inv = ROPE_THETA ** (-jnp.arange(0, HEAD_DIM, 2, dtype=F32) / HEAD_DIM)
    ang = pos[:, None] * inv[None, :]
    cosf = jnp.tile(jnp.cos(ang), (1, LANES // (HEAD_DIM // 2)))
    sins = jnp.tile(jnp.concatenate([-jnp.sin(ang), jnp.sin(ang)], axis=1), (1, LANES // HEAD_DIM))
    t_pos = jnp.linspace(0.0, 1.0, s, dtype=F32)[:, None]
    wv = 2.0 * math.pi * jnp.arange(s, dtype=F32)[:, None] / s
    fr = jnp.linspace(1e-4, HY_BANDS - 1, HY_BANDS, dtype=F32)[None, :]
    z_pos = jnp.concatenate([t_pos, jnp.cos(fr * wv), -jnp.sin(fr * wv)], axis=-1)
    emb_pad = 64
    z_pos = jnp.pad(z_pos, ((0, 0), (0, emb_pad - HY_EMB)))
    absd = jnp.abs(jnp.linspace(HY_MIN_DECAY, HY_MAX_DECAY, HY_ORDER * HY_WIDTH, dtype=F32)).reshape(1, -1)
    tables = _fft_tables()

    mem2d = mem.reshape(-1, d)
    x2d = x.reshape(n_tok, d)
    out = None
    for l in range(depth):
        q, kx, vx, u = _inproj(x2d, row(mix_norm[l]), w_in[l].astype(BF16), cosf, sins, s)
        a = _band_attention(q.reshape(b, s, ATTN_WIDTH), kx, vx.reshape(b, s, -1), attn_sink[l].astype(F32), s)
        filt = _hyena_filters(
            z_pos, t_pos, jnp.pad(hy_f_w1[l], ((0, emb_pad - HY_EMB), (0, 0))), row(hy_f_b1[l]),
            row(hy_f_freq1[l]), hy_f_w2[l], row(hy_f_b2[l]), row(hy_f_freq2[l]), hy_f_w3[l], absd)
        y = _hyena(u.reshape(b, s, -1), hy_conv_w[l], hy_conv_b[l], filt, hy_skip[l], tables)
        mkv = _norm_mm(mem2d, row(mem_norm), jnp.concatenate([xw_k[l], xw_v[l]], axis=1).astype(BF16))
        x3 = _mix_xattn(x2d.reshape(b, s, d), a, y, row(attn_out_norm[l]), row(hy_out_norm[l]),
                        w_out[l].astype(BF16), row(xattn_norm[l]), xw_q[l].astype(BF16),
                        mkv.reshape(b, -1, 2 * X_WIDTH), xw_o[l].astype(BF16))
        x2d = x3.reshape(n_tok, d)
        j = l // 2
        last = l == depth - 1
        if l % 2 == 0:
            x2d = _dense_ffn(x2d, row(ffn_norm[l]), ffn_w_gate[j].astype(BF16), ffn_w_up[j].astype(BF16),
                             ffn_w_down[j].astype(BF16))
            if last:
                out = _final_norm(x2d, row(final_norm))
        else:
            wr = jnp.pad(moe_router[j].astype(F32), ((0, 0), (0, LANES - N_EXPERTS)))
            hb, route = _router(x2d, row(ffn_norm[l]), wr)
            buf_tok, blk_e, dest = _moe_dispatch(route, n_tok)
            yb = _moe_experts(blk_e, hb[buf_tok], moe_w_gate[j].astype(BF16), moe_w_up[j].astype(BF16),
                              moe_w_down[j].astype(BF16))
            res = _combine(x2d, yb[dest[:, 0]], yb[dest[:, 1]], route, row(final_norm) if last else None)
            if last:
                out = res
            else:
                x2d = res
    return out.reshape(b, s, d)
```

```python
import functools
import math

import jax
import jax.numpy as jnp
from jax import lax
from jax.experimental import pallas as pl
from jax.experimental.pallas import tpu as pltpu

F32 = jnp.float32
BF16 = jnp.bfloat16

EPS = 1e-6
N_HEADS = 8
N_KV_HEADS = 2
HEAD_DIM = 64
ATTN_WIDTH = N_HEADS * HEAD_DIM
KV_WIDTH = N_KV_HEADS * HEAD_DIM
WINDOW = 128
ROPE_THETA = 10000.0
HY_WIDTH = 512
HY_ORDER = 2
HY_EMB = 33
HY_BANDS = (HY_EMB - 1) // 2
HY_FILTER_HIDDEN = 64
HY_TARGET = 1e-2
HY_FAST_DECAY = 0.3
HY_SLOW_DECAY = 1.5
HY_MIN_DECAY = math.log(HY_TARGET) / HY_SLOW_DECAY
HY_MAX_DECAY = math.log(HY_TARGET) / HY_FAST_DECAY
Q_END = ATTN_WIDTH
K_END = Q_END + KV_WIDTH
V_END = K_END + KV_WIDTH
X_HEADS = 4
X_HEAD_DIM = 128
X_WIDTH = X_HEADS * X_HEAD_DIM
N_EXPERTS = 8
TOP_K = 2

LANES = 128
SUBLANES = 8
MXU_DIM = 256
VMEM_LIMIT = 56 * 1024 * 1024

FFT_N1 = 256
FFT_N2 = 32
FFT_N = FFT_N1 * FFT_N2
FFT_J = SUBLANES
FFT_GROUPS = FFT_N1 // FFT_J
FFT_CHUNK = 64
FFT_SUB = FFT_CHUNK // FFT_J

MOE_ROWS = 1024
NEG = float(jnp.finfo(jnp.float32).min)
MASKED = -1e30
LOG2E = math.log2(math.e)


def _params(*sem):
    return pltpu.CompilerParams(dimension_semantics=sem, vmem_limit_bytes=VMEM_LIMIT)


def _dot(a, b):
    return jnp.dot(a, b, preferred_element_type=F32)


def _dot_t(a, b):
    return lax.dot_general(a, b, (((1,), (1,)), ((), ())), preferred_element_type=F32)


def _rms(xf, g):
    ms = jnp.mean(xf * xf, axis=-1, keepdims=True)
    return xf * lax.rsqrt(ms + EPS) * g


def _inproj_kernel(x_ref, g_ref, w_ref, cos_ref, sin_ref, q_ref, kx_ref, vx_ref, u_ref):
    h = _rms(x_ref[...], g_ref[...]).astype(BF16)
    cosf = cos_ref[...]
    sins = sin_ref[...]
    lane = lax.broadcasted_iota(jnp.int32, cosf.shape, 1)
    low = (lane % HEAD_DIM) < (HEAD_DIM // 2)

    def rope(c):
        rot = jnp.where(low, pltpu.roll(c, LANES - HEAD_DIM // 2, 1), pltpu.roll(c, HEAD_DIM // 2, 1))
        return c * cosf + rot * sins

    scale = HEAD_DIM ** -0.5 * LOG2E
    for j in range(ATTN_WIDTH // MXU_DIM):
        qc = _dot(h, w_ref[:, j * MXU_DIM:(j + 1) * MXU_DIM])
        for t in range(MXU_DIM // LANES):
            c0 = j * MXU_DIM + t * LANES
            q_ref[:, c0:c0 + LANES] = (rope(qc[:, t * LANES:(t + 1) * LANES]) * scale).astype(BF16)
    kvc = _dot(h, w_ref[:, Q_END:V_END])
    kt = rope(kvc[:, :KV_WIDTH]).T
    trow = lax.broadcasted_iota(jnp.int32, kt.shape, 0)
    top = jnp.where(trow < HEAD_DIM, kt, 0.0)
    bot = jnp.where(trow >= HEAD_DIM, kt, 0.0)
    kx_ref[0, 0] = top.astype(BF16)
    kx_ref[0, 1] = pltpu.roll(top, HEAD_DIM, 0).astype(BF16)
    kx_ref[0, 2] = pltpu.roll(bot, HEAD_DIM, 0).astype(BF16)
    kx_ref[0, 3] = bot.astype(BF16)
    vc = kvc[:, KV_WIDTH:]
    vsw = pltpu.roll(vc, HEAD_DIM, 1)
    lo = lane < HEAD_DIM
    vx_ref[:, 0 * LANES:1 * LANES] = jnp.where(lo, vc, 1.0).astype(BF16)
    vx_ref[:, 1 * LANES:2 * LANES] = jnp.where(lo, 1.0, vsw).astype(BF16)
    vx_ref[:, 2 * LANES:3 * LANES] = jnp.where(lo, vsw, 1.0).astype(BF16)
    vx_ref[:, 3 * LANES:4 * LANES] = jnp.where(lo, 1.0, vc).astype(BF16)
    n_u = u_ref.shape[1]
    for j in range(n_u // 512):
        u_ref[:, j * 512:(j + 1) * 512] = _dot(h, w_ref[:, V_END + j * 512:V_END + (j + 1) * 512]).astype(BF16)


def _inproj(x2d, g, w, cosf, sins, seq):
    n, d = x2d.shape
    tm = 512
    n_u = w.shape[1] - V_END
    spb = seq // tm
    return pl.pallas_call(
        _inproj_kernel,
        grid=(n // tm,),
        in_specs=[
            pl.BlockSpec((tm, d), lambda i: (i, 0)),
            pl.BlockSpec((1, d), lambda i: (0, 0)),
            pl.BlockSpec(w.shape, lambda i: (0, 0)),
            pl.BlockSpec((tm, LANES), lambda i: (i % spb, 0)),
            pl.BlockSpec((tm, LANES), lambda i: (i % spb, 0)),
        ],
        out_specs=[
            pl.BlockSpec((tm, ATTN_WIDTH), lambda i: (i, 0)),
            pl.BlockSpec((1, 2 * N_KV_HEADS, LANES, tm), lambda i: (i // spb, 0, 0, i % spb)),
            pl.BlockSpec((tm, 2 * N_KV_HEADS * LANES), lambda i: (i, 0)),
            pl.BlockSpec((tm, n_u), lambda i: (i, 0)),
        ],
        out_shape=[
            jax.ShapeDtypeStruct((n, ATTN_WIDTH), BF16),
            jax.ShapeDtypeStruct((n // seq, 2 * N_KV_HEADS, LANES, seq), BF16),
            jax.ShapeDtypeStruct((n, 2 * N_KV_HEADS * LANES), BF16),
            jax.ShapeDtypeStruct((n, n_u), BF16),
        ],
        compiler_params=_params("parallel"),
        name="inproj",
    )(x2d, g, w, cosf, sins)


def _battn_kernel(sink_ref, q_ref, kx_ref, vx_ref, o_ref, *, seq, tq):
    i = pl.program_id(1)
    blk = WINDOW
    n_blk = seq // blk
    row = lax.broadcasted_iota(jnp.int32, (blk, blk), 0)
    col = lax.broadcasted_iota(jnp.int32, (blk, blk), 1)
    tri_prev = jnp.where(col >= row, 0.0, MASKED)
    tri_next = jnp.where(col <= row, 0.0, MASKED)
    lo = col < HEAD_DIM
    heads_per_pair = LANES // HEAD_DIM
    group_pairs = N_HEADS // N_KV_HEADS // heads_per_pair
    units = [(jb, hp) for jb in range(tq // blk) for hp in range(N_HEADS // heads_per_pair)]

    def window(jb):
        bi = i * (tq // blk) + jb
        starts = [jnp.maximum(bi - 1, 0), bi, jnp.minimum(bi + 1, n_blk - 1)]
        return bi, [pl.multiple_of(s * blk, blk) for s in starts]

    def scores(jb, hp):
        _, starts = window(jb)
        qp = q_ref[0, jb * blk:(jb + 1) * blk, hp * LANES:(hp + 1) * LANES]
        out = []
        for t in range(heads_per_pair):
            var = heads_per_pair * (hp // group_pairs) + t
            kwin = jnp.concatenate([kx_ref[0, var, :, pl.ds(s, blk)] for s in starts], axis=1)
            out.append(_dot(qp, kwin))
        return out

    def shifted(jb, hp, s_pair):
        bi, _ = window(jb)
        b_prev = tri_prev + jnp.where(bi == 0, MASKED, 0.0)
        b_next = tri_next + jnp.where(bi == n_blk - 1, MASKED, 0.0)
        out = []
        for t in range(heads_per_pair):
            s = s_pair[t]
            s0 = s[:, :blk] + b_prev
            s1 = s[:, blk:2 * blk]
            s2 = s[:, 2 * blk:] + b_next
            sk = sink_ref[hp * heads_per_pair + t] * LOG2E
            m = jnp.maximum(jnp.max(jnp.maximum(jnp.maximum(s0, s1), s2), axis=-1, keepdims=True), sk)
            x = jnp.concatenate([s0 - m, s1 - m, s2 - m], axis=1).astype(BF16)
            out.append((x, sk - m))
        return out

    def probs(x_pair):
        return [(jnp.exp2(x), jnp.exp2(d)) for x, d in x_pair]

    def finish(jb, hp, p_pair):
        _, starts = window(jb)
        res = []
        for t in range(heads_per_pair):
            var = heads_per_pair * (hp // group_pairs) + t
            vwin = jnp.concatenate([vx_ref[0, pl.ds(s, blk), var * LANES:(var + 1) * LANES] for s in starts], axis=0)
            res.append(_dot(p_pair[t][0], vwin))
        num = jnp.where(lo, res[0], res[1])
        den = pltpu.roll(jnp.where(lo, res[1], res[0]), HEAD_DIM, 1) + jnp.where(lo, p_pair[0][1], p_pair[1][1])
        o_ref[0, jb * blk:(jb + 1) * blk, hp * LANES:(hp + 1) * LANES] = (num * (1.0 / den)).astype(BF16)

    n_u = len(units)
    st_s, st_x, st_p = {}, {}, {}
    for n in range(n_u + 3):
        if 0 <= n - 3 < n_u:
            finish(*units[n - 3], st_p.pop(n - 3))
        if 0 <= n - 2 < n_u:
            st_p[n - 2] = probs(st_x.pop(n - 2))
        if 0 <= n - 1 < n_u:
            st_x[n - 1] = shifted(*units[n - 1], st_s.pop(n - 1))
        if n < n_u:
            st_s[n] = scores(*units[n])


def _band_attention(q, kx, vx, sink, seq):
    b = q.shape[0]
    tq = 512
    return pl.pallas_call(
        functools.partial(_battn_kernel, seq=seq, tq=tq),
        grid=(b, seq // tq),
        in_specs=[
            pl.BlockSpec(memory_space=pltpu.SMEM),
            pl.BlockSpec((1, tq, ATTN_WIDTH), lambda bi, i: (bi, i, 0)),
            pl.BlockSpec((1,) + kx.shape[1:], lambda bi, i: (bi, 0, 0, 0)),
            pl.BlockSpec((1,) + vx.shape[1:], lambda bi, i: (bi, 0, 0)),
        ],
        out_specs=pl.BlockSpec((1, tq, ATTN_WIDTH), lambda bi, i: (bi, i, 0)),
        out_shape=jax.ShapeDtypeStruct((b, seq, ATTN_WIDTH), BF16),
        compiler_params=_params("parallel", "arbitrary"),
        name="band_attention",
    )(sink, q, kx, vx)


def _filter_kernel(z_ref, t_ref, w1_ref, b1_ref, f1_ref, w2_ref, b2_ref, f2_ref, w3_ref, ad_ref, o_ref, *, tl):
    hp = lax.Precision.HIGHEST
    h = jnp.sin(f1_ref[...] * (jnp.dot(z_ref[...], w1_ref[...], precision=hp, preferred_element_type=F32)
                               + b1_ref[...]))
    h = jnp.sin(f2_ref[...] * (jnp.dot(h, w2_ref[...], precision=hp, preferred_element_type=F32) + b2_ref[...]))
    t = t_ref[...]
    rowid = pl.program_id(0) * tl + lax.broadcasted_iota(jnp.int32, (tl, HY_WIDTH), 0)
    for d in range(2):
        for o in range(HY_ORDER):
            c0 = (d * HY_ORDER + o) * HY_WIDTH
            v = jnp.dot(h, w3_ref[:, c0:c0 + HY_WIDTH], precision=hp, preferred_element_type=F32)
            v = v * jnp.exp(-t * ad_ref[:, o * HY_WIDTH:(o + 1) * HY_WIDTH])
            if d == 1:
                v = jnp.where(rowid == 0, 0.0, v)
            v = v.reshape(tl // FFT_N1, FFT_GROUPS, FFT_J, HY_WIDTH)
            o_ref[d * HY_ORDER + o] = jnp.concatenate([v, jnp.zeros_like(v)], axis=2).astype(BF16)


def _hyena_filters(z_pos, t_pos, w1, b1, f1, w2, b2, f2, w3, absd):
    L, e = z_pos.shape
    tl = 512
    full = lambda a: pl.BlockSpec(a.shape, lambda i: (0,) * a.ndim)
    return pl.pallas_call(
        functools.partial(_filter_kernel, tl=tl),
        grid=(L // tl,),
        in_specs=[
            pl.BlockSpec((tl, e), lambda i: (i, 0)),
            pl.BlockSpec((tl, 1), lambda i: (i, 0)),
            full(w1), full(b1), full(f1), full(w2), full(b2), full(f2), full(w3), full(absd),
        ],
        out_specs=pl.BlockSpec((2 * HY_ORDER, tl // FFT_N1, FFT_GROUPS, 2 * FFT_J, HY_WIDTH),
                               lambda i: (0, i, 0, 0, 0)),
        out_shape=jax.ShapeDtypeStruct((2 * HY_ORDER, L // FFT_N1, FFT_GROUPS, 2 * FFT_J, HY_WIDTH), BF16),
        compiler_params=_params("parallel"),
        name="hyena_filters",
    )(z_pos, t_pos, w1, b1, f1, w2, b2, f2, w3, absd)


def _shortconv_kernel(ua_ref, ub_ref, w_ref, b_ref, o_ref):
    def conv(u_ref):
        u = u_ref[0].astype(F32)
        L, ct = u.shape
        row = lax.broadcasted_iota(jnp.int32, u.shape, 0)
        prev = jnp.where(row == 0, 0.0, pltpu.roll(u, 1, 0))
        nxt = jnp.where(row == L - 1, 0.0, pltpu.roll(u, L - 1, 0))
        r = prev * w_ref[0:1, :] + u * w_ref[1:2, :] + nxt * w_ref[2:3, :] + b_ref[...]
        return r.reshape(L // FFT_N1, FFT_GROUPS, FFT_J, ct)

    o_ref[0] = jnp.concatenate([conv(ua_ref), conv(ub_ref)], axis=2).astype(BF16)


def _shortconv(u, w, bias):
    b, L, c3 = u.shape
    ct = MXU_DIM
    ncb = b // 2
    return pl.pallas_call(
        _shortconv_kernel,
        grid=(ncb, c3 // ct),
        in_specs=[
            pl.BlockSpec((1, L, ct), lambda bi, ci: (bi, 0, ci)),
            pl.BlockSpec((1, L, ct), lambda bi, ci: (bi + ncb, 0, ci)),
            pl.BlockSpec((3, ct), lambda bi, ci: (0, ci)),
            pl.BlockSpec((1, ct), lambda bi, ci: (0, ci)),
        ],
        out_specs=pl.BlockSpec((1, L // FFT_N1, FFT_GROUPS, 2 * FFT_J, ct), lambda bi, ci: (bi, 0, 0, 0, ci)),
        out_shape=jax.ShapeDtypeStruct((ncb, L // FFT_N1, FFT_GROUPS, 2 * FFT_J, c3), BF16),
        compiler_params=_params("parallel", "parallel"),
        name="shortconv",
    )(u, u, w, bias)


def _real_block(m, n, po, pi, sign, scale=1.0):
    ang = (2.0 * math.pi / n) * (m % n).astype(F32)
    re = jnp.cos(ang) * scale
    im = jnp.sin(ang) * (sign * scale)
    return jnp.where(po == pi, re, jnp.where(po > pi, im, -im))


def _fft_tables():
    nh = FFT_N2 // 2
    j2 = 2 * FFT_J

    def split(idx):
        return idx // j2, (idx // FFT_J) % 2, idx % FFT_J

    def small(n_out, n_in, sign, scale, out_is_freq):
        rows = FFT_GROUPS * n_out * j2
        a = lax.broadcasted_iota(jnp.int32, (n_in * 2, rows), 0)
        b = lax.broadcasted_iota(jnp.int32, (n_in * 2, rows), 1)
        major_out, po, j = split(b % (n_out * j2))
        g = b // (n_out * j2)
        major_in, pi = a // 2, a % 2
        k2, n2 = (major_out, major_in) if out_is_freq else (major_in, major_out)
        m = FFT_N1 * n2 * k2 + (FFT_J * g + j) * k2
        compact = _real_block(m, FFT_N, po, pi, sign, scale).astype(BF16)
        cols = n_in * j2
        rep = (lax.broadcasted_iota(jnp.int32, (n_in * 2, cols), 1) // FFT_J
               == lax.broadcasted_iota(jnp.int32, (n_in * 2, cols), 0)).astype(BF16)
        full = lax.dot_general(compact, rep, (((0,), (0,)), ((), ())), preferred_element_type=F32)
        diag = (lax.broadcasted_iota(jnp.int32, (rows, cols), 0) % FFT_J
                == lax.broadcasted_iota(jnp.int32, (rows, cols), 1) % FFT_J)
        return jnp.where(diag, full, 0.0).astype(BF16).reshape(FFT_GROUPS, n_out * j2, cols)

    g_fwd = small(FFT_N2, nh, -1.0, 1.0, True)
    g_inv = small(nh, FFT_N2, 1.0, 1.0 / FFT_N, False)

    def big(sign):
        r = lax.broadcasted_iota(jnp.int32, (2 * FFT_N1, 2 * FFT_N1), 0)
        c = lax.broadcasted_iota(jnp.int32, (2 * FFT_N1, 2 * FFT_N1), 1)
        gk, po, jk = split(r)
        g, pi, j = split(c)
        m = (FFT_J * gk + jk) * (FFT_J * g + j)
        return _real_block(m, FFT_N1, po, pi, sign).astype(BF16)

    return g_fwd, g_inv, big(-1.0), big(1.0)


def _small_fwd_rows(g_ref, s, tile):
    r = _dot(g_ref[s], tile)
    return r.reshape(FFT_N2, 2 * FFT_J, tile.shape[-1]).astype(BF16)


def _fs_kernel(z_ref, g_ref, o_ref):
    ct = o_ref.shape[-1]
    for s in range(FFT_SUB):
        tile = z_ref[0, :, s, :, :].reshape(FFT_N2 // 2 * 2 * FFT_J, ct)
        o_ref[0, :, s, :, :] = _small_fwd_rows(g_ref, s, tile)


def _fft_small_fwd(zil, g_fwd, *, c_off, n_c):
    ncb, nh, ng, _, _ = zil.shape
    ct = MXU_DIM
    cblk = c_off // ct
    return pl.pallas_call(
        _fs_kernel,
        grid=(ncb, n_c // ct, ng // FFT_SUB),
        in_specs=[
            pl.BlockSpec((1, nh, FFT_SUB, 2 * FFT_J, ct), lambda b, ci, q: (b, 0, q, 0, cblk + ci)),
            pl.BlockSpec((FFT_SUB,) + g_fwd.shape[1:], lambda b, ci, q: (q, 0, 0)),
        ],
        out_specs=pl.BlockSpec((1, FFT_N2, FFT_SUB, 2 * FFT_J, ct), lambda b, ci, q: (b, 0, q, 0, ci)),
        out_shape=jax.ShapeDtypeStruct((ncb, FFT_N2, ng, 2 * FFT_J, n_c), BF16),
        compiler_params=_params("parallel", "parallel", "arbitrary"),
        name="fft_small_fwd",
    )(zil, g_fwd)


def _big_kernel(x_ref, h_ref, fb_ref, fbi_ref, o_ref):
    ct = o_ref.shape[-1]
    h4 = h_ref[0, 0].reshape(FFT_GROUPS, 2, FFT_J, ct)
    hre = h4[:, 0]
    him = h4[:, 1]
    for b in range(x_ref.shape[0]):
        xf = _dot(fb_ref[...], x_ref[b, 0].reshape(2 * FFT_N1, ct)).reshape(FFT_GROUPS, 2, FFT_J, ct)
        xr = xf[:, 0]
        xi = xf[:, 1]
        y = jnp.stack([xr * hre - xi * him, xr * him + xi * hre], axis=1).reshape(2 * FFT_N1, ct).astype(BF16)
        o_ref[b, 0] = _dot(fbi_ref[...], y).reshape(FFT_GROUPS, 2 * FFT_J, ct).astype(BF16)


def _fft_big(xs, spec, order, fb, fbi):
    ncb, n2, ng, _, c = xs.shape
    ct = MXU_DIM
    nbb = 4 if ncb % 4 == 0 else 1
    return pl.pallas_call(
        _big_kernel,
        grid=(c // ct, n2, ncb // nbb),
        in_specs=[
            pl.BlockSpec((nbb, 1, ng, 2 * FFT_J, ct), lambda ci, k, b: (b, k, 0, 0, ci)),
            pl.BlockSpec((1, 1, ng, 2 * FFT_J, ct), lambda ci, k, b: (order, k, 0, 0, ci)),
            pl.BlockSpec(fb.shape, lambda ci, k, b: (0, 0)),
            pl.BlockSpec(fbi.shape, lambda ci, k, b: (0, 0)),
        ],
        out_specs=pl.BlockSpec((nbb, 1, ng, 2 * FFT_J, ct), lambda ci, k, b: (b, k, 0, 0, ci)),
        out_shape=jax.ShapeDtypeStruct(xs.shape, BF16),
        compiler_params=_params("parallel", "parallel", "arbitrary"),
        name="fft_big",
    )(xs, spec, fb, fbi)


def _spectrum_kernel(x_ref, fb_ref, o_ref):
    ct = o_ref.shape[-1]
    xf = [_dot(fb_ref[...], x_ref[i, 0].reshape(2 * FFT_N1, ct)).reshape(FFT_GROUPS, 2, FFT_J, ct)
          for i in range(2 * HY_ORDER)]
    for o in range(HY_ORDER):
        a = xf[o]
        r = xf[HY_ORDER + o]
        spec = jnp.stack([a[:, 0] + r[:, 0], a[:, 1] - r[:, 1]], axis=1)
        o_ref[o, 0] = spec.reshape(FFT_GROUPS, 2 * FFT_J, ct)


def _filter_spectrum(xs, fb):
    nf, n2, ng, _, c = xs.shape
    ct = MXU_DIM
    return pl.pallas_call(
        _spectrum_kernel,
        grid=(c // ct, n2),
        in_specs=[
            pl.BlockSpec((nf, 1, ng, 2 * FFT_J, ct), lambda ci, k: (0, k, 0, 0, ci)),
            pl.BlockSpec(fb.shape, lambda ci, k: (0, 0)),
        ],
        out_specs=pl.BlockSpec((HY_ORDER, 1, ng, 2 * FFT_J, ct), lambda ci, k: (0, k, 0, 0, ci)),
        out_shape=jax.ShapeDtypeStruct((HY_ORDER, n2, ng, 2 * FFT_J, c), F32),
        compiler_params=_params("parallel", "parallel"),
        name="filter_spectrum",
    )(xs, fb)


def _gated_inverse(c_ref, gi_ref, z_ref, gate_ref, skip, s):
    nh = FFT_N2 // 2
    ct = c_ref.shape[-1]
    rows = nh * 2 * FFT_J
    r = _dot(gi_ref[s], c_ref[0, :, s, :, :].reshape(FFT_N2 * 2 * FFT_J, ct))
    z = z_ref[0, :, s, :, :].astype(F32).reshape(rows, ct)
    gate = gate_ref[0, :, s, :, :].astype(F32).reshape(rows, ct)
    return gate * (r + z * skip)


def _is_fs_kernel(c_ref, gi_ref, z_ref, gate_ref, skip_ref, gf_ref, zz_ref, o_ref):
    nh = FFT_N2 // 2
    ct = o_ref.shape[-1]
    skip = skip_ref[...]
    for s in range(FFT_SUB):
        zz = _gated_inverse(c_ref, gi_ref, z_ref, gate_ref, skip, s).astype(BF16)
        zz_ref[0, :, s, :, :] = zz.reshape(nh, 2 * FFT_J, ct)
        o_ref[0, :, s, :, :] = _small_fwd_rows(gf_ref, s, zz)


def _is_last_kernel(c_ref, gi_ref, z_ref, gate_ref, skip_ref, y_ref):
    nh = FFT_N2 // 2
    ct = y_ref.shape[-1]
    skip = skip_ref[...]
    for s in range(FFT_SUB):
        y = _gated_inverse(c_ref, gi_ref, z_ref, gate_ref, skip, s).reshape(nh, 2, FFT_J, ct)
        for p in range(2):
            y_ref[p, 0, :, FFT_J * s:FFT_J * (s + 1), :] = y[:, p]


def _fft_small_inv(cs, g_inv, zil, z_off, gil, gate_off, skip, g_fwd=None):
    ncb, n2, ng, _, c = cs.shape
    nh = n2 // 2
    ct = MXU_DIM
    zb = z_off // ct
    gb = gate_off // ct
    til = lambda off: pl.BlockSpec((1, nh, FFT_SUB, 2 * FFT_J, ct), lambda b, ci, q: (b, 0, q, 0, off + ci))
    freq = pl.BlockSpec((1, n2, FFT_SUB, 2 * FFT_J, ct), lambda b, ci, q: (b, 0, q, 0, ci))
    mat = lambda m: pl.BlockSpec((FFT_SUB,) + m.shape[1:], lambda b, ci, q: (q, 0, 0))
    ins = [freq, mat(g_inv), til(zb), til(gb), pl.BlockSpec((1, ct), lambda b, ci, q: (0, ci))]
    args = [cs, g_inv, zil, gil, skip]
    if g_fwd is not None:
        body = _is_fs_kernel
        ins.append(mat(g_fwd))
        args.append(g_fwd)
        out_specs = [til(0), freq]
        out_shape = [jax.ShapeDtypeStruct((ncb, nh, ng, 2 * FFT_J, c), BF16), jax.ShapeDtypeStruct(cs.shape, BF16)]
    else:
        body = _is_last_kernel
        out_specs = pl.BlockSpec((2, 1, nh, FFT_CHUNK, ct), lambda b, ci, q: (0, b, 0, q, ci))
        out_shape = jax.ShapeDtypeStruct((2, ncb, nh, ng * FFT_J, c), F32)
    return pl.pallas_call(
        body,
        grid=(ncb, c // ct, ng // FFT_SUB),
        in_specs=ins,
        out_specs=out_specs,
        out_shape=out_shape,
        compiler_params=_params("parallel", "parallel", "arbitrary"),
        name="fft_small_inv",
    )(*args)


def _norm_mm_kernel(x_ref, g_ref, w_ref, o_ref):
    o_ref[...] = _dot(_rms(x_ref[...], g_ref[...]).astype(BF16), w_ref[...]).astype(o_ref.dtype)


def _norm_mm(x2d, g, w):
    n, d = x2d.shape
    tm = 512
    return pl.pallas_call(
        _norm_mm_kernel,
        grid=(n // tm,),
        in_specs=[
            pl.BlockSpec((tm, d), lambda i: (i, 0)),
            pl.BlockSpec((1, d), lambda i: (0, 0)),
            pl.BlockSpec(w.shape, lambda i: (0, 0)),
        ],
        out_specs=pl.BlockSpec((tm, w.shape[1]), lambda i: (i, 0)),
        out_shape=jax.ShapeDtypeStruct((n, w.shape[1]), BF16),
        compiler_params=_params("parallel"),
        name="memory_kv",
    )(x2d, g, w)


def _mix_xattn_kernel(x_ref, a_ref, y_ref, ga_ref, gy_ref, wo_ref, gx_ref, wq_ref, kv_ref, wxo_ref, o_ref):
    an = _rms(a_ref[0].astype(F32), ga_ref[...]).astype(BF16)
    yn = _rms(y_ref[0], gy_ref[...]).astype(BF16)
    x1 = x_ref[0] + _dot(an, wo_ref[:ATTN_WIDTH, :]) + _dot(yn, wo_ref[ATTN_WIDTH:, :])
    h = _rms(x1, gx_ref[...]).astype(BF16)
    q = (_dot(h, wq_ref[...]) * (X_HEAD_DIM ** -0.5)).astype(BF16)
    outs = []
    for hh in range(X_HEADS):
        cols = slice(hh * X_HEAD_DIM, (hh + 1) * X_HEAD_DIM)
        kh = kv_ref[0, :, cols]
        vh = kv_ref[0, :, X_WIDTH + hh * X_HEAD_DIM:X_WIDTH + (hh + 1) * X_HEAD_DIM]
        s = _dot_t(q[:, cols], kh)
        p = jnp.exp(s - jnp.max(s, axis=-1, keepdims=True))
        den = jnp.sum(p, axis=-1, keepdims=True)
        outs.append((_dot(p.astype(BF16), vh) / den).astype(BF16))
    o_ref[0] = x1 + _dot(jnp.concatenate(outs, axis=1), wxo_ref[...])


def _mix_xattn(x, a, y, ga, gy, wo, gx, wq, kv, wxo):
    b, s, d = x.shape
    tq = 512
    m = kv.shape[1]
    full = lambda arr: pl.BlockSpec(arr.shape, lambda bi, i: (0,) * arr.ndim)
    tok = lambda w: pl.BlockSpec((1, tq, w), lambda bi, i: (bi, i, 0))
    return pl.pallas_call(
        _mix_xattn_kernel,
        grid=(b, s // tq),
        in_specs=[tok(d), tok(ATTN_WIDTH), tok(HY_WIDTH), full(ga), full(gy), full(wo), full(gx), full(wq),
                  pl.BlockSpec((1, m, 2 * X_WIDTH), lambda bi, i: (bi, 0, 0)), full(wxo)],
        out_specs=tok(d),
        out_shape=jax.ShapeDtypeStruct((b, s, d), F32),
        compiler_params=_params("parallel", "arbitrary"),
        name="mix_xattn",
    )(x, a, y, ga, gy, wo, gx, wq, kv, wxo)


def _swiglu_chunks(h, wg, wu, wd, width, acc):
    for c in range(width // MXU_DIM):
        cols = slice(c * MXU_DIM, (c + 1) * MXU_DIM)
        g = _dot(h, wg(cols))
        u = _dot(h, wu(cols))
        a = (g * (1.0 / (1.0 + jnp.exp(-g))) * u).astype(BF16)
        acc = acc + _dot(a, wd(cols))
    return acc


def _ffn_kernel(x_ref, g_ref, wg_ref, wu_ref, wd_ref, o_ref):
    x = x_ref[...]
    h = _rms(x, g_ref[...]).astype(BF16)
    o_ref[...] = _swiglu_chunks(h, lambda c: wg_ref[:, c], lambda c: wu_ref[:, c], lambda c: wd_ref[c, :],
                                wg_ref.shape[1], x)


def _dense_ffn(x2d, g, wg, wu, wd):
    n, d = x2d.shape
    tm = 512
    resident = lambda w: pl.BlockSpec(w.shape, lambda i: (0, 0), pipeline_mode=pl.Buffered(1))
    return pl.pallas_call(
        _ffn_kernel,
        grid=(n // tm,),
        in_specs=[
            pl.BlockSpec((tm, d), lambda i: (i, 0)),
            pl.BlockSpec((1, d), lambda i: (0, 0)),
            resident(wg), resident(wu), resident(wd),
        ],
        out_specs=pl.BlockSpec((tm, d), lambda i: (i, 0)),
        out_shape=jax.ShapeDtypeStruct((n, d), F32),
        compiler_params=_params("parallel"),
        name="dense_ffn",
    )(x2d, g, wg, wu, wd)


def _router_kernel(x_ref, g_ref, wr_ref, h_ref, r_ref):
    hf = _rms(x_ref[...], g_ref[...])
    h_ref[...] = hf.astype(BF16)
    logits = jnp.dot(hf, wr_ref[...], precision=lax.Precision.HIGHEST, preferred_element_type=F32)
    lane = lax.broadcasted_iota(jnp.int32, logits.shape, 1)
    logits = jnp.where(lane < N_EXPERTS, logits, NEG)
    lanef = lane.astype(F32)
    big = float(LANES)
    m1 = jnp.max(logits, axis=-1, keepdims=True)
    i1 = jnp.min(jnp.where(logits == m1, lanef, big), axis=-1, keepdims=True)
    rest = jnp.where(lanef == i1, NEG, logits)
    m2 = jnp.max(rest, axis=-1, keepdims=True)
    i2 = jnp.min(jnp.where(rest == m2, lanef, big), axis=-1, keepdims=True)
    e2 = jnp.exp(m2 - m1)
    w1 = 1.0 / (1.0 + e2)
    w2 = e2 / (1.0 + e2)
    r_ref[...] = jnp.where(lane == 0, i1,
                           jnp.where(lane == 1, i2, jnp.where(lane == 2, w1, jnp.where(lane == 3, w2, 0.0))))


def _router(x2d, g, wr_pad):
    n, d = x2d.shape
    tm = 512
    return pl.pallas_call(
        _router_kernel,
        grid=(n // tm,),
        in_specs=[
            pl.BlockSpec((tm, d), lambda i: (i, 0)),
            pl.BlockSpec((1, d), lambda i: (0, 0)),
            pl.BlockSpec(wr_pad.shape, lambda i: (0, 0)),
        ],
        out_specs=[pl.BlockSpec((tm, d), lambda i: (i, 0)), pl.BlockSpec((tm, LANES), lambda i: (i, 0))],
        out_shape=[jax.ShapeDtypeStruct((n, d), BF16), jax.ShapeDtypeStruct((n, LANES), F32)],
        compiler_params=_params("parallel"),
        name="router",
    )(x2d, g, wr_pad)


def _moe_kernel(be_ref, x_ref, wg_ref, wu_ref, wd_ref, o_ref, acc_ref, *, n_blk):
    f = pl.program_id(1)

    @pl.when(f == 0)
    def _():
        acc_ref[...] = jnp.zeros_like(acc_ref)

    @pl.when(pl.program_id(0) < be_ref[n_blk])
    def _():
        acc_ref[...] = _swiglu_chunks(x_ref[...], lambda c: wg_ref[0, :, c], lambda c: wu_ref[0, :, c],
                                      lambda c: wd_ref[0, c, :], wg_ref.shape[2], acc_ref[...])

    @pl.when(f == pl.num_programs(1) - 1)
    def _():
        o_ref[...] = acc_ref[...].astype(BF16)


def _moe_experts(blk_meta, xb, wg, wu, wd):
    rows, d = xb.shape
    ff = wg.shape[2]
    tf = ff // 2
    nf = ff // tf
    n_blk = rows // MOE_ROWS
    ftile = lambda i, f, be: jnp.where(i < be[n_blk], f, nf - 1)
    grid_spec = pltpu.PrefetchScalarGridSpec(
        num_scalar_prefetch=1,
        grid=(n_blk, nf),
        in_specs=[
            pl.BlockSpec((MOE_ROWS, d), lambda i, f, be: (i, 0)),
            pl.BlockSpec((1, d, tf), lambda i, f, be: (be[i], 0, ftile(i, f, be))),
            pl.BlockSpec((1, d, tf), lambda i, f, be: (be[i], 0, ftile(i, f, be))),
            pl.BlockSpec((1, tf, d), lambda i, f, be: (be[i], ftile(i, f, be), 0)),
        ],
        out_specs=pl.BlockSpec((MOE_ROWS, d), lambda i, f, be: (i, 0)),
        scratch_shapes=[pltpu.VMEM((MOE_ROWS, d), F32)],
    )
    return pl.pallas_call(
        functools.partial(_moe_kernel, n_blk=n_blk),
        grid_spec=grid_spec,
        out_shape=jax.ShapeDtypeStruct((rows, d), BF16),
        compiler_params=_params("parallel", "arbitrary"),
        name="moe_experts",
    )(blk_meta, xb, wg, wu, wd)


def _combine_kernel(*refs, normed):
    if normed:
        x_ref, y1_ref, y2_ref, r_ref, g_ref, o_ref = refs
    else:
        x_ref, y1_ref, y2_ref, r_ref, o_ref = refs
    w1 = r_ref[:, 2:3]
    w2 = r_ref[:, 3:4]
    x = x_ref[...] + y1_ref[...].astype(F32) * w1 + y2_ref[...].astype(F32) * w2
    o_ref[...] = _rms(x, g_ref[...]) if normed else x


def _combine(x2d, y1, y2, route, gain):
    n, d = x2d.shape
    tm = 512
    normed = gain is not None
    tok = lambda w: pl.BlockSpec((tm, w), lambda i: (i, 0))
    ins = [tok(d), tok(d), tok(d), tok(LANES)]
    args = [x2d, y1, y2, route]
    if normed:
        ins.append(pl.BlockSpec((1, d), lambda i: (0, 0)))
        args.append(gain)
    return pl.pallas_call(
        functools.partial(_combine_kernel, normed=normed),
        grid=(n // tm,),
        in_specs=ins,
        out_specs=tok(d),
        out_shape=jax.ShapeDtypeStruct((n, d), F32),
        compiler_params=_params("parallel"),
        name="moe_combine",
    )(*args)


def _norm_kernel(x_ref, g_ref, o_ref):
    o_ref[...] = _rms(x_ref[...], g_ref[...])


def _final_norm(x2d, g):
    n, d = x2d.shape
    tm = 512
    return pl.pallas_call(
        _norm_kernel,
        grid=(n // tm,),
        in_specs=[pl.BlockSpec((tm, d), lambda i: (i, 0)), pl.BlockSpec((1, d), lambda i: (0, 0))],
        out_specs=pl.BlockSpec((tm, d), lambda i: (i, 0)),
        out_shape=jax.ShapeDtypeStruct((n, d), F32),
        compiler_params=_params("parallel"),
        name="final_norm",
    )(x2d, g)


def _hyena(u, conv_w, conv_b, filt, skip, tables):
    g_fwd, g_inv, fb, fbi = tables
    b, L, c3 = u.shape
    c = c3 // (HY_ORDER + 1)
    uc = _shortconv(u, conv_w, conv_b.reshape(1, c3))
    spec = _filter_spectrum(_fft_small_fwd(filt, g_fwd, c_off=0, n_c=c), fb)
    t = _fft_small_fwd(uc, g_fwd, c_off=2 * c, n_c=c)
    t = _fft_big(t, spec, 0, fb, fbi)
    zz, t = _fft_small_inv(t, g_inv, uc, 2 * c, uc, 0, skip[0:1], g_fwd)
    t = _fft_big(t, spec, 1, fb, fbi)
    y5 = _fft_small_inv(t, g_inv, zz, 0, uc, c, skip[1:2])
    return y5.reshape(b, L, c)


def _moe_dispatch(route, n_tok):
    n_asg = n_tok * TOP_K
    flat_e = route[:, :TOP_K].astype(jnp.int32).reshape(-1)
    onehot = (flat_e[:, None] == jnp.arange(N_EXPERTS, dtype=jnp.int32)[None, :]).astype(jnp.int32)
    csum = jnp.cumsum(onehot, axis=0)
    counts = csum[-1]
    rank = jnp.sum((csum - 1) * onehot, axis=1)
    padded = (counts + MOE_ROWS - 1) // MOE_ROWS * MOE_ROWS
    pad_end = jnp.cumsum(padded)
    pad_start = pad_end - padded
    dest = pad_start[flat_e] + rank
    seg_start = jnp.cumsum(counts) - counts
    n_blk = -(-(n_asg + N_EXPERTS * (MOE_ROWS - 1)) // MOE_ROWS)
    blk_start = jnp.arange(n_blk, dtype=jnp.int32) * MOE_ROWS
    blk_e = jnp.minimum(jnp.sum((blk_start[:, None] >= pad_end[None, :]).astype(jnp.int32), axis=1), N_EXPERTS - 1)
    order = jnp.argsort(flat_e, stable=True).astype(jnp.int32)
    e_row = jnp.repeat(blk_e, MOE_ROWS)
    r = jnp.arange(n_blk * MOE_ROWS, dtype=jnp.int32) - pad_start[e_row]
    src = order[jnp.clip(seg_start[e_row] + r, 0, n_asg - 1)]
    buf_tok = jnp.where(r < counts[e_row], src // TOP_K, 0)
    blk_meta = jnp.concatenate([blk_e.astype(jnp.int32), (pad_end[-1:] // MOE_ROWS).astype(jnp.int32)])
    return buf_tok, blk_meta, dest.reshape(n_tok, TOP_K)


def kernel(x, mem, mem_norm, mix_norm, w_in, attn_sink, hy_conv_w, hy_conv_b, hy_f_w1, hy_f_b1, hy_f_freq1,
           hy_f_w2, hy_f_b2, hy_f_freq2, hy_f_w3, hy_skip, attn_out_norm, hy_out_norm, w_out, xattn_norm,
           xw_q, xw_k, xw_v, xw_o, ffn_norm, ffn_w_gate, ffn_w_up, ffn_w_down,
           moe_router, moe_w_gate, moe_w_up, moe_w_down, final_norm):
    b, s, d = x.shape
    depth = w_in.shape[0]
    n_tok = b * s
    assert s == FFT_N // 2 and b % 2 == 0
    row = lambda v: v.reshape(1, -1).astype(F32)

    pos = jnp.arange(s, dtype=F32)
    inv = ROPE_THETA ** (-jnp.arange(0, HEAD_DIM, 2, dtype=F32) / HEAD_DIM)
    ang = pos[:, None] * inv[None, :]
    cosf = jnp.tile(jnp.cos(ang), (1, LANES // (HEAD_DIM // 2)))
    sins = jnp.tile(jnp.concatenate([-jnp.sin(ang), jnp.sin(ang)], axis=1), (1, LANES // HEAD_DIM))
    t_pos = jnp.linspace(0.0, 1.0, s, dtype=F32)[:, None]
    wv = 2.0 * math.pi * jnp.arange(s, dtype=F32)[:, None] / s
    fr = jnp.linspace(1e-4, HY_BANDS - 1, HY_BANDS, dtype=F32)[None, :]
    z_pos = jnp.concatenate([t_pos, jnp.cos(fr * wv), -jnp.sin(fr * wv)], axis=-1)
    emb_pad = 64
    z_pos = jnp.pad(z_pos, ((0, 0), (0, emb_pad - HY_EMB)))
    absd = jnp.abs(jnp.linspace(HY_MIN_DECAY, HY_MAX_DECAY, HY_ORDER * HY_WIDTH, dtype=F32)).reshape(1, -1)
    tables = _fft_tables()

    mem2d = mem.reshape(-1, d)
    x2d = x.reshape(n_tok, d)
    out = None
    for l in range(depth):
        q, kx, vx, u = _inproj(x2d, row(mix_norm[l]), w_in[l].astype(BF16), cosf, sins, s)
        a = _band_attention(q.reshape(b, s, ATTN_WIDTH), kx, vx.reshape(b, s, -1), attn_sink[l].astype(F32), s)
        filt = _hyena_filters(
            z_pos, t_pos, jnp.pad(hy_f_w1[l], ((0, emb_pad - HY_EMB), (0, 0))), row(hy_f_b1[l]),
            row(hy_f_freq1[l]), hy_f_w2[l], row(hy_f_b2[l]), row(hy_f_freq2[l]), hy_f_w3[l], absd)
        y = _hyena(u.reshape(b, s, -1), hy_conv_w[l], hy_conv_b[l], filt, hy_skip[l], tables)
        mkv = _norm_mm(mem2d, row(mem_norm), jnp.concatenate([xw_k[l], xw_v[l]], axis=1).astype(BF16))
        x3 = _mix_xattn(x2d.reshape(b, s, d), a, y, row(attn_out_norm[l]), row(hy_out_norm[l]),
                        w_out[l].astype(BF16), row(xattn_norm[l]), xw_q[l].astype(BF16),
                        mkv.reshape(b, -1, 2 * X_WIDTH), xw_o[l].astype(BF16))
        x2d = x3.reshape(n_tok, d)
        j = l // 2
        last = l == depth - 1
        if l % 2 == 0:
            x2d = _dense_ffn(x2d, row(ffn_norm[l]), ffn_w_gate[j].astype(BF16), ffn_w_up[j].astype(BF16),
                             ffn_w_down[j].astype(BF16))
            if last:
                out = _final_norm(x2d, row(final_norm))
        else:
            wr = jnp.pad(moe_router[j].astype(F32), ((0, 0), (0, LANES - N_EXPERTS)))
            hb, route = _router(x2d, row(ffn_norm[l]), wr)
            buf_tok, blk_e, dest = _moe_dispatch(route, n_tok)
            yb = _moe_experts(blk_e, hb[buf_tok], moe_w_gate[j].astype(BF16), moe_w_up[j].astype(BF16),
                              moe_w_down[j].astype(BF16))
            res = _combine(x2d, yb[dest[:, 0]], yb[dest[:, 1]], route, row(final_norm) if last else None)
            if last:
                out = res
            else:
                x2d = res
    return out.reshape(b, s, d)
```

```python
import functools
import math

import jax
import jax.numpy as jnp
from jax import lax
from jax.experimental import pallas as pl
from jax.experimental.pallas import tpu as pltpu

F32 = jnp.float32
BF16 = jnp.bfloat16

EPS = 1e-6
N_HEADS = 8
N_KV_HEADS = 2
HEAD_DIM = 64
ATTN_WIDTH = N_HEADS * HEAD_DIM
KV_WIDTH = N_KV_HEADS * HEAD_DIM
WINDOW = 128
ROPE_THETA = 10000.0
HY_WIDTH = 512
HY_ORDER = 2
HY_EMB = 33
HY_BANDS = (HY_EMB - 1) // 2
HY_FILTER_HIDDEN = 64
HY_TARGET = 1e-2
HY_FAST_DECAY = 0.3
HY_SLOW_DECAY = 1.5
HY_MIN_DECAY = math.log(HY_TARGET) / HY_SLOW_DECAY
HY_MAX_DECAY = math.log(HY_TARGET) / HY_FAST_DECAY
Q_END = ATTN_WIDTH
K_END = Q_END + KV_WIDTH
V_END = K_END + KV_WIDTH
X_HEADS = 4
X_HEAD_DIM = 128
X_WIDTH = X_HEADS * X_HEAD_DIM
N_EXPERTS = 8
TOP_K = 2

LANES = 128
SUBLANES = 8
MXU_DIM = 256
VMEM_LIMIT = 56 * 1024 * 1024

FFT_N1 = 256
FFT_N2 = 32
FFT_N = FFT_N1 * FFT_N2
FFT_J = SUBLANES
FFT_GROUPS = FFT_N1 // FFT_J
FFT_CHUNK = 64
FFT_SUB = FFT_CHUNK // FFT_J

MOE_ROWS = 1024
NEG = float(jnp.finfo(jnp.float32).min)
MASKED = -1e30
LOG2E = math.log2(math.e)


def _params(*sem):
    return pltpu.CompilerParams(dimension_semantics=sem, vmem_limit_bytes=VMEM_LIMIT)


def _dot(a, b):
    return jnp.dot(a, b, preferred_element_type=F32)


def _dot_t(a, b):
    return lax.dot_general(a, b, (((1,), (1,)), ((), ())), preferred_element_type=F32)


def _rms(xf, g):
    ms = jnp.mean(xf * xf, axis=-1, keepdims=True)
    return xf * lax.rsqrt(ms + EPS) * g


def _inproj_kernel(x_ref, g_ref, w_ref, cos_ref, sin_ref, q_ref, kx_ref, vx_ref, u_ref):
    h = _rms(x_ref[...], g_ref[...]).astype(BF16)
    cosf = cos_ref[...]
    sins = sin_ref[...]
    lane = lax.broadcasted_iota(jnp.int32, cosf.shape, 1)
    low = (lane % HEAD_DIM) < (HEAD_DIM // 2)

    def rope(c):
        rot = jnp.where(low, pltpu.roll(c, LANES - HEAD_DIM // 2, 1), pltpu.roll(c, HEAD_DIM // 2, 1))
        return c * cosf + rot * sins

    scale = HEAD_DIM ** -0.5 * LOG2E
    for j in range(ATTN_WIDTH // MXU_DIM):
        qc = _dot(h, w_ref[:, j * MXU_DIM:(j + 1) * MXU_DIM])
        for t in range(MXU_DIM // LANES):
            c0 = j * MXU_DIM + t * LANES
            q_ref[:, c0:c0 + LANES] = (rope(qc[:, t * LANES:(t + 1) * LANES]) * scale).astype(BF16)
    kvc = _dot(h, w_ref[:, Q_END:V_END])
    kt = rope(kvc[:, :KV_WIDTH]).T
    trow = lax.broadcasted_iota(jnp.int32, kt.shape, 0)
    top = jnp.where(trow < HEAD_DIM, kt, 0.0)
    bot = jnp.where(trow >= HEAD_DIM, kt, 0.0)
    kx_ref[0, 0] = top.astype(BF16)
    kx_ref[0, 1] = pltpu.roll(top, HEAD_DIM, 0).astype(BF16)
    kx_ref[0, 2] = pltpu.roll(bot, HEAD_DIM, 0).astype(BF16)
    kx_ref[0, 3] = bot.astype(BF16)
    vc = kvc[:, KV_WIDTH:]
    vsw = pltpu.roll(vc, HEAD_DIM, 1)
    lo = lane < HEAD_DIM
    vx_ref[:, 0 * LANES:1 * LANES] = jnp.where(lo, vc, 1.0).astype(BF16)
    vx_ref[:, 1 * LANES:2 * LANES] = jnp.where(lo, 1.0, vsw).astype(BF16)
    vx_ref[:, 2 * LANES:3 * LANES] = jnp.where(lo, vsw, 1.0).astype(BF16)
    vx_ref[:, 3 * LANES:4 * LANES] = jnp.where(lo, 1.0, vc).astype(BF16)
    n_u = u_ref.shape[1]
    for j in range(n_u // 512):
        u_ref[:, j * 512:(j + 1) * 512] = _dot(h, w_ref[:, V_END + j * 512:V_END + (j + 1) * 512]).astype(BF16)


def _inproj(x2d, g, w, cosf, sins, seq):
    n, d = x2d.shape
    tm = 512
    n_u = w.shape[1] - V_END
    spb = seq // tm
    return pl.pallas_call(
        _inproj_kernel,
        grid=(n // tm,),
        in_specs=[
            pl.BlockSpec((tm, d), lambda i: (i, 0)),
            pl.BlockSpec((1, d), lambda i: (0, 0)),
            pl.BlockSpec(w.shape, lambda i: (0, 0)),
            pl.BlockSpec((tm, LANES), lambda i: (i % spb, 0)),
            pl.BlockSpec((tm, LANES), lambda i: (i % spb, 0)),
        ],
        out_specs=[
            pl.BlockSpec((tm, ATTN_WIDTH), lambda i: (i, 0)),
            pl.BlockSpec((1, 2 * N_KV_HEADS, LANES, tm), lambda i: (i // spb, 0, 0, i % spb)),
            pl.BlockSpec((tm, 2 * N_KV_HEADS * LANES), lambda i: (i, 0)),
            pl.BlockSpec((tm, n_u), lambda i: (i, 0)),
        ],
        out_shape=[
            jax.ShapeDtypeStruct((n, ATTN_WIDTH), BF16),
            jax.ShapeDtypeStruct((n // seq, 2 * N_KV_HEADS, LANES, seq), BF16),
            jax.ShapeDtypeStruct((n, 2 * N_KV_HEADS * LANES), BF16),
            jax.ShapeDtypeStruct((n, n_u), BF16),
        ],
        compiler_params=_params("parallel"),
        name="inproj",
    )(x2d, g, w, cosf, sins)


def _battn_kernel(sink_ref, q_ref, kx_ref, vx_ref, o_ref, *, seq, tq):
    i = pl.program_id(1)
    blk = WINDOW
    n_blk = seq // blk
    row = lax.broadcasted_iota(jnp.int32, (blk, blk), 0)
    col = lax.broadcasted_iota(jnp.int32, (blk, blk), 1)
    tri_prev = jnp.where(col >= row, 0.0, MASKED)
    tri_next = jnp.where(col <= row, 0.0, MASKED)
    lo = col < HEAD_DIM
    heads_per_pair = LANES // HEAD_DIM
    group_pairs = N_HEADS // N_KV_HEADS // heads_per_pair
    units = [(jb, hp) for jb in range(tq // blk) for hp in range(N_HEADS // heads_per_pair)]

    def window(jb):
        bi = i * (tq // blk) + jb
        starts = [jnp.maximum(bi - 1, 0), bi, jnp.minimum(bi + 1, n_blk - 1)]
        return bi, [pl.multiple_of(s * blk, blk) for s in starts]

    def scores(jb, hp):
        _, starts = window(jb)
        qp = q_ref[0, jb * blk:(jb + 1) * blk, hp * LANES:(hp + 1) * LANES]
        out = []
        for t in range(heads_per_pair):
            var = heads_per_pair * (hp // group_pairs) + t
            kwin = jnp.concatenate([kx_ref[0, var, :, pl.ds(s, blk)] for s in starts], axis=1)
            out.append(_dot(qp, kwin))
        return out

    def shifted(jb, hp, s_pair):
        bi, _ = window(jb)
        b_prev = tri_prev + jnp.where(bi == 0, MASKED, 0.0)
        b_next = tri_next + jnp.where(bi == n_blk - 1, MASKED, 0.0)
        out = []
        for t in range(heads_per_pair):
            s = s_pair[t]
            s0 = s[:, :blk] + b_prev
            s1 = s[:, blk:2 * blk]
            s2 = s[:, 2 * blk:] + b_next
            sk = sink_ref[hp * heads_per_pair + t] * LOG2E
            m = jnp.maximum(jnp.max(jnp.maximum(jnp.maximum(s0, s1), s2), axis=-1, keepdims=True), sk)
            x = jnp.concatenate([s0 - m, s1 - m, s2 - m], axis=1).astype(BF16)
            out.append((x, sk - m))
        return out

    def probs(x_pair):
        return [(jnp.exp2(x), jnp.exp2(d)) for x, d in x_pair]

    def finish(jb, hp, p_pair):
        _, starts = window(jb)
        res = []
        for t in range(heads_per_pair):
            var = heads_per_pair * (hp // group_pairs) + t
            vwin = jnp.concatenate([vx_ref[0, pl.ds(s, blk), var * LANES:(var + 1) * LANES] for s in starts], axis=0)
            res.append(_dot(p_pair[t][0], vwin))
        num = jnp.where(lo, res[0], res[1])
        den = pltpu.roll(jnp.where(lo, res[1], res[0]), HEAD_DIM, 1) + jnp.where(lo, p_pair[0][1], p_pair[1][1])
        o_ref[0, jb * blk:(jb + 1) * blk, hp * LANES:(hp + 1) * LANES] = (num * (1.0 / den)).astype(BF16)

    n_u = len(units)
    st_s, st_x, st_p = {}, {}, {}
    for n in range(n_u + 3):
        if 0 <= n - 3 < n_u:
            finish(*units[n - 3], st_p.pop(n - 3))
        if 0 <= n - 2 < n_u:
            st_p[n - 2] = probs(st_x.pop(n - 2))
        if 0 <= n - 1 < n_u:
            st_x[n - 1] = shifted(*units[n - 1], st_s.pop(n - 1))
        if n < n_u:
            st_s[n] = scores(*units[n])


def _band_attention(q, kx, vx, sink, seq):
    b = q.shape[0]
    tq = 512
    return pl.pallas_call(
        functools.partial(_battn_kernel, seq=seq, tq=tq),
        grid=(b, seq // tq),
        in_specs=[
            pl.BlockSpec(memory_space=pltpu.SMEM),
            pl.BlockSpec((1, tq, ATTN_WIDTH), lambda bi, i: (bi, i, 0)),
            pl.BlockSpec((1,) + kx.shape[1:], lambda bi, i: (bi, 0, 0, 0)),
            pl.BlockSpec((1,) + vx.shape[1:], lambda bi, i: (bi, 0, 0)),
        ],
        out_specs=pl.BlockSpec((1, tq, ATTN_WIDTH), lambda bi, i: (bi, i, 0)),
        out_shape=jax.ShapeDtypeStruct((b, seq, ATTN_WIDTH), BF16),
        compiler_params=_params("parallel", "arbitrary"),
        name="band_attention",
    )(sink, q, kx, vx)


def _filter_kernel(z_ref, t_ref, w1_ref, b1_ref, f1_ref, w2_ref, b2_ref, f2_ref, w3_ref, ad_ref, o_ref, *, tl):
    hp = lax.Precision.HIGHEST
    h = jnp.sin(f1_ref[...] * (jnp.dot(z_ref[...], w1_ref[...], precision=hp, preferred_element_type=F32)
                               + b1_ref[...]))
    h = jnp.sin(f2_ref[...] * (jnp.dot(h, w2_ref[...], precision=hp, preferred_element_type=F32) + b2_ref[...]))
    t = t_ref[...]
    rowid = pl.program_id(0) * tl + lax.broadcasted_iota(jnp.int32, (tl, HY_WIDTH), 0)
    for d in range(2):
        for o in range(HY_ORDER):
            c0 = (d * HY_ORDER + o) * HY_WIDTH
            v = jnp.dot(h, w3_ref[:, c0:c0 + HY_WIDTH], precision=hp, preferred_element_type=F32)
            v = v * jnp.exp(-t * ad_ref[:, o * HY_WIDTH:(o + 1) * HY_WIDTH])
            if d == 1:
                v = jnp.where(rowid == 0, 0.0, v)
            v = v.reshape(tl // FFT_N1, FFT_GROUPS, FFT_J, HY_WIDTH)
            o_ref[d * HY_ORDER + o] = jnp.concatenate([v, jnp.zeros_like(v)], axis=2).astype(BF16)


def _hyena_filters(z_pos, t_pos, w1, b1, f1, w2, b2, f2, w3, absd):
    L, e = z_pos.shape
    tl = 512
    full = lambda a: pl.BlockSpec(a.shape, lambda i: (0,) * a.ndim)
    return pl.pallas_call(
        functools.partial(_filter_kernel, tl=tl),
        grid=(L // tl,),
        in_specs=[
            pl.BlockSpec((tl, e), lambda i: (i, 0)),
            pl.BlockSpec((tl, 1), lambda i: (i, 0)),
            full(w1), full(b1), full(f1), full(w2), full(b2), full(f2), full(w3), full(absd),
        ],
        out_specs=pl.BlockSpec((2 * HY_ORDER, tl // FFT_N1, FFT_GROUPS, 2 * FFT_J, HY_WIDTH),
                               lambda i: (0, i, 0, 0, 0)),
        out_shape=jax.ShapeDtypeStruct((2 * HY_ORDER, L // FFT_N1, FFT_GROUPS, 2 * FFT_J, HY_WIDTH), BF16),
        compiler_params=_params("parallel"),
        name="hyena_filters",
    )(z_pos, t_pos, w1, b1, f1, w2, b2, f2, w3, absd)


def _shortconv_kernel(ua_ref, ub_ref, w_ref, b_ref, o_ref):
    def conv(u_ref):
        u = u_ref[0].astype(F32)
        L, ct = u.shape
        r8 = lax.broadcasted_iota(jnp.int32, (SUBLANES, ct), 0)
        down = pltpu.roll(u, 1, 0)
        up = pltpu.roll(u, L - 1, 0)
        prev = jnp.concatenate([jnp.where(r8 == 0, 0.0, down[:SUBLANES]), down[SUBLANES:]], axis=0)
        nxt = jnp.concatenate([up[:L - SUBLANES], jnp.where(r8 == SUBLANES - 1, 0.0, up[L - SUBLANES:])], axis=0)
        r = prev * w_ref[0:1, :] + u * w_ref[1:2, :] + nxt * w_ref[2:3, :] + b_ref[...]
        return r.reshape(L // FFT_N1, FFT_GROUPS, FFT_J, ct)

    o_ref[0] = jnp.concatenate([conv(ua_ref), conv(ub_ref)], axis=2).astype(BF16)


def _shortconv(u, w, bias):
    b, L, c3 = u.shape
    ct = MXU_DIM
    ncb = b // 2
    return pl.pallas_call(
        _shortconv_kernel,
        grid=(ncb, c3 // ct),
        in_specs=[
            pl.BlockSpec((1, L, ct), lambda bi, ci: (bi, 0, ci)),
            pl.BlockSpec((1, L, ct), lambda bi, ci: (bi + ncb, 0, ci)),
            pl.BlockSpec((3, ct), lambda bi, ci: (0, ci)),
            pl.BlockSpec((1, ct), lambda bi, ci: (0, ci)),
        ],
        out_specs=pl.BlockSpec((1, L // FFT_N1, FFT_GROUPS, 2 * FFT_J, ct), lambda bi, ci: (bi, 0, 0, 0, ci)),
        out_shape=jax.ShapeDtypeStruct((ncb, L // FFT_N1, FFT_GROUPS, 2 * FFT_J, c3), BF16),
        compiler_params=_params("parallel", "parallel"),
        name="shortconv",
    )(u, u, w, bias)


def _real_block(m, n, po, pi, sign, scale=1.0):
    ang = (2.0 * math.pi / n) * (m % n).astype(F32)
    re = jnp.cos(ang) * scale
    im = jnp.sin(ang) * (sign * scale)
    return jnp.where(po == pi, re, jnp.where(po > pi, im, -im))


def _fft_tables():
    nh = FFT_N2 // 2
    j2 = 2 * FFT_J

    def split(idx):
        return idx // j2, (idx // FFT_J) % 2, idx % FFT_J

    def small(n_out, n_in, sign, scale, out_is_freq):
        rows = FFT_GROUPS * n_out * j2
        a = lax.broadcasted_iota(jnp.int32, (n_in * 2, rows), 0)
        b = lax.broadcasted_iota(jnp.int32, (n_in * 2, rows), 1)
        major_out, po, j = split(b % (n_out * j2))
        g = b // (n_out * j2)
        major_in, pi = a // 2, a % 2
        k2, n2 = (major_out, major_in) if out_is_freq else (major_in, major_out)
        m = FFT_N1 * n2 * k2 + (FFT_J * g + j) * k2
        compact = _real_block(m, FFT_N, po, pi, sign, scale).astype(BF16)
        cols = n_in * j2
        rep = (lax.broadcasted_iota(jnp.int32, (n_in * 2, cols), 1) // FFT_J
               == lax.broadcasted_iota(jnp.int32, (n_in * 2, cols), 0)).astype(BF16)
        full = lax.dot_general(compact, rep, (((0,), (0,)), ((), ())), preferred_element_type=F32)
        diag = (lax.broadcasted_iota(jnp.int32, (rows, cols), 0) % FFT_J
                == lax.broadcasted_iota(jnp.int32, (rows, cols), 1) % FFT_J)
        return jnp.where(diag, full, 0.0).astype(BF16).reshape(FFT_GROUPS, n_out * j2, cols)

    g_fwd = small(FFT_N2, nh, -1.0, 1.0, True)
    g_inv = small(nh, FFT_N2, 1.0, 1.0 / FFT_N, False)

    def big(sign):
        r = lax.broadcasted_iota(jnp.int32, (2 * FFT_N1, 2 * FFT_N1), 0)
        c = lax.broadcasted_iota(jnp.int32, (2 * FFT_N1, 2 * FFT_N1), 1)
        gk, po, jk = split(r)
        g, pi, j = split(c)
        m = (FFT_J * gk + jk) * (FFT_J * g + j)
        return _real_block(m, FFT_N1, po, pi, sign).astype(BF16)

    return g_fwd, g_inv, big(-1.0), big(1.0)


def _small_fwd_rows(g_ref, s, tile):
    r = _dot(g_ref[s], tile)
    return r.reshape(FFT_N2, 2 * FFT_J, tile.shape[-1]).astype(BF16)


def _fs_kernel(z_ref, g_ref, o_ref):
    ct = o_ref.shape[-1]
    for s in range(FFT_SUB):
        tile = z_ref[0, :, s, :, :].reshape(FFT_N2 // 2 * 2 * FFT_J, ct)
        o_ref[0, :, s, :, :] = _small_fwd_rows(g_ref, s, tile)


def _fft_small_fwd(zil, g_fwd, *, c_off, n_c):
    ncb, nh, ng, _, _ = zil.shape
    ct = MXU_DIM
    cblk = c_off // ct
    return pl.pallas_call(
        _fs_kernel,
        grid=(ng // FFT_SUB, ncb, n_c // ct),
        in_specs=[
            pl.BlockSpec((1, nh, FFT_SUB, 2 * FFT_J, ct), lambda q, b, ci: (b, 0, q, 0, cblk + ci)),
            pl.BlockSpec((FFT_SUB,) + g_fwd.shape[1:], lambda q, b, ci: (q, 0, 0)),
        ],
        out_specs=pl.BlockSpec((1, FFT_N2, FFT_SUB, 2 * FFT_J, ct), lambda q, b, ci: (b, 0, q, 0, ci)),
        out_shape=jax.ShapeDtypeStruct((ncb, FFT_N2, ng, 2 * FFT_J, n_c), BF16),
        compiler_params=_params("parallel", "parallel", "arbitrary"),
        name="fft_small_fwd",
    )(zil, g_fwd)


def _big_kernel(x_ref, h_ref, fb_ref, fbi_ref, o_ref):
    ct = o_ref.shape[-1]
    h4 = h_ref[0, 0].reshape(FFT_GROUPS, 2, FFT_J, ct)
    hre = h4[:, 0]
    him = h4[:, 1]
    n_b = x_ref.shape[0]

    def forward(b):
        return _dot(fb_ref[...], x_ref[b, 0].reshape(2 * FFT_N1, ct)).reshape(FFT_GROUPS, 2, FFT_J, ct)

    xf = forward(0)
    for b in range(n_b):
        xr = xf[:, 0]
        xi = xf[:, 1]
        if b + 1 < n_b:
            xf = forward(b + 1)
        y = jnp.stack([xr * hre - xi * him, xr * him + xi * hre], axis=1).reshape(2 * FFT_N1, ct).astype(BF16)
        o_ref[b, 0] = _dot(fbi_ref[...], y).reshape(FFT_GROUPS, 2 * FFT_J, ct).astype(BF16)


def _fft_big(xs, spec, order, fb, fbi):
    ncb, n2, ng, _, c = xs.shape
    ct = MXU_DIM
    nbb = ncb
    return pl.pallas_call(
        _big_kernel,
        grid=(c // ct, n2, ncb // nbb),
        in_specs=[
            pl.BlockSpec((nbb, 1, ng, 2 * FFT_J, ct), lambda ci, k, b: (b, k, 0, 0, ci)),
            pl.BlockSpec((1, 1, ng, 2 * FFT_J, ct), lambda ci, k, b: (order, k, 0, 0, ci)),
            pl.BlockSpec(fb.shape, lambda ci, k, b: (0, 0)),
            pl.BlockSpec(fbi.shape, lambda ci, k, b: (0, 0)),
        ],
        out_specs=pl.BlockSpec((nbb, 1, ng, 2 * FFT_J, ct), lambda ci, k, b: (b, k, 0, 0, ci)),
        out_shape=jax.ShapeDtypeStruct(xs.shape, BF16),
        compiler_params=_params("parallel", "parallel", "arbitrary"),
        name="fft_big",
    )(xs, spec, fb, fbi)


def _spectrum_kernel(x_ref, fb_ref, o_ref):
    ct = o_ref.shape[-1]
    xf = [_dot(fb_ref[...], x_ref[i, 0].reshape(2 * FFT_N1, ct)).reshape(FFT_GROUPS, 2, FFT_J, ct)
          for i in range(2 * HY_ORDER)]
    for o in range(HY_ORDER):
        a = xf[o]
        r = xf[HY_ORDER + o]
        spec = jnp.stack([a[:, 0] + r[:, 0], a[:, 1] - r[:, 1]], axis=1)
        o_ref[o, 0] = spec.reshape(FFT_GROUPS, 2 * FFT_J, ct)


def _filter_spectrum(xs, fb):
    nf, n2, ng, _, c = xs.shape
    ct = MXU_DIM
    return pl.pallas_call(
        _spectrum_kernel,
        grid=(c // ct, n2),
        in_specs=[
            pl.BlockSpec((nf, 1, ng, 2 * FFT_J, ct), lambda ci, k: (0, k, 0, 0, ci)),
            pl.BlockSpec(fb.shape, lambda ci, k: (0, 0)),
        ],
        out_specs=pl.BlockSpec((HY_ORDER, 1, ng, 2 * FFT_J, ct), lambda ci, k: (0, k, 0, 0, ci)),
        out_shape=jax.ShapeDtypeStruct((HY_ORDER, n2, ng, 2 * FFT_J, c), F32),
        compiler_params=_params("parallel", "parallel"),
        name="filter_spectrum",
    )(xs, fb)


def _gated_inverse(c_ref, gi_ref, z_ref, gate_ref, skip, s):
    nh = FFT_N2 // 2
    ct = c_ref.shape[-1]
    rows = nh * 2 * FFT_J
    r = _dot(gi_ref[s], c_ref[0, :, s, :, :].reshape(FFT_N2 * 2 * FFT_J, ct))
    z = z_ref[0, :, s, :, :].astype(F32).reshape(rows, ct)
    gate = gate_ref[0, :, s, :, :].astype(F32).reshape(rows, ct)
    return gate * (r + z * skip)


def _is_fs_kernel(c_ref, gi_ref, z_ref, gate_ref, skip_ref, gf_ref, zz_ref, o_ref):
    nh = FFT_N2 // 2
    ct = o_ref.shape[-1]
    skip = skip_ref[...]
    for s in range(FFT_SUB):
        zz = _gated_inverse(c_ref, gi_ref, z_ref, gate_ref, skip, s).astype(BF16)
        zz_ref[0, :, s, :, :] = zz.reshape(nh, 2 * FFT_J, ct)
        o_ref[0, :, s, :, :] = _small_fwd_rows(gf_ref, s, zz)


def _is_last_kernel(c_ref, gi_ref, z_ref, gate_ref, skip_ref, y_ref):
    nh = FFT_N2 // 2
    ct = y_ref.shape[-1]
    skip = skip_ref[...]
    for s in range(FFT_SUB):
        y = _gated_inverse(c_ref, gi_ref, z_ref, gate_ref, skip, s).reshape(nh, 2, FFT_J, ct)
        for p in range(2):
            y_ref[p, 0, :, FFT_J * s:FFT_J * (s + 1), :] = y[:, p]


def _fft_small_inv(cs, g_inv, zil, z_off, gil, gate_off, skip, g_fwd=None):
    ncb, n2, ng, _, c = cs.shape
    nh = n2 // 2
    ct = MXU_DIM
    zb = z_off // ct
    gb = gate_off // ct
    til = lambda off: pl.BlockSpec((1, nh, FFT_SUB, 2 * FFT_J, ct), lambda q, b, ci: (b, 0, q, 0, off + ci))
    freq = pl.BlockSpec((1, n2, FFT_SUB, 2 * FFT_J, ct), lambda q, b, ci: (b, 0, q, 0, ci))
    mat = lambda m: pl.BlockSpec((FFT_SUB,) + m.shape[1:], lambda q, b, ci: (q, 0, 0))
    ins = [freq, mat(g_inv), til(zb), til(gb), pl.BlockSpec((1, ct), lambda q, b, ci: (0, ci))]
    args = [cs, g_inv, zil, gil, skip]
    if g_fwd is not None:
        body = _is_fs_kernel
        ins.append(mat(g_fwd))
        args.append(g_fwd)
        out_specs = [til(0), freq]
        out_shape = [jax.ShapeDtypeStruct((ncb, nh, ng, 2 * FFT_J, c), BF16), jax.ShapeDtypeStruct(cs.shape, BF16)]
    else:
        body = _is_last_kernel
        out_specs = pl.BlockSpec((2, 1, nh, FFT_CHUNK, ct), lambda q, b, ci: (0, b, 0, q, ci))
        out_shape = jax.ShapeDtypeStruct((2, ncb, nh, ng * FFT_J, c), F32)
    return pl.pallas_call(
        body,
        grid=(ng // FFT_SUB, ncb, c // ct),
        in_specs=ins,
        out_specs=out_specs,
        out_shape=out_shape,
        compiler_params=_params("parallel", "parallel", "arbitrary"),
        name="fft_small_inv",
    )(*args)


def _norm_mm_kernel(x_ref, g_ref, w_ref, o_ref):
    o_ref[...] = _dot(_rms(x_ref[...], g_ref[...]).astype(BF16), w_ref[...]).astype(o_ref.dtype)


def _norm_mm(x2d, g, w):
    n, d = x2d.shape
    tm = 512
    return pl.pallas_call(
        _norm_mm_kernel,
        grid=(n // tm,),
        in_specs=[
            pl.BlockSpec((tm, d), lambda i: (i, 0)),
            pl.BlockSpec((1, d), lambda i: (0, 0)),
            pl.BlockSpec(w.shape, lambda i: (0, 0)),
        ],
        out_specs=pl.BlockSpec((tm, w.shape[1]), lambda i: (i, 0)),
        out_shape=jax.ShapeDtypeStruct((n, w.shape[1]), BF16),
        compiler_params=_params("parallel"),
        name="memory_kv",
    )(x2d, g, w)


def _mix_xattn_kernel(x_ref, a_ref, y_ref, ga_ref, gy_ref, wo_ref, gx_ref, wq_ref, kv_ref, wxo_ref, o_ref):
    an = _rms(a_ref[0].astype(F32), ga_ref[...]).astype(BF16)
    yn = _rms(y_ref[0], gy_ref[...]).astype(BF16)
    x1 = x_ref[0] + _dot(an, wo_ref[:ATTN_WIDTH, :]) + _dot(yn, wo_ref[ATTN_WIDTH:, :])
    h = _rms(x1, gx_ref[...]).astype(BF16)
    q = (_dot(h, wq_ref[...]) * (X_HEAD_DIM ** -0.5)).astype(BF16)
    outs = []
    for hh in range(X_HEADS):
        cols = slice(hh * X_HEAD_DIM, (hh + 1) * X_HEAD_DIM)
        kh = kv_ref[0, :, cols]
        vh = kv_ref[0, :, X_WIDTH + hh * X_HEAD_DIM:X_WIDTH + (hh + 1) * X_HEAD_DIM]
        s = _dot_t(q[:, cols], kh)
        p = jnp.exp(s - jnp.max(s, axis=-1, keepdims=True))
        den = jnp.sum(p, axis=-1, keepdims=True)
        outs.append((_dot(p.astype(BF16), vh) / den).astype(BF16))
    o_ref[0] = x1 + _dot(jnp.concatenate(outs, axis=1), wxo_ref[...])


def _mix_xattn(x, a, y, ga, gy, wo, gx, wq, kv, wxo):
    b, s, d = x.shape
    tq = 512
    m = kv.shape[1]
    full = lambda arr: pl.BlockSpec(arr.shape, lambda bi, i: (0,) * arr.ndim)
    tok = lambda w: pl.BlockSpec((1, tq, w), lambda bi, i: (bi, i, 0))
    return pl.pallas_call(
        _mix_xattn_kernel,
        grid=(b, s // tq),
        in_specs=[tok(d), tok(ATTN_WIDTH), tok(HY_WIDTH), full(ga), full(gy), full(wo), full(gx), full(wq),
                  pl.BlockSpec((1, m, 2 * X_WIDTH), lambda bi, i: (bi, 0, 0)), full(wxo)],
        out_specs=tok(d),
        out_shape=jax.ShapeDtypeStruct((b, s, d), F32),
        compiler_params=_params("parallel", "arbitrary"),
        name="mix_xattn",
    )(x, a, y, ga, gy, wo, gx, wq, kv, wxo)


def _swiglu_chunks(h, wg, wu, wd, width, acc):
    for c in range(width // MXU_DIM):
        cols = slice(c * MXU_DIM, (c + 1) * MXU_DIM)
        g = _dot(h, wg(cols))
        u = _dot(h, wu(cols))
        a = (g * (1.0 / (1.0 + jnp.exp(-g))) * u).astype(BF16)
        acc = acc + _dot(a, wd(cols))
    return acc


def _ffn_kernel(x_ref, g_ref, wg_ref, wu_ref, wd_ref, o_ref):
    x = x_ref[...]
    h = _rms(x, g_ref[...]).astype(BF16)
    o_ref[...] = _swiglu_chunks(h, lambda c: wg_ref[:, c], lambda c: wu_ref[:, c], lambda c: wd_ref[c, :],
                                wg_ref.shape[1], x)


def _dense_ffn(x2d, g, wg, wu, wd):
    n, d = x2d.shape
    tm = 512
    resident = lambda w: pl.BlockSpec(w.shape, lambda i: (0, 0), pipeline_mode=pl.Buffered(1))
    return pl.pallas_call(
        _ffn_kernel,
        grid=(n // tm,),
        in_specs=[
            pl.BlockSpec((tm, d), lambda i: (i, 0)),
            pl.BlockSpec((1, d), lambda i: (0, 0)),
            resident(wg), resident(wu), resident(wd),
        ],
        out_specs=pl.BlockSpec((tm, d), lambda i: (i, 0)),
        out_shape=jax.ShapeDtypeStruct((n, d), F32),
        compiler_params=_params("parallel"),
        name="dense_ffn",
    )(x2d, g, wg, wu, wd)


def _router_kernel(x_ref, g_ref, wr_ref, h_ref, r_ref):
    hf = _rms(x_ref[...], g_ref[...])
    hb = hf.astype(BF16)
    h_ref[...] = hb
    h_lo = (hf - hb.astype(F32)).astype(BF16)
    parts = _dot(hb, wr_ref[...]) + _dot(h_lo, wr_ref[...])
    logits = parts[:, :LANES] + parts[:, LANES:]
    lane = lax.broadcasted_iota(jnp.int32, logits.shape, 1)
    logits = jnp.where(lane < N_EXPERTS, logits, NEG)
    lanef = lane.astype(F32)
    big = float(LANES)
    m1 = jnp.max(logits, axis=-1, keepdims=True)
    i1 = jnp.min(jnp.where(logits == m1, lanef, big), axis=-1, keepdims=True)
    rest = jnp.where(lanef == i1, NEG, logits)
    m2 = jnp.max(rest, axis=-1, keepdims=True)
    i2 = jnp.min(jnp.where(rest == m2, lanef, big), axis=-1, keepdims=True)
    e2 = jnp.exp(m2 - m1)
    w1 = 1.0 / (1.0 + e2)
    w2 = e2 / (1.0 + e2)
    r_ref[...] = jnp.where(lane == 0, i1,
                           jnp.where(lane == 1, i2, jnp.where(lane == 2, w1, jnp.where(lane == 3, w2, 0.0))))


def _router(x2d, g, wr_pad):
    n, d = x2d.shape
    tm = 512
    return pl.pallas_call(
        _router_kernel,
        grid=(n // tm,),
        in_specs=[
            pl.BlockSpec((tm, d), lambda i: (i, 0)),
            pl.BlockSpec((1, d), lambda i: (0, 0)),
            pl.BlockSpec(wr_pad.shape, lambda i: (0, 0)),
        ],
        out_specs=[pl.BlockSpec((tm, d), lambda i: (i, 0)), pl.BlockSpec((tm, LANES), lambda i: (i, 0))],
        out_shape=[jax.ShapeDtypeStruct((n, d), BF16), jax.ShapeDtypeStruct((n, LANES), F32)],
        compiler_params=_params("parallel"),
        name="router",
    )(x2d, g, wr_pad)


def _moe_kernel(be_ref, x_ref, wg_ref, wu_ref, wd_ref, o_ref, acc_ref, *, n_blk):
    f = pl.program_id(1)

    @pl.when(f == 0)
    def _():
        acc_ref[...] = jnp.zeros_like(acc_ref)

    @pl.when(pl.program_id(0) < be_ref[n_blk])
    def _():
        acc_ref[...] = _swiglu_chunks(x_ref[...], lambda c: wg_ref[0, :, c], lambda c: wu_ref[0, :, c],
                                      lambda c: wd_ref[0, c, :], wg_ref.shape[2], acc_ref[...])

    @pl.when(f == pl.num_programs(1) - 1)
    def _():
        o_ref[...] = acc_ref[...].astype(BF16)


def _moe_experts(blk_meta, xb, wg, wu, wd):
    rows, d = xb.shape
    ff = wg.shape[2]
    tf = ff // 2
    nf = ff // tf
    n_blk = rows // MOE_ROWS
    ftile = lambda i, f, be: jnp.where(i < be[n_blk], f, nf - 1)
    grid_spec = pltpu.PrefetchScalarGridSpec(
        num_scalar_prefetch=1,
        grid=(n_blk, nf),
        in_specs=[
            pl.BlockSpec((MOE_ROWS, d), lambda i, f, be: (i, 0)),
            pl.BlockSpec((1, d, tf), lambda i, f, be: (be[i], 0, ftile(i, f, be))),
            pl.BlockSpec((1, d, tf), lambda i, f, be: (be[i], 0, ftile(i, f, be))),
            pl.BlockSpec((1, tf, d), lambda i, f, be: (be[i], ftile(i, f, be), 0)),
        ],
        out_specs=pl.BlockSpec((MOE_ROWS, d), lambda i, f, be: (i, 0)),
        scratch_shapes=[pltpu.VMEM((MOE_ROWS, d), F32)],
    )
    return pl.pallas_call(
        functools.partial(_moe_kernel, n_blk=n_blk),
        grid_spec=grid_spec,
        out_shape=jax.ShapeDtypeStruct((rows, d), BF16),
        compiler_params=_params("parallel", "arbitrary"),
        name="moe_experts",
    )(blk_meta, xb, wg, wu, wd)


def _combine_kernel(*refs, normed):
    if normed:
        x_ref, y1_ref, y2_ref, r_ref, g_ref, o_ref = refs
    else:
        x_ref, y1_ref, y2_ref, r_ref, o_ref = refs
    w1 = r_ref[:, 2:3]
    w2 = r_ref[:, 3:4]
    x = x_ref[...] + y1_ref[...].astype(F32) * w1 + y2_ref[...].astype(F32) * w2
    o_ref[...] = _rms(x, g_ref[...]) if normed else x


def _combine(x2d, y1, y2, route, gain):
    n, d = x2d.shape
    tm = 512
    normed = gain is not None
    tok = lambda w: pl.BlockSpec((tm, w), lambda i: (i, 0))
    ins = [tok(d), tok(d), tok(d), tok(LANES)]
    args = [x2d, y1, y2, route]
    if normed:
        ins.append(pl.BlockSpec((1, d), lambda i: (0, 0)))
        args.append(gain)
    return pl.pallas_call(
        functools.partial(_combine_kernel, normed=normed),
        grid=(n // tm,),
        in_specs=ins,
        out_specs=tok(d),
        out_shape=jax.ShapeDtypeStruct((n, d), F32),
        compiler_params=_params("parallel"),
        name="moe_combine",
    )(*args)


def _norm_kernel(x_ref, g_ref, o_ref):
    o_ref[...] = _rms(x_ref[...], g_ref[...])


def _final_norm(x2d, g):
    n, d = x2d.shape
    tm = 512
    return pl.pallas_call(
        _norm_kernel,
        grid=(n // tm,),
        in_specs=[pl.BlockSpec((tm, d), lambda i: (i, 0)), pl.BlockSpec((1, d), lambda i: (0, 0))],
        out_specs=pl.BlockSpec((tm, d), lambda i: (i, 0)),
        out_shape=jax.ShapeDtypeStruct((n, d), F32),
        compiler_params=_params("parallel"),
        name="final_norm",
    )(x2d, g)


def _hyena(u, conv_w, conv_b, filt, skip, tables):
    g_fwd, g_inv, fb, fbi = tables
    b, L, c3 = u.shape
    c = c3 // (HY_ORDER + 1)
    uc = _shortconv(u, conv_w, conv_b.reshape(1, c3))
    spec = _filter_spectrum(_fft_small_fwd(filt, g_fwd, c_off=0, n_c=c), fb)
    t = _fft_small_fwd(uc, g_fwd, c_off=2 * c, n_c=c)
    t = _fft_big(t, spec, 0, fb, fbi)
    zz, t = _fft_small_inv(t, g_inv, uc, 2 * c, uc, 0, skip[0:1], g_fwd)
    t = _fft_big(t, spec, 1, fb, fbi)
    y5 = _fft_small_inv(t, g_inv, zz, 0, uc, c, skip[1:2])
    return y5.reshape(b, L, c)


def _moe_dispatch(route, n_tok):
    n_asg = n_tok * TOP_K
    flat_e = route[:, :TOP_K].astype(jnp.int32).reshape(-1)
    onehot = (flat_e[:, None] == jnp.arange(N_EXPERTS, dtype=jnp.int32)[None, :]).astype(jnp.int32)
    csum = jnp.cumsum(onehot, axis=0)
    counts = csum[-1]
    rank = jnp.sum((csum - 1) * onehot, axis=1)
    padded = (counts + MOE_ROWS - 1) // MOE_ROWS * MOE_ROWS
    pad_end = jnp.cumsum(padded)
    pad_start = pad_end - padded
    dest = pad_start[flat_e] + rank
    seg_start = jnp.cumsum(counts) - counts
    n_blk = -(-(n_asg + N_EXPERTS * (MOE_ROWS - 1)) // MOE_ROWS)
    blk_start = jnp.arange(n_blk, dtype=jnp.int32) * MOE_ROWS
    blk_e = jnp.minimum(jnp.sum((blk_start[:, None] >= pad_end[None, :]).astype(jnp.int32), axis=1), N_EXPERTS - 1)
    order = jnp.argsort(flat_e, stable=True).astype(jnp.int32)
    e_row = jnp.repeat(blk_e, MOE_ROWS)
    r = jnp.arange(n_blk * MOE_ROWS, dtype=jnp.int32) - pad_start[e_row]
    src = order[jnp.clip(seg_start[e_row] + r, 0, n_asg - 1)]
    buf_tok = jnp.where(r < counts[e_row], src // TOP_K, 0)
    blk_meta = jnp.concatenate([blk_e.astype(jnp.int32), (pad_end[-1:] // MOE_ROWS).astype(jnp.int32)])
    return buf_tok, blk_meta, dest.reshape(n_tok, TOP_K)


def kernel(x, mem, mem_norm, mix_norm, w_in, attn_sink, hy_conv_w, hy_conv_b, hy_f_w1, hy_f_b1, hy_f_freq1,
           hy_f_w2, hy_f_b2, hy_f_freq2, hy_f_w3, hy_skip, attn_out_norm, hy_out_norm, w_out, xattn_norm,
           xw_q, xw_k, xw_v, xw_o, ffn_norm, ffn_w_gate, ffn_w_up, ffn_w_down,
           moe_router, moe_w_gate, moe_w_up, moe_w_down, final_norm):
    b, s, d = x.shape
    depth = w_in.shape[0]
    n_tok = b * s
    assert s == FFT_N // 2 and b % 2 == 0
    row = lambda v: v.reshape(1, -1).astype(F32)

    pos = jnp.arange(s, dtype=F32)
    inv = ROPE_THETA ** (-jnp.arange(0, HEAD_DIM, 2, dtype=F32) / HEAD_DIM)
    ang = pos[:, None] * inv[None, :]
    cosf = jnp.tile(jnp.cos(ang), (1, LANES // (HEAD_DIM // 2)))
    sins = jnp.tile(jnp.concatenate([-jnp.sin(ang), jnp.sin(ang)], axis=1), (1, LANES // HEAD_DIM))
    t_pos = jnp.linspace(0.0, 1.0, s, dtype=F32)[:, None]
    wv = 2.0 * math.pi * jnp.arange(s, dtype=F32)[:, None] / s
    fr = jnp.linspace(1e-4, HY_BANDS - 1, HY_BANDS, dtype=F32)[None, :]
    z_pos = jnp.concatenate([t_pos, jnp.cos(fr * wv), -jnp.sin(fr * wv)], axis=-1)
    emb_pad = 64
    z_pos = jnp.pad(z_pos, ((0, 0), (0, emb_pad - HY_EMB)))
    absd = jnp.abs(jnp.linspace(HY_MIN_DECAY, HY_MAX_DECAY, HY_ORDER * HY_WIDTH, dtype=F32)).reshape(1, -1)
    tables = _fft_tables()

    mem2d = mem.reshape(-1, d)
    x2d = x.reshape(n_tok, d)
    out = None
    for l in range(depth):
        q, kx, vx, u = _inproj(x2d, row(mix_norm[l]), w_in[l].astype(BF16), cosf, sins, s)
        a = _band_attention(q.reshape(b, s, ATTN_WIDTH), kx, vx.reshape(b, s, -1), attn_sink[l].astype(F32), s)
        filt = _hyena_filters(
            z_pos, t_pos, jnp.pad(hy_f_w1[l], ((0, emb_pad - HY_EMB), (0, 0))), row(hy_f_b1[l]),
            row(hy_f_freq1[l]), hy_f_w2[l], row(hy_f_b2[l]), row(hy_f_freq2[l]), hy_f_w3[l], absd)
        y = _hyena(u.reshape(b, s, -1), hy_conv_w[l], hy_conv_b[l], filt, hy_skip[l], tables)
        mkv = _norm_mm(mem2d, row(mem_norm), jnp.concatenate([xw_k[l], xw_v[l]], axis=1).astype(BF16))
        x3 = _mix_xattn(x2d.reshape(b, s, d), a, y, row(attn_out_norm[l]), row(hy_out_norm[l]),
                        w_out[l].astype(BF16), row(xattn_norm[l]), xw_q[l].astype(BF16),
                        mkv.reshape(b, -1, 2 * X_WIDTH), xw_o[l].astype(BF16))
        x2d = x3.reshape(n_tok, d)
        j = l // 2
        last = l == depth - 1
        if l % 2 == 0:
            x2d = _dense_ffn(x2d, row(ffn_norm[l]), ffn_w_gate[j].astype(BF16), ffn_w_up[j].astype(BF16),
                             ffn_w_down[j].astype(BF16))
            if last:
                out = _final_norm(x2d, row(final_norm))
        else:
            wr = jnp.pad(moe_router[j].astype(F32), ((0, 0), (0, LANES - N_EXPERTS)))
            wr_hi = wr.astype(BF16)
            wr = jnp.concatenate([wr_hi, (wr - wr_hi.astype(F32)).astype(BF16)], axis=1)
            hb, route = _router(x2d, row(ffn_norm[l]), wr)
            buf_tok, blk_e, dest = _moe_dispatch(route, n_tok)
            yb = _moe_experts(blk_e, hb[buf_tok], moe_w_gate[j].astype(BF16), moe_w_up[j].astype(BF16),
                              moe_w_down[j].astype(BF16))
            res = _combine(x2d, yb[dest[:, 0]], yb[dest[:, 1]], route, row(final_norm) if last else None)
            if last:
                out = res
            else:
                x2d = res
    return out.reshape(b, s, d)
```

```python
import functools
import math

import jax
import jax.numpy as jnp
from jax import lax
from jax.experimental import pallas as pl
from jax.experimental.pallas import tpu as pltpu

F32 = jnp.float32
BF16 = jnp.bfloat16

EPS = 1e-6
N_HEADS = 8
N_KV_HEADS = 2
HEAD_DIM = 64
ATTN_WIDTH = N_HEADS * HEAD_DIM
KV_WIDTH = N_KV_HEADS * HEAD_DIM
WINDOW = 128
ROPE_THETA = 10000.0
HY_WIDTH = 512
HY_ORDER = 2
HY_EMB = 33
HY_BANDS = (HY_EMB - 1) // 2
HY_FILTER_HIDDEN = 64
HY_TARGET = 1e-2
HY_FAST_DECAY = 0.3
HY_SLOW_DECAY = 1.5
HY_MIN_DECAY = math.log(HY_TARGET) / HY_SLOW_DECAY
HY_MAX_DECAY = math.log(HY_TARGET) / HY_FAST_DECAY
Q_END = ATTN_WIDTH
K_END = Q_END + KV_WIDTH
V_END = K_END + KV_WIDTH
X_HEADS = 4
X_HEAD_DIM = 128
X_WIDTH = X_HEADS * X_HEAD_DIM
N_EXPERTS = 8
TOP_K = 2

LANES = 128
SUBLANES = 8
MXU_DIM = 256
VMEM_LIMIT = 56 * 1024 * 1024

FFT_N1 = 256
FFT_N2 = 32
FFT_N = FFT_N1 * FFT_N2
FFT_J = SUBLANES
FFT_GROUPS = FFT_N1 // FFT_J
FFT_CHUNK = 64
FFT_SUB = FFT_CHUNK // FFT_J

MOE_ROWS = 1024
NEG = float(jnp.finfo(jnp.float32).min)
MASKED = -1e30
LOG2E = math.log2(math.e)


def _params(*sem):
    return pltpu.CompilerParams(dimension_semantics=sem, vmem_limit_bytes=VMEM_LIMIT)


def _dot(a, b):
    return jnp.dot(a, b, preferred_element_type=F32)


def _dot_t(a, b):
    return lax.dot_general(a, b, (((1,), (1,)), ((), ())), preferred_element_type=F32)


def _rms(xf, g):
    ms = jnp.mean(xf * xf, axis=-1, keepdims=True)
    return xf * lax.rsqrt(ms + EPS) * g


CAST_BLOCK_BYTES = 4 * 1024 * 1024


def _cast_kernel(w_ref, o_ref):
    o_ref[...] = w_ref[...].astype(BF16)


def _to_bf16(w):
    cols = w.shape[-1]
    w2 = w.reshape(-1, cols)
    rows = w2.shape[0]
    tile = 2 * SUBLANES
    br = rows
    if rows % tile == 0:
        br = max(t for t in range(tile, rows + 1, tile)
                 if rows % t == 0 and t * cols * 4 <= max(CAST_BLOCK_BYTES, tile * cols * 4))
    out = pl.pallas_call(
        _cast_kernel,
        grid=(rows // br,),
        in_specs=[pl.BlockSpec((br, cols), lambda i: (i, 0))],
        out_specs=pl.BlockSpec((br, cols), lambda i: (i, 0)),
        out_shape=jax.ShapeDtypeStruct((rows, cols), BF16),
        compiler_params=_params("parallel"),
        name="cast_bf16",
    )(w2)
    return out.reshape(w.shape)


def _inproj_kernel(x_ref, g_ref, w_ref, cos_ref, sin_ref, q_ref, kx_ref, vx_ref, u_ref):
    h = _rms(x_ref[...], g_ref[...]).astype(BF16)
    cosf = cos_ref[...]
    sins = sin_ref[...]
    lane = lax.broadcasted_iota(jnp.int32, cosf.shape, 1)
    low = (lane % HEAD_DIM) < (HEAD_DIM // 2)

    def rope(c):
        rot = jnp.where(low, pltpu.roll(c, LANES - HEAD_DIM // 2, 1), pltpu.roll(c, HEAD_DIM // 2, 1))
        return c * cosf + rot * sins

    scale = HEAD_DIM ** -0.5 * LOG2E
    for j in range(ATTN_WIDTH // MXU_DIM):
        qc = _dot(h, w_ref[:, j * MXU_DIM:(j + 1) * MXU_DIM])
        for t in range(MXU_DIM // LANES):
            c0 = j * MXU_DIM + t * LANES
            q_ref[:, c0:c0 + LANES] = (rope(qc[:, t * LANES:(t + 1) * LANES]) * scale).astype(BF16)
    kvc = _dot(h, w_ref[:, Q_END:V_END])
    kt = rope(kvc[:, :KV_WIDTH]).T
    trow = lax.broadcasted_iota(jnp.int32, kt.shape, 0)
    top = jnp.where(trow < HEAD_DIM, kt, 0.0)
    bot = jnp.where(trow >= HEAD_DIM, kt, 0.0)
    kx_ref[0, 0] = top.astype(BF16)
    kx_ref[0, 1] = pltpu.roll(top, HEAD_DIM, 0).astype(BF16)
    kx_ref[0, 2] = pltpu.roll(bot, HEAD_DIM, 0).astype(BF16)
    kx_ref[0, 3] = bot.astype(BF16)
    vc = kvc[:, KV_WIDTH:]
    vsw = pltpu.roll(vc, HEAD_DIM, 1)
    lo = lane < HEAD_DIM
    vx_ref[:, 0 * LANES:1 * LANES] = jnp.where(lo, vc, 1.0).astype(BF16)
    vx_ref[:, 1 * LANES:2 * LANES] = jnp.where(lo, 1.0, vsw).astype(BF16)
    vx_ref[:, 2 * LANES:3 * LANES] = jnp.where(lo, vsw, 1.0).astype(BF16)
    vx_ref[:, 3 * LANES:4 * LANES] = jnp.where(lo, 1.0, vc).astype(BF16)
    n_u = u_ref.shape[1]
    for j in range(n_u // 512):
        u_ref[:, j * 512:(j + 1) * 512] = _dot(h, w_ref[:, V_END + j * 512:V_END + (j + 1) * 512]).astype(BF16)


def _inproj(x2d, g, w, cosf, sins, seq):
    n, d = x2d.shape
    tm = 512
    n_u = w.shape[1] - V_END
    spb = seq // tm
    return pl.pallas_call(
        _inproj_kernel,
        grid=(n // tm,),
        in_specs=[
            pl.BlockSpec((tm, d), lambda i: (i, 0)),
            pl.BlockSpec((1, d), lambda i: (0, 0)),
            pl.BlockSpec(w.shape, lambda i: (0, 0)),
            pl.BlockSpec((tm, LANES), lambda i: (i % spb, 0)),
            pl.BlockSpec((tm, LANES), lambda i: (i % spb, 0)),
        ],
        out_specs=[
            pl.BlockSpec((tm, ATTN_WIDTH), lambda i: (i, 0)),
            pl.BlockSpec((1, 2 * N_KV_HEADS, LANES, tm), lambda i: (i // spb, 0, 0, i % spb)),
            pl.BlockSpec((tm, 2 * N_KV_HEADS * LANES), lambda i: (i, 0)),
            pl.BlockSpec((tm, n_u), lambda i: (i, 0)),
        ],
        out_shape=[
            jax.ShapeDtypeStruct((n, ATTN_WIDTH), BF16),
            jax.ShapeDtypeStruct((n // seq, 2 * N_KV_HEADS, LANES, seq), BF16),
            jax.ShapeDtypeStruct((n, 2 * N_KV_HEADS * LANES), BF16),
            jax.ShapeDtypeStruct((n, n_u), BF16),
        ],
        compiler_params=_params("parallel"),
        name="inproj",
    )(x2d, g, w, cosf, sins)


def _battn_kernel(sink_ref, q_ref, kx_ref, vx_ref, o_ref, *, seq, tq):
    i = pl.program_id(1)
    blk = WINDOW
    n_blk = seq // blk
    row = lax.broadcasted_iota(jnp.int32, (blk, blk), 0)
    col = lax.broadcasted_iota(jnp.int32, (blk, blk), 1)
    tri_prev = jnp.where(col >= row, 0.0, MASKED)
    tri_next = jnp.where(col <= row, 0.0, MASKED)
    lo = col < HEAD_DIM
    heads_per_pair = LANES // HEAD_DIM
    group_pairs = N_HEADS // N_KV_HEADS // heads_per_pair
    units = [(jb, hp) for jb in range(tq // blk) for hp in range(N_HEADS // heads_per_pair)]

    def window(jb):
        bi = i * (tq // blk) + jb
        starts = [jnp.maximum(bi - 1, 0), bi, jnp.minimum(bi + 1, n_blk - 1)]
        return bi, [pl.multiple_of(s * blk, blk) for s in starts]

    def scores(jb, hp):
        _, starts = window(jb)
        qp = q_ref[0, jb * blk:(jb + 1) * blk, hp * LANES:(hp + 1) * LANES]
        out = []
        for t in range(heads_per_pair):
            var = heads_per_pair * (hp // group_pairs) + t
            kwin = jnp.concatenate([kx_ref[0, var, :, pl.ds(s, blk)] for s in starts], axis=1)
            out.append(_dot(qp, kwin))
        return out

    def shifted(jb, hp, s_pair):
        bi, _ = window(jb)
        b_prev = tri_prev + jnp.where(bi == 0, MASKED, 0.0)
        b_next = tri_next + jnp.where(bi == n_blk - 1, MASKED, 0.0)
        out = []
        for t in range(heads_per_pair):
            s = s_pair[t]
            s0 = s[:, :blk] + b_prev
            s1 = s[:, blk:2 * blk]
            s2 = s[:, 2 * blk:] + b_next
            sk = sink_ref[hp * heads_per_pair + t] * LOG2E
            m = jnp.maximum(jnp.max(jnp.maximum(jnp.maximum(s0, s1), s2), axis=-1, keepdims=True), sk)
            x = jnp.concatenate([s0 - m, s1 - m, s2 - m], axis=1).astype(BF16)
            out.append((x, sk - m))
        return out

    def probs(x_pair):
        return [(jnp.exp2(x), jnp.exp2(d)) for x, d in x_pair]

    def finish(jb, hp, p_pair):
        _, starts = window(jb)
        res = []
        for t in range(heads_per_pair):
            var = heads_per_pair * (hp // group_pairs) + t
            vwin = jnp.concatenate([vx_ref[0, pl.ds(s, blk), var * LANES:(var + 1) * LANES] for s in starts], axis=0)
            res.append(_dot(p_pair[t][0], vwin))
        num = jnp.where(lo, res[0], res[1])
        den = pltpu.roll(jnp.where(lo, res[1], res[0]), HEAD_DIM, 1) + jnp.where(lo, p_pair[0][1], p_pair[1][1])
        o_ref[0, jb * blk:(jb + 1) * blk, hp * LANES:(hp + 1) * LANES] = (num * (1.0 / den)).astype(BF16)

    n_u = len(units)
    st_s, st_x, st_p = {}, {}, {}
    for n in range(n_u + 3):
        if 0 <= n - 3 < n_u:
            finish(*units[n - 3], st_p.pop(n - 3))
        if 0 <= n - 2 < n_u:
            st_p[n - 2] = probs(st_x.pop(n - 2))
        if 0 <= n - 1 < n_u:
            st_x[n - 1] = shifted(*units[n - 1], st_s.pop(n - 1))
        if n < n_u:
            st_s[n] = scores(*units[n])


def _band_attention(q, kx, vx, sink, seq):
    b = q.shape[0]
    tq = 512
    return pl.pallas_call(
        functools.partial(_battn_kernel, seq=seq, tq=tq),
        grid=(b, seq // tq),
        in_specs=[
            pl.BlockSpec(memory_space=pltpu.SMEM),
            pl.BlockSpec((1, tq, ATTN_WIDTH), lambda bi, i: (bi, i, 0)),
            pl.BlockSpec((1,) + kx.shape[1:], lambda bi, i: (bi, 0, 0, 0)),
            pl.BlockSpec((1,) + vx.shape[1:], lambda bi, i: (bi, 0, 0)),
        ],
        out_specs=pl.BlockSpec((1, tq, ATTN_WIDTH), lambda bi, i: (bi, i, 0)),
        out_shape=jax.ShapeDtypeStruct((b, seq, ATTN_WIDTH), BF16),
        compiler_params=_params("parallel", "arbitrary"),
        name="band_attention",
    )(sink, q, kx, vx)


def _filter_kernel(z_ref, t_ref, w1_ref, b1_ref, f1_ref, w2_ref, b2_ref, f2_ref, w3_ref, ad_ref, o_ref, *, tl):
    hp = lax.Precision.HIGHEST
    h = jnp.sin(f1_ref[...] * (jnp.dot(z_ref[...], w1_ref[...], precision=hp, preferred_element_type=F32)
                               + b1_ref[...]))
    h = jnp.sin(f2_ref[...] * (jnp.dot(h, w2_ref[...], precision=hp, preferred_element_type=F32) + b2_ref[...]))
    t = t_ref[...]
    rowid = pl.program_id(0) * tl + lax.broadcasted_iota(jnp.int32, (tl, HY_WIDTH), 0)
    for d in range(2):
        for o in range(HY_ORDER):
            c0 = (d * HY_ORDER + o) * HY_WIDTH
            v = jnp.dot(h, w3_ref[:, c0:c0 + HY_WIDTH], precision=hp, preferred_element_type=F32)
            v = v * jnp.exp(-t * ad_ref[:, o * HY_WIDTH:(o + 1) * HY_WIDTH])
            if d == 1:
                v = jnp.where(rowid == 0, 0.0, v)
            v = v.reshape(tl // FFT_N1, FFT_GROUPS, FFT_J, HY_WIDTH)
            o_ref[d * HY_ORDER + o] = jnp.concatenate([v, jnp.zeros_like(v)], axis=2).astype(BF16)


def _hyena_filters(z_pos, t_pos, w1, b1, f1, w2, b2, f2, w3, absd):
    L, e = z_pos.shape
    tl = 512
    full = lambda a: pl.BlockSpec(a.shape, lambda i: (0,) * a.ndim)
    return pl.pallas_call(
        functools.partial(_filter_kernel, tl=tl),
        grid=(L // tl,),
        in_specs=[
            pl.BlockSpec((tl, e), lambda i: (i, 0)),
            pl.BlockSpec((tl, 1), lambda i: (i, 0)),
            full(w1), full(b1), full(f1), full(w2), full(b2), full(f2), full(w3), full(absd),
        ],
        out_specs=pl.BlockSpec((2 * HY_ORDER, tl // FFT_N1, FFT_GROUPS, 2 * FFT_J, HY_WIDTH),
                               lambda i: (0, i, 0, 0, 0)),
        out_shape=jax.ShapeDtypeStruct((2 * HY_ORDER, L // FFT_N1, FFT_GROUPS, 2 * FFT_J, HY_WIDTH), BF16),
        compiler_params=_params("parallel"),
        name="hyena_filters",
    )(z_pos, t_pos, w1, b1, f1, w2, b2, f2, w3, absd)


def _shortconv_kernel(ua_ref, ub_ref, w_ref, b_ref, o_ref):
    def conv(u_ref):
        u = u_ref[0].astype(F32)
        L, ct = u.shape
        r8 = lax.broadcasted_iota(jnp.int32, (SUBLANES, ct), 0)
        down = pltpu.roll(u, 1, 0)
        up = pltpu.roll(u, L - 1, 0)
        prev = jnp.concatenate([jnp.where(r8 == 0, 0.0, down[:SUBLANES]), down[SUBLANES:]], axis=0)
        nxt = jnp.concatenate([up[:L - SUBLANES], jnp.where(r8 == SUBLANES - 1, 0.0, up[L - SUBLANES:])], axis=0)
        r = prev * w_ref[0:1, :] + u * w_ref[1:2, :] + nxt * w_ref[2:3, :] + b_ref[...]
        return r.reshape(L // FFT_N1, FFT_GROUPS, FFT_J, ct)

    o_ref[0] = jnp.concatenate([conv(ua_ref), conv(ub_ref)], axis=2).astype(BF16)


def _shortconv(u, w, bias):
    b, L, c3 = u.shape
    ct = MXU_DIM
    ncb = b // 2
    return pl.pallas_call(
        _shortconv_kernel,
        grid=(ncb, c3 // ct),
        in_specs=[
            pl.BlockSpec((1, L, ct), lambda bi, ci: (bi, 0, ci)),
            pl.BlockSpec((1, L, ct), lambda bi, ci: (bi + ncb, 0, ci)),
            pl.BlockSpec((3, ct), lambda bi, ci: (0, ci)),
            pl.BlockSpec((1, ct), lambda bi, ci: (0, ci)),
        ],
        out_specs=pl.BlockSpec((1, L // FFT_N1, FFT_GROUPS, 2 * FFT_J, ct), lambda bi, ci: (bi, 0, 0, 0, ci)),
        out_shape=jax.ShapeDtypeStruct((ncb, L // FFT_N1, FFT_GROUPS, 2 * FFT_J, c3), BF16),
        compiler_params=_params("parallel", "parallel"),
        name="shortconv",
    )(u, u, w, bias)


def _real_block(m, n, po, pi, sign, scale=1.0):
    ang = (2.0 * math.pi / n) * (m % n).astype(F32)
    re = jnp.cos(ang) * scale
    im = jnp.sin(ang) * (sign * scale)
    return jnp.where(po == pi, re, jnp.where(po > pi, im, -im))


def _fft_tables():
    nh = FFT_N2 // 2
    j2 = 2 * FFT_J

    def split(idx):
        return idx // j2, (idx // FFT_J) % 2, idx % FFT_J

    def small(n_out, n_in, sign, scale, out_is_freq):
        rows = FFT_GROUPS * n_out * j2
        a = lax.broadcasted_iota(jnp.int32, (n_in * 2, rows), 0)
        b = lax.broadcasted_iota(jnp.int32, (n_in * 2, rows), 1)
        major_out, po, j = split(b % (n_out * j2))
        g = b // (n_out * j2)
        major_in, pi = a // 2, a % 2
        k2, n2 = (major_out, major_in) if out_is_freq else (major_in, major_out)
        m = FFT_N1 * n2 * k2 + (FFT_J * g + j) * k2
        compact = _real_block(m, FFT_N, po, pi, sign, scale).astype(BF16)
        cols = n_in * j2
        rep = (lax.broadcasted_iota(jnp.int32, (n_in * 2, cols), 1) // FFT_J
               == lax.broadcasted_iota(jnp.int32, (n_in * 2, cols), 0)).astype(BF16)
        full = lax.dot_general(compact, rep, (((0,), (0,)), ((), ())), preferred_element_type=F32)
        diag = (lax.broadcasted_iota(jnp.int32, (rows, cols), 0) % FFT_J
                == lax.broadcasted_iota(jnp.int32, (rows, cols), 1) % FFT_J)
        return jnp.where(diag, full, 0.0).astype(BF16).reshape(FFT_GROUPS, n_out * j2, cols)

    g_fwd = small(FFT_N2, nh, -1.0, 1.0, True)
    g_inv = small(nh, FFT_N2, 1.0, 1.0 / FFT_N, False)

    def big(sign):
        r = lax.broadcasted_iota(jnp.int32, (2 * FFT_N1, 2 * FFT_N1), 0)
        c = lax.broadcasted_iota(jnp.int32, (2 * FFT_N1, 2 * FFT_N1), 1)
        gk, po, jk = split(r)
        g, pi, j = split(c)
        m = (FFT_J * gk + jk) * (FFT_J * g + j)
        return _real_block(m, FFT_N1, po, pi, sign).astype(BF16)

    return g_fwd, g_inv, big(-1.0), big(1.0)


def _small_fwd_rows(g_ref, s, tile):
    r = _dot(g_ref[s], tile)
    return r.reshape(FFT_N2, 2 * FFT_J, tile.shape[-1]).astype(BF16)


def _fs_kernel(z_ref, g_ref, o_ref):
    ct = o_ref.shape[-1]
    for s in range(FFT_SUB):
        tile = z_ref[0, :, s, :, :].reshape(FFT_N2 // 2 * 2 * FFT_J, ct)
        o_ref[0, :, s, :, :] = _small_fwd_rows(g_ref, s, tile)


def _fft_small_fwd(zil, g_fwd, *, c_off, n_c):
    ncb, nh, ng, _, _ = zil.shape
    ct = MXU_DIM
    cblk = c_off // ct
    return pl.pallas_call(
        _fs_kernel,
        grid=(ng // FFT_SUB, ncb, n_c // ct),
        in_specs=[
            pl.BlockSpec((1, nh, FFT_SUB, 2 * FFT_J, ct), lambda q, b, ci: (b, 0, q, 0, cblk + ci)),
            pl.BlockSpec((FFT_SUB,) + g_fwd.shape[1:], lambda q, b, ci: (q, 0, 0)),
        ],
        out_specs=pl.BlockSpec((1, FFT_N2, FFT_SUB, 2 * FFT_J, ct), lambda q, b, ci: (b, 0, q, 0, ci)),
        out_shape=jax.ShapeDtypeStruct((ncb, FFT_N2, ng, 2 * FFT_J, n_c), BF16),
        compiler_params=_params("parallel", "parallel", "arbitrary"),
        name="fft_small_fwd",
    )(zil, g_fwd)


def _big_kernel(x_ref, h_ref, fb_ref, fbi_ref, o_ref):
    ct = o_ref.shape[-1]
    h4 = h_ref[0, 0].reshape(FFT_GROUPS, 2, FFT_J, ct)
    hre = h4[:, 0]
    him = h4[:, 1]
    n_b = x_ref.shape[0]

    def forward(b):
        return _dot(fb_ref[...], x_ref[b, 0].reshape(2 * FFT_N1, ct)).reshape(FFT_GROUPS, 2, FFT_J, ct)

    xf = forward(0)
    for b in range(n_b):
        xr = xf[:, 0]
        xi = xf[:, 1]
        if b + 1 < n_b:
            xf = forward(b + 1)
        y = jnp.stack([xr * hre - xi * him, xr * him + xi * hre], axis=1).reshape(2 * FFT_N1, ct).astype(BF16)
        o_ref[b, 0] = _dot(fbi_ref[...], y).reshape(FFT_GROUPS, 2 * FFT_J, ct).astype(BF16)


def _fft_big(xs, spec, order, fb, fbi):
    ncb, n2, ng, _, c = xs.shape
    ct = MXU_DIM
    nbb = ncb
    return pl.pallas_call(
        _big_kernel,
        grid=(c // ct, n2, ncb // nbb),
        in_specs=[
            pl.BlockSpec((nbb, 1, ng, 2 * FFT_J, ct), lambda ci, k, b: (b, k, 0, 0, ci)),
            pl.BlockSpec((1, 1, ng, 2 * FFT_J, ct), lambda ci, k, b: (order, k, 0, 0, ci)),
            pl.BlockSpec(fb.shape, lambda ci, k, b: (0, 0)),
            pl.BlockSpec(fbi.shape, lambda ci, k, b: (0, 0)),
        ],
        out_specs=pl.BlockSpec((nbb, 1, ng, 2 * FFT_J, ct), lambda ci, k, b: (b, k, 0, 0, ci)),
        out_shape=jax.ShapeDtypeStruct(xs.shape, BF16),
        compiler_params=_params("parallel", "parallel", "arbitrary"),
        name="fft_big",
    )(xs, spec, fb, fbi)


def _spectrum_kernel(x_ref, fb_ref, o_ref):
    ct = o_ref.shape[-1]
    xf = [_dot(fb_ref[...], x_ref[i, 0].reshape(2 * FFT_N1, ct)).reshape(FFT_GROUPS, 2, FFT_J, ct)
          for i in range(2 * HY_ORDER)]
    for o in range(HY_ORDER):
        a = xf[o]
        r = xf[HY_ORDER + o]
        spec = jnp.stack([a[:, 0] + r[:, 0], a[:, 1] - r[:, 1]], axis=1)
        o_ref[o, 0] = spec.reshape(FFT_GROUPS, 2 * FFT_J, ct)


def _filter_spectrum(xs, fb):
    nf, n2, ng, _, c = xs.shape
    ct = MXU_DIM
    return pl.pallas_call(
        _spectrum_kernel,
        grid=(c // ct, n2),
        in_specs=[
            pl.BlockSpec((nf, 1, ng, 2 * FFT_J, ct), lambda ci, k: (0, k, 0, 0, ci)),
            pl.BlockSpec(fb.shape, lambda ci, k: (0, 0)),
        ],
        out_specs=pl.BlockSpec((HY_ORDER, 1, ng, 2 * FFT_J, ct), lambda ci, k: (0, k, 0, 0, ci)),
        out_shape=jax.ShapeDtypeStruct((HY_ORDER, n2, ng, 2 * FFT_J, c), F32),
        compiler_params=_params("parallel", "parallel"),
        name="filter_spectrum",
    )(xs, fb)


def _gated_inverse(c_ref, gi_ref, z_ref, gate_ref, skip, s):
    nh = FFT_N2 // 2
    ct = c_ref.shape[-1]
    rows = nh * 2 * FFT_J
    r = _dot(gi_ref[s], c_ref[0, :, s, :, :].reshape(FFT_N2 * 2 * FFT_J, ct))
    z = z_ref[0, :, s, :, :].astype(F32).reshape(rows, ct)
    gate = gate_ref[0, :, s, :, :].astype(F32).reshape(rows, ct)
    return gate * (r + z * skip)


def _is_fs_kernel(c_ref, gi_ref, z_ref, gate_ref, skip_ref, gf_ref, zz_ref, o_ref):
    nh = FFT_N2 // 2
    ct = o_ref.shape[-1]
    skip = skip_ref[...]
    for s in range(FFT_SUB):
        zz = _gated_inverse(c_ref, gi_ref, z_ref, gate_ref, skip, s).astype(BF16)
        zz_ref[0, :, s, :, :] = zz.reshape(nh, 2 * FFT_J, ct)
        o_ref[0, :, s, :, :] = _small_fwd_rows(gf_ref, s, zz)


def _is_last_kernel(c_ref, gi_ref, z_ref, gate_ref, skip_ref, y_ref):
    nh = FFT_N2 // 2
    ct = y_ref.shape[-1]
    skip = skip_ref[...]
    for s in range(FFT_SUB):
        y = _gated_inverse(c_ref, gi_ref, z_ref, gate_ref, skip, s).reshape(nh, 2, FFT_J, ct)
        for p in range(2):
            y_ref[p, 0, :, FFT_J * s:FFT_J * (s + 1), :] = y[:, p]


def _fft_small_inv(cs, g_inv, zil, z_off, gil, gate_off, skip, g_fwd=None):
    ncb, n2, ng, _, c = cs.shape
    nh = n2 // 2
    ct = MXU_DIM
    zb = z_off // ct
    gb = gate_off // ct
    til = lambda off: pl.BlockSpec((1, nh, FFT_SUB, 2 * FFT_J, ct), lambda q, b, ci: (b, 0, q, 0, off + ci))
    freq = pl.BlockSpec((1, n2, FFT_SUB, 2 * FFT_J, ct), lambda q, b, ci: (b, 0, q, 0, ci))
    mat = lambda m: pl.BlockSpec((FFT_SUB,) + m.shape[1:], lambda q, b, ci: (q, 0, 0))
    ins = [freq, mat(g_inv), til(zb), til(gb), pl.BlockSpec((1, ct), lambda q, b, ci: (0, ci))]
    args = [cs, g_inv, zil, gil, skip]
    if g_fwd is not None:
        body = _is_fs_kernel
        ins.append(mat(g_fwd))
        args.append(g_fwd)
        out_specs = [til(0), freq]
        out_shape = [jax.ShapeDtypeStruct((ncb, nh, ng, 2 * FFT_J, c), BF16), jax.ShapeDtypeStruct(cs.shape, BF16)]
    else:
        body = _is_last_kernel
        out_specs = pl.BlockSpec((2, 1, nh, FFT_CHUNK, ct), lambda q, b, ci: (0, b, 0, q, ci))
        out_shape = jax.ShapeDtypeStruct((2, ncb, nh, ng * FFT_J, c), F32)
    return pl.pallas_call(
        body,
        grid=(ng // FFT_SUB, ncb, c // ct),
        in_specs=ins,
        out_specs=out_specs,
        out_shape=out_shape,
        compiler_params=_params("parallel", "parallel", "arbitrary"),
        name="fft_small_inv",
    )(*args)


def _norm_mm_kernel(x_ref, g_ref, w_ref, o_ref):
    o_ref[...] = _dot(_rms(x_ref[...], g_ref[...]).astype(BF16), w_ref[...]).astype(o_ref.dtype)


def _norm_mm(x2d, g, w):
    n, d = x2d.shape
    tm = 512
    return pl.pallas_call(
        _norm_mm_kernel,
        grid=(n // tm,),
        in_specs=[
            pl.BlockSpec((tm, d), lambda i: (i, 0)),
            pl.BlockSpec((1, d), lambda i: (0, 0)),
            pl.BlockSpec(w.shape, lambda i: (0, 0)),
        ],
        out_specs=pl.BlockSpec((tm, w.shape[1]), lambda i: (i, 0)),
        out_shape=jax.ShapeDtypeStruct((n, w.shape[1]), BF16),
        compiler_params=_params("parallel"),
        name="memory_kv",
    )(x2d, g, w)


def _route(xf, g_ref, wr_ref, h_ref, r_ref, rt_ref):
    hf = _rms(xf, g_ref[...])
    hb = hf.astype(BF16)
    h_ref[...] = hb
    h_lo = (hf - hb.astype(F32)).astype(BF16)
    parts = _dot(hb, wr_ref[...]) + _dot(h_lo, wr_ref[...])
    logits = parts[:, :LANES] + parts[:, LANES:]
    lane = lax.broadcasted_iota(jnp.int32, logits.shape, 1)
    logits = jnp.where(lane < N_EXPERTS, logits, NEG)
    lanef = lane.astype(F32)
    big = float(LANES)
    m1 = jnp.max(logits, axis=-1, keepdims=True)
    i1 = jnp.min(jnp.where(logits == m1, lanef, big), axis=-1, keepdims=True)
    rest = jnp.where(lanef == i1, NEG, logits)
    m2 = jnp.max(rest, axis=-1, keepdims=True)
    i2 = jnp.min(jnp.where(rest == m2, lanef, big), axis=-1, keepdims=True)
    e2 = jnp.exp(m2 - m1)
    w1 = 1.0 / (1.0 + e2)
    w2 = e2 / (1.0 + e2)
    rec = jnp.where(lane == 0, i1, jnp.where(lane == 1, i2, jnp.where(lane == 2, w1, jnp.where(lane == 3, w2, 0.0))))
    r_ref[...] = rec
    rt_ref[...] = rec.T[:SUBLANES, :]


def _mix_xattn_kernel(x_ref, a_ref, y_ref, ga_ref, gy_ref, wo_ref, gx_ref, wq_ref, kv_ref, wxo_ref, o_ref):
    an = _rms(a_ref[0].astype(F32), ga_ref[...]).astype(BF16)
    yn = _rms(y_ref[0], gy_ref[...]).astype(BF16)
    x1 = x_ref[0] + _dot(an, wo_ref[:ATTN_WIDTH, :]) + _dot(yn, wo_ref[ATTN_WIDTH:, :])
    h = _rms(x1, gx_ref[...]).astype(BF16)
    q = (_dot(h, wq_ref[...]) * (X_HEAD_DIM ** -0.5)).astype(BF16)
    outs = []
    for hh in range(X_HEADS):
        cols = slice(hh * X_HEAD_DIM, (hh + 1) * X_HEAD_DIM)
        kh = kv_ref[0, :, cols]
        vh = kv_ref[0, :, X_WIDTH + hh * X_HEAD_DIM:X_WIDTH + (hh + 1) * X_HEAD_DIM]
        s = _dot_t(q[:, cols], kh)
        p = jnp.exp(s - jnp.max(s, axis=-1, keepdims=True))
        den = jnp.sum(p, axis=-1, keepdims=True)
        outs.append((_dot(p.astype(BF16), vh) / den).astype(BF16))
    o_ref[0] = x1 + _dot(jnp.concatenate(outs, axis=1), wxo_ref[...])


def _mix_xattn(x, a, y, ga, gy, wo, gx, wq, kv, wxo):
    b, s, d = x.shape
    tq = 512
    m = kv.shape[1]
    full = lambda arr: pl.BlockSpec(arr.shape, lambda bi, i: (0,) * arr.ndim)
    tok = lambda w: pl.BlockSpec((1, tq, w), lambda bi, i: (bi, i, 0))
    return pl.pallas_call(
        _mix_xattn_kernel,
        grid=(b, s // tq),
        in_specs=[tok(d), tok(ATTN_WIDTH), tok(HY_WIDTH), full(ga), full(gy), full(wo), full(gx), full(wq),
                  pl.BlockSpec((1, m, 2 * X_WIDTH), lambda bi, i: (bi, 0, 0)), full(wxo)],
        out_specs=tok(d),
        out_shape=jax.ShapeDtypeStruct((b, s, d), F32),
        compiler_params=_params("parallel", "arbitrary"),
        name="mix_xattn",
    )(x, a, y, ga, gy, wo, gx, wq, kv, wxo)


def _swiglu_chunks(h, wg, wu, wd, width, acc):
    for c in range(width // MXU_DIM):
        cols = slice(c * MXU_DIM, (c + 1) * MXU_DIM)
        g = _dot(h, wg(cols))
        u = _dot(h, wu(cols))
        a = (g * (1.0 / (1.0 + jnp.exp(-g))) * u).astype(BF16)
        acc = acc + _dot(a, wd(cols))
    return acc


def _ffn_kernel(x_ref, g_ref, wg_ref, wu_ref, wd_ref, o_ref):
    x = x_ref[...]
    h = _rms(x, g_ref[...]).astype(BF16)
    o_ref[...] = _swiglu_chunks(h, lambda c: wg_ref[:, c], lambda c: wu_ref[:, c], lambda c: wd_ref[c, :],
                                wg_ref.shape[1], x)


def _dense_ffn(x2d, g, wg, wu, wd):
    n, d = x2d.shape
    tm = 512
    resident = lambda w: pl.BlockSpec(w.shape, lambda i: (0, 0), pipeline_mode=pl.Buffered(1))
    return pl.pallas_call(
        _ffn_kernel,
        grid=(n // tm,),
        in_specs=[
            pl.BlockSpec((tm, d), lambda i: (i, 0)),
            pl.BlockSpec((1, d), lambda i: (0, 0)),
            resident(wg), resident(wu), resident(wd),
        ],
        out_specs=pl.BlockSpec((tm, d), lambda i: (i, 0)),
        out_shape=jax.ShapeDtypeStruct((n, d), F32),
        compiler_params=_params("parallel"),
        name="dense_ffn",
    )(x2d, g, wg, wu, wd)


def _router_kernel(x_ref, g_ref, wr_ref, h_ref, r_ref, rt_ref):
    _route(x_ref[...], g_ref, wr_ref, h_ref, r_ref, rt_ref)


def _router(x2d, g, wr):
    n, d = x2d.shape
    tm = 512
    return pl.pallas_call(
        _router_kernel,
        grid=(n // tm,),
        in_specs=[
            pl.BlockSpec((tm, d), lambda i: (i, 0)),
            pl.BlockSpec((1, d), lambda i: (0, 0)),
            pl.BlockSpec(wr.shape, lambda i: (0, 0)),
        ],
        out_specs=[pl.BlockSpec((tm, d), lambda i: (i, 0)), pl.BlockSpec((tm, LANES), lambda i: (i, 0)),
                   pl.BlockSpec((SUBLANES, tm), lambda i: (0, i))],
        out_shape=[jax.ShapeDtypeStruct((n, d), BF16), jax.ShapeDtypeStruct((n, LANES), F32),
                   jax.ShapeDtypeStruct((SUBLANES, n), F32)],
        compiler_params=_params("parallel"),
        name="router",
    )(x2d, g, wr)


def _moe_kernel(be_ref, x_ref, wg_ref, wu_ref, wd_ref, o_ref, acc_ref, *, n_blk):
    f = pl.program_id(1)

    @pl.when(f == 0)
    def _():
        acc_ref[...] = jnp.zeros_like(acc_ref)

    @pl.when(pl.program_id(0) < be_ref[n_blk])
    def _():
        acc_ref[...] = _swiglu_chunks(x_ref[...], lambda c: wg_ref[0, :, c], lambda c: wu_ref[0, :, c],
                                      lambda c: wd_ref[0, c, :], wg_ref.shape[2], acc_ref[...])

    @pl.when(f == pl.num_programs(1) - 1)
    def _():
        o_ref[...] = acc_ref[...].astype(BF16)


def _moe_experts(blk_meta, xb, wg, wu, wd):
    rows, d = xb.shape
    ff = wg.shape[2]
    tf = ff // 2
    nf = ff // tf
    n_blk = rows // MOE_ROWS
    ftile = lambda i, f, be: jnp.where(i < be[n_blk], f, nf - 1)
    grid_spec = pltpu.PrefetchScalarGridSpec(
        num_scalar_prefetch=1,
        grid=(n_blk, nf),
        in_specs=[
            pl.BlockSpec((MOE_ROWS, d), lambda i, f, be: (i, 0)),
            pl.BlockSpec((1, d, tf), lambda i, f, be: (be[i], 0, ftile(i, f, be))),
            pl.BlockSpec((1, d, tf), lambda i, f, be: (be[i], 0, ftile(i, f, be))),
            pl.BlockSpec((1, tf, d), lambda i, f, be: (be[i], ftile(i, f, be), 0)),
        ],
        out_specs=pl.BlockSpec((MOE_ROWS, d), lambda i, f, be: (i, 0)),
        scratch_shapes=[pltpu.VMEM((MOE_ROWS, d), F32)],
    )
    return pl.pallas_call(
        functools.partial(_moe_kernel, n_blk=n_blk),
        grid_spec=grid_spec,
        out_shape=jax.ShapeDtypeStruct((rows, d), BF16),
        compiler_params=_params("parallel", "arbitrary"),
        name="moe_experts",
    )(blk_meta, xb, wg, wu, wd)


def _combine_kernel(*refs, normed):
    if normed:
        x_ref, y1_ref, y2_ref, r_ref, g_ref, o_ref = refs
    else:
        x_ref, y1_ref, y2_ref, r_ref, o_ref = refs
    w1 = r_ref[:, 2:3]
    w2 = r_ref[:, 3:4]
    x = x_ref[...] + y1_ref[...].astype(F32) * w1 + y2_ref[...].astype(F32) * w2
    o_ref[...] = _rms(x, g_ref[...]) if normed else x


def _combine(x2d, y1, y2, route, gain):
    n, d = x2d.shape
    tm = 512
    normed = gain is not None
    tok = lambda w: pl.BlockSpec((tm, w), lambda i: (i, 0))
    ins = [tok(d), tok(d), tok(d), tok(LANES)]
    args = [x2d, y1, y2, route]
    if normed:
        ins.append(pl.BlockSpec((1, d), lambda i: (0, 0)))
        args.append(gain)
    return pl.pallas_call(
        functools.partial(_combine_kernel, normed=normed),
        grid=(n // tm,),
        in_specs=ins,
        out_specs=tok(d),
        out_shape=jax.ShapeDtypeStruct((n, d), F32),
        compiler_params=_params("parallel"),
        name="moe_combine",
    )(*args)


def _norm_kernel(x_ref, g_ref, o_ref):
    o_ref[...] = _rms(x_ref[...], g_ref[...])


def _final_norm(x2d, g):
    n, d = x2d.shape
    tm = 512
    return pl.pallas_call(
        _norm_kernel,
        grid=(n // tm,),
        in_specs=[pl.BlockSpec((tm, d), lambda i: (i, 0)), pl.BlockSpec((1, d), lambda i: (0, 0))],
        out_specs=pl.BlockSpec((tm, d), lambda i: (i, 0)),
        out_shape=jax.ShapeDtypeStruct((n, d), F32),
        compiler_params=_params("parallel"),
        name="final_norm",
    )(x2d, g)


def _hyena(u, conv_w, conv_b, filt, skip, tables):
    g_fwd, g_inv, fb, fbi = tables
    b, L, c3 = u.shape
    c = c3 // (HY_ORDER + 1)
    uc = _shortconv(u, conv_w, conv_b.reshape(1, c3))
    spec = _filter_spectrum(_fft_small_fwd(filt, g_fwd, c_off=0, n_c=c), fb)
    t = _fft_small_fwd(uc, g_fwd, c_off=2 * c, n_c=c)
    t = _fft_big(t, spec, 0, fb, fbi)
    zz, t = _fft_small_inv(t, g_inv, uc, 2 * c, uc, 0, skip[0:1], g_fwd)
    t = _fft_big(t, spec, 1, fb, fbi)
    y5 = _fft_small_inv(t, g_inv, zz, 0, uc, c, skip[1:2])
    return y5.reshape(b, L, c)


def _moe_dispatch(route_t, n_tok):
    n_asg = n_tok * TOP_K
    flat_e = route_t[:TOP_K].astype(jnp.int32).reshape(n_asg)
    experts = jnp.arange(N_EXPERTS, dtype=jnp.int32)[:, None]
    onehot = (flat_e[None, :] == experts).astype(jnp.int32)
    csum = jnp.cumsum(onehot, axis=1)
    counts = csum[:, -1]
    rank = jnp.sum(onehot * (csum - 1), axis=0)
    padded = (counts + MOE_ROWS - 1) // MOE_ROWS * MOE_ROWS
    pad_end = jnp.cumsum(padded)
    pad_start = pad_end - padded
    dest = jnp.sum(onehot * pad_start[:, None], axis=0) + rank
    seg_start = jnp.cumsum(counts) - counts
    n_blk = -(-(n_asg + N_EXPERTS * (MOE_ROWS - 1)) // MOE_ROWS)
    blk_start = jnp.arange(n_blk, dtype=jnp.int32) * MOE_ROWS
    blk_e = jnp.minimum(jnp.sum((blk_start[:, None] >= pad_end[None, :]).astype(jnp.int32), axis=1), N_EXPERTS - 1)
    order = jnp.argsort(flat_e, stable=True).astype(jnp.int32)
    e_row = jnp.repeat(blk_e, MOE_ROWS)
    r = jnp.arange(n_blk * MOE_ROWS, dtype=jnp.int32) - pad_start[e_row]
    src = order[jnp.clip(seg_start[e_row] + r, 0, n_asg - 1)]
    buf_tok = jnp.where(r < counts[e_row], src % n_tok, 0)
    blk_meta = jnp.concatenate([blk_e.astype(jnp.int32), (pad_end[-1:] // MOE_ROWS).astype(jnp.int32)])
    return buf_tok, blk_meta, dest.reshape(TOP_K, n_tok)


def kernel(x, mem, mem_norm, mix_norm, w_in, attn_sink, hy_conv_w, hy_conv_b, hy_f_w1, hy_f_b1, hy_f_freq1,
           hy_f_w2, hy_f_b2, hy_f_freq2, hy_f_w3, hy_skip, attn_out_norm, hy_out_norm, w_out, xattn_norm,
           xw_q, xw_k, xw_v, xw_o, ffn_norm, ffn_w_gate, ffn_w_up, ffn_w_down,
           moe_router, moe_w_gate, moe_w_up, moe_w_down, final_norm):
    b, s, d = x.shape
    depth = w_in.shape[0]
    n_tok = b * s
    assert s == FFT_N // 2 and b % 2 == 0
    row = lambda v: v.reshape(1, -1).astype(F32)

    pos = jnp.arange(s, dtype=F32)
    inv = ROPE_THETA ** (-jnp.arange(0, HEAD_DIM, 2, dtype=F32) / HEAD_DIM)
    ang = pos[:, None] * inv[None, :]
    cosf = jnp.tile(jnp.cos(ang), (1, LANES // (HEAD_DIM // 2)))
    sins = jnp.tile(jnp.concatenate([-jnp.sin(ang), jnp.sin(ang)], axis=1), (1, LANES // HEAD_DIM))
    t_pos = jnp.linspace(0.0, 1.0, s, dtype=F32)[:, None]
    wv = 2.0 * math.pi * jnp.arange(s, dtype=F32)[:, None] / s
    fr = jnp.linspace(1e-4, HY_BANDS - 1, HY_BANDS, dtype=F32)[None, :]
    z_pos = jnp.concatenate([t_pos, jnp.cos(fr * wv), -jnp.sin(fr * wv)], axis=-1)
    emb_pad = 64
    z_pos = jnp.pad(z_pos, ((0, 0), (0, emb_pad - HY_EMB)))
    absd = jnp.abs(jnp.linspace(HY_MIN_DECAY, HY_MAX_DECAY, HY_ORDER * HY_WIDTH, dtype=F32)).reshape(1, -1)
    tables = _fft_tables()

    mem2d = mem.reshape(-1, d)
    x2d = x.reshape(n_tok, d)
    out = None
    for l in range(depth):
        q, kx, vx, u = _inproj(x2d, row(mix_norm[l]), _to_bf16(w_in[l]), cosf, sins, s)
        a = _band_attention(q.reshape(b, s, ATTN_WIDTH), kx, vx.reshape(b, s, -1), attn_sink[l].astype(F32), s)
        filt = _hyena_filters(
            z_pos, t_pos, jnp.pad(hy_f_w1[l], ((0, emb_pad - HY_EMB), (0, 0))), row(hy_f_b1[l]),
            row(hy_f_freq1[l]), hy_f_w2[l], row(hy_f_b2[l]), row(hy_f_freq2[l]), hy_f_w3[l], absd)
        y = _hyena(u.reshape(b, s, -1), hy_conv_w[l], hy_conv_b[l], filt, hy_skip[l], tables)
        mkv = _norm_mm(mem2d, row(mem_norm), _to_bf16(jnp.concatenate([xw_k[l], xw_v[l]], axis=1)))
        j = l // 2
        x3 = _mix_xattn(x2d.reshape(b, s, d), a, y, row(attn_out_norm[l]), row(hy_out_norm[l]),
                        _to_bf16(w_out[l]), row(xattn_norm[l]), _to_bf16(xw_q[l]),
                        mkv.reshape(b, -1, 2 * X_WIDTH), _to_bf16(xw_o[l]))
        x2d = x3.reshape(n_tok, d)
        last = l == depth - 1
        if l % 2 == 0:
            x2d = _dense_ffn(x2d, row(ffn_norm[l]), _to_bf16(ffn_w_gate[j]), _to_bf16(ffn_w_up[j]),
                             _to_bf16(ffn_w_down[j]))
            if last:
                out = _final_norm(x2d, row(final_norm))
        else:
            wr = jnp.pad(moe_router[j].astype(F32), ((0, 0), (0, LANES - N_EXPERTS)))
            wr_hi = wr.astype(BF16)
            wr = jnp.concatenate([wr_hi, (wr - wr_hi.astype(F32)).astype(BF16)], axis=1)
            hb, route, route_t = _router(x2d, row(ffn_norm[l]), wr)
            buf_tok, blk_meta, dest = _moe_dispatch(route_t, n_tok)
            yb = _moe_experts(blk_meta, hb[buf_tok], _to_bf16(moe_w_gate[j]), _to_bf16(moe_w_up[j]),
                              _to_bf16(moe_w_down[j]))
            res = _combine(x2d, yb[dest[0]], yb[dest[1]], route, row(final_norm) if last else None)
            if last:
                out = res
            else:
                x2d = res
    return out.reshape(b, s, d)
```

```python
import functools
import math

import jax
import jax.numpy as jnp
from jax import lax
from jax.experimental import pallas as pl
from jax.experimental.pallas import tpu as pltpu

F32 = jnp.float32
BF16 = jnp.bfloat16

EPS = 1e-6
N_HEADS = 8
N_KV_HEADS = 2
HEAD_DIM = 64
ATTN_WIDTH = N_HEADS * HEAD_DIM
KV_WIDTH = N_KV_HEADS * HEAD_DIM
WINDOW = 128
ROPE_THETA = 10000.0
HY_WIDTH = 512
HY_ORDER = 2
HY_EMB = 33
HY_BANDS = (HY_EMB - 1) // 2
HY_FILTER_HIDDEN = 64
HY_TARGET = 1e-2
HY_FAST_DECAY = 0.3
HY_SLOW_DECAY = 1.5
HY_MIN_DECAY = math.log(HY_TARGET) / HY_SLOW_DECAY
HY_MAX_DECAY = math.log(HY_TARGET) / HY_FAST_DECAY
Q_END = ATTN_WIDTH
K_END = Q_END + KV_WIDTH
V_END = K_END + KV_WIDTH
X_HEADS = 4
X_HEAD_DIM = 128
X_WIDTH = X_HEADS * X_HEAD_DIM
N_EXPERTS = 8
TOP_K = 2

LANES = 128
SUBLANES = 8
MXU_DIM = 256
VMEM_LIMIT = 56 * 1024 * 1024

FFT_N1 = 256
FFT_N2 = 32
FFT_N = FFT_N1 * FFT_N2
FFT_J = SUBLANES
FFT_GROUPS = FFT_N1 // FFT_J
FFT_CHUNK = 64
FFT_SUB = FFT_CHUNK // FFT_J

MOE_ROWS = 1024
MOE_SLICES = 4
NEG = float(jnp.finfo(jnp.float32).min)
MASKED = -1e30
LOG2E = math.log2(math.e)


def _params(*sem):
    return pltpu.CompilerParams(dimension_semantics=sem, vmem_limit_bytes=VMEM_LIMIT)


def _dot(a, b):
    return jnp.dot(a, b, preferred_element_type=F32)


def _dot_t(a, b):
    return lax.dot_general(a, b, (((1,), (1,)), ((), ())), preferred_element_type=F32)


def _rms(xf, g):
    ms = jnp.mean(xf * xf, axis=-1, keepdims=True)
    return xf * lax.rsqrt(ms + EPS) * g


def _inproj_kernel(x_ref, g_ref, w_ref, cos_ref, sin_ref, q_ref, kx_ref, vx_ref, u_ref):
    h = _rms(x_ref[...], g_ref[...]).astype(BF16)
    cosf = cos_ref[...]
    sins = sin_ref[...]
    lane = lax.broadcasted_iota(jnp.int32, cosf.shape, 1)
    low = (lane % HEAD_DIM) < (HEAD_DIM // 2)

    def rope(c):
        rot = jnp.where(low, pltpu.roll(c, LANES - HEAD_DIM // 2, 1), pltpu.roll(c, HEAD_DIM // 2, 1))
        return c * cosf + rot * sins

    scale = HEAD_DIM ** -0.5 * LOG2E
    for j in range(ATTN_WIDTH // MXU_DIM):
        qc = _dot(h, w_ref[:, j * MXU_DIM:(j + 1) * MXU_DIM])
        for t in range(MXU_DIM // LANES):
            c0 = j * MXU_DIM + t * LANES
            q_ref[:, c0:c0 + LANES] = (rope(qc[:, t * LANES:(t + 1) * LANES]) * scale).astype(BF16)
    kvc = _dot(h, w_ref[:, Q_END:V_END])
    kt = rope(kvc[:, :KV_WIDTH]).T
    trow = lax.broadcasted_iota(jnp.int32, kt.shape, 0)
    top = jnp.where(trow < HEAD_DIM, kt, 0.0)
    bot = jnp.where(trow >= HEAD_DIM, kt, 0.0)
    kx_ref[0, 0] = top.astype(BF16)
    kx_ref[0, 1] = pltpu.roll(top, HEAD_DIM, 0).astype(BF16)
    kx_ref[0, 2] = pltpu.roll(bot, HEAD_DIM, 0).astype(BF16)
    kx_ref[0, 3] = bot.astype(BF16)
    vc = kvc[:, KV_WIDTH:]
    vsw = pltpu.roll(vc, HEAD_DIM, 1)
    lo = lane < HEAD_DIM
    vx_ref[:, 0 * LANES:1 * LANES] = jnp.where(lo, vc, 1.0).astype(BF16)
    vx_ref[:, 1 * LANES:2 * LANES] = jnp.where(lo, 1.0, vsw).astype(BF16)
    vx_ref[:, 2 * LANES:3 * LANES] = jnp.where(lo, vsw, 1.0).astype(BF16)
    vx_ref[:, 3 * LANES:4 * LANES] = jnp.where(lo, 1.0, vc).astype(BF16)
    n_u = u_ref.shape[1]
    for j in range(n_u // 512):
        u_ref[:, j * 512:(j + 1) * 512] = _dot(h, w_ref[:, V_END + j * 512:V_END + (j + 1) * 512]).astype(BF16)


def _inproj(x2d, g, w, cosf, sins, seq):
    n, d = x2d.shape
    tm = 512
    n_u = w.shape[1] - V_END
    spb = seq // tm
    return pl.pallas_call(
        _inproj_kernel,
        grid=(n // tm,),
        in_specs=[
            pl.BlockSpec((tm, d), lambda i: (i, 0)),
            pl.BlockSpec((1, d), lambda i: (0, 0)),
            pl.BlockSpec(w.shape, lambda i: (0, 0)),
            pl.BlockSpec((tm, LANES), lambda i: (i % spb, 0)),
            pl.BlockSpec((tm, LANES), lambda i: (i % spb, 0)),
        ],
        out_specs=[
            pl.BlockSpec((tm, ATTN_WIDTH), lambda i: (i, 0)),
            pl.BlockSpec((1, 2 * N_KV_HEADS, LANES, tm), lambda i: (i // spb, 0, 0, i % spb)),
            pl.BlockSpec((tm, 2 * N_KV_HEADS * LANES), lambda i: (i, 0)),
            pl.BlockSpec((tm, n_u), lambda i: (i, 0)),
        ],
        out_shape=[
            jax.ShapeDtypeStruct((n, ATTN_WIDTH), BF16),
            jax.ShapeDtypeStruct((n // seq, 2 * N_KV_HEADS, LANES, seq), BF16),
            jax.ShapeDtypeStruct((n, 2 * N_KV_HEADS * LANES), BF16),
            jax.ShapeDtypeStruct((n, n_u), BF16),
        ],
        compiler_params=_params("parallel"),
        name="inproj",
    )(x2d, g, w, cosf, sins)


def _battn_kernel(sink_ref, q_ref, kx_ref, vx_ref, o_ref, *, seq, tq):
    i = pl.program_id(1)
    blk = WINDOW
    n_blk = seq // blk
    row = lax.broadcasted_iota(jnp.int32, (blk, blk), 0)
    col = lax.broadcasted_iota(jnp.int32, (blk, blk), 1)
    tri_prev = jnp.where(col >= row, 0.0, MASKED)
    tri_next = jnp.where(col <= row, 0.0, MASKED)
    lo = col < HEAD_DIM
    heads_per_pair = LANES // HEAD_DIM
    group_pairs = N_HEADS // N_KV_HEADS // heads_per_pair
    units = [(jb, hp) for jb in range(tq // blk) for hp in range(N_HEADS // heads_per_pair)]

    def window(jb):
        bi = i * (tq // blk) + jb
        starts = [jnp.maximum(bi - 1, 0), bi, jnp.minimum(bi + 1, n_blk - 1)]
        return bi, [pl.multiple_of(s * blk, blk) for s in starts]

    def scores(jb, hp):
        _, starts = window(jb)
        qp = q_ref[0, jb * blk:(jb + 1) * blk, hp * LANES:(hp + 1) * LANES]
        out = []
        for t in range(heads_per_pair):
            var = heads_per_pair * (hp // group_pairs) + t
            kwin = jnp.concatenate([kx_ref[0, var, :, pl.ds(s, blk)] for s in starts], axis=1)
            out.append(_dot(qp, kwin))
        return out

    def shifted(jb, hp, s_pair):
        bi, _ = window(jb)
        b_prev = tri_prev + jnp.where(bi == 0, MASKED, 0.0)
        b_next = tri_next + jnp.where(bi == n_blk - 1, MASKED, 0.0)
        out = []
        for t in range(heads_per_pair):
            s = s_pair[t]
            s0 = s[:, :blk] + b_prev
            s1 = s[:, blk:2 * blk]
            s2 = s[:, 2 * blk:] + b_next
            sk = sink_ref[hp * heads_per_pair + t] * LOG2E
            m = jnp.maximum(jnp.max(jnp.maximum(jnp.maximum(s0, s1), s2), axis=-1, keepdims=True), sk)
            x = jnp.concatenate([s0 - m, s1 - m, s2 - m], axis=1).astype(BF16)
            out.append((x, sk - m))
        return out

    def probs(x_pair):
        return [(jnp.exp2(x), jnp.exp2(d)) for x, d in x_pair]

    def finish(jb, hp, p_pair):
        _, starts = window(jb)
        res = []
        for t in range(heads_per_pair):
            var = heads_per_pair * (hp // group_pairs) + t
            vwin = jnp.concatenate([vx_ref[0, pl.ds(s, blk), var * LANES:(var + 1) * LANES] for s in starts], axis=0)
            res.append(_dot(p_pair[t][0], vwin))
        num = jnp.where(lo, res[0], res[1])
        den = pltpu.roll(jnp.where(lo, res[1], res[0]), HEAD_DIM, 1) + jnp.where(lo, p_pair[0][1], p_pair[1][1])
        o_ref[0, jb * blk:(jb + 1) * blk, hp * LANES:(hp + 1) * LANES] = (num * (1.0 / den)).astype(BF16)

    n_u = len(units)
    st_s, st_x, st_p = {}, {}, {}
    for n in range(n_u + 3):
        if 0 <= n - 3 < n_u:
            finish(*units[n - 3], st_p.pop(n - 3))
        if 0 <= n - 2 < n_u:
            st_p[n - 2] = probs(st_x.pop(n - 2))
        if 0 <= n - 1 < n_u:
            st_x[n - 1] = shifted(*units[n - 1], st_s.pop(n - 1))
        if n < n_u:
            st_s[n] = scores(*units[n])


def _band_attention(q, kx, vx, sink, seq):
    b = q.shape[0]
    tq = 512
    return pl.pallas_call(
        functools.partial(_battn_kernel, seq=seq, tq=tq),
        grid=(b, seq // tq),
        in_specs=[
            pl.BlockSpec(memory_space=pltpu.SMEM),
            pl.BlockSpec((1, tq, ATTN_WIDTH), lambda bi, i: (bi, i, 0)),
            pl.BlockSpec((1,) + kx.shape[1:], lambda bi, i: (bi, 0, 0, 0)),
            pl.BlockSpec((1,) + vx.shape[1:], lambda bi, i: (bi, 0, 0)),
        ],
        out_specs=pl.BlockSpec((1, tq, ATTN_WIDTH), lambda bi, i: (bi, i, 0)),
        out_shape=jax.ShapeDtypeStruct((b, seq, ATTN_WIDTH), BF16),
        compiler_params=_params("parallel", "arbitrary"),
        name="band_attention",
    )(sink, q, kx, vx)


def _filter_kernel(z_ref, t_ref, w1_ref, b1_ref, f1_ref, w2_ref, b2_ref, f2_ref, w3_ref, ad_ref, o_ref, *, tl):
    hp = lax.Precision.HIGHEST
    h = jnp.sin(f1_ref[...] * (jnp.dot(z_ref[...], w1_ref[...], precision=hp, preferred_element_type=F32)
                               + b1_ref[...]))
    h = jnp.sin(f2_ref[...] * (jnp.dot(h, w2_ref[...], precision=hp, preferred_element_type=F32) + b2_ref[...]))
    t = t_ref[...]
    rowid = pl.program_id(0) * tl + lax.broadcasted_iota(jnp.int32, (tl, HY_WIDTH), 0)
    for d in range(2):
        for o in range(HY_ORDER):
            c0 = (d * HY_ORDER + o) * HY_WIDTH
            v = jnp.dot(h, w3_ref[:, c0:c0 + HY_WIDTH], precision=hp, preferred_element_type=F32)
            v = v * jnp.exp(-t * ad_ref[:, o * HY_WIDTH:(o + 1) * HY_WIDTH])
            if d == 1:
                v = jnp.where(rowid == 0, 0.0, v)
            v = v.reshape(tl // FFT_N1, FFT_GROUPS, FFT_J, HY_WIDTH)
            o_ref[d * HY_ORDER + o] = jnp.concatenate([v, jnp.zeros_like(v)], axis=2).astype(BF16)


def _hyena_filters(z_pos, t_pos, w1, b1, f1, w2, b2, f2, w3, absd):
    L, e = z_pos.shape
    tl = 512
    full = lambda a: pl.BlockSpec(a.shape, lambda i: (0,) * a.ndim)
    return pl.pallas_call(
        functools.partial(_filter_kernel, tl=tl),
        grid=(L // tl,),
        in_specs=[
            pl.BlockSpec((tl, e), lambda i: (i, 0)),
            pl.BlockSpec((tl, 1), lambda i: (i, 0)),
            full(w1), full(b1), full(f1), full(w2), full(b2), full(f2), full(w3), full(absd),
        ],
        out_specs=pl.BlockSpec((2 * HY_ORDER, tl // FFT_N1, FFT_GROUPS, 2 * FFT_J, HY_WIDTH),
                               lambda i: (0, i, 0, 0, 0)),
        out_shape=jax.ShapeDtypeStruct((2 * HY_ORDER, L // FFT_N1, FFT_GROUPS, 2 * FFT_J, HY_WIDTH), BF16),
        compiler_params=_params("parallel"),
        name="hyena_filters",
    )(z_pos, t_pos, w1, b1, f1, w2, b2, f2, w3, absd)


def _shortconv_kernel(ua_ref, ub_ref, w_ref, b_ref, o_ref):
    def conv(u_ref):
        u = u_ref[0].astype(F32)
        L, ct = u.shape
        r8 = lax.broadcasted_iota(jnp.int32, (SUBLANES, ct), 0)
        down = pltpu.roll(u, 1, 0)
        up = pltpu.roll(u, L - 1, 0)
        prev = jnp.concatenate([jnp.where(r8 == 0, 0.0, down[:SUBLANES]), down[SUBLANES:]], axis=0)
        nxt = jnp.concatenate([up[:L - SUBLANES], jnp.where(r8 == SUBLANES - 1, 0.0, up[L - SUBLANES:])], axis=0)
        r = prev * w_ref[0:1, :] + u * w_ref[1:2, :] + nxt * w_ref[2:3, :] + b_ref[...]
        return r.reshape(L // FFT_N1, FFT_GROUPS, FFT_J, ct)

    o_ref[0] = jnp.concatenate([conv(ua_ref), conv(ub_ref)], axis=2).astype(BF16)


def _shortconv(u, w, bias):
    b, L, c3 = u.shape
    ct = MXU_DIM
    ncb = b // 2
    return pl.pallas_call(
        _shortconv_kernel,
        grid=(ncb, c3 // ct),
        in_specs=[
            pl.BlockSpec((1, L, ct), lambda bi, ci: (bi, 0, ci)),
            pl.BlockSpec((1, L, ct), lambda bi, ci: (bi + ncb, 0, ci)),
            pl.BlockSpec((3, ct), lambda bi, ci: (0, ci)),
            pl.BlockSpec((1, ct), lambda bi, ci: (0, ci)),
        ],
        out_specs=pl.BlockSpec((1, L // FFT_N1, FFT_GROUPS, 2 * FFT_J, ct), lambda bi, ci: (bi, 0, 0, 0, ci)),
        out_shape=jax.ShapeDtypeStruct((ncb, L // FFT_N1, FFT_GROUPS, 2 * FFT_J, c3), BF16),
        compiler_params=_params("parallel", "parallel"),
        name="shortconv",
    )(u, u, w, bias)


def _real_block(m, n, po, pi, sign, scale=1.0):
    ang = (2.0 * math.pi / n) * (m % n).astype(F32)
    re = jnp.cos(ang) * scale
    im = jnp.sin(ang) * (sign * scale)
    return jnp.where(po == pi, re, jnp.where(po > pi, im, -im))


def _fft_tables():
    nh = FFT_N2 // 2
    j2 = 2 * FFT_J

    def split(idx):
        return idx // j2, (idx // FFT_J) % 2, idx % FFT_J

    def small(n_out, n_in, sign, scale, out_is_freq):
        rows = FFT_GROUPS * n_out * j2
        a = lax.broadcasted_iota(jnp.int32, (n_in * 2, rows), 0)
        b = lax.broadcasted_iota(jnp.int32, (n_in * 2, rows), 1)
        major_out, po, j = split(b % (n_out * j2))
        g = b // (n_out * j2)
        major_in, pi = a // 2, a % 2
        k2, n2 = (major_out, major_in) if out_is_freq else (major_in, major_out)
        m = FFT_N1 * n2 * k2 + (FFT_J * g + j) * k2
        compact = _real_block(m, FFT_N, po, pi, sign, scale).astype(BF16)
        cols = n_in * j2
        rep = (lax.broadcasted_iota(jnp.int32, (n_in * 2, cols), 1) // FFT_J
               == lax.broadcasted_iota(jnp.int32, (n_in * 2, cols), 0)).astype(BF16)
        full = lax.dot_general(compact, rep, (((0,), (0,)), ((), ())), preferred_element_type=F32)
        diag = (lax.broadcasted_iota(jnp.int32, (rows, cols), 0) % FFT_J
                == lax.broadcasted_iota(jnp.int32, (rows, cols), 1) % FFT_J)
        return jnp.where(diag, full, 0.0).astype(BF16).reshape(FFT_GROUPS, n_out * j2, cols)

    g_fwd = small(FFT_N2, nh, -1.0, 1.0, True)
    g_inv = small(nh, FFT_N2, 1.0, 1.0 / FFT_N, False)

    def big(sign):
        r = lax.broadcasted_iota(jnp.int32, (2 * FFT_N1, 2 * FFT_N1), 0)
        c = lax.broadcasted_iota(jnp.int32, (2 * FFT_N1, 2 * FFT_N1), 1)
        gk, po, jk = split(r)
        g, pi, j = split(c)
        m = (FFT_J * gk + jk) * (FFT_J * g + j)
        return _real_block(m, FFT_N1, po, pi, sign).astype(BF16)

    return g_fwd, g_inv, big(-1.0), big(1.0)


def _small_fwd_rows(g_ref, s, tile):
    r = _dot(g_ref[s], tile)
    return r.reshape(FFT_N2, 2 * FFT_J, tile.shape[-1]).astype(BF16)


def _fs_kernel(z_ref, g_ref, o_ref):
    ct = o_ref.shape[-1]
    for s in range(FFT_SUB):
        tile = z_ref[0, :, s, :, :].reshape(FFT_N2 // 2 * 2 * FFT_J, ct)
        o_ref[0, :, s, :, :] = _small_fwd_rows(g_ref, s, tile)


def _fft_small_fwd(zil, g_fwd, *, c_off, n_c):
    ncb, nh, ng, _, _ = zil.shape
    ct = MXU_DIM
    cblk = c_off // ct
    return pl.pallas_call(
        _fs_kernel,
        grid=(ng // FFT_SUB, ncb, n_c // ct),
        in_specs=[
            pl.BlockSpec((1, nh, FFT_SUB, 2 * FFT_J, ct), lambda q, b, ci: (b, 0, q, 0, cblk + ci)),
            pl.BlockSpec((FFT_SUB,) + g_fwd.shape[1:], lambda q, b, ci: (q, 0, 0)),
        ],
        out_specs=pl.BlockSpec((1, FFT_N2, FFT_SUB, 2 * FFT_J, ct), lambda q, b, ci: (b, 0, q, 0, ci)),
        out_shape=jax.ShapeDtypeStruct((ncb, FFT_N2, ng, 2 * FFT_J, n_c), BF16),
        compiler_params=_params("parallel", "parallel", "arbitrary"),
        name="fft_small_fwd",
    )(zil, g_fwd)


def _big_kernel(x_ref, h_ref, fb_ref, fbi_ref, o_ref):
    ct = o_ref.shape[-1]
    h4 = h_ref[0, 0].reshape(FFT_GROUPS, 2, FFT_J, ct)
    hre = h4[:, 0]
    him = h4[:, 1]
    n_b = x_ref.shape[0]

    def forward(b):
        return _dot(fb_ref[...], x_ref[b, 0].reshape(2 * FFT_N1, ct)).reshape(FFT_GROUPS, 2, FFT_J, ct)

    xf = forward(0)
    for b in range(n_b):
        xr = xf[:, 0]
        xi = xf[:, 1]
        if b + 1 < n_b:
            xf = forward(b + 1)
        y = jnp.stack([xr * hre - xi * him, xr * him + xi * hre], axis=1).reshape(2 * FFT_N1, ct).astype(BF16)
        o_ref[b, 0] = _dot(fbi_ref[...], y).reshape(FFT_GROUPS, 2 * FFT_J, ct).astype(BF16)


def _fft_big(xs, spec, order, fb, fbi):
    ncb, n2, ng, _, c = xs.shape
    ct = MXU_DIM
    nbb = ncb
    return pl.pallas_call(
        _big_kernel,
        grid=(c // ct, n2, ncb // nbb),
        in_specs=[
            pl.BlockSpec((nbb, 1, ng, 2 * FFT_J, ct), lambda ci, k, b: (b, k, 0, 0, ci)),
            pl.BlockSpec((1, 1, ng, 2 * FFT_J, ct), lambda ci, k, b: (order, k, 0, 0, ci)),
            pl.BlockSpec(fb.shape, lambda ci, k, b: (0, 0)),
            pl.BlockSpec(fbi.shape, lambda ci, k, b: (0, 0)),
        ],
        out_specs=pl.BlockSpec((nbb, 1, ng, 2 * FFT_J, ct), lambda ci, k, b: (b, k, 0, 0, ci)),
        out_shape=jax.ShapeDtypeStruct(xs.shape, BF16),
        compiler_params=_params("parallel", "parallel", "arbitrary"),
        name="fft_big",
    )(xs, spec, fb, fbi)


def _spectrum_kernel(x_ref, fb_ref, o_ref):
    ct = o_ref.shape[-1]
    xf = [_dot(fb_ref[...], x_ref[i, 0].reshape(2 * FFT_N1, ct)).reshape(FFT_GROUPS, 2, FFT_J, ct)
          for i in range(2 * HY_ORDER)]
    for o in range(HY_ORDER):
        a = xf[o]
        r = xf[HY_ORDER + o]
        spec = jnp.stack([a[:, 0] + r[:, 0], a[:, 1] - r[:, 1]], axis=1)
        o_ref[o, 0] = spec.reshape(FFT_GROUPS, 2 * FFT_J, ct)


def _filter_spectrum(xs, fb):
    nf, n2, ng, _, c = xs.shape
    ct = MXU_DIM
    return pl.pallas_call(
        _spectrum_kernel,
        grid=(c // ct, n2),
        in_specs=[
            pl.BlockSpec((nf, 1, ng, 2 * FFT_J, ct), lambda ci, k: (0, k, 0, 0, ci)),
            pl.BlockSpec(fb.shape, lambda ci, k: (0, 0)),
        ],
        out_specs=pl.BlockSpec((HY_ORDER, 1, ng, 2 * FFT_J, ct), lambda ci, k: (0, k, 0, 0, ci)),
        out_shape=jax.ShapeDtypeStruct((HY_ORDER, n2, ng, 2 * FFT_J, c), F32),
        compiler_params=_params("parallel", "parallel"),
        name="filter_spectrum",
    )(xs, fb)


def _gated_inverse(c_ref, gi_ref, z_ref, gate_ref, skip, s):
    nh = FFT_N2 // 2
    ct = c_ref.shape[-1]
    rows = nh * 2 * FFT_J
    r = _dot(gi_ref[s], c_ref[0, :, s, :, :].reshape(FFT_N2 * 2 * FFT_J, ct))
    z = z_ref[0, :, s, :, :].astype(F32).reshape(rows, ct)
    gate = gate_ref[0, :, s, :, :].astype(F32).reshape(rows, ct)
    return gate * (r + z * skip)


def _is_fs_kernel(c_ref, gi_ref, z_ref, gate_ref, skip_ref, gf_ref, zz_ref, o_ref):
    nh = FFT_N2 // 2
    ct = o_ref.shape[-1]
    skip = skip_ref[...]
    for s in range(FFT_SUB):
        zz = _gated_inverse(c_ref, gi_ref, z_ref, gate_ref, skip, s).astype(BF16)
        zz_ref[0, :, s, :, :] = zz.reshape(nh, 2 * FFT_J, ct)
        o_ref[0, :, s, :, :] = _small_fwd_rows(gf_ref, s, zz)


def _is_last_kernel(c_ref, gi_ref, z_ref, gate_ref, skip_ref, y_ref):
    nh = FFT_N2 // 2
    ct = y_ref.shape[-1]
    skip = skip_ref[...]
    for s in range(FFT_SUB):
        y = _gated_inverse(c_ref, gi_ref, z_ref, gate_ref, skip, s).reshape(nh, 2, FFT_J, ct)
        for p in range(2):
            y_ref[p, 0, :, FFT_J * s:FFT_J * (s + 1), :] = y[:, p]


def _fft_small_inv(cs, g_inv, zil, z_off, gil, gate_off, skip, g_fwd=None):
    ncb, n2, ng, _, c = cs.shape
    nh = n2 // 2
    ct = MXU_DIM
    zb = z_off // ct
    gb = gate_off // ct
    til = lambda off: pl.BlockSpec((1, nh, FFT_SUB, 2 * FFT_J, ct), lambda q, b, ci: (b, 0, q, 0, off + ci))
    freq = pl.BlockSpec((1, n2, FFT_SUB, 2 * FFT_J, ct), lambda q, b, ci: (b, 0, q, 0, ci))
    mat = lambda m: pl.BlockSpec((FFT_SUB,) + m.shape[1:], lambda q, b, ci: (q, 0, 0))
    ins = [freq, mat(g_inv), til(zb), til(gb), pl.BlockSpec((1, ct), lambda q, b, ci: (0, ci))]
    args = [cs, g_inv, zil, gil, skip]
    if g_fwd is not None:
        body = _is_fs_kernel
        ins.append(mat(g_fwd))
        args.append(g_fwd)
        out_specs = [til(0), freq]
        out_shape = [jax.ShapeDtypeStruct((ncb, nh, ng, 2 * FFT_J, c), BF16), jax.ShapeDtypeStruct(cs.shape, BF16)]
    else:
        body = _is_last_kernel
        out_specs = pl.BlockSpec((2, 1, nh, FFT_CHUNK, ct), lambda q, b, ci: (0, b, 0, q, ci))
        out_shape = jax.ShapeDtypeStruct((2, ncb, nh, ng * FFT_J, c), F32)
    return pl.pallas_call(
        body,
        grid=(ng // FFT_SUB, ncb, c // ct),
        in_specs=ins,
        out_specs=out_specs,
        out_shape=out_shape,
        compiler_params=_params("parallel", "parallel", "arbitrary"),
        name="fft_small_inv",
    )(*args)


def _norm_mm_kernel(x_ref, g_ref, w_ref, o_ref):
    o_ref[...] = _dot(_rms(x_ref[...], g_ref[...]).astype(BF16), w_ref[...]).astype(o_ref.dtype)


def _norm_mm(x2d, g, w):
    n, d = x2d.shape
    tm = 512
    return pl.pallas_call(
        _norm_mm_kernel,
        grid=(n // tm,),
        in_specs=[
            pl.BlockSpec((tm, d), lambda i: (i, 0)),
            pl.BlockSpec((1, d), lambda i: (0, 0)),
            pl.BlockSpec(w.shape, lambda i: (0, 0)),
        ],
        out_specs=pl.BlockSpec((tm, w.shape[1]), lambda i: (i, 0)),
        out_shape=jax.ShapeDtypeStruct((n, w.shape[1]), BF16),
        compiler_params=_params("parallel"),
        name="memory_kv",
    )(x2d, g, w)


def _route(xf, g_ref, wr_ref, h_ref, r_ref, rt_ref):
    hf = _rms(xf, g_ref[...])
    hb = hf.astype(BF16)
    h_ref[...] = hb
    h_lo = (hf - hb.astype(F32)).astype(BF16)
    parts = _dot(hb, wr_ref[...]) + _dot(h_lo, wr_ref[...])
    logits = parts[:, :LANES] + parts[:, LANES:]
    lane = lax.broadcasted_iota(jnp.int32, logits.shape, 1)
    logits = jnp.where(lane < N_EXPERTS, logits, NEG)
    lanef = lane.astype(F32)
    big = float(LANES)
    m1 = jnp.max(logits, axis=-1, keepdims=True)
    i1 = jnp.min(jnp.where(logits == m1, lanef, big), axis=-1, keepdims=True)
    rest = jnp.where(lanef == i1, NEG, logits)
    m2 = jnp.max(rest, axis=-1, keepdims=True)
    i2 = jnp.min(jnp.where(rest == m2, lanef, big), axis=-1, keepdims=True)
    e2 = jnp.exp(m2 - m1)
    w1 = 1.0 / (1.0 + e2)
    w2 = e2 / (1.0 + e2)
    rec = jnp.where(lane == 0, i1, jnp.where(lane == 1, i2, jnp.where(lane == 2, w1, jnp.where(lane == 3, w2, 0.0))))
    r_ref[...] = rec
    rt_ref[...] = rec.T[:SUBLANES, :]


def _mix_xattn_kernel(x_ref, a_ref, y_ref, ga_ref, gy_ref, wo_ref, gx_ref, wq_ref, kv_ref, wxo_ref, o_ref):
    an = _rms(a_ref[0].astype(F32), ga_ref[...]).astype(BF16)
    yn = _rms(y_ref[0], gy_ref[...]).astype(BF16)
    x1 = x_ref[0] + _dot(an, wo_ref[:ATTN_WIDTH, :]) + _dot(yn, wo_ref[ATTN_WIDTH:, :])
    h = _rms(x1, gx_ref[...]).astype(BF16)
    q = (_dot(h, wq_ref[...]) * (X_HEAD_DIM ** -0.5)).astype(BF16)
    outs = []
    for hh in range(X_HEADS):
        cols = slice(hh * X_HEAD_DIM, (hh + 1) * X_HEAD_DIM)
        kh = kv_ref[0, :, cols]
        vh = kv_ref[0, :, X_WIDTH + hh * X_HEAD_DIM:X_WIDTH + (hh + 1) * X_HEAD_DIM]
        s = _dot_t(q[:, cols], kh)
        p = jnp.exp(s - jnp.max(s, axis=-1, keepdims=True))
        den = jnp.sum(p, axis=-1, keepdims=True)
        outs.append((_dot(p.astype(BF16), vh) / den).astype(BF16))
    o_ref[0] = x1 + _dot(jnp.concatenate(outs, axis=1), wxo_ref[...])


def _mix_xattn(x, a, y, ga, gy, wo, gx, wq, kv, wxo):
    b, s, d = x.shape
    tq = 512
    m = kv.shape[1]
    full = lambda arr: pl.BlockSpec(arr.shape, lambda bi, i: (0,) * arr.ndim)
    tok = lambda w: pl.BlockSpec((1, tq, w), lambda bi, i: (bi, i, 0))
    return pl.pallas_call(
        _mix_xattn_kernel,
        grid=(b, s // tq),
        in_specs=[tok(d), tok(ATTN_WIDTH), tok(HY_WIDTH), full(ga), full(gy), full(wo), full(gx), full(wq),
                  pl.BlockSpec((1, m, 2 * X_WIDTH), lambda bi, i: (bi, 0, 0)), full(wxo)],
        out_specs=tok(d),
        out_shape=jax.ShapeDtypeStruct((b, s, d), F32),
        compiler_params=_params("parallel", "arbitrary"),
        name="mix_xattn",
    )(x, a, y, ga, gy, wo, gx, wq, kv, wxo)


def _swiglu_chunks(h, wg, wu, wd, width, acc):
    for c in range(width // MXU_DIM):
        cols = slice(c * MXU_DIM, (c + 1) * MXU_DIM)
        g = _dot(h, wg(cols))
        u = _dot(h, wu(cols))
        a = (g * (1.0 / (1.0 + jnp.exp(-g))) * u).astype(BF16)
        acc = acc + _dot(a, wd(cols))
    return acc


def _ffn_kernel(x_ref, g_ref, wg_ref, wu_ref, wd_ref, o_ref):
    x = x_ref[...]
    h = _rms(x, g_ref[...]).astype(BF16)
    o_ref[...] = _swiglu_chunks(h, lambda c: wg_ref[:, c], lambda c: wu_ref[:, c], lambda c: wd_ref[c, :],
                                wg_ref.shape[1], x)


def _dense_ffn(x2d, g, wg, wu, wd):
    n, d = x2d.shape
    tm = 512
    resident = lambda w: pl.BlockSpec(w.shape, lambda i: (0, 0), pipeline_mode=pl.Buffered(1))
    return pl.pallas_call(
        _ffn_kernel,
        grid=(n // tm,),
        in_specs=[
            pl.BlockSpec((tm, d), lambda i: (i, 0)),
            pl.BlockSpec((1, d), lambda i: (0, 0)),
            resident(wg), resident(wu), resident(wd),
        ],
        out_specs=pl.BlockSpec((tm, d), lambda i: (i, 0)),
        out_shape=jax.ShapeDtypeStruct((n, d), F32),
        compiler_params=_params("parallel"),
        name="dense_ffn",
    )(x2d, g, wg, wu, wd)


def _router_kernel(x_ref, g_ref, wr_ref, h_ref, r_ref, rt_ref):
    _route(x_ref[...], g_ref, wr_ref, h_ref, r_ref, rt_ref)


def _router(x2d, g, wr):
    n, d = x2d.shape
    tm = 512
    return pl.pallas_call(
        _router_kernel,
        grid=(n // tm,),
        in_specs=[
            pl.BlockSpec((tm, d), lambda i: (i, 0)),
            pl.BlockSpec((1, d), lambda i: (0, 0)),
            pl.BlockSpec(wr.shape, lambda i: (0, 0)),
        ],
        out_specs=[pl.BlockSpec((tm, d), lambda i: (i, 0)), pl.BlockSpec((tm, LANES), lambda i: (i, 0)),
                   pl.BlockSpec((SUBLANES, tm), lambda i: (0, i))],
        out_shape=[jax.ShapeDtypeStruct((n, d), BF16), jax.ShapeDtypeStruct((n, LANES), F32),
                   jax.ShapeDtypeStruct((SUBLANES, n), F32)],
        compiler_params=_params("parallel"),
        name="router",
    )(x2d, g, wr)


def _moe_kernel(be_ref, x_ref, wg_ref, wu_ref, wd_ref, *rest, n_blk):
    o_ref, acc_ref = rest[-2:]
    f = pl.program_id(1)

    @pl.when(f == 0)
    def _():
        acc_ref[...] = jnp.zeros_like(acc_ref)

    @pl.when(pl.program_id(0) < be_ref[n_blk])
    def _():
        acc_ref[...] = _swiglu_chunks(x_ref[...], lambda c: wg_ref[0, :, c], lambda c: wu_ref[0, :, c],
                                      lambda c: wd_ref[0, c, :], wg_ref.shape[2], acc_ref[...])

    @pl.when(f == pl.num_programs(1) - 1)
    def _():
        o_ref[...] = acc_ref[...].astype(BF16)


def _moe_experts(blk_meta, xb, wg, wu, wd, y_prev, blk_off, n_blk_total):
    rows, d = xb.shape
    ff = wg.shape[2]
    tf = ff // 2
    nf = ff // tf
    n_blk = rows // MOE_ROWS
    ftile = lambda i, f, be: jnp.where(i < be[n_blk], f, nf - 1)
    in_specs = [
        pl.BlockSpec((MOE_ROWS, d), lambda i, f, be: (i, 0)),
        pl.BlockSpec((1, d, tf), lambda i, f, be: (be[i], 0, ftile(i, f, be))),
        pl.BlockSpec((1, d, tf), lambda i, f, be: (be[i], 0, ftile(i, f, be))),
        pl.BlockSpec((1, tf, d), lambda i, f, be: (be[i], ftile(i, f, be), 0)),
    ]
    args = [blk_meta, xb, wg, wu, wd]
    aliases = {}
    if y_prev is not None:
        in_specs.append(pl.BlockSpec(memory_space=pl.ANY))
        args.append(y_prev)
        aliases = {len(args) - 1: 0}
    grid_spec = pltpu.PrefetchScalarGridSpec(
        num_scalar_prefetch=1,
        grid=(n_blk, nf),
        in_specs=in_specs,
        out_specs=pl.BlockSpec((MOE_ROWS, d), lambda i, f, be: (blk_off + i, 0)),
        scratch_shapes=[pltpu.VMEM((MOE_ROWS, d), F32)],
    )
    return pl.pallas_call(
        functools.partial(_moe_kernel, n_blk=n_blk),
        grid_spec=grid_spec,
        out_shape=jax.ShapeDtypeStruct((n_blk_total * MOE_ROWS, d), BF16),
        input_output_aliases=aliases,
        compiler_params=_params("parallel", "arbitrary"),
        name="moe_experts",
    )(*args)


def _combine_kernel(*refs, normed):
    if normed:
        x_ref, y1_ref, y2_ref, r_ref, g_ref, o_ref = refs
    else:
        x_ref, y1_ref, y2_ref, r_ref, o_ref = refs
    w1 = r_ref[:, 2:3]
    w2 = r_ref[:, 3:4]
    x = x_ref[...] + y1_ref[...].astype(F32) * w1 + y2_ref[...].astype(F32) * w2
    o_ref[...] = _rms(x, g_ref[...]) if normed else x


def _combine(x2d, y1, y2, route, gain):
    n, d = x2d.shape
    tm = 512
    normed = gain is not None
    tok = lambda w: pl.BlockSpec((tm, w), lambda i: (i, 0))
    ins = [tok(d), tok(d), tok(d), tok(LANES)]
    args = [x2d, y1, y2, route]
    if normed:
        ins.append(pl.BlockSpec((1, d), lambda i: (0, 0)))
        args.append(gain)
    return pl.pallas_call(
        functools.partial(_combine_kernel, normed=normed),
        grid=(n // tm,),
        in_specs=ins,
        out_specs=tok(d),
        out_shape=jax.ShapeDtypeStruct((n, d), F32),
        compiler_params=_params("parallel"),
        name="moe_combine",
    )(*args)


def _norm_kernel(x_ref, g_ref, o_ref):
    o_ref[...] = _rms(x_ref[...], g_ref[...])


def _final_norm(x2d, g):
    n, d = x2d.shape
    tm = 512
    return pl.pallas_call(
        _norm_kernel,
        grid=(n // tm,),
        in_specs=[pl.BlockSpec((tm, d), lambda i: (i, 0)), pl.BlockSpec((1, d), lambda i: (0, 0))],
        out_specs=pl.BlockSpec((tm, d), lambda i: (i, 0)),
        out_shape=jax.ShapeDtypeStruct((n, d), F32),
        compiler_params=_params("parallel"),
        name="final_norm",
    )(x2d, g)


def _hyena(u, conv_w, conv_b, filt, skip, tables):
    g_fwd, g_inv, fb, fbi = tables
    b, L, c3 = u.shape
    c = c3 // (HY_ORDER + 1)
    uc = _shortconv(u, conv_w, conv_b.reshape(1, c3))
    spec = _filter_spectrum(_fft_small_fwd(filt, g_fwd, c_off=0, n_c=c), fb)
    t = _fft_small_fwd(uc, g_fwd, c_off=2 * c, n_c=c)
    t = _fft_big(t, spec, 0, fb, fbi)
    zz, t = _fft_small_inv(t, g_inv, uc, 2 * c, uc, 0, skip[0:1], g_fwd)
    t = _fft_big(t, spec, 1, fb, fbi)
    y5 = _fft_small_inv(t, g_inv, zz, 0, uc, c, skip[1:2])
    return y5.reshape(b, L, c)


def _moe_dispatch(route_t, n_tok):
    n_asg = n_tok * TOP_K
    flat_e = route_t[:TOP_K].astype(jnp.int32).reshape(n_asg)
    experts = jnp.arange(N_EXPERTS, dtype=jnp.int32)[:, None]
    onehot = (flat_e[None, :] == experts).astype(jnp.int32)
    csum = jnp.cumsum(onehot, axis=1)
    counts = csum[:, -1]
    rank = jnp.sum(onehot * (csum - 1), axis=0)
    padded = (counts + MOE_ROWS - 1) // MOE_ROWS * MOE_ROWS
    pad_end = jnp.cumsum(padded)
    pad_start = pad_end - padded
    dest = jnp.sum(onehot * pad_start[:, None], axis=0) + rank
    seg_start = jnp.cumsum(counts) - counts
    n_blk = -(-(n_asg + N_EXPERTS * (MOE_ROWS - 1)) // MOE_ROWS)
    blk_start = jnp.arange(n_blk, dtype=jnp.int32) * MOE_ROWS
    blk_e = jnp.minimum(jnp.sum((blk_start[:, None] >= pad_end[None, :]).astype(jnp.int32), axis=1), N_EXPERTS - 1)
    order = jnp.argsort(flat_e, stable=True).astype(jnp.int32)
    e_row = jnp.repeat(blk_e, MOE_ROWS)
    r = jnp.arange(n_blk * MOE_ROWS, dtype=jnp.int32) - pad_start[e_row]
    src = order[jnp.clip(seg_start[e_row] + r, 0, n_asg - 1)]
    buf_tok = jnp.where(r < counts[e_row], src % n_tok, 0)
    n_used = (pad_end[-1] // MOE_ROWS).astype(jnp.int32)
    return buf_tok, blk_e.astype(jnp.int32), n_used, dest.reshape(TOP_K, n_tok)


def kernel(x, mem, mem_norm, mix_norm, w_in, attn_sink, hy_conv_w, hy_conv_b, hy_f_w1, hy_f_b1, hy_f_freq1,
           hy_f_w2, hy_f_b2, hy_f_freq2, hy_f_w3, hy_skip, attn_out_norm, hy_out_norm, w_out, xattn_norm,
           xw_q, xw_k, xw_v, xw_o, ffn_norm, ffn_w_gate, ffn_w_up, ffn_w_down,
           moe_router, moe_w_gate, moe_w_up, moe_w_down, final_norm):
    b, s, d = x.shape
    depth = w_in.shape[0]
    n_tok = b * s
    assert s == FFT_N // 2 and b % 2 == 0
    row = lambda v: v.reshape(1, -1).astype(F32)

    pos = jnp.arange(s, dtype=F32)
    inv = ROPE_THETA ** (-jnp.arange(0, HEAD_DIM, 2, dtype=F32) / HEAD_DIM)
    ang = pos[:, None] * inv[None, :]
    cosf = jnp.tile(jnp.cos(ang), (1, LANES // (HEAD_DIM // 2)))
    sins = jnp.tile(jnp.concatenate([-jnp.sin(ang), jnp.sin(ang)], axis=1), (1, LANES // HEAD_DIM))
    t_pos = jnp.linspace(0.0, 1.0, s, dtype=F32)[:, None]
    wv = 2.0 * math.pi * jnp.arange(s, dtype=F32)[:, None] / s
    fr = jnp.linspace(1e-4, HY_BANDS - 1, HY_BANDS, dtype=F32)[None, :]
    z_pos = jnp.concatenate([t_pos, jnp.cos(fr * wv), -jnp.sin(fr * wv)], axis=-1)
    emb_pad = 64
    z_pos = jnp.pad(z_pos, ((0, 0), (0, emb_pad - HY_EMB)))
    absd = jnp.abs(jnp.linspace(HY_MIN_DECAY, HY_MAX_DECAY, HY_ORDER * HY_WIDTH, dtype=F32)).reshape(1, -1)
    tables = _fft_tables()

    mem2d = mem.reshape(-1, d)
    x2d = x.reshape(n_tok, d)
    out = None
    for l in range(depth):
        q, kx, vx, u = _inproj(x2d, row(mix_norm[l]), w_in[l].astype(BF16), cosf, sins, s)
        a = _band_attention(q.reshape(b, s, ATTN_WIDTH), kx, vx.reshape(b, s, -1), attn_sink[l].astype(F32), s)
        filt = _hyena_filters(
            z_pos, t_pos, jnp.pad(hy_f_w1[l], ((0, emb_pad - HY_EMB), (0, 0))), row(hy_f_b1[l]),
            row(hy_f_freq1[l]), hy_f_w2[l], row(hy_f_b2[l]), row(hy_f_freq2[l]), hy_f_w3[l], absd)
        y = _hyena(u.reshape(b, s, -1), hy_conv_w[l], hy_conv_b[l], filt, hy_skip[l], tables)
        mkv = _norm_mm(mem2d, row(mem_norm), jnp.concatenate([xw_k[l], xw_v[l]], axis=1).astype(BF16))
        j = l // 2
        x3 = _mix_xattn(x2d.reshape(b, s, d), a, y, row(attn_out_norm[l]), row(hy_out_norm[l]),
                        w_out[l].astype(BF16), row(xattn_norm[l]), xw_q[l].astype(BF16),
                        mkv.reshape(b, -1, 2 * X_WIDTH), xw_o[l].astype(BF16))
        x2d = x3.reshape(n_tok, d)
        last = l == depth - 1
        if l % 2 == 0:
            x2d = _dense_ffn(x2d, row(ffn_norm[l]), ffn_w_gate[j].astype(BF16), ffn_w_up[j].astype(BF16),
                             ffn_w_down[j].astype(BF16))
            if last:
                out = _final_norm(x2d, row(final_norm))
        else:
            wr = jnp.pad(moe_router[j].astype(F32), ((0, 0), (0, LANES - N_EXPERTS)))
            wr_hi = wr.astype(BF16)
            wr = jnp.concatenate([wr_hi, (wr - wr_hi.astype(F32)).astype(BF16)], axis=1)
            hb, route, route_t = _router(x2d, row(ffn_norm[l]), wr)
            buf_tok, blk_e, n_used, dest = _moe_dispatch(route_t, n_tok)
            wg, wu, wd = (w[j].astype(BF16) for w in (moe_w_gate, moe_w_up, moe_w_down))
            n_blk = blk_e.shape[0]
            per = n_blk // MOE_SLICES
            assert per * MOE_SLICES == n_blk
            yb = None
            for k in range(MOE_SLICES):
                lo = k * per
                meta = jnp.concatenate([blk_e[lo:lo + per], (n_used - lo)[None]])
                xb = hb[buf_tok[lo * MOE_ROWS:(lo + per) * MOE_ROWS]]
                yb = _moe_experts(meta, xb, wg, wu, wd, yb, lo, n_blk)
            res = _combine(x2d, yb[dest[0]], yb[dest[1]], route, row(final_norm) if last else None)
            if last:
                out = res
            else:
                x2d = res
    return out.reshape(b, s, d)
```

```python
import functools
import math

import jax
import jax.numpy as jnp
from jax import lax
from jax.experimental import pallas as pl
from jax.experimental.pallas import tpu as pltpu

F32 = jnp.float32
BF16 = jnp.bfloat16

EPS = 1e-6
N_HEADS = 8
N_KV_HEADS = 2
HEAD_DIM = 64
ATTN_WIDTH = N_HEADS * HEAD_DIM
KV_WIDTH = N_KV_HEADS * HEAD_DIM
WINDOW = 128
ROPE_THETA = 10000.0
HY_WIDTH = 512
HY_ORDER = 2
HY_EMB = 33
HY_BANDS = (HY_EMB - 1) // 2
HY_FILTER_HIDDEN = 64
HY_TARGET = 1e-2
HY_FAST_DECAY = 0.3
HY_SLOW_DECAY = 1.5
HY_MIN_DECAY = math.log(HY_TARGET) / HY_SLOW_DECAY
HY_MAX_DECAY = math.log(HY_TARGET) / HY_FAST_DECAY
Q_END = ATTN_WIDTH
K_END = Q_END + KV_WIDTH
V_END = K_END + KV_WIDTH
X_HEADS = 4
X_HEAD_DIM = 128
X_WIDTH = X_HEADS * X_HEAD_DIM
N_EXPERTS = 8
TOP_K = 2

LANES = 128
SUBLANES = 8
MXU_DIM = 256
VMEM_LIMIT = 56 * 1024 * 1024

FFT_N1 = 256
FFT_N2 = 32
FFT_N = FFT_N1 * FFT_N2
FFT_J = SUBLANES
FFT_GROUPS = FFT_N1 // FFT_J
FFT_CHUNK = 64
FFT_SUB = FFT_CHUNK // FFT_J

MOE_ROWS = 1024
MOE_SLICES = 8
COMBINE_PARTS = 2
NEG = float(jnp.finfo(jnp.float32).min)
MASKED = -1e30
LOG2E = math.log2(math.e)


def _params(*sem):
    return pltpu.CompilerParams(dimension_semantics=sem, vmem_limit_bytes=VMEM_LIMIT)


def _dot(a, b):
    return jnp.dot(a, b, preferred_element_type=F32)


def _dot_t(a, b):
    return lax.dot_general(a, b, (((1,), (1,)), ((), ())), preferred_element_type=F32)


def _rms(xf, g):
    ms = jnp.mean(xf * xf, axis=-1, keepdims=True)
    return xf * lax.rsqrt(ms + EPS) * g


def _inproj_kernel(x_ref, g_ref, w_ref, cos_ref, sin_ref, q_ref, kx_ref, vx_ref, u_ref):
    h = _rms(x_ref[...], g_ref[...]).astype(BF16)
    cosf = cos_ref[...]
    sins = sin_ref[...]
    lane = lax.broadcasted_iota(jnp.int32, cosf.shape, 1)
    low = (lane % HEAD_DIM) < (HEAD_DIM // 2)

    def rope(c):
        rot = jnp.where(low, pltpu.roll(c, LANES - HEAD_DIM // 2, 1), pltpu.roll(c, HEAD_DIM // 2, 1))
        return c * cosf + rot * sins

    scale = HEAD_DIM ** -0.5 * LOG2E
    for j in range(ATTN_WIDTH // MXU_DIM):
        qc = _dot(h, w_ref[:, j * MXU_DIM:(j + 1) * MXU_DIM])
        for t in range(MXU_DIM // LANES):
            c0 = j * MXU_DIM + t * LANES
            q_ref[:, c0:c0 + LANES] = (rope(qc[:, t * LANES:(t + 1) * LANES]) * scale).astype(BF16)
    kvc = _dot(h, w_ref[:, Q_END:V_END])
    kt = rope(kvc[:, :KV_WIDTH]).T
    trow = lax.broadcasted_iota(jnp.int32, kt.shape, 0)
    top = jnp.where(trow < HEAD_DIM, kt, 0.0)
    bot = jnp.where(trow >= HEAD_DIM, kt, 0.0)
    kx_ref[0, 0] = top.astype(BF16)
    kx_ref[0, 1] = pltpu.roll(top, HEAD_DIM, 0).astype(BF16)
    kx_ref[0, 2] = pltpu.roll(bot, HEAD_DIM, 0).astype(BF16)
    kx_ref[0, 3] = bot.astype(BF16)
    vc = kvc[:, KV_WIDTH:]
    vsw = pltpu.roll(vc, HEAD_DIM, 1)
    lo = lane < HEAD_DIM
    vx_ref[:, 0 * LANES:1 * LANES] = jnp.where(lo, vc, 1.0).astype(BF16)
    vx_ref[:, 1 * LANES:2 * LANES] = jnp.where(lo, 1.0, vsw).astype(BF16)
    vx_ref[:, 2 * LANES:3 * LANES] = jnp.where(lo, vsw, 1.0).astype(BF16)
    vx_ref[:, 3 * LANES:4 * LANES] = jnp.where(lo, 1.0, vc).astype(BF16)
    n_u = u_ref.shape[1]
    for j in range(n_u // 512):
        u_ref[:, j * 512:(j + 1) * 512] = _dot(h, w_ref[:, V_END + j * 512:V_END + (j + 1) * 512]).astype(BF16)


def _inproj(x2d, g, w, cosf, sins, seq):
    n, d = x2d.shape
    tm = 512
    n_u = w.shape[1] - V_END
    spb = seq // tm
    return pl.pallas_call(
        _inproj_kernel,
        grid=(n // tm,),
        in_specs=[
            pl.BlockSpec((tm, d), lambda i: (i, 0)),
            pl.BlockSpec((1, d), lambda i: (0, 0)),
            pl.BlockSpec(w.shape, lambda i: (0, 0)),
            pl.BlockSpec((tm, LANES), lambda i: (i % spb, 0)),
            pl.BlockSpec((tm, LANES), lambda i: (i % spb, 0)),
        ],
        out_specs=[
            pl.BlockSpec((tm, ATTN_WIDTH), lambda i: (i, 0)),
            pl.BlockSpec((1, 2 * N_KV_HEADS, LANES, tm), lambda i: (i // spb, 0, 0, i % spb)),
            pl.BlockSpec((tm, 2 * N_KV_HEADS * LANES), lambda i: (i, 0)),
            pl.BlockSpec((tm, n_u), lambda i: (i, 0)),
        ],
        out_shape=[
            jax.ShapeDtypeStruct((n, ATTN_WIDTH), BF16),
            jax.ShapeDtypeStruct((n // seq, 2 * N_KV_HEADS, LANES, seq), BF16),
            jax.ShapeDtypeStruct((n, 2 * N_KV_HEADS * LANES), BF16),
            jax.ShapeDtypeStruct((n, n_u), BF16),
        ],
        compiler_params=_params("parallel"),
        name="inproj",
    )(x2d, g, w, cosf, sins)


def _battn_kernel(sink_ref, q_ref, kx_ref, vx_ref, o_ref, *, seq, tq):
    i = pl.program_id(1)
    blk = WINDOW
    n_blk = seq // blk
    row = lax.broadcasted_iota(jnp.int32, (blk, blk), 0)
    col = lax.broadcasted_iota(jnp.int32, (blk, blk), 1)
    tri_prev = jnp.where(col >= row, 0.0, MASKED)
    tri_next = jnp.where(col <= row, 0.0, MASKED)
    lo = col < HEAD_DIM
    heads_per_pair = LANES // HEAD_DIM
    group_pairs = N_HEADS // N_KV_HEADS // heads_per_pair
    units = [(jb, hp) for jb in range(tq // blk) for hp in range(N_HEADS // heads_per_pair)]

    def window(jb):
        bi = i * (tq // blk) + jb
        starts = [jnp.maximum(bi - 1, 0), bi, jnp.minimum(bi + 1, n_blk - 1)]
        return bi, [pl.multiple_of(s * blk, blk) for s in starts]

    def scores(jb, hp):
        _, starts = window(jb)
        qp = q_ref[0, jb * blk:(jb + 1) * blk, hp * LANES:(hp + 1) * LANES]
        out = []
        for t in range(heads_per_pair):
            var = heads_per_pair * (hp // group_pairs) + t
            kwin = jnp.concatenate([kx_ref[0, var, :, pl.ds(s, blk)] for s in starts], axis=1)
            out.append(_dot(qp, kwin))
        return out

    def shifted(jb, hp, s_pair):
        bi, _ = window(jb)
        b_prev = tri_prev + jnp.where(bi == 0, MASKED, 0.0)
        b_next = tri_next + jnp.where(bi == n_blk - 1, MASKED, 0.0)
        out = []
        for t in range(heads_per_pair):
            s = s_pair[t]
            s0 = s[:, :blk] + b_prev
            s1 = s[:, blk:2 * blk]
            s2 = s[:, 2 * blk:] + b_next
            sk = sink_ref[hp * heads_per_pair + t] * LOG2E
            m = jnp.maximum(jnp.max(jnp.maximum(jnp.maximum(s0, s1), s2), axis=-1, keepdims=True), sk)
            x = jnp.concatenate([s0 - m, s1 - m, s2 - m], axis=1).astype(BF16)
            out.append((x, sk - m))
        return out

    def probs(x_pair):
        return [(jnp.exp2(x), jnp.exp2(d)) for x, d in x_pair]

    def finish(jb, hp, p_pair):
        _, starts = window(jb)
        res = []
        for t in range(heads_per_pair):
            var = heads_per_pair * (hp // group_pairs) + t
            vwin = jnp.concatenate([vx_ref[0, pl.ds(s, blk), var * LANES:(var + 1) * LANES] for s in starts], axis=0)
            res.append(_dot(p_pair[t][0], vwin))
        num = jnp.where(lo, res[0], res[1])
        den = pltpu.roll(jnp.where(lo, res[1], res[0]), HEAD_DIM, 1) + jnp.where(lo, p_pair[0][1], p_pair[1][1])
        o_ref[0, jb * blk:(jb + 1) * blk, hp * LANES:(hp + 1) * LANES] = (num * (1.0 / den)).astype(BF16)

    n_u = len(units)
    st_s, st_x, st_p = {}, {}, {}
    for n in range(n_u + 3):
        if 0 <= n - 3 < n_u:
            finish(*units[n - 3], st_p.pop(n - 3))
        if 0 <= n - 2 < n_u:
            st_p[n - 2] = probs(st_x.pop(n - 2))
        if 0 <= n - 1 < n_u:
            st_x[n - 1] = shifted(*units[n - 1], st_s.pop(n - 1))
        if n < n_u:
            st_s[n] = scores(*units[n])


def _band_attention(q, kx, vx, sink, seq):
    b = q.shape[0]
    tq = 512
    return pl.pallas_call(
        functools.partial(_battn_kernel, seq=seq, tq=tq),
        grid=(b, seq // tq),
        in_specs=[
            pl.BlockSpec(memory_space=pltpu.SMEM),
            pl.BlockSpec((1, tq, ATTN_WIDTH), lambda bi, i: (bi, i, 0)),
            pl.BlockSpec((1,) + kx.shape[1:], lambda bi, i: (bi, 0, 0, 0)),
            pl.BlockSpec((1,) + vx.shape[1:], lambda bi, i: (bi, 0, 0)),
        ],
        out_specs=pl.BlockSpec((1, tq, ATTN_WIDTH), lambda bi, i: (bi, i, 0)),
        out_shape=jax.ShapeDtypeStruct((b, seq, ATTN_WIDTH), BF16),
        compiler_params=_params("parallel", "arbitrary"),
        name="band_attention",
    )(sink, q, kx, vx)


def _filter_kernel(z_ref, t_ref, w1_ref, b1_ref, f1_ref, w2_ref, b2_ref, f2_ref, w3_ref, ad_ref, o_ref, *, tl):
    hp = lax.Precision.HIGHEST
    h = jnp.sin(f1_ref[...] * (jnp.dot(z_ref[...], w1_ref[...], precision=hp, preferred_element_type=F32)
                               + b1_ref[...]))
    h = jnp.sin(f2_ref[...] * (jnp.dot(h, w2_ref[...], precision=hp, preferred_element_type=F32) + b2_ref[...]))
    t = t_ref[...]
    rowid = pl.program_id(0) * tl + lax.broadcasted_iota(jnp.int32, (tl, HY_WIDTH), 0)
    for d in range(2):
        for o in range(HY_ORDER):
            c0 = (d * HY_ORDER + o) * HY_WIDTH
            v = jnp.dot(h, w3_ref[:, c0:c0 + HY_WIDTH], precision=hp, preferred_element_type=F32)
            v = v * jnp.exp(-t * ad_ref[:, o * HY_WIDTH:(o + 1) * HY_WIDTH])
            if d == 1:
                v = jnp.where(rowid == 0, 0.0, v)
            v = v.reshape(tl // FFT_N1, FFT_GROUPS, FFT_J, HY_WIDTH)
            o_ref[d * HY_ORDER + o] = jnp.concatenate([v, jnp.zeros_like(v)], axis=2).astype(BF16)


def _hyena_filters(z_pos, t_pos, w1, b1, f1, w2, b2, f2, w3, absd):
    L, e = z_pos.shape
    tl = 512
    full = lambda a: pl.BlockSpec(a.shape, lambda i: (0,) * a.ndim)
    return pl.pallas_call(
        functools.partial(_filter_kernel, tl=tl),
        grid=(L // tl,),
        in_specs=[
            pl.BlockSpec((tl, e), lambda i: (i, 0)),
            pl.BlockSpec((tl, 1), lambda i: (i, 0)),
            full(w1), full(b1), full(f1), full(w2), full(b2), full(f2), full(w3), full(absd),
        ],
        out_specs=pl.BlockSpec((2 * HY_ORDER, tl // FFT_N1, FFT_GROUPS, 2 * FFT_J, HY_WIDTH),
                               lambda i: (0, i, 0, 0, 0)),
        out_shape=jax.ShapeDtypeStruct((2 * HY_ORDER, L // FFT_N1, FFT_GROUPS, 2 * FFT_J, HY_WIDTH), BF16),
        compiler_params=_params("parallel"),
        name="hyena_filters",
    )(z_pos, t_pos, w1, b1, f1, w2, b2, f2, w3, absd)


def _shortconv_kernel(ua_ref, ub_ref, w_ref, b_ref, o_ref):
    def conv(u_ref):
        u = u_ref[0].astype(F32)
        L, ct = u.shape
        r8 = lax.broadcasted_iota(jnp.int32, (SUBLANES, ct), 0)
        down = pltpu.roll(u, 1, 0)
        up = pltpu.roll(u, L - 1, 0)
        prev = jnp.concatenate([jnp.where(r8 == 0, 0.0, down[:SUBLANES]), down[SUBLANES:]], axis=0)
        nxt = jnp.concatenate([up[:L - SUBLANES], jnp.where(r8 == SUBLANES - 1, 0.0, up[L - SUBLANES:])], axis=0)
        r = prev * w_ref[0:1, :] + u * w_ref[1:2, :] + nxt * w_ref[2:3, :] + b_ref[...]
        return r.reshape(L // FFT_N1, FFT_GROUPS, FFT_J, ct)

    o_ref[0] = jnp.concatenate([conv(ua_ref), conv(ub_ref)], axis=2).astype(BF16)


def _shortconv(u, w, bias):
    b, L, c3 = u.shape
    ct = MXU_DIM
    ncb = b // 2
    return pl.pallas_call(
        _shortconv_kernel,
        grid=(ncb, c3 // ct),
        in_specs=[
            pl.BlockSpec((1, L, ct), lambda bi, ci: (bi, 0, ci)),
            pl.BlockSpec((1, L, ct), lambda bi, ci: (bi + ncb, 0, ci)),
            pl.BlockSpec((3, ct), lambda bi, ci: (0, ci)),
            pl.BlockSpec((1, ct), lambda bi, ci: (0, ci)),
        ],
        out_specs=pl.BlockSpec((1, L // FFT_N1, FFT_GROUPS, 2 * FFT_J, ct), lambda bi, ci: (bi, 0, 0, 0, ci)),
        out_shape=jax.ShapeDtypeStruct((ncb, L // FFT_N1, FFT_GROUPS, 2 * FFT_J, c3), BF16),
        compiler_params=_params("parallel", "parallel"),
        name="shortconv",
    )(u, u, w, bias)


def _real_block(m, n, po, pi, sign, scale=1.0):
    ang = (2.0 * math.pi / n) * (m % n).astype(F32)
    re = jnp.cos(ang) * scale
    im = jnp.sin(ang) * (sign * scale)
    return jnp.where(po == pi, re, jnp.where(po > pi, im, -im))


def _fft_tables():
    nh = FFT_N2 // 2
    j2 = 2 * FFT_J

    def split(idx):
        return idx // j2, (idx // FFT_J) % 2, idx % FFT_J

    def small(n_out, n_in, sign, scale, out_is_freq):
        rows = FFT_GROUPS * n_out * j2
        a = lax.broadcasted_iota(jnp.int32, (n_in * 2, rows), 0)
        b = lax.broadcasted_iota(jnp.int32, (n_in * 2, rows), 1)
        major_out, po, j = split(b % (n_out * j2))
        g = b // (n_out * j2)
        major_in, pi = a // 2, a % 2
        k2, n2 = (major_out, major_in) if out_is_freq else (major_in, major_out)
        m = FFT_N1 * n2 * k2 + (FFT_J * g + j) * k2
        compact = _real_block(m, FFT_N, po, pi, sign, scale).astype(BF16)
        cols = n_in * j2
        rep = (lax.broadcasted_iota(jnp.int32, (n_in * 2, cols), 1) // FFT_J
               == lax.broadcasted_iota(jnp.int32, (n_in * 2, cols), 0)).astype(BF16)
        full = lax.dot_general(compact, rep, (((0,), (0,)), ((), ())), preferred_element_type=F32)
        diag = (lax.broadcasted_iota(jnp.int32, (rows, cols), 0) % FFT_J
                == lax.broadcasted_iota(jnp.int32, (rows, cols), 1) % FFT_J)
        return jnp.where(diag, full, 0.0).astype(BF16).reshape(FFT_GROUPS, n_out * j2, cols)

    g_fwd = small(FFT_N2, nh, -1.0, 1.0, True)
    g_inv = small(nh, FFT_N2, 1.0, 1.0 / FFT_N, False)

    def big(sign):
        r = lax.broadcasted_iota(jnp.int32, (2 * FFT_N1, 2 * FFT_N1), 0)
        c = lax.broadcasted_iota(jnp.int32, (2 * FFT_N1, 2 * FFT_N1), 1)
        gk, po, jk = split(r)
        g, pi, j = split(c)
        m = (FFT_J * gk + jk) * (FFT_J * g + j)
        return _real_block(m, FFT_N1, po, pi, sign).astype(BF16)

    return g_fwd, g_inv, big(-1.0), big(1.0)


def _small_fwd_rows(g_ref, s, tile):
    r = _dot(g_ref[s], tile)
    return r.reshape(FFT_N2, 2 * FFT_J, tile.shape[-1]).astype(BF16)


def _fs_kernel(z_ref, g_ref, o_ref):
    ct = o_ref.shape[-1]
    for s in range(FFT_SUB):
        tile = z_ref[0, :, s, :, :].reshape(FFT_N2 // 2 * 2 * FFT_J, ct)
        o_ref[0, :, s, :, :] = _small_fwd_rows(g_ref, s, tile)


def _fft_small_fwd(zil, g_fwd, *, c_off, n_c):
    ncb, nh, ng, _, _ = zil.shape
    ct = MXU_DIM
    cblk = c_off // ct
    return pl.pallas_call(
        _fs_kernel,
        grid=(ng // FFT_SUB, ncb, n_c // ct),
        in_specs=[
            pl.BlockSpec((1, nh, FFT_SUB, 2 * FFT_J, ct), lambda q, b, ci: (b, 0, q, 0, cblk + ci)),
            pl.BlockSpec((FFT_SUB,) + g_fwd.shape[1:], lambda q, b, ci: (q, 0, 0)),
        ],
        out_specs=pl.BlockSpec((1, FFT_N2, FFT_SUB, 2 * FFT_J, ct), lambda q, b, ci: (b, 0, q, 0, ci)),
        out_shape=jax.ShapeDtypeStruct((ncb, FFT_N2, ng, 2 * FFT_J, n_c), BF16),
        compiler_params=_params("parallel", "parallel", "arbitrary"),
        name="fft_small_fwd",
    )(zil, g_fwd)


def _big_kernel(x_ref, h_ref, fb_ref, fbi_ref, o_ref):
    ct = o_ref.shape[-1]
    h4 = h_ref[0, 0].reshape(FFT_GROUPS, 2, FFT_J, ct)
    hre = h4[:, 0]
    him = h4[:, 1]
    n_b = x_ref.shape[0]

    def forward(b):
        return _dot(fb_ref[...], x_ref[b, 0].reshape(2 * FFT_N1, ct)).reshape(FFT_GROUPS, 2, FFT_J, ct)

    xf = forward(0)
    for b in range(n_b):
        xr = xf[:, 0]
        xi = xf[:, 1]
        if b + 1 < n_b:
            xf = forward(b + 1)
        y = jnp.stack([xr * hre - xi * him, xr * him + xi * hre], axis=1).reshape(2 * FFT_N1, ct).astype(BF16)
        o_ref[b, 0] = _dot(fbi_ref[...], y).reshape(FFT_GROUPS, 2 * FFT_J, ct).astype(BF16)


def _fft_big(xs, spec, order, fb, fbi):
    ncb, n2, ng, _, c = xs.shape
    ct = MXU_DIM
    nbb = ncb
    return pl.pallas_call(
        _big_kernel,
        grid=(c // ct, n2, ncb // nbb),
        in_specs=[
            pl.BlockSpec((nbb, 1, ng, 2 * FFT_J, ct), lambda ci, k, b: (b, k, 0, 0, ci)),
            pl.BlockSpec((1, 1, ng, 2 * FFT_J, ct), lambda ci, k, b: (order, k, 0, 0, ci)),
            pl.BlockSpec(fb.shape, lambda ci, k, b: (0, 0)),
            pl.BlockSpec(fbi.shape, lambda ci, k, b: (0, 0)),
        ],
        out_specs=pl.BlockSpec((nbb, 1, ng, 2 * FFT_J, ct), lambda ci, k, b: (b, k, 0, 0, ci)),
        out_shape=jax.ShapeDtypeStruct(xs.shape, BF16),
        compiler_params=_params("parallel", "parallel", "arbitrary"),
        name="fft_big",
    )(xs, spec, fb, fbi)


def _spectrum_kernel(x_ref, fb_ref, o_ref):
    ct = o_ref.shape[-1]
    xf = [_dot(fb_ref[...], x_ref[i, 0].reshape(2 * FFT_N1, ct)).reshape(FFT_GROUPS, 2, FFT_J, ct)
          for i in range(2 * HY_ORDER)]
    for o in range(HY_ORDER):
        a = xf[o]
        r = xf[HY_ORDER + o]
        spec = jnp.stack([a[:, 0] + r[:, 0], a[:, 1] - r[:, 1]], axis=1)
        o_ref[o, 0] = spec.reshape(FFT_GROUPS, 2 * FFT_J, ct)


def _filter_spectrum(xs, fb):
    nf, n2, ng, _, c = xs.shape
    ct = MXU_DIM
    return pl.pallas_call(
        _spectrum_kernel,
        grid=(c // ct, n2),
        in_specs=[
            pl.BlockSpec((nf, 1, ng, 2 * FFT_J, ct), lambda ci, k: (0, k, 0, 0, ci)),
            pl.BlockSpec(fb.shape, lambda ci, k: (0, 0)),
        ],
        out_specs=pl.BlockSpec((HY_ORDER, 1, ng, 2 * FFT_J, ct), lambda ci, k: (0, k, 0, 0, ci)),
        out_shape=jax.ShapeDtypeStruct((HY_ORDER, n2, ng, 2 * FFT_J, c), F32),
        compiler_params=_params("parallel", "parallel"),
        name="filter_spectrum",
    )(xs, fb)


def _gated_inverse(c_ref, gi_ref, z_ref, gate_ref, skip, s):
    nh = FFT_N2 // 2
    ct = c_ref.shape[-1]
    rows = nh * 2 * FFT_J
    r = _dot(gi_ref[s], c_ref[0, :, s, :, :].reshape(FFT_N2 * 2 * FFT_J, ct))
    z = z_ref[0, :, s, :, :].astype(F32).reshape(rows, ct)
    gate = gate_ref[0, :, s, :, :].astype(F32).reshape(rows, ct)
    return gate * (r + z * skip)


def _is_fs_kernel(c_ref, gi_ref, z_ref, gate_ref, skip_ref, gf_ref, zz_ref, o_ref):
    nh = FFT_N2 // 2
    ct = o_ref.shape[-1]
    skip = skip_ref[...]
    for s in range(FFT_SUB):
        zz = _gated_inverse(c_ref, gi_ref, z_ref, gate_ref, skip, s).astype(BF16)
        zz_ref[0, :, s, :, :] = zz.reshape(nh, 2 * FFT_J, ct)
        o_ref[0, :, s, :, :] = _small_fwd_rows(gf_ref, s, zz)


def _is_last_kernel(c_ref, gi_ref, z_ref, gate_ref, skip_ref, y_ref):
    nh = FFT_N2 // 2
    ct = y_ref.shape[-1]
    skip = skip_ref[...]
    for s in range(FFT_SUB):
        y = _gated_inverse(c_ref, gi_ref, z_ref, gate_ref, skip, s).reshape(nh, 2, FFT_J, ct)
        for p in range(2):
            y_ref[p, 0, :, FFT_J * s:FFT_J * (s + 1), :] = y[:, p]


def _fft_small_inv(cs, g_inv, zil, z_off, gil, gate_off, skip, g_fwd=None):
    ncb, n2, ng, _, c = cs.shape
    nh = n2 // 2
    ct = MXU_DIM
    zb = z_off // ct
    gb = gate_off // ct
    til = lambda off: pl.BlockSpec((1, nh, FFT_SUB, 2 * FFT_J, ct), lambda q, b, ci: (b, 0, q, 0, off + ci))
    freq = pl.BlockSpec((1, n2, FFT_SUB, 2 * FFT_J, ct), lambda q, b, ci: (b, 0, q, 0, ci))
    mat = lambda m: pl.BlockSpec((FFT_SUB,) + m.shape[1:], lambda q, b, ci: (q, 0, 0))
    ins = [freq, mat(g_inv), til(zb), til(gb), pl.BlockSpec((1, ct), lambda q, b, ci: (0, ci))]
    args = [cs, g_inv, zil, gil, skip]
    if g_fwd is not None:
        body = _is_fs_kernel
        ins.append(mat(g_fwd))
        args.append(g_fwd)
        out_specs = [til(0), freq]
        out_shape = [jax.ShapeDtypeStruct((ncb, nh, ng, 2 * FFT_J, c), BF16), jax.ShapeDtypeStruct(cs.shape, BF16)]
    else:
        body = _is_last_kernel
        out_specs = pl.BlockSpec((2, 1, nh, FFT_CHUNK, ct), lambda q, b, ci: (0, b, 0, q, ci))
        out_shape = jax.ShapeDtypeStruct((2, ncb, nh, ng * FFT_J, c), F32)
    return pl.pallas_call(
        body,
        grid=(ng // FFT_SUB, ncb, c // ct),
        in_specs=ins,
        out_specs=out_specs,
        out_shape=out_shape,
        compiler_params=_params("parallel", "parallel", "arbitrary"),
        name="fft_small_inv",
    )(*args)


def _norm_mm_kernel(x_ref, g_ref, w_ref, o_ref):
    o_ref[...] = _dot(_rms(x_ref[...], g_ref[...]).astype(BF16), w_ref[...]).astype(o_ref.dtype)


def _norm_mm(x2d, g, w):
    n, d = x2d.shape
    tm = 512
    return pl.pallas_call(
        _norm_mm_kernel,
        grid=(n // tm,),
        in_specs=[
            pl.BlockSpec((tm, d), lambda i: (i, 0)),
            pl.BlockSpec((1, d), lambda i: (0, 0)),
            pl.BlockSpec(w.shape, lambda i: (0, 0)),
        ],
        out_specs=pl.BlockSpec((tm, w.shape[1]), lambda i: (i, 0)),
        out_shape=jax.ShapeDtypeStruct((n, w.shape[1]), BF16),
        compiler_params=_params("parallel"),
        name="memory_kv",
    )(x2d, g, w)


def _route(xf, g_ref, wr_ref, h_ref, r_ref, rt_ref):
    hf = _rms(xf, g_ref[...])
    hb = hf.astype(BF16)
    h_ref[...] = hb
    h_lo = (hf - hb.astype(F32)).astype(BF16)
    parts = _dot(hb, wr_ref[...]) + _dot(h_lo, wr_ref[...])
    logits = parts[:, :LANES] + parts[:, LANES:]
    lane = lax.broadcasted_iota(jnp.int32, logits.shape, 1)
    logits = jnp.where(lane < N_EXPERTS, logits, NEG)
    lanef = lane.astype(F32)
    big = float(LANES)
    m1 = jnp.max(logits, axis=-1, keepdims=True)
    i1 = jnp.min(jnp.where(logits == m1, lanef, big), axis=-1, keepdims=True)
    rest = jnp.where(lanef == i1, NEG, logits)
    m2 = jnp.max(rest, axis=-1, keepdims=True)
    i2 = jnp.min(jnp.where(rest == m2, lanef, big), axis=-1, keepdims=True)
    e2 = jnp.exp(m2 - m1)
    w1 = 1.0 / (1.0 + e2)
    w2 = e2 / (1.0 + e2)
    rec = jnp.where(lane == 0, i1, jnp.where(lane == 1, i2, jnp.where(lane == 2, w1, jnp.where(lane == 3, w2, 0.0))))
    r_ref[...] = rec
    rt_ref[...] = rec.T[:SUBLANES, :]


def _mix_xattn_kernel(x_ref, a_ref, y_ref, ga_ref, gy_ref, wo_ref, gx_ref, wq_ref, kv_ref, wxo_ref, o_ref):
    an = _rms(a_ref[0].astype(F32), ga_ref[...]).astype(BF16)
    yn = _rms(y_ref[0], gy_ref[...]).astype(BF16)
    x1 = x_ref[0] + _dot(an, wo_ref[:ATTN_WIDTH, :]) + _dot(yn, wo_ref[ATTN_WIDTH:, :])
    h = _rms(x1, gx_ref[...]).astype(BF16)
    q = (_dot(h, wq_ref[...]) * (X_HEAD_DIM ** -0.5)).astype(BF16)
    outs = []
    for hh in range(X_HEADS):
        cols = slice(hh * X_HEAD_DIM, (hh + 1) * X_HEAD_DIM)
        kh = kv_ref[0, :, cols]
        vh = kv_ref[0, :, X_WIDTH + hh * X_HEAD_DIM:X_WIDTH + (hh + 1) * X_HEAD_DIM]
        s = _dot_t(q[:, cols], kh)
        p = jnp.exp(s - jnp.max(s, axis=-1, keepdims=True))
        den = jnp.sum(p, axis=-1, keepdims=True)
        outs.append((_dot(p.astype(BF16), vh) / den).astype(BF16))
    o_ref[0] = x1 + _dot(jnp.concatenate(outs, axis=1), wxo_ref[...])


def _mix_xattn(x, a, y, ga, gy, wo, gx, wq, kv, wxo):
    b, s, d = x.shape
    tq = 512
    m = kv.shape[1]
    full = lambda arr: pl.BlockSpec(arr.shape, lambda bi, i: (0,) * arr.ndim)
    tok = lambda w: pl.BlockSpec((1, tq, w), lambda bi, i: (bi, i, 0))
    return pl.pallas_call(
        _mix_xattn_kernel,
        grid=(b, s // tq),
        in_specs=[tok(d), tok(ATTN_WIDTH), tok(HY_WIDTH), full(ga), full(gy), full(wo), full(gx), full(wq),
                  pl.BlockSpec((1, m, 2 * X_WIDTH), lambda bi, i: (bi, 0, 0)), full(wxo)],
        out_specs=tok(d),
        out_shape=jax.ShapeDtypeStruct((b, s, d), F32),
        compiler_params=_params("parallel", "arbitrary"),
        name="mix_xattn",
    )(x, a, y, ga, gy, wo, gx, wq, kv, wxo)


def _swiglu_chunks(h, wg, wu, wd, width, acc):
    for c in range(width // MXU_DIM):
        cols = slice(c * MXU_DIM, (c + 1) * MXU_DIM)
        g = _dot(h, wg(cols))
        u = _dot(h, wu(cols))
        a = (g * (1.0 / (1.0 + jnp.exp(-g))) * u).astype(BF16)
        acc = acc + _dot(a, wd(cols))
    return acc


def _ffn_kernel(x_ref, g_ref, wg_ref, wu_ref, wd_ref, o_ref):
    x = x_ref[...]
    h = _rms(x, g_ref[...]).astype(BF16)
    o_ref[...] = _swiglu_chunks(h, lambda c: wg_ref[:, c], lambda c: wu_ref[:, c], lambda c: wd_ref[c, :],
                                wg_ref.shape[1], x)


def _dense_ffn(x2d, g, wg, wu, wd):
    n, d = x2d.shape
    tm = 512
    resident = lambda w: pl.BlockSpec(w.shape, lambda i: (0, 0), pipeline_mode=pl.Buffered(1))
    return pl.pallas_call(
        _ffn_kernel,
        grid=(n // tm,),
        in_specs=[
            pl.BlockSpec((tm, d), lambda i: (i, 0)),
            pl.BlockSpec((1, d), lambda i: (0, 0)),
            resident(wg), resident(wu), resident(wd),
        ],
        out_specs=pl.BlockSpec((tm, d), lambda i: (i, 0)),
        out_shape=jax.ShapeDtypeStruct((n, d), F32),
        compiler_params=_params("parallel"),
        name="dense_ffn",
    )(x2d, g, wg, wu, wd)


def _router_kernel(x_ref, g_ref, wr_ref, h_ref, r_ref, rt_ref):
    _route(x_ref[...], g_ref, wr_ref, h_ref, r_ref, rt_ref)


def _router(x2d, g, wr):
    n, d = x2d.shape
    tm = 512
    return pl.pallas_call(
        _router_kernel,
        grid=(n // tm,),
        in_specs=[
            pl.BlockSpec((tm, d), lambda i: (i, 0)),
            pl.BlockSpec((1, d), lambda i: (0, 0)),
            pl.BlockSpec(wr.shape, lambda i: (0, 0)),
        ],
        out_specs=[pl.BlockSpec((tm, d), lambda i: (i, 0)), pl.BlockSpec((tm, LANES), lambda i: (i, 0)),
                   pl.BlockSpec((SUBLANES, tm), lambda i: (0, i))],
        out_shape=[jax.ShapeDtypeStruct((n, d), BF16), jax.ShapeDtypeStruct((n, LANES), F32),
                   jax.ShapeDtypeStruct((SUBLANES, n), F32)],
        compiler_params=_params("parallel"),
        name="router",
    )(x2d, g, wr)


def _moe_kernel(be_ref, x_ref, wg_ref, wu_ref, wd_ref, y_prev_ref, o_ref, acc_ref, *, n_blk):
    del y_prev_ref
    f = pl.program_id(1)

    @pl.when(f == 0)
    def _():
        acc_ref[...] = jnp.zeros_like(acc_ref)

    @pl.when(pl.program_id(0) < be_ref[n_blk])
    def _():
        acc_ref[...] = _swiglu_chunks(x_ref[...], lambda c: wg_ref[0, :, c], lambda c: wu_ref[0, :, c],
                                      lambda c: wd_ref[0, c, :], wg_ref.shape[2], acc_ref[...])

    @pl.when(f == pl.num_programs(1) - 1)
    def _():
        o_ref[...] = acc_ref[...].astype(BF16)


def _moe_experts(blk_meta, xb, wg, wu, wd, y_prev, blk_off):
    rows, d = xb.shape
    ff = wg.shape[2]
    tf = ff // 2
    nf = ff // tf
    n_blk = rows // MOE_ROWS
    ftile = lambda i, f, be: jnp.where(i < be[n_blk], f, nf - 1)
    grid_spec = pltpu.PrefetchScalarGridSpec(
        num_scalar_prefetch=1,
        grid=(n_blk, nf),
        in_specs=[
            pl.BlockSpec((MOE_ROWS, d), lambda i, f, be: (i, 0)),
            pl.BlockSpec((1, d, tf), lambda i, f, be: (be[i], 0, ftile(i, f, be))),
            pl.BlockSpec((1, d, tf), lambda i, f, be: (be[i], 0, ftile(i, f, be))),
            pl.BlockSpec((1, tf, d), lambda i, f, be: (be[i], ftile(i, f, be), 0)),
            pl.BlockSpec(memory_space=pl.ANY),
        ],
        out_specs=pl.BlockSpec((MOE_ROWS, d), lambda i, f, be: (blk_off + i, 0)),
        scratch_shapes=[pltpu.VMEM((MOE_ROWS, d), F32)],
    )
    return pl.pallas_call(
        functools.partial(_moe_kernel, n_blk=n_blk),
        grid_spec=grid_spec,
        out_shape=jax.ShapeDtypeStruct(y_prev.shape, BF16),
        input_output_aliases={5: 0},
        compiler_params=_params("parallel", "arbitrary"),
        name="moe_experts",
    )(blk_meta, xb, wg, wu, wd, y_prev)


def _combine_kernel(*refs, normed):
    if normed:
        x_ref, y1_ref, y2_ref, r_ref, g_ref, o_ref = refs
    else:
        x_ref, y1_ref, y2_ref, r_ref, o_ref = refs
    w1 = r_ref[:, 2:3]
    w2 = r_ref[:, 3:4]
    x = x_ref[...] + y1_ref[...].astype(F32) * w1 + y2_ref[...].astype(F32) * w2
    o_ref[...] = _rms(x, g_ref[...]) if normed else x


def _combine(x2d, y1, y2, route, gain, row_off):
    n, d = x2d.shape
    tm = 512
    normed = gain is not None
    off = row_off // tm
    here = lambda w: pl.BlockSpec((tm, w), lambda i: (off + i, 0))
    part = pl.BlockSpec((tm, d), lambda i: (i, 0))
    ins = [here(d), part, part, here(LANES)]
    args = [x2d, y1, y2, route]
    if normed:
        ins.append(pl.BlockSpec((1, d), lambda i: (0, 0)))
        args.append(gain)
    return pl.pallas_call(
        functools.partial(_combine_kernel, normed=normed),
        grid=(y1.shape[0] // tm,),
        in_specs=ins,
        out_specs=here(d),
        out_shape=jax.ShapeDtypeStruct((n, d), F32),
        input_output_aliases={0: 0},
        compiler_params=_params("parallel"),
        name="moe_combine",
    )(*args)


def _norm_kernel(x_ref, g_ref, o_ref):
    o_ref[...] = _rms(x_ref[...], g_ref[...])


def _final_norm(x2d, g):
    n, d = x2d.shape
    tm = 512
    return pl.pallas_call(
        _norm_kernel,
        grid=(n // tm,),
        in_specs=[pl.BlockSpec((tm, d), lambda i: (i, 0)), pl.BlockSpec((1, d), lambda i: (0, 0))],
        out_specs=pl.BlockSpec((tm, d), lambda i: (i, 0)),
        out_shape=jax.ShapeDtypeStruct((n, d), F32),
        compiler_params=_params("parallel"),
        name="final_norm",
    )(x2d, g)


def _hyena(u, conv_w, conv_b, filt, skip, tables):
    g_fwd, g_inv, fb, fbi = tables
    b, L, c3 = u.shape
    c = c3 // (HY_ORDER + 1)
    uc = _shortconv(u, conv_w, conv_b.reshape(1, c3))
    spec = _filter_spectrum(_fft_small_fwd(filt, g_fwd, c_off=0, n_c=c), fb)
    t = _fft_small_fwd(uc, g_fwd, c_off=2 * c, n_c=c)
    t = _fft_big(t, spec, 0, fb, fbi)
    zz, t = _fft_small_inv(t, g_inv, uc, 2 * c, uc, 0, skip[0:1], g_fwd)
    t = _fft_big(t, spec, 1, fb, fbi)
    y5 = _fft_small_inv(t, g_inv, zz, 0, uc, c, skip[1:2])
    return y5.reshape(b, L, c)


def _moe_dispatch(route_t, n_tok):
    n_asg = n_tok * TOP_K
    flat_e = route_t[:TOP_K].astype(jnp.int32).reshape(n_asg)
    experts = jnp.arange(N_EXPERTS, dtype=jnp.int32)[:, None]
    onehot = (flat_e[None, :] == experts).astype(jnp.int32)
    csum = jnp.cumsum(onehot, axis=1)
    counts = csum[:, -1]
    rank = jnp.sum(onehot * (csum - 1), axis=0)
    padded = (counts + MOE_ROWS - 1) // MOE_ROWS * MOE_ROWS
    pad_end = jnp.cumsum(padded)
    pad_start = pad_end - padded
    dest = jnp.sum(onehot * pad_start[:, None], axis=0) + rank
    seg_start = jnp.cumsum(counts) - counts
    n_blk = -(-(n_asg + N_EXPERTS * (MOE_ROWS - 1)) // MOE_ROWS)
    blk_start = jnp.arange(n_blk, dtype=jnp.int32) * MOE_ROWS
    blk_e = jnp.minimum(jnp.sum((blk_start[:, None] >= pad_end[None, :]).astype(jnp.int32), axis=1), N_EXPERTS - 1)
    order = jnp.argsort(flat_e, stable=True).astype(jnp.int32)
    e_row = jnp.repeat(blk_e, MOE_ROWS)
    r = jnp.arange(n_blk * MOE_ROWS, dtype=jnp.int32) - pad_start[e_row]
    src = order[jnp.clip(seg_start[e_row] + r, 0, n_asg - 1)]
    buf_tok = jnp.where(r < counts[e_row], src % n_tok, 0)
    n_used = (pad_end[-1] // MOE_ROWS).astype(jnp.int32)
    return buf_tok, blk_e.astype(jnp.int32), n_used, dest.reshape(TOP_K, n_tok)


def kernel(x, mem, mem_norm, mix_norm, w_in, attn_sink, hy_conv_w, hy_conv_b, hy_f_w1, hy_f_b1, hy_f_freq1,
           hy_f_w2, hy_f_b2, hy_f_freq2, hy_f_w3, hy_skip, attn_out_norm, hy_out_norm, w_out, xattn_norm,
           xw_q, xw_k, xw_v, xw_o, ffn_norm, ffn_w_gate, ffn_w_up, ffn_w_down,
           moe_router, moe_w_gate, moe_w_up, moe_w_down, final_norm):
    b, s, d = x.shape
    depth = w_in.shape[0]
    n_tok = b * s
    assert s == FFT_N // 2 and b % 2 == 0
    row = lambda v: v.reshape(1, -1).astype(F32)

    pos = jnp.arange(s, dtype=F32)
    inv = ROPE_THETA ** (-jnp.arange(0, HEAD_DIM, 2, dtype=F32) / HEAD_DIM)
    ang = pos[:, None] * inv[None, :]
    cosf = jnp.tile(jnp.cos(ang), (1, LANES // (HEAD_DIM // 2)))
    sins = jnp.tile(jnp.concatenate([-jnp.sin(ang), jnp.sin(ang)], axis=1), (1, LANES // HEAD_DIM))
    t_pos = jnp.linspace(0.0, 1.0, s, dtype=F32)[:, None]
    wv = 2.0 * math.pi * jnp.arange(s, dtype=F32)[:, None] / s
    fr = jnp.linspace(1e-4, HY_BANDS - 1, HY_BANDS, dtype=F32)[None, :]
    z_pos = jnp.concatenate([t_pos, jnp.cos(fr * wv), -jnp.sin(fr * wv)], axis=-1)
    emb_pad = 64
    z_pos = jnp.pad(z_pos, ((0, 0), (0, emb_pad - HY_EMB)))
    absd = jnp.abs(jnp.linspace(HY_MIN_DECAY, HY_MAX_DECAY, HY_ORDER * HY_WIDTH, dtype=F32)).reshape(1, -1)
    tables = _fft_tables()

    mem2d = mem.reshape(-1, d)
    x2d = x.reshape(n_tok, d)
    out = None
    for l in range(depth):
        q, kx, vx, u = _inproj(x2d, row(mix_norm[l]), w_in[l].astype(BF16), cosf, sins, s)
        a = _band_attention(q.reshape(b, s, ATTN_WIDTH), kx, vx.reshape(b, s, -1), attn_sink[l].astype(F32), s)
        filt = _hyena_filters(
            z_pos, t_pos, jnp.pad(hy_f_w1[l], ((0, emb_pad - HY_EMB), (0, 0))), row(hy_f_b1[l]),
            row(hy_f_freq1[l]), hy_f_w2[l], row(hy_f_b2[l]), row(hy_f_freq2[l]), hy_f_w3[l], absd)
        y = _hyena(u.reshape(b, s, -1), hy_conv_w[l], hy_conv_b[l], filt, hy_skip[l], tables)
        mkv = _norm_mm(mem2d, row(mem_norm), jnp.concatenate([xw_k[l], xw_v[l]], axis=1).astype(BF16))
        j = l // 2
        x3 = _mix_xattn(x2d.reshape(b, s, d), a, y, row(attn_out_norm[l]), row(hy_out_norm[l]),
                        w_out[l].astype(BF16), row(xattn_norm[l]), xw_q[l].astype(BF16),
                        mkv.reshape(b, -1, 2 * X_WIDTH), xw_o[l].astype(BF16))
        x2d = x3.reshape(n_tok, d)
        last = l == depth - 1
        if l % 2 == 0:
            x2d = _dense_ffn(x2d, row(ffn_norm[l]), ffn_w_gate[j].astype(BF16), ffn_w_up[j].astype(BF16),
                             ffn_w_down[j].astype(BF16))
            if last:
                out = _final_norm(x2d, row(final_norm))
        else:
            wr = jnp.pad(moe_router[j].astype(F32), ((0, 0), (0, LANES - N_EXPERTS)))
            wr_hi = wr.astype(BF16)
            wr = jnp.concatenate([wr_hi, (wr - wr_hi.astype(F32)).astype(BF16)], axis=1)
            hb, route, route_t = _router(x2d, row(ffn_norm[l]), wr)
            buf_tok, blk_e, n_used, dest = _moe_dispatch(route_t, n_tok)
            wg, wu, wd = (w[j].astype(BF16) for w in (moe_w_gate, moe_w_up, moe_w_down))
            n_blk = blk_e.shape[0]
            per = n_blk // MOE_SLICES
            assert per * MOE_SLICES == n_blk
            yb = jnp.zeros((n_blk * MOE_ROWS, d), BF16)
            for k in range(MOE_SLICES):
                lo = k * per
                meta = jnp.concatenate([blk_e[lo:lo + per], (n_used - lo)[None]])
                xb = hb[buf_tok[lo * MOE_ROWS:(lo + per) * MOE_ROWS]]
                yb = _moe_experts(meta, xb, wg, wu, wd, yb, lo)
            res = x2d
            part = n_tok // COMBINE_PARTS
            for k in range(COMBINE_PARTS):
                rows = slice(k * part, (k + 1) * part)
                res = _combine(res, yb[dest[0, rows]], yb[dest[1, rows]], route,
                               row(final_norm) if last else None, k * part)
            if last:
                out = res
            else:
                x2d = res
    return out.reshape(b, s, d)
```

```python
import functools
import math

import jax
import jax.numpy as jnp
from jax import lax
from jax.experimental import pallas as pl
from jax.experimental.pallas import tpu as pltpu

F32 = jnp.float32
BF16 = jnp.bfloat16

EPS = 1e-6
N_HEADS = 8
N_KV_HEADS = 2
HEAD_DIM = 64
ATTN_WIDTH = N_HEADS * HEAD_DIM
KV_WIDTH = N_KV_HEADS * HEAD_DIM
WINDOW = 128
ROPE_THETA = 10000.0
HY_WIDTH = 512
HY_ORDER = 2
HY_EMB = 33
HY_BANDS = (HY_EMB - 1) // 2
HY_FILTER_HIDDEN = 64
HY_TARGET = 1e-2
HY_FAST_DECAY = 0.3
HY_SLOW_DECAY = 1.5
HY_MIN_DECAY = math.log(HY_TARGET) / HY_SLOW_DECAY
HY_MAX_DECAY = math.log(HY_TARGET) / HY_FAST_DECAY
Q_END = ATTN_WIDTH
K_END = Q_END + KV_WIDTH
V_END = K_END + KV_WIDTH
X_HEADS = 4
X_HEAD_DIM = 128
X_WIDTH = X_HEADS * X_HEAD_DIM
N_EXPERTS = 8
TOP_K = 2

LANES = 128
SUBLANES = 8
MXU_DIM = 256
VMEM_LIMIT = 56 * 1024 * 1024

FFT_N1 = 256
FFT_N2 = 32
FFT_N = FFT_N1 * FFT_N2
FFT_J = SUBLANES
FFT_GROUPS = FFT_N1 // FFT_J
FFT_CHUNK = 64
FFT_SUB = FFT_CHUNK // FFT_J

MOE_ROWS = 1024
MOE_SLICES = 8
NEG = float(jnp.finfo(jnp.float32).min)
MASKED = -1e30
LOG2E = math.log2(math.e)


def _params(*sem):
    return pltpu.CompilerParams(dimension_semantics=sem, vmem_limit_bytes=VMEM_LIMIT)


def _dot(a, b):
    return jnp.dot(a, b, preferred_element_type=F32)


def _dot_t(a, b):
    return lax.dot_general(a, b, (((1,), (1,)), ((), ())), preferred_element_type=F32)


def _rms(xf, g):
    ms = jnp.mean(xf * xf, axis=-1, keepdims=True)
    return xf * lax.rsqrt(ms + EPS) * g


def _inproj_kernel(x_ref, g_ref, w_ref, cos_ref, sin_ref, q_ref, kx_ref, vx_ref, u_ref):
    h = _rms(x_ref[...], g_ref[...]).astype(BF16)
    cosf = cos_ref[...]
    sins = sin_ref[...]
    lane = lax.broadcasted_iota(jnp.int32, cosf.shape, 1)
    low = (lane % HEAD_DIM) < (HEAD_DIM // 2)

    def rope(c):
        rot = jnp.where(low, pltpu.roll(c, LANES - HEAD_DIM // 2, 1), pltpu.roll(c, HEAD_DIM // 2, 1))
        return c * cosf + rot * sins

    scale = HEAD_DIM ** -0.5 * LOG2E
    for j in range(ATTN_WIDTH // MXU_DIM):
        qc = _dot(h, w_ref[:, j * MXU_DIM:(j + 1) * MXU_DIM])
        for t in range(MXU_DIM // LANES):
            c0 = j * MXU_DIM + t * LANES
            q_ref[:, c0:c0 + LANES] = (rope(qc[:, t * LANES:(t + 1) * LANES]) * scale).astype(BF16)
    kvc = _dot(h, w_ref[:, Q_END:V_END])
    kt = rope(kvc[:, :KV_WIDTH]).T
    trow = lax.broadcasted_iota(jnp.int32, kt.shape, 0)
    top = jnp.where(trow < HEAD_DIM, kt, 0.0)
    bot = jnp.where(trow >= HEAD_DIM, kt, 0.0)
    kx_ref[0, 0] = top.astype(BF16)
    kx_ref[0, 1] = pltpu.roll(top, HEAD_DIM, 0).astype(BF16)
    kx_ref[0, 2] = pltpu.roll(bot, HEAD_DIM, 0).astype(BF16)
    kx_ref[0, 3] = bot.astype(BF16)
    vc = kvc[:, KV_WIDTH:]
    vsw = pltpu.roll(vc, HEAD_DIM, 1)
    lo = lane < HEAD_DIM
    ones_lo = jnp.where(lo, 1.0, 0.0).astype(BF16)
    ones_hi = jnp.where(lo, 0.0, 1.0).astype(BF16)
    for var, val in enumerate([jnp.where(lo, vc, 0.0), jnp.where(lo, 0.0, vsw),
                               jnp.where(lo, vsw, 0.0), jnp.where(lo, 0.0, vc)]):
        vx_ref[:, (2 * var) * LANES:(2 * var + 1) * LANES] = val.astype(BF16)
        vx_ref[:, (2 * var + 1) * LANES:(2 * var + 2) * LANES] = ones_hi if var % 2 else ones_lo
    n_u = u_ref.shape[1]
    for j in range(n_u // 512):
        u_ref[:, j * 512:(j + 1) * 512] = _dot(h, w_ref[:, V_END + j * 512:V_END + (j + 1) * 512]).astype(BF16)


def _inproj(x2d, g, w, cosf, sins, seq):
    n, d = x2d.shape
    tm = 512
    n_u = w.shape[1] - V_END
    spb = seq // tm
    return pl.pallas_call(
        _inproj_kernel,
        grid=(n // tm,),
        in_specs=[
            pl.BlockSpec((tm, d), lambda i: (i, 0)),
            pl.BlockSpec((1, d), lambda i: (0, 0)),
            pl.BlockSpec(w.shape, lambda i: (0, 0)),
            pl.BlockSpec((tm, LANES), lambda i: (i % spb, 0)),
            pl.BlockSpec((tm, LANES), lambda i: (i % spb, 0)),
        ],
        out_specs=[
            pl.BlockSpec((tm, ATTN_WIDTH), lambda i: (i, 0)),
            pl.BlockSpec((1, 2 * N_KV_HEADS, LANES, tm), lambda i: (i // spb, 0, 0, i % spb)),
            pl.BlockSpec((tm, 2 * N_KV_HEADS * MXU_DIM), lambda i: (i, 0)),
            pl.BlockSpec((tm, n_u), lambda i: (i, 0)),
        ],
        out_shape=[
            jax.ShapeDtypeStruct((n, ATTN_WIDTH), BF16),
            jax.ShapeDtypeStruct((n // seq, 2 * N_KV_HEADS, LANES, seq), BF16),
            jax.ShapeDtypeStruct((n, 2 * N_KV_HEADS * MXU_DIM), BF16),
            jax.ShapeDtypeStruct((n, n_u), BF16),
        ],
        compiler_params=_params("parallel"),
        name="inproj",
    )(x2d, g, w, cosf, sins)


def _battn_kernel(sink_ref, q_ref, kx_ref, vx_ref, o_ref, *, seq, tq):
    i = pl.program_id(1)
    blk = WINDOW
    n_blk = seq // blk
    row = lax.broadcasted_iota(jnp.int32, (blk, blk), 0)
    col = lax.broadcasted_iota(jnp.int32, (blk, blk), 1)
    tri_prev = jnp.where(col >= row, 0.0, MASKED)
    tri_next = jnp.where(col <= row, 0.0, MASKED)
    lo = col < HEAD_DIM
    heads_per_pair = LANES // HEAD_DIM
    group_pairs = N_HEADS // N_KV_HEADS // heads_per_pair
    units = [(jb, hp) for jb in range(tq // blk) for hp in range(N_HEADS // heads_per_pair)]

    def window(jb):
        bi = i * (tq // blk) + jb
        starts = [jnp.maximum(bi - 1, 0), bi, jnp.minimum(bi + 1, n_blk - 1)]
        return bi, [pl.multiple_of(s * blk, blk) for s in starts]

    def scores(jb, hp):
        _, starts = window(jb)
        qp = q_ref[0, jb * blk:(jb + 1) * blk, hp * LANES:(hp + 1) * LANES]
        var0 = heads_per_pair * (hp // group_pairs)
        kwin = jnp.concatenate([kx_ref[0, var0 + t, :, pl.ds(s, blk)]
                                for t in range(heads_per_pair) for s in starts], axis=1)
        s_both = _dot(qp, kwin)
        return [s_both[:, t * 3 * blk:(t + 1) * 3 * blk] for t in range(heads_per_pair)]

    def shifted(jb, hp, s_pair):
        bi, _ = window(jb)
        b_prev = tri_prev + jnp.where(bi == 0, MASKED, 0.0)
        b_next = tri_next + jnp.where(bi == n_blk - 1, MASKED, 0.0)
        out = []
        for t in range(heads_per_pair):
            s = s_pair[t]
            s0 = s[:, :blk] + b_prev
            s1 = s[:, blk:2 * blk]
            s2 = s[:, 2 * blk:] + b_next
            sk = sink_ref[hp * heads_per_pair + t] * LOG2E
            m = jnp.maximum(jnp.max(jnp.maximum(jnp.maximum(s0, s1), s2), axis=-1, keepdims=True), sk)
            x = jnp.concatenate([s0 - m, s1 - m, s2 - m], axis=1).astype(BF16)
            out.append((x, sk - m))
        return out

    def probs(x_pair):
        return [(jnp.exp2(x), jnp.exp2(d)) for x, d in x_pair]

    def finish(jb, hp, p_pair):
        _, starts = window(jb)
        var0 = heads_per_pair * (hp // group_pairs)
        vwin = jnp.concatenate([vx_ref[0, pl.ds(s, blk), (var0 + t) * MXU_DIM:(var0 + t + 1) * MXU_DIM]
                                for t in range(heads_per_pair) for s in starts], axis=0)
        p_both = jnp.concatenate([p_pair[t][0] for t in range(heads_per_pair)], axis=1)
        res = _dot(p_both, vwin)
        den = res[:, LANES:] + jnp.where(lo, p_pair[0][1], p_pair[1][1])
        o_ref[0, jb * blk:(jb + 1) * blk, hp * LANES:(hp + 1) * LANES] = (res[:, :LANES] * (1.0 / den)).astype(BF16)

    n_u = len(units)
    st_s, st_x, st_p = {}, {}, {}
    for n in range(n_u + 3):
        if 0 <= n - 3 < n_u:
            finish(*units[n - 3], st_p.pop(n - 3))
        if 0 <= n - 2 < n_u:
            st_p[n - 2] = probs(st_x.pop(n - 2))
        if 0 <= n - 1 < n_u:
            st_x[n - 1] = shifted(*units[n - 1], st_s.pop(n - 1))
        if n < n_u:
            st_s[n] = scores(*units[n])


def _band_attention(q, kx, vx, sink, seq):
    b = q.shape[0]
    tq = 512
    return pl.pallas_call(
        functools.partial(_battn_kernel, seq=seq, tq=tq),
        grid=(b, seq // tq),
        in_specs=[
            pl.BlockSpec(memory_space=pltpu.SMEM),
            pl.BlockSpec((1, tq, ATTN_WIDTH), lambda bi, i: (bi, i, 0)),
            pl.BlockSpec((1,) + kx.shape[1:], lambda bi, i: (bi, 0, 0, 0)),
            pl.BlockSpec((1,) + vx.shape[1:], lambda bi, i: (bi, 0, 0)),
        ],
        out_specs=pl.BlockSpec((1, tq, ATTN_WIDTH), lambda bi, i: (bi, i, 0)),
        out_shape=jax.ShapeDtypeStruct((b, seq, ATTN_WIDTH), BF16),
        compiler_params=_params("parallel", "arbitrary"),
        name="band_attention",
    )(sink, q, kx, vx)


def _filter_kernel(z_ref, t_ref, w1_ref, b1_ref, f1_ref, w2_ref, b2_ref, f2_ref, w3_ref, ad_ref, o_ref, *, tl):
    hp = lax.Precision.HIGHEST
    h = jnp.sin(f1_ref[...] * (jnp.dot(z_ref[...], w1_ref[...], precision=hp, preferred_element_type=F32)
                               + b1_ref[...]))
    h = jnp.sin(f2_ref[...] * (jnp.dot(h, w2_ref[...], precision=hp, preferred_element_type=F32) + b2_ref[...]))
    t = t_ref[...]
    rowid = pl.program_id(0) * tl + lax.broadcasted_iota(jnp.int32, (tl, HY_WIDTH), 0)
    for d in range(2):
        for o in range(HY_ORDER):
            c0 = (d * HY_ORDER + o) * HY_WIDTH
            v = jnp.dot(h, w3_ref[:, c0:c0 + HY_WIDTH], precision=hp, preferred_element_type=F32)
            v = v * jnp.exp(-t * ad_ref[:, o * HY_WIDTH:(o + 1) * HY_WIDTH])
            if d == 1:
                v = jnp.where(rowid == 0, 0.0, v)
            v = v.reshape(tl // FFT_N1, FFT_GROUPS, FFT_J, HY_WIDTH)
            o_ref[d * HY_ORDER + o] = jnp.concatenate([v, jnp.zeros_like(v)], axis=2).astype(BF16)


def _hyena_filters(z_pos, t_pos, w1, b1, f1, w2, b2, f2, w3, absd):
    L, e = z_pos.shape
    tl = 512
    full = lambda a: pl.BlockSpec(a.shape, lambda i: (0,) * a.ndim)
    return pl.pallas_call(
        functools.partial(_filter_kernel, tl=tl),
        grid=(L // tl,),
        in_specs=[
            pl.BlockSpec((tl, e), lambda i: (i, 0)),
            pl.BlockSpec((tl, 1), lambda i: (i, 0)),
            full(w1), full(b1), full(f1), full(w2), full(b2), full(f2), full(w3), full(absd),
        ],
        out_specs=pl.BlockSpec((2 * HY_ORDER, tl // FFT_N1, FFT_GROUPS, 2 * FFT_J, HY_WIDTH),
                               lambda i: (0, i, 0, 0, 0)),
        out_shape=jax.ShapeDtypeStruct((2 * HY_ORDER, L // FFT_N1, FFT_GROUPS, 2 * FFT_J, HY_WIDTH), BF16),
        compiler_params=_params("parallel"),
        name="hyena_filters",
    )(z_pos, t_pos, w1, b1, f1, w2, b2, f2, w3, absd)


def _shortconv_kernel(ua_ref, ub_ref, w_ref, b_ref, o_ref):
    def conv(u_ref):
        u = u_ref[0].astype(F32)
        L, ct = u.shape
        r8 = lax.broadcasted_iota(jnp.int32, (SUBLANES, ct), 0)
        down = pltpu.roll(u, 1, 0)
        up = pltpu.roll(u, L - 1, 0)
        prev = jnp.concatenate([jnp.where(r8 == 0, 0.0, down[:SUBLANES]), down[SUBLANES:]], axis=0)
        nxt = jnp.concatenate([up[:L - SUBLANES], jnp.where(r8 == SUBLANES - 1, 0.0, up[L - SUBLANES:])], axis=0)
        r = prev * w_ref[0:1, :] + u * w_ref[1:2, :] + nxt * w_ref[2:3, :] + b_ref[...]
        return r.reshape(L // FFT_N1, FFT_GROUPS, FFT_J, ct)

    o_ref[0] = jnp.concatenate([conv(ua_ref), conv(ub_ref)], axis=2).astype(BF16)


def _shortconv(u, w, bias):
    b, L, c3 = u.shape
    ct = MXU_DIM
    ncb = b // 2
    return pl.pallas_call(
        _shortconv_kernel,
        grid=(ncb, c3 // ct),
        in_specs=[
            pl.BlockSpec((1, L, ct), lambda bi, ci: (bi, 0, ci)),
            pl.BlockSpec((1, L, ct), lambda bi, ci: (bi + ncb, 0, ci)),
            pl.BlockSpec((3, ct), lambda bi, ci: (0, ci)),
            pl.BlockSpec((1, ct), lambda bi, ci: (0, ci)),
        ],
        out_specs=pl.BlockSpec((1, L // FFT_N1, FFT_GROUPS, 2 * FFT_J, ct), lambda bi, ci: (bi, 0, 0, 0, ci)),
        out_shape=jax.ShapeDtypeStruct((ncb, L // FFT_N1, FFT_GROUPS, 2 * FFT_J, c3), BF16),
        compiler_params=_params("parallel", "parallel"),
        name="shortconv",
    )(u, u, w, bias)


def _real_block(m, n, po, pi, sign, scale=1.0):
    ang = (2.0 * math.pi / n) * (m % n).astype(F32)
    re = jnp.cos(ang) * scale
    im = jnp.sin(ang) * (sign * scale)
    return jnp.where(po == pi, re, jnp.where(po > pi, im, -im))


def _fft_tables():
    nh = FFT_N2 // 2
    j2 = 2 * FFT_J

    def split(idx):
        return idx // j2, (idx // FFT_J) % 2, idx % FFT_J

    def small(n_out, n_in, sign, scale, out_is_freq):
        rows = FFT_GROUPS * n_out * j2
        a = lax.broadcasted_iota(jnp.int32, (n_in * 2, rows), 0)
        b = lax.broadcasted_iota(jnp.int32, (n_in * 2, rows), 1)
        major_out, po, j = split(b % (n_out * j2))
        g = b // (n_out * j2)
        major_in, pi = a // 2, a % 2
        k2, n2 = (major_out, major_in) if out_is_freq else (major_in, major_out)
        m = FFT_N1 * n2 * k2 + (FFT_J * g + j) * k2
        compact = _real_block(m, FFT_N, po, pi, sign, scale).astype(BF16)
        cols = n_in * j2
        rep = (lax.broadcasted_iota(jnp.int32, (n_in * 2, cols), 1) // FFT_J
               == lax.broadcasted_iota(jnp.int32, (n_in * 2, cols), 0)).astype(BF16)
        full = lax.dot_general(compact, rep, (((0,), (0,)), ((), ())), preferred_element_type=F32)
        diag = (lax.broadcasted_iota(jnp.int32, (rows, cols), 0) % FFT_J
                == lax.broadcasted_iota(jnp.int32, (rows, cols), 1) % FFT_J)
        return jnp.where(diag, full, 0.0).astype(BF16).reshape(FFT_GROUPS, n_out * j2, cols)

    g_fwd = small(FFT_N2, nh, -1.0, 1.0, True)
    g_inv = small(nh, FFT_N2, 1.0, 1.0 / FFT_N, False)

    def big(sign):
        r = lax.broadcasted_iota(jnp.int32, (2 * FFT_N1, 2 * FFT_N1), 0)
        c = lax.broadcasted_iota(jnp.int32, (2 * FFT_N1, 2 * FFT_N1), 1)
        gk, po, jk = split(r)
        g, pi, j = split(c)
        m = (FFT_J * gk + jk) * (FFT_J * g + j)
        return _real_block(m, FFT_N1, po, pi, sign).astype(BF16)

    return g_fwd, g_inv, big(-1.0), big(1.0)


def _small_fwd_rows(g_ref, s, tile):
    r = _dot(g_ref[s], tile)
    return r.reshape(FFT_N2, 2 * FFT_J, tile.shape[-1]).astype(BF16)


def _fs_kernel(z_ref, g_ref, o_ref):
    ct = o_ref.shape[-1]
    for s in range(FFT_SUB):
        tile = z_ref[0, :, s, :, :].reshape(FFT_N2 // 2 * 2 * FFT_J, ct)
        o_ref[0, :, s, :, :] = _small_fwd_rows(g_ref, s, tile)


def _fft_small_fwd(zil, g_fwd, *, c_off, n_c):
    ncb, nh, ng, _, _ = zil.shape
    ct = MXU_DIM
    cblk = c_off // ct
    return pl.pallas_call(
        _fs_kernel,
        grid=(ng // FFT_SUB, ncb, n_c // ct),
        in_specs=[
            pl.BlockSpec((1, nh, FFT_SUB, 2 * FFT_J, ct), lambda q, b, ci: (b, 0, q, 0, cblk + ci)),
            pl.BlockSpec((FFT_SUB,) + g_fwd.shape[1:], lambda q, b, ci: (q, 0, 0)),
        ],
        out_specs=pl.BlockSpec((1, FFT_N2, FFT_SUB, 2 * FFT_J, ct), lambda q, b, ci: (b, 0, q, 0, ci)),
        out_shape=jax.ShapeDtypeStruct((ncb, FFT_N2, ng, 2 * FFT_J, n_c), BF16),
        compiler_params=_params("parallel", "parallel", "arbitrary"),
        name="fft_small_fwd",
    )(zil, g_fwd)


def _big_kernel(x_ref, h_ref, fb_ref, fbi_ref, o_ref):
    ct = o_ref.shape[-1]
    h4 = h_ref[0, 0].reshape(FFT_GROUPS, 2, FFT_J, ct)
    hre = h4[:, 0]
    him = h4[:, 1]
    n_b = x_ref.shape[0]

    def forward(b):
        return _dot(fb_ref[...], x_ref[b, 0].reshape(2 * FFT_N1, ct)).reshape(FFT_GROUPS, 2, FFT_J, ct)

    xf = forward(0)
    for b in range(n_b):
        xr = xf[:, 0]
        xi = xf[:, 1]
        if b + 1 < n_b:
            xf = forward(b + 1)
        y = jnp.stack([xr * hre - xi * him, xr * him + xi * hre], axis=1).reshape(2 * FFT_N1, ct).astype(BF16)
        o_ref[b, 0] = _dot(fbi_ref[...], y).reshape(FFT_GROUPS, 2 * FFT_J, ct).astype(BF16)


def _fft_big(xs, spec, order, fb, fbi):
    ncb, n2, ng, _, c = xs.shape
    ct = MXU_DIM
    nbb = ncb
    return pl.pallas_call(
        _big_kernel,
        grid=(c // ct, n2, ncb // nbb),
        in_specs=[
            pl.BlockSpec((nbb, 1, ng, 2 * FFT_J, ct), lambda ci, k, b: (b, k, 0, 0, ci)),
            pl.BlockSpec((1, 1, ng, 2 * FFT_J, ct), lambda ci, k, b: (order, k, 0, 0, ci)),
            pl.BlockSpec(fb.shape, lambda ci, k, b: (0, 0)),
            pl.BlockSpec(fbi.shape, lambda ci, k, b: (0, 0)),
        ],
        out_specs=pl.BlockSpec((nbb, 1, ng, 2 * FFT_J, ct), lambda ci, k, b: (b, k, 0, 0, ci)),
        out_shape=jax.ShapeDtypeStruct(xs.shape, BF16),
        compiler_params=_params("parallel", "parallel", "arbitrary"),
        name="fft_big",
    )(xs, spec, fb, fbi)


def _spectrum_kernel(x_ref, fb_ref, o_ref):
    ct = o_ref.shape[-1]
    xf = [_dot(fb_ref[...], x_ref[i, 0].reshape(2 * FFT_N1, ct)).reshape(FFT_GROUPS, 2, FFT_J, ct)
          for i in range(2 * HY_ORDER)]
    for o in range(HY_ORDER):
        a = xf[o]
        r = xf[HY_ORDER + o]
        spec = jnp.stack([a[:, 0] + r[:, 0], a[:, 1] - r[:, 1]], axis=1)
        o_ref[o, 0] = spec.reshape(FFT_GROUPS, 2 * FFT_J, ct)


def _filter_spectrum(xs, fb):
    nf, n2, ng, _, c = xs.shape
    ct = MXU_DIM
    return pl.pallas_call(
        _spectrum_kernel,
        grid=(c // ct, n2),
        in_specs=[
            pl.BlockSpec((nf, 1, ng, 2 * FFT_J, ct), lambda ci, k: (0, k, 0, 0, ci)),
            pl.BlockSpec(fb.shape, lambda ci, k: (0, 0)),
        ],
        out_specs=pl.BlockSpec((HY_ORDER, 1, ng, 2 * FFT_J, ct), lambda ci, k: (0, k, 0, 0, ci)),
        out_shape=jax.ShapeDtypeStruct((HY_ORDER, n2, ng, 2 * FFT_J, c), F32),
        compiler_params=_params("parallel", "parallel"),
        name="filter_spectrum",
    )(xs, fb)


def _gated_inverse(c_ref, gi_ref, z_ref, gate_ref, skip, s):
    nh = FFT_N2 // 2
    ct = c_ref.shape[-1]
    rows = nh * 2 * FFT_J
    r = _dot(gi_ref[s], c_ref[0, :, s, :, :].reshape(FFT_N2 * 2 * FFT_J, ct))
    z = z_ref[0, :, s, :, :].astype(F32).reshape(rows, ct)
    gate = gate_ref[0, :, s, :, :].astype(F32).reshape(rows, ct)
    return gate * (r + z * skip)


def _is_fs_kernel(c_ref, gi_ref, z_ref, gate_ref, skip_ref, gf_ref, zz_ref, o_ref):
    nh = FFT_N2 // 2
    ct = o_ref.shape[-1]
    skip = skip_ref[...]
    for s in range(FFT_SUB):
        zz = _gated_inverse(c_ref, gi_ref, z_ref, gate_ref, skip, s).astype(BF16)
        zz_ref[0, :, s, :, :] = zz.reshape(nh, 2 * FFT_J, ct)
        o_ref[0, :, s, :, :] = _small_fwd_rows(gf_ref, s, zz)


def _is_last_kernel(c_ref, gi_ref, z_ref, gate_ref, skip_ref, y_ref):
    nh = FFT_N2 // 2
    ct = y_ref.shape[-1]
    skip = skip_ref[...]
    for s in range(FFT_SUB):
        y = _gated_inverse(c_ref, gi_ref, z_ref, gate_ref, skip, s).reshape(nh, 2, FFT_J, ct)
        for p in range(2):
            y_ref[p, 0, :, FFT_J * s:FFT_J * (s + 1), :] = y[:, p]


def _fft_small_inv(cs, g_inv, zil, z_off, gil, gate_off, skip, g_fwd=None):
    ncb, n2, ng, _, c = cs.shape
    nh = n2 // 2
    ct = MXU_DIM
    zb = z_off // ct
    gb = gate_off // ct
    til = lambda off: pl.BlockSpec((1, nh, FFT_SUB, 2 * FFT_J, ct), lambda q, b, ci: (b, 0, q, 0, off + ci))
    freq = pl.BlockSpec((1, n2, FFT_SUB, 2 * FFT_J, ct), lambda q, b, ci: (b, 0, q, 0, ci))
    mat = lambda m: pl.BlockSpec((FFT_SUB,) + m.shape[1:], lambda q, b, ci: (q, 0, 0))
    ins = [freq, mat(g_inv), til(zb), til(gb), pl.BlockSpec((1, ct), lambda q, b, ci: (0, ci))]
    args = [cs, g_inv, zil, gil, skip]
    if g_fwd is not None:
        body = _is_fs_kernel
        ins.append(mat(g_fwd))
        args.append(g_fwd)
        out_specs = [til(0), freq]
        out_shape = [jax.ShapeDtypeStruct((ncb, nh, ng, 2 * FFT_J, c), BF16), jax.ShapeDtypeStruct(cs.shape, BF16)]
    else:
        body = _is_last_kernel
        out_specs = pl.BlockSpec((2, 1, nh, FFT_CHUNK, ct), lambda q, b, ci: (0, b, 0, q, ci))
        out_shape = jax.ShapeDtypeStruct((2, ncb, nh, ng * FFT_J, c), F32)
    return pl.pallas_call(
        body,
        grid=(ng // FFT_SUB, ncb, c // ct),
        in_specs=ins,
        out_specs=out_specs,
        out_shape=out_shape,
        compiler_params=_params("parallel", "parallel", "arbitrary"),
        name="fft_small_inv",
    )(*args)


def _norm_mm_kernel(x_ref, g_ref, w_ref, o_ref):
    o_ref[...] = _dot(_rms(x_ref[...], g_ref[...]).astype(BF16), w_ref[...]).astype(o_ref.dtype)


def _norm_mm(x2d, g, w):
    n, d = x2d.shape
    tm = 512
    return pl.pallas_call(
        _norm_mm_kernel,
        grid=(n // tm,),
        in_specs=[
            pl.BlockSpec((tm, d), lambda i: (i, 0)),
            pl.BlockSpec((1, d), lambda i: (0, 0)),
            pl.BlockSpec(w.shape, lambda i: (0, 0)),
        ],
        out_specs=pl.BlockSpec((tm, w.shape[1]), lambda i: (i, 0)),
        out_shape=jax.ShapeDtypeStruct((n, w.shape[1]), BF16),
        compiler_params=_params("parallel"),
        name="memory_kv",
    )(x2d, g, w)


def _route(xf, g_ref, wr_ref, h_ref, r_ref, rt_ref):
    hf = _rms(xf, g_ref[...])
    hb = hf.astype(BF16)
    h_ref[...] = hb
    h_lo = (hf - hb.astype(F32)).astype(BF16)
    parts = _dot(hb, wr_ref[...]) + _dot(h_lo, wr_ref[...])
    logits = parts[:, :LANES] + parts[:, LANES:]
    lane = lax.broadcasted_iota(jnp.int32, logits.shape, 1)
    logits = jnp.where(lane < N_EXPERTS, logits, NEG)
    lanef = lane.astype(F32)
    big = float(LANES)
    m1 = jnp.max(logits, axis=-1, keepdims=True)
    i1 = jnp.min(jnp.where(logits == m1, lanef, big), axis=-1, keepdims=True)
    rest = jnp.where(lanef == i1, NEG, logits)
    m2 = jnp.max(rest, axis=-1, keepdims=True)
    i2 = jnp.min(jnp.where(rest == m2, lanef, big), axis=-1, keepdims=True)
    e2 = jnp.exp(m2 - m1)
    w1 = 1.0 / (1.0 + e2)
    w2 = e2 / (1.0 + e2)
    rec = jnp.where(lane == 0, i1, jnp.where(lane == 1, i2, jnp.where(lane == 2, w1, jnp.where(lane == 3, w2, 0.0))))
    r_ref[...] = rec
    rt_ref[...] = rec.T[:SUBLANES, :]


def _mix_xattn_kernel(x_ref, a_ref, y_ref, ga_ref, gy_ref, wo_ref, gx_ref, wq_ref, kv_ref, wxo_ref, o_ref):
    an = _rms(a_ref[0].astype(F32), ga_ref[...]).astype(BF16)
    yn = _rms(y_ref[0], gy_ref[...]).astype(BF16)
    x1 = x_ref[0] + _dot(an, wo_ref[:ATTN_WIDTH, :]) + _dot(yn, wo_ref[ATTN_WIDTH:, :])
    h = _rms(x1, gx_ref[...]).astype(BF16)
    q = (_dot(h, wq_ref[...]) * (X_HEAD_DIM ** -0.5)).astype(BF16)
    outs = []
    for hh in range(X_HEADS):
        cols = slice(hh * X_HEAD_DIM, (hh + 1) * X_HEAD_DIM)
        kh = kv_ref[0, :, cols]
        vh = kv_ref[0, :, X_WIDTH + hh * X_HEAD_DIM:X_WIDTH + (hh + 1) * X_HEAD_DIM]
        s = _dot_t(q[:, cols], kh)
        p = jnp.exp(s - jnp.max(s, axis=-1, keepdims=True))
        den = jnp.sum(p, axis=-1, keepdims=True)
        outs.append((_dot(p.astype(BF16), vh) / den).astype(BF16))
    o_ref[0] = x1 + _dot(jnp.concatenate(outs, axis=1), wxo_ref[...])


def _mix_xattn(x, a, y, ga, gy, wo, gx, wq, kv, wxo):
    b, s, d = x.shape
    tq = 512
    m = kv.shape[1]
    full = lambda arr: pl.BlockSpec(arr.shape, lambda bi, i: (0,) * arr.ndim)
    tok = lambda w: pl.BlockSpec((1, tq, w), lambda bi, i: (bi, i, 0))
    return pl.pallas_call(
        _mix_xattn_kernel,
        grid=(b, s // tq),
        in_specs=[tok(d), tok(ATTN_WIDTH), tok(HY_WIDTH), full(ga), full(gy), full(wo), full(gx), full(wq),
                  pl.BlockSpec((1, m, 2 * X_WIDTH), lambda bi, i: (bi, 0, 0)), full(wxo)],
        out_specs=tok(d),
        out_shape=jax.ShapeDtypeStruct((b, s, d), F32),
        compiler_params=_params("parallel", "arbitrary"),
        name="mix_xattn",
    )(x, a, y, ga, gy, wo, gx, wq, kv, wxo)


def _swiglu_chunks(h, wg, wu, wd, width, acc):
    for c in range(width // MXU_DIM):
        cols = slice(c * MXU_DIM, (c + 1) * MXU_DIM)
        g = _dot(h, wg(cols))
        u = _dot(h, wu(cols))
        a = (g * (1.0 / (1.0 + jnp.exp(-g))) * u).astype(BF16)
        acc = acc + _dot(a, wd(cols))
    return acc


def _ffn_kernel(x_ref, g_ref, wg_ref, wu_ref, wd_ref, o_ref):
    x = x_ref[...]
    h = _rms(x, g_ref[...]).astype(BF16)
    o_ref[...] = _swiglu_chunks(h, lambda c: wg_ref[:, c], lambda c: wu_ref[:, c], lambda c: wd_ref[c, :],
                                wg_ref.shape[1], x)


def _dense_ffn(x2d, g, wg, wu, wd):
    n, d = x2d.shape
    tm = 512
    resident = lambda w: pl.BlockSpec(w.shape, lambda i: (0, 0), pipeline_mode=pl.Buffered(1))
    return pl.pallas_call(
        _ffn_kernel,
        grid=(n // tm,),
        in_specs=[
            pl.BlockSpec((tm, d), lambda i: (i, 0)),
            pl.BlockSpec((1, d), lambda i: (0, 0)),
            resident(wg), resident(wu), resident(wd),
        ],
        out_specs=pl.BlockSpec((tm, d), lambda i: (i, 0)),
        out_shape=jax.ShapeDtypeStruct((n, d), F32),
        compiler_params=_params("parallel"),
        name="dense_ffn",
    )(x2d, g, wg, wu, wd)


def _router_kernel(x_ref, g_ref, wr_ref, h_ref, r_ref, rt_ref):
    _route(x_ref[...], g_ref, wr_ref, h_ref, r_ref, rt_ref)


def _router(x2d, g, wr):
    n, d = x2d.shape
    tm = 512
    return pl.pallas_call(
        _router_kernel,
        grid=(n // tm,),
        in_specs=[
            pl.BlockSpec((tm, d), lambda i: (i, 0)),
            pl.BlockSpec((1, d), lambda i: (0, 0)),
            pl.BlockSpec(wr.shape, lambda i: (0, 0)),
        ],
        out_specs=[pl.BlockSpec((tm, d), lambda i: (i, 0)), pl.BlockSpec((tm, LANES), lambda i: (i, 0)),
                   pl.BlockSpec((SUBLANES, tm), lambda i: (0, i))],
        out_shape=[jax.ShapeDtypeStruct((n, d), BF16), jax.ShapeDtypeStruct((n, LANES), F32),
                   jax.ShapeDtypeStruct((SUBLANES, n), F32)],
        compiler_params=_params("parallel"),
        name="router",
    )(x2d, g, wr)


def _moe_kernel(be_ref, x_ref, wg_ref, wu_ref, wd_ref, y_prev_ref, o_ref, acc_ref, *, n_blk):
    del y_prev_ref
    f = pl.program_id(1)

    @pl.when(f == 0)
    def _():
        acc_ref[...] = jnp.zeros_like(acc_ref)

    @pl.when(pl.program_id(0) < be_ref[n_blk])
    def _():
        acc_ref[...] = _swiglu_chunks(x_ref[...], lambda c: wg_ref[0, :, c], lambda c: wu_ref[0, :, c],
                                      lambda c: wd_ref[0, c, :], wg_ref.shape[2], acc_ref[...])

    @pl.when(f == pl.num_programs(1) - 1)
    def _():
        o_ref[...] = acc_ref[...].astype(BF16)


def _moe_experts(blk_meta, xb, wg, wu, wd, y_prev, blk_off):
    rows, d = xb.shape
    ff = wg.shape[2]
    tf = ff // 2
    nf = ff // tf
    n_blk = rows // MOE_ROWS
    ftile = lambda i, f, be: jnp.where(i < be[n_blk], f, nf - 1)
    grid_spec = pltpu.PrefetchScalarGridSpec(
        num_scalar_prefetch=1,
        grid=(n_blk, nf),
        in_specs=[
            pl.BlockSpec((MOE_ROWS, d), lambda i, f, be: (i, 0)),
            pl.BlockSpec((1, d, tf), lambda i, f, be: (be[i], 0, ftile(i, f, be))),
            pl.BlockSpec((1, d, tf), lambda i, f, be: (be[i], 0, ftile(i, f, be))),
            pl.BlockSpec((1, tf, d), lambda i, f, be: (be[i], ftile(i, f, be), 0)),
            pl.BlockSpec(memory_space=pl.ANY),
        ],
        out_specs=pl.BlockSpec((MOE_ROWS, d), lambda i, f, be: (blk_off + i, 0)),
        scratch_shapes=[pltpu.VMEM((MOE_ROWS, d), F32)],
    )
    return pl.pallas_call(
        functools.partial(_moe_kernel, n_blk=n_blk),
        grid_spec=grid_spec,
        out_shape=jax.ShapeDtypeStruct(y_prev.shape, BF16),
        input_output_aliases={5: 0},
        compiler_params=_params("parallel", "arbitrary"),
        name="moe_experts",
    )(blk_meta, xb, wg, wu, wd, y_prev)


def _combine_kernel(*refs, normed):
    if normed:
        x_ref, y1_ref, y2_ref, r_ref, g_ref, o_ref = refs
    else:
        x_ref, y1_ref, y2_ref, r_ref, o_ref = refs
    w1 = r_ref[:, 2:3]
    w2 = r_ref[:, 3:4]
    x = x_ref[...] + y1_ref[...].astype(F32) * w1 + y2_ref[...].astype(F32) * w2
    o_ref[...] = _rms(x, g_ref[...]) if normed else x


def _combine(x2d, y1, y2, route, gain):
    n, d = x2d.shape
    tm = 512
    normed = gain is not None
    tok = lambda w: pl.BlockSpec((tm, w), lambda i: (i, 0))
    ins = [tok(d), tok(d), tok(d), tok(LANES)]
    args = [x2d, y1, y2, route]
    if normed:
        ins.append(pl.BlockSpec((1, d), lambda i: (0, 0)))
        args.append(gain)
    return pl.pallas_call(
        functools.partial(_combine_kernel, normed=normed),
        grid=(n // tm,),
        in_specs=ins,
        out_specs=tok(d),
        out_shape=jax.ShapeDtypeStruct((n, d), F32),
        compiler_params=_params("parallel"),
        name="moe_combine",
    )(*args)


def _zero_rows_kernel(o_ref):
    o_ref[...] = jnp.zeros(o_ref.shape, o_ref.dtype)


def _zero_rows(n_blk, d):
    return pl.pallas_call(
        _zero_rows_kernel,
        grid=(n_blk,),
        out_specs=pl.BlockSpec((MOE_ROWS, d), lambda i: (i, 0)),
        out_shape=jax.ShapeDtypeStruct((n_blk * MOE_ROWS, d), BF16),
        compiler_params=_params("parallel"),
        name="moe_buffer_init",
    )()


def _norm_kernel(x_ref, g_ref, o_ref):
    o_ref[...] = _rms(x_ref[...], g_ref[...])


def _final_norm(x2d, g):
    n, d = x2d.shape
    tm = 512
    return pl.pallas_call(
        _norm_kernel,
        grid=(n // tm,),
        in_specs=[pl.BlockSpec((tm, d), lambda i: (i, 0)), pl.BlockSpec((1, d), lambda i: (0, 0))],
        out_specs=pl.BlockSpec((tm, d), lambda i: (i, 0)),
        out_shape=jax.ShapeDtypeStruct((n, d), F32),
        compiler_params=_params("parallel"),
        name="final_norm",
    )(x2d, g)


def _hyena(u, conv_w, conv_b, filt, skip, tables):
    g_fwd, g_inv, fb, fbi = tables
    b, L, c3 = u.shape
    c = c3 // (HY_ORDER + 1)
    uc = _shortconv(u, conv_w, conv_b.reshape(1, c3))
    spec = _filter_spectrum(_fft_small_fwd(filt, g_fwd, c_off=0, n_c=c), fb)
    t = _fft_small_fwd(uc, g_fwd, c_off=2 * c, n_c=c)
    t = _fft_big(t, spec, 0, fb, fbi)
    zz, t = _fft_small_inv(t, g_inv, uc, 2 * c, uc, 0, skip[0:1], g_fwd)
    t = _fft_big(t, spec, 1, fb, fbi)
    y5 = _fft_small_inv(t, g_inv, zz, 0, uc, c, skip[1:2])
    return y5.reshape(b, L, c)


def _moe_dispatch(route_t, n_tok):
    n_asg = n_tok * TOP_K
    flat_e = route_t[:TOP_K].astype(jnp.int32).reshape(n_asg)
    experts = jnp.arange(N_EXPERTS, dtype=jnp.int32)[:, None]
    onehot = (flat_e[None, :] == experts).astype(jnp.int32)
    csum = jnp.cumsum(onehot, axis=1)
    counts = csum[:, -1]
    rank = jnp.sum(onehot * (csum - 1), axis=0)
    padded = (counts + MOE_ROWS - 1) // MOE_ROWS * MOE_ROWS
    pad_end = jnp.cumsum(padded)
    pad_start = pad_end - padded
    dest = jnp.sum(onehot * pad_start[:, None], axis=0) + rank
    seg_start = jnp.cumsum(counts) - counts
    n_blk = -(-(n_asg + N_EXPERTS * (MOE_ROWS - 1)) // MOE_ROWS)
    blk_start = jnp.arange(n_blk, dtype=jnp.int32) * MOE_ROWS
    blk_e = jnp.minimum(jnp.sum((blk_start[:, None] >= pad_end[None, :]).astype(jnp.int32), axis=1), N_EXPERTS - 1)
    order = jnp.argsort(flat_e, stable=True).astype(jnp.int32)
    e_row = jnp.repeat(blk_e, MOE_ROWS)
    r = jnp.arange(n_blk * MOE_ROWS, dtype=jnp.int32) - pad_start[e_row]
    src = order[jnp.clip(seg_start[e_row] + r, 0, n_asg - 1)]
    buf_tok = jnp.where(r < counts[e_row], src % n_tok, 0)
    n_used = (pad_end[-1] // MOE_ROWS).astype(jnp.int32)
    return buf_tok, blk_e.astype(jnp.int32), n_used, dest.reshape(TOP_K, n_tok)


def kernel(x, mem, mem_norm, mix_norm, w_in, attn_sink, hy_conv_w, hy_conv_b, hy_f_w1, hy_f_b1, hy_f_freq1,
           hy_f_w2, hy_f_b2, hy_f_freq2, hy_f_w3, hy_skip, attn_out_norm, hy_out_norm, w_out, xattn_norm,
           xw_q, xw_k, xw_v, xw_o, ffn_norm, ffn_w_gate, ffn_w_up, ffn_w_down,
           moe_router, moe_w_gate, moe_w_up, moe_w_down, final_norm):
    b, s, d = x.shape
    depth = w_in.shape[0]
    n_tok = b * s
    assert s == FFT_N // 2 and b % 2 == 0
    row = lambda v: v.reshape(1, -1).astype(F32)

    pos = jnp.arange(s, dtype=F32)
    inv = ROPE_THETA ** (-jnp.arange(0, HEAD_DIM, 2, dtype=F32) / HEAD_DIM)
    ang = pos[:, None] * inv[None, :]
    cosf = jnp.tile(jnp.cos(ang), (1, LANES // (HEAD_DIM // 2)))
    sins = jnp.tile(jnp.concatenate([-jnp.sin(ang), jnp.sin(ang)], axis=1), (1, LANES // HEAD_DIM))
    t_pos = jnp.linspace(0.0, 1.0, s, dtype=F32)[:, None]
    wv = 2.0 * math.pi * jnp.arange(s, dtype=F32)[:, None] / s
    fr = jnp.linspace(1e-4, HY_BANDS - 1, HY_BANDS, dtype=F32)[None, :]
    z_pos = jnp.concatenate([t_pos, jnp.cos(fr * wv), -jnp.sin(fr * wv)], axis=-1)
    emb_pad = 64
    z_pos = jnp.pad(z_pos, ((0, 0), (0, emb_pad - HY_EMB)))
    absd = jnp.abs(jnp.linspace(HY_MIN_DECAY, HY_MAX_DECAY, HY_ORDER * HY_WIDTH, dtype=F32)).reshape(1, -1)
    tables = _fft_tables()

    mem2d = mem.reshape(-1, d)
    x2d = x.reshape(n_tok, d)
    out = None
    for l in range(depth):
        q, kx, vx, u = _inproj(x2d, row(mix_norm[l]), w_in[l].astype(BF16), cosf, sins, s)
        a = _band_attention(q.reshape(b, s, ATTN_WIDTH), kx, vx.reshape(b, s, -1), attn_sink[l].astype(F32), s)
        filt = _hyena_filters(
            z_pos, t_pos, jnp.pad(hy_f_w1[l], ((0, emb_pad - HY_EMB), (0, 0))), row(hy_f_b1[l]),
            row(hy_f_freq1[l]), hy_f_w2[l], row(hy_f_b2[l]), row(hy_f_freq2[l]), hy_f_w3[l], absd)
        y = _hyena(u.reshape(b, s, -1), hy_conv_w[l], hy_conv_b[l], filt, hy_skip[l], tables)
        mkv = _norm_mm(mem2d, row(mem_norm), jnp.concatenate([xw_k[l], xw_v[l]], axis=1).astype(BF16))
        j = l // 2
        x3 = _mix_xattn(x2d.reshape(b, s, d), a, y, row(attn_out_norm[l]), row(hy_out_norm[l]),
                        w_out[l].astype(BF16), row(xattn_norm[l]), xw_q[l].astype(BF16),
                        mkv.reshape(b, -1, 2 * X_WIDTH), xw_o[l].astype(BF16))
        x2d = x3.reshape(n_tok, d)
        last = l == depth - 1
        if l % 2 == 0:
            x2d = _dense_ffn(x2d, row(ffn_norm[l]), ffn_w_gate[j].astype(BF16), ffn_w_up[j].astype(BF16),
                             ffn_w_down[j].astype(BF16))
            if last:
                out = _final_norm(x2d, row(final_norm))
        else:
            wr = jnp.pad(moe_router[j].astype(F32), ((0, 0), (0, LANES - N_EXPERTS)))
            wr_hi = wr.astype(BF16)
            wr = jnp.concatenate([wr_hi, (wr - wr_hi.astype(F32)).astype(BF16)], axis=1)
            hb, route, route_t = _router(x2d, row(ffn_norm[l]), wr)
            buf_tok, blk_e, n_used, dest = _moe_dispatch(route_t, n_tok)
            wg, wu, wd = (w[j].astype(BF16) for w in (moe_w_gate, moe_w_up, moe_w_down))
            n_blk = blk_e.shape[0]
            per = n_blk // MOE_SLICES
            assert per * MOE_SLICES == n_blk
            yb = _zero_rows(n_blk, d)
            for k in range(MOE_SLICES):
                lo = k * per
                meta = jnp.concatenate([blk_e[lo:lo + per], (n_used - lo)[None]])
                xb = hb[buf_tok[lo * MOE_ROWS:(lo + per) * MOE_ROWS]]
                yb = _moe_experts(meta, xb, wg, wu, wd, yb, lo)
            res = _combine(x2d, yb[dest[0]], yb[dest[1]], route, row(final_norm) if last else None)
            if last:
                out = res
            else:
                x2d = res
    return out.reshape(b, s, d)
```

```python
import functools
import math

import jax
import jax.numpy as jnp
from jax import lax
from jax.experimental import pallas as pl
from jax.experimental.pallas import tpu as pltpu

F32 = jnp.float32
BF16 = jnp.bfloat16

EPS = 1e-6
N_HEADS = 8
N_KV_HEADS = 2
HEAD_DIM = 64
ATTN_WIDTH = N_HEADS * HEAD_DIM
KV_WIDTH = N_KV_HEADS * HEAD_DIM
WINDOW = 128
ROPE_THETA = 10000.0
HY_WIDTH = 512
HY_ORDER = 2
HY_EMB = 33
HY_BANDS = (HY_EMB - 1) // 2
HY_FILTER_HIDDEN = 64
HY_TARGET = 1e-2
HY_FAST_DECAY = 0.3
HY_SLOW_DECAY = 1.5
HY_MIN_DECAY = math.log(HY_TARGET) / HY_SLOW_DECAY
HY_MAX_DECAY = math.log(HY_TARGET) / HY_FAST_DECAY
Q_END = ATTN_WIDTH
K_END = Q_END + KV_WIDTH
V_END = K_END + KV_WIDTH
X_HEADS = 4
X_HEAD_DIM = 128
X_WIDTH = X_HEADS * X_HEAD_DIM
N_EXPERTS = 8
TOP_K = 2

LANES = 128
SUBLANES = 8
MXU_DIM = 256
VMEM_LIMIT = 56 * 1024 * 1024

FFT_N1 = 256
FFT_N2 = 32
FFT_N = FFT_N1 * FFT_N2
FFT_J = SUBLANES
FFT_GROUPS = FFT_N1 // FFT_J
FFT_CHUNK = 64
FFT_SUB = FFT_CHUNK // FFT_J

MOE_ROWS = 1024
MOE_SLICES = 17
NEG = float(jnp.finfo(jnp.float32).min)
MASKED = -1e30
LOG2E = math.log2(math.e)


def _params(*sem):
    return pltpu.CompilerParams(dimension_semantics=sem, vmem_limit_bytes=VMEM_LIMIT)


def _dot(a, b):
    return jnp.dot(a, b, preferred_element_type=F32)


def _dot_t(a, b):
    return lax.dot_general(a, b, (((1,), (1,)), ((), ())), preferred_element_type=F32)


def _rms(xf, g):
    ms = jnp.mean(xf * xf, axis=-1, keepdims=True)
    return xf * lax.rsqrt(ms + EPS) * g


def _inproj_kernel(x_ref, g_ref, w_ref, cos_ref, sin_ref, q_ref, kx_ref, vx_ref, u_ref):
    h = _rms(x_ref[...], g_ref[...]).astype(BF16)
    cosf = cos_ref[...]
    sins = sin_ref[...]
    lane = lax.broadcasted_iota(jnp.int32, cosf.shape, 1)
    low = (lane % HEAD_DIM) < (HEAD_DIM // 2)

    def rope(c):
        rot = jnp.where(low, pltpu.roll(c, LANES - HEAD_DIM // 2, 1), pltpu.roll(c, HEAD_DIM // 2, 1))
        return c * cosf + rot * sins

    scale = HEAD_DIM ** -0.5 * LOG2E
    for j in range(ATTN_WIDTH // MXU_DIM):
        qc = _dot(h, w_ref[:, j * MXU_DIM:(j + 1) * MXU_DIM])
        for t in range(MXU_DIM // LANES):
            c0 = j * MXU_DIM + t * LANES
            q_ref[:, c0:c0 + LANES] = (rope(qc[:, t * LANES:(t + 1) * LANES]) * scale).astype(BF16)
    kvc = _dot(h, w_ref[:, Q_END:V_END])
    kt = rope(kvc[:, :KV_WIDTH]).T
    trow = lax.broadcasted_iota(jnp.int32, kt.shape, 0)
    top = jnp.where(trow < HEAD_DIM, kt, 0.0)
    bot = jnp.where(trow >= HEAD_DIM, kt, 0.0)
    kx_ref[0, 0] = top.astype(BF16)
    kx_ref[0, 1] = pltpu.roll(top, HEAD_DIM, 0).astype(BF16)
    kx_ref[0, 2] = pltpu.roll(bot, HEAD_DIM, 0).astype(BF16)
    kx_ref[0, 3] = bot.astype(BF16)
    vc = kvc[:, KV_WIDTH:]
    vsw = pltpu.roll(vc, HEAD_DIM, 1)
    lo = lane < HEAD_DIM
    vx_ref[:, 0 * LANES:1 * LANES] = jnp.where(lo, vc, 1.0).astype(BF16)
    vx_ref[:, 1 * LANES:2 * LANES] = jnp.where(lo, 1.0, vsw).astype(BF16)
    vx_ref[:, 2 * LANES:3 * LANES] = jnp.where(lo, vsw, 1.0).astype(BF16)
    vx_ref[:, 3 * LANES:4 * LANES] = jnp.where(lo, 1.0, vc).astype(BF16)
    n_u = u_ref.shape[1]
    for j in range(n_u // 512):
        u_ref[:, j * 512:(j + 1) * 512] = _dot(h, w_ref[:, V_END + j * 512:V_END + (j + 1) * 512]).astype(BF16)


def _inproj(x2d, g, w, cosf, sins, seq):
    n, d = x2d.shape
    tm = 512
    n_u = w.shape[1] - V_END
    spb = seq // tm
    return pl.pallas_call(
        _inproj_kernel,
        grid=(n // tm,),
        in_specs=[
            pl.BlockSpec((tm, d), lambda i: (i, 0)),
            pl.BlockSpec((1, d), lambda i: (0, 0)),
            pl.BlockSpec(w.shape, lambda i: (0, 0)),
            pl.BlockSpec((tm, LANES), lambda i: (i % spb, 0)),
            pl.BlockSpec((tm, LANES), lambda i: (i % spb, 0)),
        ],
        out_specs=[
            pl.BlockSpec((tm, ATTN_WIDTH), lambda i: (i, 0)),
            pl.BlockSpec((1, 2 * N_KV_HEADS, LANES, tm), lambda i: (i // spb, 0, 0, i % spb)),
            pl.BlockSpec((tm, 2 * N_KV_HEADS * LANES), lambda i: (i, 0)),
            pl.BlockSpec((tm, n_u), lambda i: (i, 0)),
        ],
        out_shape=[
            jax.ShapeDtypeStruct((n, ATTN_WIDTH), BF16),
            jax.ShapeDtypeStruct((n // seq, 2 * N_KV_HEADS, LANES, seq), BF16),
            jax.ShapeDtypeStruct((n, 2 * N_KV_HEADS * LANES), BF16),
            jax.ShapeDtypeStruct((n, n_u), BF16),
        ],
        compiler_params=_params("parallel"),
        name="inproj",
    )(x2d, g, w, cosf, sins)


def _battn_kernel(sink_ref, q_ref, kx_ref, vx_ref, o_ref, *, seq, tq):
    i = pl.program_id(1)
    blk = WINDOW
    n_blk = seq // blk
    row = lax.broadcasted_iota(jnp.int32, (blk, blk), 0)
    col = lax.broadcasted_iota(jnp.int32, (blk, blk), 1)
    tri_prev = jnp.where(col >= row, 0.0, MASKED)
    tri_next = jnp.where(col <= row, 0.0, MASKED)
    lo = col < HEAD_DIM
    heads_per_pair = LANES // HEAD_DIM
    group_pairs = N_HEADS // N_KV_HEADS // heads_per_pair
    units = [(jb, hp) for jb in range(tq // blk) for hp in range(N_HEADS // heads_per_pair)]

    def window(jb):
        bi = i * (tq // blk) + jb
        starts = [jnp.maximum(bi - 1, 0), bi, jnp.minimum(bi + 1, n_blk - 1)]
        return bi, [pl.multiple_of(s * blk, blk) for s in starts]

    def scores(jb, hp):
        _, starts = window(jb)
        qp = q_ref[0, jb * blk:(jb + 1) * blk, hp * LANES:(hp + 1) * LANES]
        out = []
        for t in range(heads_per_pair):
            var = heads_per_pair * (hp // group_pairs) + t
            kwin = jnp.concatenate([kx_ref[0, var, :, pl.ds(s, blk)] for s in starts], axis=1)
            out.append(_dot(qp, kwin))
        return out

    def shifted(jb, hp, s_pair):
        bi, _ = window(jb)
        b_prev = tri_prev + jnp.where(bi == 0, MASKED, 0.0)
        b_next = tri_next + jnp.where(bi == n_blk - 1, MASKED, 0.0)
        out = []
        for t in range(heads_per_pair):
            s = s_pair[t]
            s0 = s[:, :blk] + b_prev
            s1 = s[:, blk:2 * blk]
            s2 = s[:, 2 * blk:] + b_next
            sk = sink_ref[hp * heads_per_pair + t] * LOG2E
            m = jnp.maximum(jnp.max(jnp.maximum(jnp.maximum(s0, s1), s2), axis=-1, keepdims=True), sk)
            x = jnp.concatenate([s0 - m, s1 - m, s2 - m], axis=1).astype(BF16)
            out.append((x, sk - m))
        return out

    def probs(x_pair):
        return [(jnp.exp2(x), jnp.exp2(d)) for x, d in x_pair]

    def finish(jb, hp, p_pair):
        _, starts = window(jb)
        res = []
        for t in range(heads_per_pair):
            var = heads_per_pair * (hp // group_pairs) + t
            vwin = jnp.concatenate([vx_ref[0, pl.ds(s, blk), var * LANES:(var + 1) * LANES] for s in starts], axis=0)
            res.append(_dot(p_pair[t][0], vwin))
        num = jnp.where(lo, res[0], res[1])
        den = pltpu.roll(jnp.where(lo, res[1], res[0]), HEAD_DIM, 1) + jnp.where(lo, p_pair[0][1], p_pair[1][1])
        o_ref[0, jb * blk:(jb + 1) * blk, hp * LANES:(hp + 1) * LANES] = (num * (1.0 / den)).astype(BF16)

    n_u = len(units)
    st_s, st_x, st_p = {}, {}, {}
    for n in range(n_u + 3):
        if 0 <= n - 3 < n_u:
            finish(*units[n - 3], st_p.pop(n - 3))
        if 0 <= n - 2 < n_u:
            st_p[n - 2] = probs(st_x.pop(n - 2))
        if 0 <= n - 1 < n_u:
            st_x[n - 1] = shifted(*units[n - 1], st_s.pop(n - 1))
        if n < n_u:
            st_s[n] = scores(*units[n])


def _band_attention(q, kx, vx, sink, seq):
    b = q.shape[0]
    tq = 512
    return pl.pallas_call(
        functools.partial(_battn_kernel, seq=seq, tq=tq),
        grid=(b, seq // tq),
        in_specs=[
            pl.BlockSpec(memory_space=pltpu.SMEM),
            pl.BlockSpec((1, tq, ATTN_WIDTH), lambda bi, i: (bi, i, 0)),
            pl.BlockSpec((1,) + kx.shape[1:], lambda bi, i: (bi, 0, 0, 0)),
            pl.BlockSpec((1,) + vx.shape[1:], lambda bi, i: (bi, 0, 0)),
        ],
        out_specs=pl.BlockSpec((1, tq, ATTN_WIDTH), lambda bi, i: (bi, i, 0)),
        out_shape=jax.ShapeDtypeStruct((b, seq, ATTN_WIDTH), BF16),
        compiler_params=_params("parallel", "arbitrary"),
        name="band_attention",
    )(sink, q, kx, vx)


def _filter_kernel(z_ref, t_ref, w1_ref, b1_ref, f1_ref, w2_ref, b2_ref, f2_ref, w3_ref, ad_ref, o_ref, *, tl):
    hp = lax.Precision.HIGHEST
    h = jnp.sin(f1_ref[...] * (jnp.dot(z_ref[...], w1_ref[...], precision=hp, preferred_element_type=F32)
                               + b1_ref[...]))
    h = jnp.sin(f2_ref[...] * (jnp.dot(h, w2_ref[...], precision=hp, preferred_element_type=F32) + b2_ref[...]))
    t = t_ref[...]
    rowid = pl.program_id(0) * tl + lax.broadcasted_iota(jnp.int32, (tl, HY_WIDTH), 0)
    for d in range(2):
        for o in range(HY_ORDER):
            c0 = (d * HY_ORDER + o) * HY_WIDTH
            v = jnp.dot(h, w3_ref[:, c0:c0 + HY_WIDTH], precision=hp, preferred_element_type=F32)
            v = v * jnp.exp(-t * ad_ref[:, o * HY_WIDTH:(o + 1) * HY_WIDTH])
            if d == 1:
                v = jnp.where(rowid == 0, 0.0, v)
            v = v.reshape(tl // FFT_N1, FFT_GROUPS, FFT_J, HY_WIDTH)
            o_ref[d * HY_ORDER + o] = jnp.concatenate([v, jnp.zeros_like(v)], axis=2).astype(BF16)


def _hyena_filters(z_pos, t_pos, w1, b1, f1, w2, b2, f2, w3, absd):
    L, e = z_pos.shape
    tl = 512
    full = lambda a: pl.BlockSpec(a.shape, lambda i: (0,) * a.ndim)
    return pl.pallas_call(
        functools.partial(_filter_kernel, tl=tl),
        grid=(L // tl,),
        in_specs=[
            pl.BlockSpec((tl, e), lambda i: (i, 0)),
            pl.BlockSpec((tl, 1), lambda i: (i, 0)),
            full(w1), full(b1), full(f1), full(w2), full(b2), full(f2), full(w3), full(absd),
        ],
        out_specs=pl.BlockSpec((2 * HY_ORDER, tl // FFT_N1, FFT_GROUPS, 2 * FFT_J, HY_WIDTH),
                               lambda i: (0, i, 0, 0, 0)),
        out_shape=jax.ShapeDtypeStruct((2 * HY_ORDER, L // FFT_N1, FFT_GROUPS, 2 * FFT_J, HY_WIDTH), BF16),
        compiler_params=_params("parallel"),
        name="hyena_filters",
    )(z_pos, t_pos, w1, b1, f1, w2, b2, f2, w3, absd)


def _shortconv_kernel(ua_ref, ub_ref, w_ref, b_ref, o_ref):
    def conv(u_ref):
        u = u_ref[0].astype(F32)
        L, ct = u.shape
        r8 = lax.broadcasted_iota(jnp.int32, (SUBLANES, ct), 0)
        down = pltpu.roll(u, 1, 0)
        up = pltpu.roll(u, L - 1, 0)
        prev = jnp.concatenate([jnp.where(r8 == 0, 0.0, down[:SUBLANES]), down[SUBLANES:]], axis=0)
        nxt = jnp.concatenate([up[:L - SUBLANES], jnp.where(r8 == SUBLANES - 1, 0.0, up[L - SUBLANES:])], axis=0)
        r = prev * w_ref[0:1, :] + u * w_ref[1:2, :] + nxt * w_ref[2:3, :] + b_ref[...]
        return r.reshape(L // FFT_N1, FFT_GROUPS, FFT_J, ct)

    o_ref[0] = jnp.concatenate([conv(ua_ref), conv(ub_ref)], axis=2).astype(BF16)


def _shortconv(u, w, bias):
    b, L, c3 = u.shape
    ct = MXU_DIM
    ncb = b // 2
    return pl.pallas_call(
        _shortconv_kernel,
        grid=(ncb, c3 // ct),
        in_specs=[
            pl.BlockSpec((1, L, ct), lambda bi, ci: (bi, 0, ci)),
            pl.BlockSpec((1, L, ct), lambda bi, ci: (bi + ncb, 0, ci)),
            pl.BlockSpec((3, ct), lambda bi, ci: (0, ci)),
            pl.BlockSpec((1, ct), lambda bi, ci: (0, ci)),
        ],
        out_specs=pl.BlockSpec((1, L // FFT_N1, FFT_GROUPS, 2 * FFT_J, ct), lambda bi, ci: (bi, 0, 0, 0, ci)),
        out_shape=jax.ShapeDtypeStruct((ncb, L // FFT_N1, FFT_GROUPS, 2 * FFT_J, c3), BF16),
        compiler_params=_params("parallel", "parallel"),
        name="shortconv",
    )(u, u, w, bias)


def _real_block(m, n, po, pi, sign, scale=1.0):
    ang = (2.0 * math.pi / n) * (m % n).astype(F32)
    re = jnp.cos(ang) * scale
    im = jnp.sin(ang) * (sign * scale)
    return jnp.where(po == pi, re, jnp.where(po > pi, im, -im))


def _fft_tables():
    nh = FFT_N2 // 2
    j2 = 2 * FFT_J

    def split(idx):
        return idx // j2, (idx // FFT_J) % 2, idx % FFT_J

    def small(n_out, n_in, sign, scale, out_is_freq):
        rows = FFT_GROUPS * n_out * j2
        a = lax.broadcasted_iota(jnp.int32, (n_in * 2, rows), 0)
        b = lax.broadcasted_iota(jnp.int32, (n_in * 2, rows), 1)
        major_out, po, j = split(b % (n_out * j2))
        g = b // (n_out * j2)
        major_in, pi = a // 2, a % 2
        k2, n2 = (major_out, major_in) if out_is_freq else (major_in, major_out)
        m = FFT_N1 * n2 * k2 + (FFT_J * g + j) * k2
        compact = _real_block(m, FFT_N, po, pi, sign, scale).astype(BF16)
        cols = n_in * j2
        rep = (lax.broadcasted_iota(jnp.int32, (n_in * 2, cols), 1) // FFT_J
               == lax.broadcasted_iota(jnp.int32, (n_in * 2, cols), 0)).astype(BF16)
        full = lax.dot_general(compact, rep, (((0,), (0,)), ((), ())), preferred_element_type=F32)
        diag = (lax.broadcasted_iota(jnp.int32, (rows, cols), 0) % FFT_J
                == lax.broadcasted_iota(jnp.int32, (rows, cols), 1) % FFT_J)
        return jnp.where(diag, full, 0.0).astype(BF16).reshape(FFT_GROUPS, n_out * j2, cols)

    g_fwd = small(FFT_N2, nh, -1.0, 1.0, True)
    g_inv = small(nh, FFT_N2, 1.0, 1.0 / FFT_N, False)

    def big(sign):
        r = lax.broadcasted_iota(jnp.int32, (2 * FFT_N1, 2 * FFT_N1), 0)
        c = lax.broadcasted_iota(jnp.int32, (2 * FFT_N1, 2 * FFT_N1), 1)
        gk, po, jk = split(r)
        g, pi, j = split(c)
        m = (FFT_J * gk + jk) * (FFT_J * g + j)
        return _real_block(m, FFT_N1, po, pi, sign).astype(BF16)

    return g_fwd, g_inv, big(-1.0), big(1.0)


def _small_fwd_rows(g_ref, s, tile):
    r = _dot(g_ref[s], tile)
    return r.reshape(FFT_N2, 2 * FFT_J, tile.shape[-1]).astype(BF16)


def _fs_kernel(z_ref, g_ref, o_ref):
    ct = o_ref.shape[-1]
    for s in range(FFT_SUB):
        tile = z_ref[0, :, s, :, :].reshape(FFT_N2 // 2 * 2 * FFT_J, ct)
        o_ref[0, :, s, :, :] = _small_fwd_rows(g_ref, s, tile)


def _fft_small_fwd(zil, g_fwd, *, c_off, n_c):
    ncb, nh, ng, _, _ = zil.shape
    ct = MXU_DIM
    cblk = c_off // ct
    return pl.pallas_call(
        _fs_kernel,
        grid=(ng // FFT_SUB, ncb, n_c // ct),
        in_specs=[
            pl.BlockSpec((1, nh, FFT_SUB, 2 * FFT_J, ct), lambda q, b, ci: (b, 0, q, 0, cblk + ci)),
            pl.BlockSpec((FFT_SUB,) + g_fwd.shape[1:], lambda q, b, ci: (q, 0, 0)),
        ],
        out_specs=pl.BlockSpec((1, FFT_N2, FFT_SUB, 2 * FFT_J, ct), lambda q, b, ci: (b, 0, q, 0, ci)),
        out_shape=jax.ShapeDtypeStruct((ncb, FFT_N2, ng, 2 * FFT_J, n_c), BF16),
        compiler_params=_params("parallel", "parallel", "arbitrary"),
        name="fft_small_fwd",
    )(zil, g_fwd)


def _big_kernel(x_ref, h_ref, fb_ref, fbi_ref, o_ref):
    ct = o_ref.shape[-1]
    h4 = h_ref[0, 0].reshape(FFT_GROUPS, 2, FFT_J, ct)
    hre = h4[:, 0]
    him = h4[:, 1]
    n_b = x_ref.shape[0]

    def forward(b):
        return _dot(fb_ref[...], x_ref[b, 0].reshape(2 * FFT_N1, ct)).reshape(FFT_GROUPS, 2, FFT_J, ct)

    xf = forward(0)
    for b in range(n_b):
        xr = xf[:, 0]
        xi = xf[:, 1]
        if b + 1 < n_b:
            xf = forward(b + 1)
        y = jnp.stack([xr * hre - xi * him, xr * him + xi * hre], axis=1).reshape(2 * FFT_N1, ct).astype(BF16)
        o_ref[b, 0] = _dot(fbi_ref[...], y).reshape(FFT_GROUPS, 2 * FFT_J, ct).astype(BF16)


def _fft_big(xs, spec, order, fb, fbi):
    ncb, n2, ng, _, c = xs.shape
    ct = MXU_DIM
    nbb = ncb
    return pl.pallas_call(
        _big_kernel,
        grid=(c // ct, n2, ncb // nbb),
        in_specs=[
            pl.BlockSpec((nbb, 1, ng, 2 * FFT_J, ct), lambda ci, k, b: (b, k, 0, 0, ci)),
            pl.BlockSpec((1, 1, ng, 2 * FFT_J, ct), lambda ci, k, b: (order, k, 0, 0, ci)),
            pl.BlockSpec(fb.shape, lambda ci, k, b: (0, 0)),
            pl.BlockSpec(fbi.shape, lambda ci, k, b: (0, 0)),
        ],
        out_specs=pl.BlockSpec((nbb, 1, ng, 2 * FFT_J, ct), lambda ci, k, b: (b, k, 0, 0, ci)),
        out_shape=jax.ShapeDtypeStruct(xs.shape, BF16),
        compiler_params=_params("parallel", "parallel", "arbitrary"),
        name="fft_big",
    )(xs, spec, fb, fbi)


def _spectrum_kernel(x_ref, fb_ref, o_ref):
    ct = o_ref.shape[-1]
    xf = [_dot(fb_ref[...], x_ref[i, 0].reshape(2 * FFT_N1, ct)).reshape(FFT_GROUPS, 2, FFT_J, ct)
          for i in range(2 * HY_ORDER)]
    for o in range(HY_ORDER):
        a = xf[o]
        r = xf[HY_ORDER + o]
        spec = jnp.stack([a[:, 0] + r[:, 0], a[:, 1] - r[:, 1]], axis=1)
        o_ref[o, 0] = spec.reshape(FFT_GROUPS, 2 * FFT_J, ct)


def _filter_spectrum(xs, fb):
    nf, n2, ng, _, c = xs.shape
    ct = MXU_DIM
    return pl.pallas_call(
        _spectrum_kernel,
        grid=(c // ct, n2),
        in_specs=[
            pl.BlockSpec((nf, 1, ng, 2 * FFT_J, ct), lambda ci, k: (0, k, 0, 0, ci)),
            pl.BlockSpec(fb.shape, lambda ci, k: (0, 0)),
        ],
        out_specs=pl.BlockSpec((HY_ORDER, 1, ng, 2 * FFT_J, ct), lambda ci, k: (0, k, 0, 0, ci)),
        out_shape=jax.ShapeDtypeStruct((HY_ORDER, n2, ng, 2 * FFT_J, c), F32),
        compiler_params=_params("parallel", "parallel"),
        name="filter_spectrum",
    )(xs, fb)


def _gated_inverse(c_ref, gi_ref, z_ref, gate_ref, skip, s):
    nh = FFT_N2 // 2
    ct = c_ref.shape[-1]
    rows = nh * 2 * FFT_J
    r = _dot(gi_ref[s], c_ref[0, :, s, :, :].reshape(FFT_N2 * 2 * FFT_J, ct))
    z = z_ref[0, :, s, :, :].astype(F32).reshape(rows, ct)
    gate = gate_ref[0, :, s, :, :].astype(F32).reshape(rows, ct)
    return gate * (r + z * skip)


def _is_fs_kernel(c_ref, gi_ref, z_ref, gate_ref, skip_ref, gf_ref, zz_ref, o_ref):
    nh = FFT_N2 // 2
    ct = o_ref.shape[-1]
    skip = skip_ref[...]
    for s in range(FFT_SUB):
        zz = _gated_inverse(c_ref, gi_ref, z_ref, gate_ref, skip, s).astype(BF16)
        zz_ref[0, :, s, :, :] = zz.reshape(nh, 2 * FFT_J, ct)
        o_ref[0, :, s, :, :] = _small_fwd_rows(gf_ref, s, zz)


def _is_last_kernel(c_ref, gi_ref, z_ref, gate_ref, skip_ref, y_ref):
    nh = FFT_N2 // 2
    ct = y_ref.shape[-1]
    skip = skip_ref[...]
    for s in range(FFT_SUB):
        y = _gated_inverse(c_ref, gi_ref, z_ref, gate_ref, skip, s).reshape(nh, 2, FFT_J, ct)
        for p in range(2):
            y_ref[p, 0, :, FFT_J * s:FFT_J * (s + 1), :] = y[:, p]


def _fft_small_inv(cs, g_inv, zil, z_off, gil, gate_off, skip, g_fwd=None):
    ncb, n2, ng, _, c = cs.shape
    nh = n2 // 2
    ct = MXU_DIM
    zb = z_off // ct
    gb = gate_off // ct
    til = lambda off: pl.BlockSpec((1, nh, FFT_SUB, 2 * FFT_J, ct), lambda q, b, ci: (b, 0, q, 0, off + ci))
    freq = pl.BlockSpec((1, n2, FFT_SUB, 2 * FFT_J, ct), lambda q, b, ci: (b, 0, q, 0, ci))
    mat = lambda m: pl.BlockSpec((FFT_SUB,) + m.shape[1:], lambda q, b, ci: (q, 0, 0))
    ins = [freq, mat(g_inv), til(zb), til(gb), pl.BlockSpec((1, ct), lambda q, b, ci: (0, ci))]
    args = [cs, g_inv, zil, gil, skip]
    if g_fwd is not None:
        body = _is_fs_kernel
        ins.append(mat(g_fwd))
        args.append(g_fwd)
        out_specs = [til(0), freq]
        out_shape = [jax.ShapeDtypeStruct((ncb, nh, ng, 2 * FFT_J, c), BF16), jax.ShapeDtypeStruct(cs.shape, BF16)]
    else:
        body = _is_last_kernel
        out_specs = pl.BlockSpec((2, 1, nh, FFT_CHUNK, ct), lambda q, b, ci: (0, b, 0, q, ci))
        out_shape = jax.ShapeDtypeStruct((2, ncb, nh, ng * FFT_J, c), F32)
    return pl.pallas_call(
        body,
        grid=(ng // FFT_SUB, ncb, c // ct),
        in_specs=ins,
        out_specs=out_specs,
        out_shape=out_shape,
        compiler_params=_params("parallel", "parallel", "arbitrary"),
        name="fft_small_inv",
    )(*args)


def _norm_mm_kernel(x_ref, g_ref, w_ref, o_ref):
    o_ref[...] = _dot(_rms(x_ref[...], g_ref[...]).astype(BF16), w_ref[...]).astype(o_ref.dtype)


def _norm_mm(x2d, g, w):
    n, d = x2d.shape
    tm = 512
    return pl.pallas_call(
        _norm_mm_kernel,
        grid=(n // tm,),
        in_specs=[
            pl.BlockSpec((tm, d), lambda i: (i, 0)),
            pl.BlockSpec((1, d), lambda i: (0, 0)),
            pl.BlockSpec(w.shape, lambda i: (0, 0)),
        ],
        out_specs=pl.BlockSpec((tm, w.shape[1]), lambda i: (i, 0)),
        out_shape=jax.ShapeDtypeStruct((n, w.shape[1]), BF16),
        compiler_params=_params("parallel"),
        name="memory_kv",
    )(x2d, g, w)


def _route(xf, g_ref, wr_ref, h_ref, r_ref, rt_ref):
    hf = _rms(xf, g_ref[...])
    hb = hf.astype(BF16)
    h_ref[...] = hb
    h_lo = (hf - hb.astype(F32)).astype(BF16)
    parts = _dot(hb, wr_ref[...]) + _dot(h_lo, wr_ref[...])
    logits = parts[:, :LANES] + parts[:, LANES:]
    lane = lax.broadcasted_iota(jnp.int32, logits.shape, 1)
    logits = jnp.where(lane < N_EXPERTS, logits, NEG)
    lanef = lane.astype(F32)
    big = float(LANES)
    m1 = jnp.max(logits, axis=-1, keepdims=True)
    i1 = jnp.min(jnp.where(logits == m1, lanef, big), axis=-1, keepdims=True)
    rest = jnp.where(lanef == i1, NEG, logits)
    m2 = jnp.max(rest, axis=-1, keepdims=True)
    i2 = jnp.min(jnp.where(rest == m2, lanef, big), axis=-1, keepdims=True)
    e2 = jnp.exp(m2 - m1)
    w1 = 1.0 / (1.0 + e2)
    w2 = e2 / (1.0 + e2)
    rec = jnp.where(lane == 0, i1, jnp.where(lane == 1, i2, jnp.where(lane == 2, w1, jnp.where(lane == 3, w2, 0.0))))
    r_ref[...] = rec
    rt_ref[...] = rec.T[:SUBLANES, :]


def _mix_xattn_kernel(x_ref, a_ref, y_ref, ga_ref, gy_ref, wo_ref, gx_ref, wq_ref, kv_ref, wxo_ref, o_ref):
    an = _rms(a_ref[0].astype(F32), ga_ref[...]).astype(BF16)
    yn = _rms(y_ref[0], gy_ref[...]).astype(BF16)
    x1 = x_ref[0] + _dot(an, wo_ref[:ATTN_WIDTH, :]) + _dot(yn, wo_ref[ATTN_WIDTH:, :])
    h = _rms(x1, gx_ref[...]).astype(BF16)
    q = (_dot(h, wq_ref[...]) * (X_HEAD_DIM ** -0.5)).astype(BF16)
    outs = []
    for hh in range(X_HEADS):
        cols = slice(hh * X_HEAD_DIM, (hh + 1) * X_HEAD_DIM)
        kh = kv_ref[0, :, cols]
        vh = kv_ref[0, :, X_WIDTH + hh * X_HEAD_DIM:X_WIDTH + (hh + 1) * X_HEAD_DIM]
        s = _dot_t(q[:, cols], kh)
        p = jnp.exp(s - jnp.max(s, axis=-1, keepdims=True))
        den = jnp.sum(p, axis=-1, keepdims=True)
        outs.append((_dot(p.astype(BF16), vh) / den).astype(BF16))
    o_ref[0] = x1 + _dot(jnp.concatenate(outs, axis=1), wxo_ref[...])


def _mix_xattn(x, a, y, ga, gy, wo, gx, wq, kv, wxo):
    b, s, d = x.shape
    tq = 512
    m = kv.shape[1]
    full = lambda arr: pl.BlockSpec(arr.shape, lambda bi, i: (0,) * arr.ndim)
    tok = lambda w: pl.BlockSpec((1, tq, w), lambda bi, i: (bi, i, 0))
    return pl.pallas_call(
        _mix_xattn_kernel,
        grid=(b, s // tq),
        in_specs=[tok(d), tok(ATTN_WIDTH), tok(HY_WIDTH), full(ga), full(gy), full(wo), full(gx), full(wq),
                  pl.BlockSpec((1, m, 2 * X_WIDTH), lambda bi, i: (bi, 0, 0)), full(wxo)],
        out_specs=tok(d),
        out_shape=jax.ShapeDtypeStruct((b, s, d), F32),
        compiler_params=_params("parallel", "arbitrary"),
        name="mix_xattn",
    )(x, a, y, ga, gy, wo, gx, wq, kv, wxo)


def _swiglu_chunks(h, wg, wu, wd, width, acc):
    for c in range(width // MXU_DIM):
        cols = slice(c * MXU_DIM, (c + 1) * MXU_DIM)
        g = _dot(h, wg(cols))
        u = _dot(h, wu(cols))
        a = (g * (1.0 / (1.0 + jnp.exp(-g))) * u).astype(BF16)
        acc = acc + _dot(a, wd(cols))
    return acc


def _ffn_kernel(x_ref, g_ref, wg_ref, wu_ref, wd_ref, o_ref):
    x = x_ref[...]
    h = _rms(x, g_ref[...]).astype(BF16)
    o_ref[...] = _swiglu_chunks(h, lambda c: wg_ref[:, c], lambda c: wu_ref[:, c], lambda c: wd_ref[c, :],
                                wg_ref.shape[1], x)


def _dense_ffn(x2d, g, wg, wu, wd):
    n, d = x2d.shape
    tm = 512
    resident = lambda w: pl.BlockSpec(w.shape, lambda i: (0, 0), pipeline_mode=pl.Buffered(1))
    return pl.pallas_call(
        _ffn_kernel,
        grid=(n // tm,),
        in_specs=[
            pl.BlockSpec((tm, d), lambda i: (i, 0)),
            pl.BlockSpec((1, d), lambda i: (0, 0)),
            resident(wg), resident(wu), resident(wd),
        ],
        out_specs=pl.BlockSpec((tm, d), lambda i: (i, 0)),
        out_shape=jax.ShapeDtypeStruct((n, d), F32),
        compiler_params=_params("parallel"),
        name="dense_ffn",
    )(x2d, g, wg, wu, wd)


def _router_kernel(x_ref, g_ref, wr_ref, h_ref, r_ref, rt_ref):
    _route(x_ref[...], g_ref, wr_ref, h_ref, r_ref, rt_ref)


def _router(x2d, g, wr):
    n, d = x2d.shape
    tm = 512
    return pl.pallas_call(
        _router_kernel,
        grid=(n // tm,),
        in_specs=[
            pl.BlockSpec((tm, d), lambda i: (i, 0)),
            pl.BlockSpec((1, d), lambda i: (0, 0)),
            pl.BlockSpec(wr.shape, lambda i: (0, 0)),
        ],
        out_specs=[pl.BlockSpec((tm, d), lambda i: (i, 0)), pl.BlockSpec((tm, LANES), lambda i: (i, 0)),
                   pl.BlockSpec((SUBLANES, tm), lambda i: (0, i))],
        out_shape=[jax.ShapeDtypeStruct((n, d), BF16), jax.ShapeDtypeStruct((n, LANES), F32),
                   jax.ShapeDtypeStruct((SUBLANES, n), F32)],
        compiler_params=_params("parallel"),
        name="router",
    )(x2d, g, wr)


def _moe_kernel(be_ref, x_ref, wg_ref, wu_ref, wd_ref, y_prev_ref, o_ref, acc_ref, *, n_blk):
    del y_prev_ref
    f = pl.program_id(1)

    @pl.when(f == 0)
    def _():
        acc_ref[...] = jnp.zeros_like(acc_ref)

    @pl.when(pl.program_id(0) < be_ref[n_blk])
    def _():
        acc_ref[...] = _swiglu_chunks(x_ref[...], lambda c: wg_ref[0, :, c], lambda c: wu_ref[0, :, c],
                                      lambda c: wd_ref[0, c, :], wg_ref.shape[2], acc_ref[...])

    @pl.when(f == pl.num_programs(1) - 1)
    def _():
        o_ref[...] = acc_ref[...].astype(BF16)


def _moe_experts(blk_meta, xb, wg, wu, wd, y_prev, blk_off):
    rows, d = xb.shape
    ff = wg.shape[2]
    tf = ff // 2
    nf = ff // tf
    n_blk = rows // MOE_ROWS
    ftile = lambda i, f, be: jnp.where(i < be[n_blk], f, nf - 1)
    grid_spec = pltpu.PrefetchScalarGridSpec(
        num_scalar_prefetch=1,
        grid=(n_blk, nf),
        in_specs=[
            pl.BlockSpec((MOE_ROWS, d), lambda i, f, be: (i, 0)),
            pl.BlockSpec((1, d, tf), lambda i, f, be: (be[i], 0, ftile(i, f, be))),
            pl.BlockSpec((1, d, tf), lambda i, f, be: (be[i], 0, ftile(i, f, be))),
            pl.BlockSpec((1, tf, d), lambda i, f, be: (be[i], ftile(i, f, be), 0)),
            pl.BlockSpec(memory_space=pl.ANY),
        ],
        out_specs=pl.BlockSpec((MOE_ROWS, d), lambda i, f, be: (blk_off + i, 0)),
        scratch_shapes=[pltpu.VMEM((MOE_ROWS, d), F32)],
    )
    return pl.pallas_call(
        functools.partial(_moe_kernel, n_blk=n_blk),
        grid_spec=grid_spec,
        out_shape=jax.ShapeDtypeStruct(y_prev.shape, BF16),
        input_output_aliases={5: 0},
        compiler_params=_params("parallel", "arbitrary"),
        name="moe_experts",
    )(blk_meta, xb, wg, wu, wd, y_prev)


def _combine_kernel(*refs, normed):
    if normed:
        x_ref, y1_ref, y2_ref, r_ref, g_ref, o_ref = refs
    else:
        x_ref, y1_ref, y2_ref, r_ref, o_ref = refs
    w1 = r_ref[:, 2:3]
    w2 = r_ref[:, 3:4]
    x = x_ref[...] + y1_ref[...].astype(F32) * w1 + y2_ref[...].astype(F32) * w2
    o_ref[...] = _rms(x, g_ref[...]) if normed else x


def _combine(x2d, y1, y2, route, gain):
    n, d = x2d.shape
    tm = 512
    normed = gain is not None
    tok = lambda w: pl.BlockSpec((tm, w), lambda i: (i, 0))
    ins = [tok(d), tok(d), tok(d), tok(LANES)]
    args = [x2d, y1, y2, route]
    if normed:
        ins.append(pl.BlockSpec((1, d), lambda i: (0, 0)))
        args.append(gain)
    return pl.pallas_call(
        functools.partial(_combine_kernel, normed=normed),
        grid=(n // tm,),
        in_specs=ins,
        out_specs=tok(d),
        out_shape=jax.ShapeDtypeStruct((n, d), F32),
        compiler_params=_params("parallel"),
        name="moe_combine",
    )(*args)


def _zero_rows_kernel(o_ref):
    o_ref[...] = jnp.zeros(o_ref.shape, o_ref.dtype)


def _zero_rows(n_blk, d):
    return pl.pallas_call(
        _zero_rows_kernel,
        grid=(n_blk,),
        out_specs=pl.BlockSpec((MOE_ROWS, d), lambda i: (i, 0)),
        out_shape=jax.ShapeDtypeStruct((n_blk * MOE_ROWS, d), BF16),
        compiler_params=_params("parallel"),
        name="moe_buffer_init",
    )()


def _norm_kernel(x_ref, g_ref, o_ref):
    o_ref[...] = _rms(x_ref[...], g_ref[...])


def _final_norm(x2d, g):
    n, d = x2d.shape
    tm = 512
    return pl.pallas_call(
        _norm_kernel,
        grid=(n // tm,),
        in_specs=[pl.BlockSpec((tm, d), lambda i: (i, 0)), pl.BlockSpec((1, d), lambda i: (0, 0))],
        out_specs=pl.BlockSpec((tm, d), lambda i: (i, 0)),
        out_shape=jax.ShapeDtypeStruct((n, d), F32),
        compiler_params=_params("parallel"),
        name="final_norm",
    )(x2d, g)


def _hyena(u, conv_w, conv_b, filt, skip, tables):
    g_fwd, g_inv, fb, fbi = tables
    b, L, c3 = u.shape
    c = c3 // (HY_ORDER + 1)
    uc = _shortconv(u, conv_w, conv_b.reshape(1, c3))
    spec = _filter_spectrum(_fft_small_fwd(filt, g_fwd, c_off=0, n_c=c), fb)
    t = _fft_small_fwd(uc, g_fwd, c_off=2 * c, n_c=c)
    t = _fft_big(t, spec, 0, fb, fbi)
    zz, t = _fft_small_inv(t, g_inv, uc, 2 * c, uc, 0, skip[0:1], g_fwd)
    t = _fft_big(t, spec, 1, fb, fbi)
    y5 = _fft_small_inv(t, g_inv, zz, 0, uc, c, skip[1:2])
    return y5.reshape(b, L, c)


def _moe_dispatch(route_t, n_tok):
    n_asg = n_tok * TOP_K
    flat_e = route_t[:TOP_K].astype(jnp.int32).reshape(n_asg)
    experts = jnp.arange(N_EXPERTS, dtype=jnp.int32)[:, None]
    onehot = (flat_e[None, :] == experts).astype(jnp.int32)
    csum = jnp.cumsum(onehot, axis=1)
    counts = csum[:, -1]
    rank = jnp.sum(onehot * (csum - 1), axis=0)
    padded = (counts + MOE_ROWS - 1) // MOE_ROWS * MOE_ROWS
    pad_end = jnp.cumsum(padded)
    pad_start = pad_end - padded
    dest = jnp.sum(onehot * pad_start[:, None], axis=0) + rank
    seg_start = jnp.cumsum(counts) - counts
    n_blk = -(-(n_asg + N_EXPERTS * (MOE_ROWS - 1)) // MOE_ROWS)
    blk_start = jnp.arange(n_blk, dtype=jnp.int32) * MOE_ROWS
    blk_e = jnp.minimum(jnp.sum((blk_start[:, None] >= pad_end[None, :]).astype(jnp.int32), axis=1), N_EXPERTS - 1)
    order = jnp.argsort(flat_e, stable=True).astype(jnp.int32)
    e_row = jnp.repeat(blk_e, MOE_ROWS)
    r = jnp.arange(n_blk * MOE_ROWS, dtype=jnp.int32) - pad_start[e_row]
    src = order[jnp.clip(seg_start[e_row] + r, 0, n_asg - 1)]
    buf_tok = jnp.where(r < counts[e_row], src % n_tok, 0)
    n_used = (pad_end[-1] // MOE_ROWS).astype(jnp.int32)
    return buf_tok, blk_e.astype(jnp.int32), n_used, dest.reshape(TOP_K, n_tok)


def kernel(x, mem, mem_norm, mix_norm, w_in, attn_sink, hy_conv_w, hy_conv_b, hy_f_w1, hy_f_b1, hy_f_freq1,
           hy_f_w2, hy_f_b2, hy_f_freq2, hy_f_w3, hy_skip, attn_out_norm, hy_out_norm, w_out, xattn_norm,
           xw_q, xw_k, xw_v, xw_o, ffn_norm, ffn_w_gate, ffn_w_up, ffn_w_down,
           moe_router, moe_w_gate, moe_w_up, moe_w_down, final_norm):
    b, s, d = x.shape
    depth = w_in.shape[0]
    n_tok = b * s
    assert s == FFT_N // 2 and b % 2 == 0
    row = lambda v: v.reshape(1, -1).astype(F32)

    pos = jnp.arange(s, dtype=F32)
    inv = ROPE_THETA ** (-jnp.arange(0, HEAD_DIM, 2, dtype=F32) / HEAD_DIM)
    ang = pos[:, None] * inv[None, :]
    cosf = jnp.tile(jnp.cos(ang), (1, LANES // (HEAD_DIM // 2)))
    sins = jnp.tile(jnp.concatenate([-jnp.sin(ang), jnp.sin(ang)], axis=1), (1, LANES // HEAD_DIM))
    t_pos = jnp.linspace(0.0, 1.0, s, dtype=F32)[:, None]
    wv = 2.0 * math.pi * jnp.arange(s, dtype=F32)[:, None] / s
    fr = jnp.linspace(1e-4, HY_BANDS - 1, HY_BANDS, dtype=F32)[None, :]
    z_pos = jnp.concatenate([t_pos, jnp.cos(fr * wv), -jnp.sin(fr * wv)], axis=-1)
    emb_pad = 64
    z_pos = jnp.pad(z_pos, ((0, 0), (0, emb_pad - HY_EMB)))
    absd = jnp.abs(jnp.linspace(HY_MIN_DECAY, HY_MAX_DECAY, HY_ORDER * HY_WIDTH, dtype=F32)).reshape(1, -1)
    tables = _fft_tables()

    mem2d = mem.reshape(-1, d)
    x2d = x.reshape(n_tok, d)
    out = None
    for l in range(depth):
        q, kx, vx, u = _inproj(x2d, row(mix_norm[l]), w_in[l].astype(BF16), cosf, sins, s)
        a = _band_attention(q.reshape(b, s, ATTN_WIDTH), kx, vx.reshape(b, s, -1), attn_sink[l].astype(F32), s)
        filt = _hyena_filters(
            z_pos, t_pos, jnp.pad(hy_f_w1[l], ((0, emb_pad - HY_EMB), (0, 0))), row(hy_f_b1[l]),
            row(hy_f_freq1[l]), hy_f_w2[l], row(hy_f_b2[l]), row(hy_f_freq2[l]), hy_f_w3[l], absd)
        y = _hyena(u.reshape(b, s, -1), hy_conv_w[l], hy_conv_b[l], filt, hy_skip[l], tables)
        mkv = _norm_mm(mem2d, row(mem_norm), jnp.concatenate([xw_k[l], xw_v[l]], axis=1).astype(BF16))
        j = l // 2
        x3 = _mix_xattn(x2d.reshape(b, s, d), a, y, row(attn_out_norm[l]), row(hy_out_norm[l]),
                        w_out[l].astype(BF16), row(xattn_norm[l]), xw_q[l].astype(BF16),
                        mkv.reshape(b, -1, 2 * X_WIDTH), xw_o[l].astype(BF16))
        x2d = x3.reshape(n_tok, d)
        last = l == depth - 1
        if l % 2 == 0:
            x2d = _dense_ffn(x2d, row(ffn_norm[l]), ffn_w_gate[j].astype(BF16), ffn_w_up[j].astype(BF16),
                             ffn_w_down[j].astype(BF16))
            if last:
                out = _final_norm(x2d, row(final_norm))
        else:
            wr = jnp.pad(moe_router[j].astype(F32), ((0, 0), (0, LANES - N_EXPERTS)))
            wr_hi = wr.astype(BF16)
            wr = jnp.concatenate([wr_hi, (wr - wr_hi.astype(F32)).astype(BF16)], axis=1)
            hb, route, route_t = _router(x2d, row(ffn_norm[l]), wr)
            buf_tok, blk_e, n_used, dest = _moe_dispatch(route_t, n_tok)
            wg, wu, wd = (w[j].astype(BF16) for w in (moe_w_gate, moe_w_up, moe_w_down))
            n_blk = blk_e.shape[0]
            per = n_blk // MOE_SLICES
            assert per * MOE_SLICES == n_blk
            yb = _zero_rows(n_blk, d)
            for k in range(MOE_SLICES):
                lo = k * per
                meta = jnp.concatenate([blk_e[lo:lo + per], (n_used - lo)[None]])
                xb = hb[buf_tok[lo * MOE_ROWS:(lo + per) * MOE_ROWS]]
                yb = _moe_experts(meta, xb, wg, wu, wd, yb, lo)
            res = _combine(x2d, yb[dest[0]], yb[dest[1]], route, row(final_norm) if last else None)
            if last:
                out = res
            else:
                x2d = res
    return out.reshape(b, s, d)
```

```python
import functools
import math

import jax
import jax.numpy as jnp
from jax import lax
from jax.experimental import pallas as pl
from jax.experimental.pallas import tpu as pltpu

F32 = jnp.float32
BF16 = jnp.bfloat16

EPS = 1e-6
N_HEADS = 8
N_KV_HEADS = 2
HEAD_DIM = 64
ATTN_WIDTH = N_HEADS * HEAD_DIM
KV_WIDTH = N_KV_HEADS * HEAD_DIM
WINDOW = 128
ROPE_THETA = 10000.0
HY_WIDTH = 512
HY_ORDER = 2
HY_EMB = 33
HY_BANDS = (HY_EMB - 1) // 2
HY_FILTER_HIDDEN = 64
HY_TARGET = 1e-2
HY_FAST_DECAY = 0.3
HY_SLOW_DECAY = 1.5
HY_MIN_DECAY = math.log(HY_TARGET) / HY_SLOW_DECAY
HY_MAX_DECAY = math.log(HY_TARGET) / HY_FAST_DECAY
Q_END = ATTN_WIDTH
K_END = Q_END + KV_WIDTH
V_END = K_END + KV_WIDTH
X_HEADS = 4
X_HEAD_DIM = 128
X_WIDTH = X_HEADS * X_HEAD_DIM
N_EXPERTS = 8
TOP_K = 2

LANES = 128
SUBLANES = 8
MXU_DIM = 256
VMEM_LIMIT = 56 * 1024 * 1024

FFT_N1 = 256
FFT_N2 = 32
FFT_N = FFT_N1 * FFT_N2
FFT_J = SUBLANES
FFT_GROUPS = FFT_N1 // FFT_J
FFT_CHUNK = 64
FFT_SUB = FFT_CHUNK // FFT_J

MOE_ROWS = 1024
MOE_SLICES = 8
NEG = float(jnp.finfo(jnp.float32).min)
MASKED = -1e30
LOG2E = math.log2(math.e)


def _params(*sem):
    return pltpu.CompilerParams(dimension_semantics=sem, vmem_limit_bytes=VMEM_LIMIT)


def _dot(a, b):
    return jnp.dot(a, b, preferred_element_type=F32)


def _dot_t(a, b):
    return lax.dot_general(a, b, (((1,), (1,)), ((), ())), preferred_element_type=F32)


def _rms(xf, g):
    ms = jnp.mean(xf * xf, axis=-1, keepdims=True)
    return xf * lax.rsqrt(ms + EPS) * g


def _inproj_kernel(x_ref, g_ref, w_ref, cos_ref, sin_ref, q_ref, kx_ref, vx_ref, u_ref):
    h = _rms(x_ref[...], g_ref[...]).astype(BF16)
    cosf = cos_ref[...]
    sins = sin_ref[...]
    lane = lax.broadcasted_iota(jnp.int32, cosf.shape, 1)
    low = (lane % HEAD_DIM) < (HEAD_DIM // 2)

    def rope(c):
        rot = jnp.where(low, pltpu.roll(c, LANES - HEAD_DIM // 2, 1), pltpu.roll(c, HEAD_DIM // 2, 1))
        return c * cosf + rot * sins

    scale = HEAD_DIM ** -0.5 * LOG2E
    for j in range(ATTN_WIDTH // MXU_DIM):
        qc = _dot(h, w_ref[:, j * MXU_DIM:(j + 1) * MXU_DIM])
        for t in range(MXU_DIM // LANES):
            c0 = j * MXU_DIM + t * LANES
            q_ref[:, c0:c0 + LANES] = (rope(qc[:, t * LANES:(t + 1) * LANES]) * scale).astype(BF16)
    kvc = _dot(h, w_ref[:, Q_END:V_END])
    kt = rope(kvc[:, :KV_WIDTH]).T
    trow = lax.broadcasted_iota(jnp.int32, kt.shape, 0)
    top = jnp.where(trow < HEAD_DIM, kt, 0.0)
    bot = jnp.where(trow >= HEAD_DIM, kt, 0.0)
    kx_ref[0, 0] = top.astype(BF16)
    kx_ref[0, 1] = pltpu.roll(top, HEAD_DIM, 0).astype(BF16)
    kx_ref[0, 2] = pltpu.roll(bot, HEAD_DIM, 0).astype(BF16)
    kx_ref[0, 3] = bot.astype(BF16)
    vc = kvc[:, KV_WIDTH:]
    vsw = pltpu.roll(vc, HEAD_DIM, 1)
    lo = lane < HEAD_DIM
    vx_ref[:, 0 * LANES:1 * LANES] = jnp.where(lo, vc, 1.0).astype(BF16)
    vx_ref[:, 1 * LANES:2 * LANES] = jnp.where(lo, 1.0, vsw).astype(BF16)
    vx_ref[:, 2 * LANES:3 * LANES] = jnp.where(lo, vsw, 1.0).astype(BF16)
    vx_ref[:, 3 * LANES:4 * LANES] = jnp.where(lo, 1.0, vc).astype(BF16)
    n_u = u_ref.shape[1]
    for j in range(n_u // 512):
        u_ref[:, j * 512:(j + 1) * 512] = _dot(h, w_ref[:, V_END + j * 512:V_END + (j + 1) * 512]).astype(BF16)


def _inproj(x2d, g, w, cosf, sins, seq):
    n, d = x2d.shape
    tm = 512
    n_u = w.shape[1] - V_END
    spb = seq // tm
    return pl.pallas_call(
        _inproj_kernel,
        grid=(n // tm,),
        in_specs=[
            pl.BlockSpec((tm, d), lambda i: (i, 0)),
            pl.BlockSpec((1, d), lambda i: (0, 0)),
            pl.BlockSpec(w.shape, lambda i: (0, 0)),
            pl.BlockSpec((tm, LANES), lambda i: (i % spb, 0)),
            pl.BlockSpec((tm, LANES), lambda i: (i % spb, 0)),
        ],
        out_specs=[
            pl.BlockSpec((tm, ATTN_WIDTH), lambda i: (i, 0)),
            pl.BlockSpec((1, 2 * N_KV_HEADS, LANES, tm), lambda i: (i // spb, 0, 0, i % spb)),
            pl.BlockSpec((tm, 2 * N_KV_HEADS * LANES), lambda i: (i, 0)),
            pl.BlockSpec((tm, n_u), lambda i: (i, 0)),
        ],
        out_shape=[
            jax.ShapeDtypeStruct((n, ATTN_WIDTH), BF16),
            jax.ShapeDtypeStruct((n // seq, 2 * N_KV_HEADS, LANES, seq), BF16),
            jax.ShapeDtypeStruct((n, 2 * N_KV_HEADS * LANES), BF16),
            jax.ShapeDtypeStruct((n, n_u), BF16),
        ],
        compiler_params=_params("parallel"),
        name="inproj",
    )(x2d, g, w, cosf, sins)


def _battn_kernel(sink_ref, q_ref, kx_ref, vx_ref, o_ref, *, seq, tq):
    i = pl.program_id(1)
    blk = WINDOW
    n_blk = seq // blk
    row = lax.broadcasted_iota(jnp.int32, (blk, blk), 0)
    col = lax.broadcasted_iota(jnp.int32, (blk, blk), 1)
    tri_prev = jnp.where(col >= row, 0.0, MASKED)
    tri_next = jnp.where(col <= row, 0.0, MASKED)
    lo = col < HEAD_DIM
    heads_per_pair = LANES // HEAD_DIM
    group_pairs = N_HEADS // N_KV_HEADS // heads_per_pair
    units = [(jb, hp) for jb in range(tq // blk) for hp in range(N_HEADS // heads_per_pair)]

    def window(jb):
        bi = i * (tq // blk) + jb
        starts = [jnp.maximum(bi - 1, 0), bi, jnp.minimum(bi + 1, n_blk - 1)]
        return bi, [pl.multiple_of(s * blk, blk) for s in starts]

    def scores(jb, hp):
        _, starts = window(jb)
        qp = q_ref[0, jb * blk:(jb + 1) * blk, hp * LANES:(hp + 1) * LANES]
        out = []
        for t in range(heads_per_pair):
            var = heads_per_pair * (hp // group_pairs) + t
            kwin = jnp.concatenate([kx_ref[0, var, :, pl.ds(s, blk)] for s in starts], axis=1)
            out.append(_dot(qp, kwin))
        return out

    def shifted(jb, hp, s_pair):
        bi, _ = window(jb)
        b_prev = tri_prev + jnp.where(bi == 0, MASKED, 0.0)
        b_next = tri_next + jnp.where(bi == n_blk - 1, MASKED, 0.0)
        out = []
        for t in range(heads_per_pair):
            s = s_pair[t]
            s0 = s[:, :blk] + b_prev
            s1 = s[:, blk:2 * blk]
            s2 = s[:, 2 * blk:] + b_next
            sk = sink_ref[hp * heads_per_pair + t] * LOG2E
            m = jnp.maximum(jnp.max(jnp.maximum(jnp.maximum(s0, s1), s2), axis=-1, keepdims=True), sk)
            x = jnp.concatenate([s0 - m, s1 - m, s2 - m], axis=1).astype(BF16)
            out.append((x, sk - m))
        return out

    def probs(x_pair):
        return [(jnp.exp2(x), jnp.exp2(d)) for x, d in x_pair]

    def finish(jb, hp, p_pair):
        _, starts = window(jb)
        res = []
        for t in range(heads_per_pair):
            var = heads_per_pair * (hp // group_pairs) + t
            vwin = jnp.concatenate([vx_ref[0, pl.ds(s, blk), var * LANES:(var + 1) * LANES] for s in starts], axis=0)
            res.append(_dot(p_pair[t][0], vwin))
        num = jnp.where(lo, res[0], res[1])
        den = pltpu.roll(jnp.where(lo, res[1], res[0]), HEAD_DIM, 1) + jnp.where(lo, p_pair[0][1], p_pair[1][1])
        o_ref[0, jb * blk:(jb + 1) * blk, hp * LANES:(hp + 1) * LANES] = (num * (1.0 / den)).astype(BF16)

    n_u = len(units)
    st_s, st_x, st_p = {}, {}, {}
    for n in range(n_u + 3):
        if 0 <= n - 3 < n_u:
            finish(*units[n - 3], st_p.pop(n - 3))
        if 0 <= n - 2 < n_u:
            st_p[n - 2] = probs(st_x.pop(n - 2))
        if 0 <= n - 1 < n_u:
            st_x[n - 1] = shifted(*units[n - 1], st_s.pop(n - 1))
        if n < n_u:
            st_s[n] = scores(*units[n])


def _band_attention(q, kx, vx, sink, seq):
    b = q.shape[0]
    tq = 512
    return pl.pallas_call(
        functools.partial(_battn_kernel, seq=seq, tq=tq),
        grid=(b, seq // tq),
        in_specs=[
            pl.BlockSpec(memory_space=pltpu.SMEM),
            pl.BlockSpec((1, tq, ATTN_WIDTH), lambda bi, i: (bi, i, 0)),
            pl.BlockSpec((1,) + kx.shape[1:], lambda bi, i: (bi, 0, 0, 0)),
            pl.BlockSpec((1,) + vx.shape[1:], lambda bi, i: (bi, 0, 0)),
        ],
        out_specs=pl.BlockSpec((1, tq, ATTN_WIDTH), lambda bi, i: (bi, i, 0)),
        out_shape=jax.ShapeDtypeStruct((b, seq, ATTN_WIDTH), BF16),
        compiler_params=_params("parallel", "arbitrary"),
        name="band_attention",
    )(sink, q, kx, vx)


def _filter_kernel(z_ref, t_ref, w1_ref, b1_ref, f1_ref, w2_ref, b2_ref, f2_ref, w3_ref, ad_ref, o_ref, *, tl):
    hp = lax.Precision.HIGHEST
    h = jnp.sin(f1_ref[...] * (jnp.dot(z_ref[...], w1_ref[...], precision=hp, preferred_element_type=F32)
                               + b1_ref[...]))
    h = jnp.sin(f2_ref[...] * (jnp.dot(h, w2_ref[...], precision=hp, preferred_element_type=F32) + b2_ref[...]))
    t = t_ref[...]
    rowid = pl.program_id(0) * tl + lax.broadcasted_iota(jnp.int32, (tl, HY_WIDTH), 0)
    for d in range(2):
        for o in range(HY_ORDER):
            c0 = (d * HY_ORDER + o) * HY_WIDTH
            v = jnp.dot(h, w3_ref[:, c0:c0 + HY_WIDTH], precision=hp, preferred_element_type=F32)
            v = v * jnp.exp(-t * ad_ref[:, o * HY_WIDTH:(o + 1) * HY_WIDTH])
            if d == 1:
                v = jnp.where(rowid == 0, 0.0, v)
            v = v.reshape(tl // FFT_N1, FFT_GROUPS, FFT_J, HY_WIDTH)
            o_ref[d * HY_ORDER + o] = jnp.concatenate([v, jnp.zeros_like(v)], axis=2).astype(BF16)


def _hyena_filters(z_pos, t_pos, w1, b1, f1, w2, b2, f2, w3, absd):
    L, e = z_pos.shape
    tl = 512
    full = lambda a: pl.BlockSpec(a.shape, lambda i: (0,) * a.ndim)
    return pl.pallas_call(
        functools.partial(_filter_kernel, tl=tl),
        grid=(L // tl,),
        in_specs=[
            pl.BlockSpec((tl, e), lambda i: (i, 0)),
            pl.BlockSpec((tl, 1), lambda i: (i, 0)),
            full(w1), full(b1), full(f1), full(w2), full(b2), full(f2), full(w3), full(absd),
        ],
        out_specs=pl.BlockSpec((2 * HY_ORDER, tl // FFT_N1, FFT_GROUPS, 2 * FFT_J, HY_WIDTH),
                               lambda i: (0, i, 0, 0, 0)),
        out_shape=jax.ShapeDtypeStruct((2 * HY_ORDER, L // FFT_N1, FFT_GROUPS, 2 * FFT_J, HY_WIDTH), BF16),
        compiler_params=_params("parallel"),
        name="hyena_filters",
    )(z_pos, t_pos, w1, b1, f1, w2, b2, f2, w3, absd)


def _shortconv_kernel(ua_ref, ub_ref, w_ref, b_ref, o_ref):
    def conv(u_ref):
        u = u_ref[0].astype(F32)
        L, ct = u.shape
        r8 = lax.broadcasted_iota(jnp.int32, (SUBLANES, ct), 0)
        down = pltpu.roll(u, 1, 0)
        up = pltpu.roll(u, L - 1, 0)
        prev = jnp.concatenate([jnp.where(r8 == 0, 0.0, down[:SUBLANES]), down[SUBLANES:]], axis=0)
        nxt = jnp.concatenate([up[:L - SUBLANES], jnp.where(r8 == SUBLANES - 1, 0.0, up[L - SUBLANES:])], axis=0)
        r = prev * w_ref[0:1, :] + u * w_ref[1:2, :] + nxt * w_ref[2:3, :] + b_ref[...]
        return r.reshape(L // FFT_N1, FFT_GROUPS, FFT_J, ct)

    o_ref[0] = jnp.concatenate([conv(ua_ref), conv(ub_ref)], axis=2).astype(BF16)


def _shortconv(u, w, bias):
    b, L, c3 = u.shape
    ct = MXU_DIM
    ncb = b // 2
    return pl.pallas_call(
        _shortconv_kernel,
        grid=(ncb, c3 // ct),
        in_specs=[
            pl.BlockSpec((1, L, ct), lambda bi, ci: (bi, 0, ci)),
            pl.BlockSpec((1, L, ct), lambda bi, ci: (bi + ncb, 0, ci)),
            pl.BlockSpec((3, ct), lambda bi, ci: (0, ci)),
            pl.BlockSpec((1, ct), lambda bi, ci: (0, ci)),
        ],
        out_specs=pl.BlockSpec((1, L // FFT_N1, FFT_GROUPS, 2 * FFT_J, ct), lambda bi, ci: (bi, 0, 0, 0, ci)),
        out_shape=jax.ShapeDtypeStruct((ncb, L // FFT_N1, FFT_GROUPS, 2 * FFT_J, c3), BF16),
        compiler_params=_params("parallel", "parallel"),
        name="shortconv",
    )(u, u, w, bias)


def _real_block(m, n, po, pi, sign, scale=1.0):
    ang = (2.0 * math.pi / n) * (m % n).astype(F32)
    re = jnp.cos(ang) * scale
    im = jnp.sin(ang) * (sign * scale)
    return jnp.where(po == pi, re, jnp.where(po > pi, im, -im))


def _fft_tables():
    nh = FFT_N2 // 2
    j2 = 2 * FFT_J

    def split(idx):
        return idx // j2, (idx // FFT_J) % 2, idx % FFT_J

    def small(n_out, n_in, sign, scale, out_is_freq):
        rows = FFT_GROUPS * n_out * j2
        a = lax.broadcasted_iota(jnp.int32, (n_in * 2, rows), 0)
        b = lax.broadcasted_iota(jnp.int32, (n_in * 2, rows), 1)
        major_out, po, j = split(b % (n_out * j2))
        g = b // (n_out * j2)
        major_in, pi = a // 2, a % 2
        k2, n2 = (major_out, major_in) if out_is_freq else (major_in, major_out)
        m = FFT_N1 * n2 * k2 + (FFT_J * g + j) * k2
        compact = _real_block(m, FFT_N, po, pi, sign, scale).astype(BF16)
        cols = n_in * j2
        rep = (lax.broadcasted_iota(jnp.int32, (n_in * 2, cols), 1) // FFT_J
               == lax.broadcasted_iota(jnp.int32, (n_in * 2, cols), 0)).astype(BF16)
        full = lax.dot_general(compact, rep, (((0,), (0,)), ((), ())), preferred_element_type=F32)
        diag = (lax.broadcasted_iota(jnp.int32, (rows, cols), 0) % FFT_J
                == lax.broadcasted_iota(jnp.int32, (rows, cols), 1) % FFT_J)
        return jnp.where(diag, full, 0.0).astype(BF16).reshape(FFT_GROUPS, n_out * j2, cols)

    g_fwd = small(FFT_N2, nh, -1.0, 1.0, True)
    g_inv = small(nh, FFT_N2, 1.0, 1.0 / FFT_N, False)

    def big(sign):
        r = lax.broadcasted_iota(jnp.int32, (2 * FFT_N1, 2 * FFT_N1), 0)
        c = lax.broadcasted_iota(jnp.int32, (2 * FFT_N1, 2 * FFT_N1), 1)
        gk, po, jk = split(r)
        g, pi, j = split(c)
        m = (FFT_J * gk + jk) * (FFT_J * g + j)
        return _real_block(m, FFT_N1, po, pi, sign).astype(BF16)

    return g_fwd, g_inv, big(-1.0), big(1.0)


def _small_fwd_rows(g_ref, s, tile):
    r = _dot(g_ref[s], tile)
    return r.reshape(FFT_N2, 2 * FFT_J, tile.shape[-1]).astype(BF16)


def _fs_kernel(z_ref, g_ref, o_ref):
    ct = o_ref.shape[-1]
    for s in range(FFT_SUB):
        tile = z_ref[0, :, s, :, :].reshape(FFT_N2 // 2 * 2 * FFT_J, ct)
        o_ref[0, :, s, :, :] = _small_fwd_rows(g_ref, s, tile)


def _fft_small_fwd(zil, g_fwd, *, c_off, n_c):
    ncb, nh, ng, _, _ = zil.shape
    ct = MXU_DIM
    cblk = c_off // ct
    return pl.pallas_call(
        _fs_kernel,
        grid=(ng // FFT_SUB, ncb, n_c // ct),
        in_specs=[
            pl.BlockSpec((1, nh, FFT_SUB, 2 * FFT_J, ct), lambda q, b, ci: (b, 0, q, 0, cblk + ci)),
            pl.BlockSpec((FFT_SUB,) + g_fwd.shape[1:], lambda q, b, ci: (q, 0, 0)),
        ],
        out_specs=pl.BlockSpec((1, FFT_N2, FFT_SUB, 2 * FFT_J, ct), lambda q, b, ci: (b, 0, q, 0, ci)),
        out_shape=jax.ShapeDtypeStruct((ncb, FFT_N2, ng, 2 * FFT_J, n_c), BF16),
        compiler_params=_params("parallel", "parallel", "arbitrary"),
        name="fft_small_fwd",
    )(zil, g_fwd)


def _big_kernel(x_ref, h_ref, fb_ref, fbi_ref, o_ref):
    ct = o_ref.shape[-1]
    h4 = h_ref[0, 0].reshape(FFT_GROUPS, 2, FFT_J, ct)
    hre = h4[:, 0]
    him = h4[:, 1]
    n_b = x_ref.shape[0]

    def forward(b):
        return _dot(fb_ref[...], x_ref[b, 0].reshape(2 * FFT_N1, ct)).reshape(FFT_GROUPS, 2, FFT_J, ct)

    xf = forward(0)
    for b in range(n_b):
        xr = xf[:, 0]
        xi = xf[:, 1]
        if b + 1 < n_b:
            xf = forward(b + 1)
        y = jnp.stack([xr * hre - xi * him, xr * him + xi * hre], axis=1).reshape(2 * FFT_N1, ct).astype(BF16)
        o_ref[b, 0] = _dot(fbi_ref[...], y).reshape(FFT_GROUPS, 2 * FFT_J, ct).astype(BF16)


def _fft_big(xs, spec, order, fb, fbi):
    ncb, n2, ng, _, c = xs.shape
    ct = MXU_DIM
    nbb = ncb
    return pl.pallas_call(
        _big_kernel,
        grid=(c // ct, n2, ncb // nbb),
        in_specs=[
            pl.BlockSpec((nbb, 1, ng, 2 * FFT_J, ct), lambda ci, k, b: (b, k, 0, 0, ci)),
            pl.BlockSpec((1, 1, ng, 2 * FFT_J, ct), lambda ci, k, b: (order, k, 0, 0, ci)),
            pl.BlockSpec(fb.shape, lambda ci, k, b: (0, 0)),
            pl.BlockSpec(fbi.shape, lambda ci, k, b: (0, 0)),
        ],
        out_specs=pl.BlockSpec((nbb, 1, ng, 2 * FFT_J, ct), lambda ci, k, b: (b, k, 0, 0, ci)),
        out_shape=jax.ShapeDtypeStruct(xs.shape, BF16),
        compiler_params=_params("parallel", "parallel", "arbitrary"),
        name="fft_big",
    )(xs, spec, fb, fbi)


def _spectrum_kernel(x_ref, fb_ref, o_ref):
    ct = o_ref.shape[-1]
    xf = [_dot(fb_ref[...], x_ref[i, 0].reshape(2 * FFT_N1, ct)).reshape(FFT_GROUPS, 2, FFT_J, ct)
          for i in range(2 * HY_ORDER)]
    for o in range(HY_ORDER):
        a = xf[o]
        r = xf[HY_ORDER + o]
        spec = jnp.stack([a[:, 0] + r[:, 0], a[:, 1] - r[:, 1]], axis=1)
        o_ref[o, 0] = spec.reshape(FFT_GROUPS, 2 * FFT_J, ct)


def _filter_spectrum(xs, fb):
    nf, n2, ng, _, c = xs.shape
    ct = MXU_DIM
    return pl.pallas_call(
        _spectrum_kernel,
        grid=(c // ct, n2),
        in_specs=[
            pl.BlockSpec((nf, 1, ng, 2 * FFT_J, ct), lambda ci, k: (0, k, 0, 0, ci)),
            pl.BlockSpec(fb.shape, lambda ci, k: (0, 0)),
        ],
        out_specs=pl.BlockSpec((HY_ORDER, 1, ng, 2 * FFT_J, ct), lambda ci, k: (0, k, 0, 0, ci)),
        out_shape=jax.ShapeDtypeStruct((HY_ORDER, n2, ng, 2 * FFT_J, c), F32),
        compiler_params=_params("parallel", "parallel"),
        name="filter_spectrum",
    )(xs, fb)


def _gated_inverse(c_ref, gi_ref, z_ref, gate_ref, skip, s):
    nh = FFT_N2 // 2
    ct = c_ref.shape[-1]
    rows = nh * 2 * FFT_J
    r = _dot(gi_ref[s], c_ref[0, :, s, :, :].reshape(FFT_N2 * 2 * FFT_J, ct))
    z = z_ref[0, :, s, :, :].astype(F32).reshape(rows, ct)
    gate = gate_ref[0, :, s, :, :].astype(F32).reshape(rows, ct)
    return gate * (r + z * skip)


def _is_fs_kernel(c_ref, gi_ref, z_ref, gate_ref, skip_ref, gf_ref, zz_ref, o_ref):
    nh = FFT_N2 // 2
    ct = o_ref.shape[-1]
    skip = skip_ref[...]
    for s in range(FFT_SUB):
        zz = _gated_inverse(c_ref, gi_ref, z_ref, gate_ref, skip, s).astype(BF16)
        zz_ref[0, :, s, :, :] = zz.reshape(nh, 2 * FFT_J, ct)
        o_ref[0, :, s, :, :] = _small_fwd_rows(gf_ref, s, zz)


def _is_last_kernel(c_ref, gi_ref, z_ref, gate_ref, skip_ref, y_ref):
    nh = FFT_N2 // 2
    ct = y_ref.shape[-1]
    skip = skip_ref[...]
    for s in range(FFT_SUB):
        y = _gated_inverse(c_ref, gi_ref, z_ref, gate_ref, skip, s).reshape(nh, 2, FFT_J, ct)
        for p in range(2):
            y_ref[p, 0, :, FFT_J * s:FFT_J * (s + 1), :] = y[:, p]


def _fft_small_inv(cs, g_inv, zil, z_off, gil, gate_off, skip, g_fwd=None):
    ncb, n2, ng, _, c = cs.shape
    nh = n2 // 2
    ct = MXU_DIM
    zb = z_off // ct
    gb = gate_off // ct
    til = lambda off: pl.BlockSpec((1, nh, FFT_SUB, 2 * FFT_J, ct), lambda q, b, ci: (b, 0, q, 0, off + ci))
    freq = pl.BlockSpec((1, n2, FFT_SUB, 2 * FFT_J, ct), lambda q, b, ci: (b, 0, q, 0, ci))
    mat = lambda m: pl.BlockSpec((FFT_SUB,) + m.shape[1:], lambda q, b, ci: (q, 0, 0))
    ins = [freq, mat(g_inv), til(zb), til(gb), pl.BlockSpec((1, ct), lambda q, b, ci: (0, ci))]
    args = [cs, g_inv, zil, gil, skip]
    if g_fwd is not None:
        body = _is_fs_kernel
        ins.append(mat(g_fwd))
        args.append(g_fwd)
        out_specs = [til(0), freq]
        out_shape = [jax.ShapeDtypeStruct((ncb, nh, ng, 2 * FFT_J, c), BF16), jax.ShapeDtypeStruct(cs.shape, BF16)]
    else:
        body = _is_last_kernel
        out_specs = pl.BlockSpec((2, 1, nh, FFT_CHUNK, ct), lambda q, b, ci: (0, b, 0, q, ci))
        out_shape = jax.ShapeDtypeStruct((2, ncb, nh, ng * FFT_J, c), F32)
    return pl.pallas_call(
        body,
        grid=(ng // FFT_SUB, ncb, c // ct),
        in_specs=ins,
        out_specs=out_specs,
        out_shape=out_shape,
        compiler_params=_params("parallel", "parallel", "arbitrary"),
        name="fft_small_inv",
    )(*args)


def _norm_mm_kernel(x_ref, g_ref, w_ref, o_ref):
    o_ref[...] = _dot(_rms(x_ref[...], g_ref[...]).astype(BF16), w_ref[...]).astype(o_ref.dtype)


def _norm_mm(x2d, g, w):
    n, d = x2d.shape
    tm = 512
    return pl.pallas_call(
        _norm_mm_kernel,
        grid=(n // tm,),
        in_specs=[
            pl.BlockSpec((tm, d), lambda i: (i, 0)),
            pl.BlockSpec((1, d), lambda i: (0, 0)),
            pl.BlockSpec(w.shape, lambda i: (0, 0)),
        ],
        out_specs=pl.BlockSpec((tm, w.shape[1]), lambda i: (i, 0)),
        out_shape=jax.ShapeDtypeStruct((n, w.shape[1]), BF16),
        compiler_params=_params("parallel"),
        name="memory_kv",
    )(x2d, g, w)


def _route(xf, g_ref, wr_ref, h_ref, r_ref, rt_ref):
    hf = _rms(xf, g_ref[...])
    hb = hf.astype(BF16)
    h_ref[...] = hb
    h_lo = (hf - hb.astype(F32)).astype(BF16)
    parts = _dot(hb, wr_ref[...]) + _dot(h_lo, wr_ref[...])
    logits = parts[:, :LANES] + parts[:, LANES:]
    lane = lax.broadcasted_iota(jnp.int32, logits.shape, 1)
    logits = jnp.where(lane < N_EXPERTS, logits, NEG)
    lanef = lane.astype(F32)
    big = float(LANES)
    m1 = jnp.max(logits, axis=-1, keepdims=True)
    i1 = jnp.min(jnp.where(logits == m1, lanef, big), axis=-1, keepdims=True)
    rest = jnp.where(lanef == i1, NEG, logits)
    m2 = jnp.max(rest, axis=-1, keepdims=True)
    i2 = jnp.min(jnp.where(rest == m2, lanef, big), axis=-1, keepdims=True)
    e2 = jnp.exp(m2 - m1)
    w1 = 1.0 / (1.0 + e2)
    w2 = e2 / (1.0 + e2)
    rec = jnp.where(lane == 0, i1, jnp.where(lane == 1, i2, jnp.where(lane == 2, w1, jnp.where(lane == 3, w2, 0.0))))
    r_ref[...] = rec
    rt_ref[...] = rec.T[:SUBLANES, :]


def _mix_xattn_kernel(x_ref, a_ref, y_ref, ga_ref, gy_ref, wo_ref, gx_ref, wq_ref, kv_ref, wxo_ref, o_ref):
    an = _rms(a_ref[0].astype(F32), ga_ref[...]).astype(BF16)
    yn = _rms(y_ref[0], gy_ref[...]).astype(BF16)
    x1 = x_ref[0] + _dot(an, wo_ref[:ATTN_WIDTH, :]) + _dot(yn, wo_ref[ATTN_WIDTH:, :])
    h = _rms(x1, gx_ref[...]).astype(BF16)
    q = (_dot(h, wq_ref[...]) * (X_HEAD_DIM ** -0.5)).astype(BF16)
    outs = []
    for hh in range(X_HEADS):
        cols = slice(hh * X_HEAD_DIM, (hh + 1) * X_HEAD_DIM)
        kh = kv_ref[0, :, cols]
        vh = kv_ref[0, :, X_WIDTH + hh * X_HEAD_DIM:X_WIDTH + (hh + 1) * X_HEAD_DIM]
        s = _dot_t(q[:, cols], kh)
        p = jnp.exp(s - jnp.max(s, axis=-1, keepdims=True))
        den = jnp.sum(p, axis=-1, keepdims=True)
        outs.append((_dot(p.astype(BF16), vh) / den).astype(BF16))
    o_ref[0] = x1 + _dot(jnp.concatenate(outs, axis=1), wxo_ref[...])


def _mix_xattn(x, a, y, ga, gy, wo, gx, wq, kv, wxo):
    b, s, d = x.shape
    tq = 512
    m = kv.shape[1]
    full = lambda arr: pl.BlockSpec(arr.shape, lambda bi, i: (0,) * arr.ndim)
    tok = lambda w: pl.BlockSpec((1, tq, w), lambda bi, i: (bi, i, 0))
    return pl.pallas_call(
        _mix_xattn_kernel,
        grid=(b, s // tq),
        in_specs=[tok(d), tok(ATTN_WIDTH), tok(HY_WIDTH), full(ga), full(gy), full(wo), full(gx), full(wq),
                  pl.BlockSpec((1, m, 2 * X_WIDTH), lambda bi, i: (bi, 0, 0)), full(wxo)],
        out_specs=tok(d),
        out_shape=jax.ShapeDtypeStruct((b, s, d), F32),
        compiler_params=_params("parallel", "arbitrary"),
        name="mix_xattn",
    )(x, a, y, ga, gy, wo, gx, wq, kv, wxo)


def _swiglu_chunks(h, wg, wu, wd, width, acc):
    for c in range(width // MXU_DIM):
        cols = slice(c * MXU_DIM, (c + 1) * MXU_DIM)
        g = _dot(h, wg(cols))
        u = _dot(h, wu(cols))
        a = (g * (1.0 / (1.0 + jnp.exp(-g))) * u).astype(BF16)
        acc = acc + _dot(a, wd(cols))
    return acc


def _ffn_kernel(x_ref, g_ref, wg_ref, wu_ref, wd_ref, o_ref):
    x = x_ref[...]
    h = _rms(x, g_ref[...]).astype(BF16)
    o_ref[...] = _swiglu_chunks(h, lambda c: wg_ref[:, c], lambda c: wu_ref[:, c], lambda c: wd_ref[c, :],
                                wg_ref.shape[1], x)


def _dense_ffn(x2d, g, wg, wu, wd):
    n, d = x2d.shape
    tm = 512
    resident = lambda w: pl.BlockSpec(w.shape, lambda i: (0, 0), pipeline_mode=pl.Buffered(1))
    return pl.pallas_call(
        _ffn_kernel,
        grid=(n // tm,),
        in_specs=[
            pl.BlockSpec((tm, d), lambda i: (i, 0)),
            pl.BlockSpec((1, d), lambda i: (0, 0)),
            resident(wg), resident(wu), resident(wd),
        ],
        out_specs=pl.BlockSpec((tm, d), lambda i: (i, 0)),
        out_shape=jax.ShapeDtypeStruct((n, d), F32),
        compiler_params=_params("parallel"),
        name="dense_ffn",
    )(x2d, g, wg, wu, wd)


def _router_kernel(x_ref, g_ref, wr_ref, h_ref, r_ref, rt_ref, z_ref):
    _route(x_ref[...], g_ref, wr_ref, h_ref, r_ref, rt_ref)
    z_ref[...] = jnp.zeros(z_ref.shape, z_ref.dtype)


def _router(x2d, g, wr, buf_rows):
    n, d = x2d.shape
    tm = 512
    steps = n // tm
    zr = buf_rows // steps
    assert zr * steps == buf_rows and zr % (2 * SUBLANES) == 0
    return pl.pallas_call(
        _router_kernel,
        grid=(steps,),
        in_specs=[
            pl.BlockSpec((tm, d), lambda i: (i, 0)),
            pl.BlockSpec((1, d), lambda i: (0, 0)),
            pl.BlockSpec(wr.shape, lambda i: (0, 0)),
        ],
        out_specs=[pl.BlockSpec((tm, d), lambda i: (i, 0)), pl.BlockSpec((tm, LANES), lambda i: (i, 0)),
                   pl.BlockSpec((SUBLANES, tm), lambda i: (0, i)), pl.BlockSpec((zr, d), lambda i: (i, 0))],
        out_shape=[jax.ShapeDtypeStruct((n, d), BF16), jax.ShapeDtypeStruct((n, LANES), F32),
                   jax.ShapeDtypeStruct((SUBLANES, n), F32), jax.ShapeDtypeStruct((buf_rows, d), BF16)],
        compiler_params=_params("parallel"),
        name="router",
    )(x2d, g, wr)


def _moe_kernel(be_ref, x_ref, wg_ref, wu_ref, wd_ref, y_prev_ref, o_ref, acc_ref, *, n_blk):
    del y_prev_ref
    f = pl.program_id(1)

    @pl.when(f == 0)
    def _():
        acc_ref[...] = jnp.zeros_like(acc_ref)

    @pl.when(pl.program_id(0) < be_ref[n_blk])
    def _():
        acc_ref[...] = _swiglu_chunks(x_ref[...], lambda c: wg_ref[0, :, c], lambda c: wu_ref[0, :, c],
                                      lambda c: wd_ref[0, c, :], wg_ref.shape[2], acc_ref[...])

    @pl.when(f == pl.num_programs(1) - 1)
    def _():
        o_ref[...] = acc_ref[...].astype(BF16)


def _moe_experts(blk_meta, xb, wg, wu, wd, y_prev, blk_off):
    rows, d = xb.shape
    ff = wg.shape[2]
    tf = ff // 2
    nf = ff // tf
    n_blk = rows // MOE_ROWS
    ftile = lambda i, f, be: jnp.where(i < be[n_blk], f, nf - 1)
    grid_spec = pltpu.PrefetchScalarGridSpec(
        num_scalar_prefetch=1,
        grid=(n_blk, nf),
        in_specs=[
            pl.BlockSpec((MOE_ROWS, d), lambda i, f, be: (i, 0)),
            pl.BlockSpec((1, d, tf), lambda i, f, be: (be[i], 0, ftile(i, f, be))),
            pl.BlockSpec((1, d, tf), lambda i, f, be: (be[i], 0, ftile(i, f, be))),
            pl.BlockSpec((1, tf, d), lambda i, f, be: (be[i], ftile(i, f, be), 0)),
            pl.BlockSpec(memory_space=pl.ANY),
        ],
        out_specs=pl.BlockSpec((MOE_ROWS, d), lambda i, f, be: (blk_off + i, 0)),
        scratch_shapes=[pltpu.VMEM((MOE_ROWS, d), F32)],
    )
    return pl.pallas_call(
        functools.partial(_moe_kernel, n_blk=n_blk),
        grid_spec=grid_spec,
        out_shape=jax.ShapeDtypeStruct(y_prev.shape, BF16),
        input_output_aliases={5: 0},
        compiler_params=_params("parallel", "arbitrary"),
        name="moe_experts",
    )(blk_meta, xb, wg, wu, wd, y_prev)


def _combine_kernel(*refs, normed):
    if normed:
        x_ref, y1_ref, y2_ref, r_ref, g_ref, o_ref = refs
    else:
        x_ref, y1_ref, y2_ref, r_ref, o_ref = refs
    w1 = r_ref[:, 2:3]
    w2 = r_ref[:, 3:4]
    x = x_ref[...] + y1_ref[...].astype(F32) * w1 + y2_ref[...].astype(F32) * w2
    o_ref[...] = _rms(x, g_ref[...]) if normed else x


def _combine(x2d, y1, y2, route, gain):
    n, d = x2d.shape
    tm = 512
    normed = gain is not None
    tok = lambda w: pl.BlockSpec((tm, w), lambda i: (i, 0))
    ins = [tok(d), tok(d), tok(d), tok(LANES)]
    args = [x2d, y1, y2, route]
    if normed:
        ins.append(pl.BlockSpec((1, d), lambda i: (0, 0)))
        args.append(gain)
    return pl.pallas_call(
        functools.partial(_combine_kernel, normed=normed),
        grid=(n // tm,),
        in_specs=ins,
        out_specs=tok(d),
        out_shape=jax.ShapeDtypeStruct((n, d), F32),
        compiler_params=_params("parallel"),
        name="moe_combine",
    )(*args)


def _norm_kernel(x_ref, g_ref, o_ref):
    o_ref[...] = _rms(x_ref[...], g_ref[...])


def _final_norm(x2d, g):
    n, d = x2d.shape
    tm = 512
    return pl.pallas_call(
        _norm_kernel,
        grid=(n // tm,),
        in_specs=[pl.BlockSpec((tm, d), lambda i: (i, 0)), pl.BlockSpec((1, d), lambda i: (0, 0))],
        out_specs=pl.BlockSpec((tm, d), lambda i: (i, 0)),
        out_shape=jax.ShapeDtypeStruct((n, d), F32),
        compiler_params=_params("parallel"),
        name="final_norm",
    )(x2d, g)


def _hyena(u, conv_w, conv_b, filt, skip, tables):
    g_fwd, g_inv, fb, fbi = tables
    b, L, c3 = u.shape
    c = c3 // (HY_ORDER + 1)
    uc = _shortconv(u, conv_w, conv_b.reshape(1, c3))
    spec = _filter_spectrum(_fft_small_fwd(filt, g_fwd, c_off=0, n_c=c), fb)
    t = _fft_small_fwd(uc, g_fwd, c_off=2 * c, n_c=c)
    t = _fft_big(t, spec, 0, fb, fbi)
    zz, t = _fft_small_inv(t, g_inv, uc, 2 * c, uc, 0, skip[0:1], g_fwd)
    t = _fft_big(t, spec, 1, fb, fbi)
    y5 = _fft_small_inv(t, g_inv, zz, 0, uc, c, skip[1:2])
    return y5.reshape(b, L, c)


def _moe_dispatch(route_t, n_tok):
    n_asg = n_tok * TOP_K
    flat_e = route_t[:TOP_K].astype(jnp.int32).reshape(n_asg)
    experts = jnp.arange(N_EXPERTS, dtype=jnp.int32)[:, None]
    onehot = (flat_e[None, :] == experts).astype(jnp.int32)
    csum = jnp.cumsum(onehot, axis=1)
    counts = csum[:, -1]
    rank = jnp.sum(onehot * (csum - 1), axis=0)
    padded = (counts + MOE_ROWS - 1) // MOE_ROWS * MOE_ROWS
    pad_end = jnp.cumsum(padded)
    pad_start = pad_end - padded
    dest = jnp.sum(onehot * pad_start[:, None], axis=0) + rank
    seg_start = jnp.cumsum(counts) - counts
    n_blk = -(-(n_asg + N_EXPERTS * (MOE_ROWS - 1)) // MOE_ROWS)
    blk_start = jnp.arange(n_blk, dtype=jnp.int32) * MOE_ROWS
    blk_e = jnp.minimum(jnp.sum((blk_start[:, None] >= pad_end[None, :]).astype(jnp.int32), axis=1), N_EXPERTS - 1)
    order = jnp.argsort(flat_e, stable=True).astype(jnp.int32)
    e_row = jnp.repeat(blk_e, MOE_ROWS)
    r = jnp.arange(n_blk * MOE_ROWS, dtype=jnp.int32) - pad_start[e_row]
    src = order[jnp.clip(seg_start[e_row] + r, 0, n_asg - 1)]
    buf_tok = jnp.where(r < counts[e_row], src % n_tok, 0)
    n_used = (pad_end[-1] // MOE_ROWS).astype(jnp.int32)
    return buf_tok, blk_e.astype(jnp.int32), n_used, dest.reshape(TOP_K, n_tok)


def kernel(x, mem, mem_norm, mix_norm, w_in, attn_sink, hy_conv_w, hy_conv_b, hy_f_w1, hy_f_b1, hy_f_freq1,
           hy_f_w2, hy_f_b2, hy_f_freq2, hy_f_w3, hy_skip, attn_out_norm, hy_out_norm, w_out, xattn_norm,
           xw_q, xw_k, xw_v, xw_o, ffn_norm, ffn_w_gate, ffn_w_up, ffn_w_down,
           moe_router, moe_w_gate, moe_w_up, moe_w_down, final_norm):
    b, s, d = x.shape
    depth = w_in.shape[0]
    n_tok = b * s
    assert s == FFT_N // 2 and b % 2 == 0
    row = lambda v: v.reshape(1, -1).astype(F32)

    pos = jnp.arange(s, dtype=F32)
    inv = ROPE_THETA ** (-jnp.arange(0, HEAD_DIM, 2, dtype=F32) / HEAD_DIM)
    ang = pos[:, None] * inv[None, :]
    cosf = jnp.tile(jnp.cos(ang), (1, LANES // (HEAD_DIM // 2)))
    sins = jnp.tile(jnp.concatenate([-jnp.sin(ang), jnp.sin(ang)], axis=1), (1, LANES // HEAD_DIM))
    t_pos = jnp.linspace(0.0, 1.0, s, dtype=F32)[:, None]
    wv = 2.0 * math.pi * jnp.arange(s, dtype=F32)[:, None] / s
    fr = jnp.linspace(1e-4, HY_BANDS - 1, HY_BANDS, dtype=F32)[None, :]
    z_pos = jnp.concatenate([t_pos, jnp.cos(fr * wv), -jnp.sin(fr * wv)], axis=-1)
    emb_pad = 64
    z_pos = jnp.pad(z_pos, ((0, 0), (0, emb_pad - HY_EMB)))
    absd = jnp.abs(jnp.linspace(HY_MIN_DECAY, HY_MAX_DECAY, HY_ORDER * HY_WIDTH, dtype=F32)).reshape(1, -1)
    tables = _fft_tables()

    mem2d = mem.reshape(-1, d)
    x2d = x.reshape(n_tok, d)
    out = None
    for l in range(depth):
        q, kx, vx, u = _inproj(x2d, row(mix_norm[l]), w_in[l].astype(BF16), cosf, sins, s)
        a = _band_attention(q.reshape(b, s, ATTN_WIDTH), kx, vx.reshape(b, s, -1), attn_sink[l].astype(F32), s)
        filt = _hyena_filters(
            z_pos, t_pos, jnp.pad(hy_f_w1[l], ((0, emb_pad - HY_EMB), (0, 0))), row(hy_f_b1[l]),
            row(hy_f_freq1[l]), hy_f_w2[l], row(hy_f_b2[l]), row(hy_f_freq2[l]), hy_f_w3[l], absd)
        y = _hyena(u.reshape(b, s, -1), hy_conv_w[l], hy_conv_b[l], filt, hy_skip[l], tables)
        mkv = _norm_mm(mem2d, row(mem_norm), jnp.concatenate([xw_k[l], xw_v[l]], axis=1).astype(BF16))
        j = l // 2
        x3 = _mix_xattn(x2d.reshape(b, s, d), a, y, row(attn_out_norm[l]), row(hy_out_norm[l]),
                        w_out[l].astype(BF16), row(xattn_norm[l]), xw_q[l].astype(BF16),
                        mkv.reshape(b, -1, 2 * X_WIDTH), xw_o[l].astype(BF16))
        x2d = x3.reshape(n_tok, d)
        last = l == depth - 1
        if l % 2 == 0:
            x2d = _dense_ffn(x2d, row(ffn_norm[l]), ffn_w_gate[j].astype(BF16), ffn_w_up[j].astype(BF16),
                             ffn_w_down[j].astype(BF16))
            if last:
                out = _final_norm(x2d, row(final_norm))
        else:
            wr = jnp.pad(moe_router[j].astype(F32), ((0, 0), (0, LANES - N_EXPERTS)))
            wr_hi = wr.astype(BF16)
            wr = jnp.concatenate([wr_hi, (wr - wr_hi.astype(F32)).astype(BF16)], axis=1)
            n_blk = -(-(n_tok * TOP_K + N_EXPERTS * (MOE_ROWS - 1)) // MOE_ROWS)
            hb, route, route_t, yb = _router(x2d, row(ffn_norm[l]), wr, n_blk * MOE_ROWS)
            buf_tok, blk_e, n_used, dest = _moe_dispatch(route_t, n_tok)
            assert blk_e.shape[0] == n_blk
            wg, wu, wd = (w[j].astype(BF16) for w in (moe_w_gate, moe_w_up, moe_w_down))
            per = n_blk // MOE_SLICES
            assert per * MOE_SLICES == n_blk
            for k in range(MOE_SLICES):
                lo = k * per
                meta = jnp.concatenate([blk_e[lo:lo + per], (n_used - lo)[None]])
                xb = hb[buf_tok[lo * MOE_ROWS:(lo + per) * MOE_ROWS]]
                yb = _moe_experts(meta, xb, wg, wu, wd, yb, lo)
            res = _combine(x2d, yb[dest[0]], yb[dest[1]], route, row(final_norm) if last else None)
            if last:
                out = res
            else:
                x2d = res
    return out.reshape(b, s, d)
```

```python
import functools
import math

import jax
import jax.numpy as jnp
from jax import lax
from jax.experimental import pallas as pl
from jax.experimental.pallas import tpu as pltpu

F32 = jnp.float32
BF16 = jnp.bfloat16

EPS = 1e-6
N_HEADS = 8
N_KV_HEADS = 2
HEAD_DIM = 64
ATTN_WIDTH = N_HEADS * HEAD_DIM
KV_WIDTH = N_KV_HEADS * HEAD_DIM
WINDOW = 128
ROPE_THETA = 10000.0
HY_WIDTH = 512
HY_ORDER = 2
HY_EMB = 33
HY_BANDS = (HY_EMB - 1) // 2
HY_FILTER_HIDDEN = 64
HY_TARGET = 1e-2
HY_FAST_DECAY = 0.3
HY_SLOW_DECAY = 1.5
HY_MIN_DECAY = math.log(HY_TARGET) / HY_SLOW_DECAY
HY_MAX_DECAY = math.log(HY_TARGET) / HY_FAST_DECAY
Q_END = ATTN_WIDTH
K_END = Q_END + KV_WIDTH
V_END = K_END + KV_WIDTH
X_HEADS = 4
X_HEAD_DIM = 128
X_WIDTH = X_HEADS * X_HEAD_DIM
N_EXPERTS = 8
TOP_K = 2

LANES = 128
SUBLANES = 8
MXU_DIM = 256
VMEM_LIMIT = 56 * 1024 * 1024

FFT_N1 = 256
FFT_N2 = 32
FFT_N = FFT_N1 * FFT_N2
FFT_J = SUBLANES
FFT_GROUPS = FFT_N1 // FFT_J
FFT_CHUNK = 64
FFT_SUB = FFT_CHUNK // FFT_J

MOE_ROWS = 1024
MOE_FIRST_SLICE = 2
MOE_SLICE_GROWTH = 6
NEG = float(jnp.finfo(jnp.float32).min)
MASKED = -1e30
LOG2E = math.log2(math.e)


def _params(*sem):
    return pltpu.CompilerParams(dimension_semantics=sem, vmem_limit_bytes=VMEM_LIMIT)


def _dot(a, b):
    return jnp.dot(a, b, preferred_element_type=F32)


def _dot_t(a, b):
    return lax.dot_general(a, b, (((1,), (1,)), ((), ())), preferred_element_type=F32)


def _rms(xf, g):
    ms = jnp.mean(xf * xf, axis=-1, keepdims=True)
    return xf * lax.rsqrt(ms + EPS) * g


def _inproj_kernel(x_ref, g_ref, w_ref, cos_ref, sin_ref, q_ref, kx_ref, vx_ref, u_ref):
    h = _rms(x_ref[...], g_ref[...]).astype(BF16)
    cosf = cos_ref[...]
    sins = sin_ref[...]
    lane = lax.broadcasted_iota(jnp.int32, cosf.shape, 1)
    low = (lane % HEAD_DIM) < (HEAD_DIM // 2)

    def rope(c):
        rot = jnp.where(low, pltpu.roll(c, LANES - HEAD_DIM // 2, 1), pltpu.roll(c, HEAD_DIM // 2, 1))
        return c * cosf + rot * sins

    scale = HEAD_DIM ** -0.5 * LOG2E
    for j in range(ATTN_WIDTH // MXU_DIM):
        qc = _dot(h, w_ref[:, j * MXU_DIM:(j + 1) * MXU_DIM])
        for t in range(MXU_DIM // LANES):
            c0 = j * MXU_DIM + t * LANES
            q_ref[:, c0:c0 + LANES] = (rope(qc[:, t * LANES:(t + 1) * LANES]) * scale).astype(BF16)
    kvc = _dot(h, w_ref[:, Q_END:V_END])
    kt = rope(kvc[:, :KV_WIDTH]).T
    trow = lax.broadcasted_iota(jnp.int32, kt.shape, 0)
    top = jnp.where(trow < HEAD_DIM, kt, 0.0)
    bot = jnp.where(trow >= HEAD_DIM, kt, 0.0)
    kx_ref[0, 0] = top.astype(BF16)
    kx_ref[0, 1] = pltpu.roll(top, HEAD_DIM, 0).astype(BF16)
    kx_ref[0, 2] = pltpu.roll(bot, HEAD_DIM, 0).astype(BF16)
    kx_ref[0, 3] = bot.astype(BF16)
    vc = kvc[:, KV_WIDTH:]
    vsw = pltpu.roll(vc, HEAD_DIM, 1)
    lo = lane < HEAD_DIM
    vx_ref[:, 0 * LANES:1 * LANES] = jnp.where(lo, vc, 1.0).astype(BF16)
    vx_ref[:, 1 * LANES:2 * LANES] = jnp.where(lo, 1.0, vsw).astype(BF16)
    vx_ref[:, 2 * LANES:3 * LANES] = jnp.where(lo, vsw, 1.0).astype(BF16)
    vx_ref[:, 3 * LANES:4 * LANES] = jnp.where(lo, 1.0, vc).astype(BF16)
    n_u = u_ref.shape[1]
    for j in range(n_u // 512):
        u_ref[:, j * 512:(j + 1) * 512] = _dot(h, w_ref[:, V_END + j * 512:V_END + (j + 1) * 512]).astype(BF16)


def _inproj(x2d, g, w, cosf, sins, seq):
    n, d = x2d.shape
    tm = 512
    n_u = w.shape[1] - V_END
    spb = seq // tm
    return pl.pallas_call(
        _inproj_kernel,
        grid=(n // tm,),
        in_specs=[
            pl.BlockSpec((tm, d), lambda i: (i, 0)),
            pl.BlockSpec((1, d), lambda i: (0, 0)),
            pl.BlockSpec(w.shape, lambda i: (0, 0)),
            pl.BlockSpec((tm, LANES), lambda i: (i % spb, 0)),
            pl.BlockSpec((tm, LANES), lambda i: (i % spb, 0)),
        ],
        out_specs=[
            pl.BlockSpec((tm, ATTN_WIDTH), lambda i: (i, 0)),
            pl.BlockSpec((1, 2 * N_KV_HEADS, LANES, tm), lambda i: (i // spb, 0, 0, i % spb)),
            pl.BlockSpec((tm, 2 * N_KV_HEADS * LANES), lambda i: (i, 0)),
            pl.BlockSpec((tm, n_u), lambda i: (i, 0)),
        ],
        out_shape=[
            jax.ShapeDtypeStruct((n, ATTN_WIDTH), BF16),
            jax.ShapeDtypeStruct((n // seq, 2 * N_KV_HEADS, LANES, seq), BF16),
            jax.ShapeDtypeStruct((n, 2 * N_KV_HEADS * LANES), BF16),
            jax.ShapeDtypeStruct((n, n_u), BF16),
        ],
        compiler_params=_params("parallel"),
        name="inproj",
    )(x2d, g, w, cosf, sins)


def _battn_kernel(sink_ref, q_ref, kx_ref, vx_ref, o_ref, *, seq, tq):
    i = pl.program_id(1)
    blk = WINDOW
    n_blk = seq // blk
    row = lax.broadcasted_iota(jnp.int32, (blk, blk), 0)
    col = lax.broadcasted_iota(jnp.int32, (blk, blk), 1)
    tri_prev = jnp.where(col >= row, 0.0, MASKED)
    tri_next = jnp.where(col <= row, 0.0, MASKED)
    lo = col < HEAD_DIM
    heads_per_pair = LANES // HEAD_DIM
    group_pairs = N_HEADS // N_KV_HEADS // heads_per_pair
    units = [(jb, hp) for jb in range(tq // blk) for hp in range(N_HEADS // heads_per_pair)]

    def window(jb):
        bi = i * (tq // blk) + jb
        starts = [jnp.maximum(bi - 1, 0), bi, jnp.minimum(bi + 1, n_blk - 1)]
        return bi, [pl.multiple_of(s * blk, blk) for s in starts]

    def scores(jb, hp):
        _, starts = window(jb)
        qp = q_ref[0, jb * blk:(jb + 1) * blk, hp * LANES:(hp + 1) * LANES]
        out = []
        for t in range(heads_per_pair):
            var = heads_per_pair * (hp // group_pairs) + t
            kwin = jnp.concatenate([kx_ref[0, var, :, pl.ds(s, blk)] for s in starts], axis=1)
            out.append(_dot(qp, kwin))
        return out

    def shifted(jb, hp, s_pair):
        bi, _ = window(jb)
        b_prev = tri_prev + jnp.where(bi == 0, MASKED, 0.0)
        b_next = tri_next + jnp.where(bi == n_blk - 1, MASKED, 0.0)
        out = []
        for t in range(heads_per_pair):
            s = s_pair[t]
            s0 = s[:, :blk] + b_prev
            s1 = s[:, blk:2 * blk]
            s2 = s[:, 2 * blk:] + b_next
            sk = sink_ref[hp * heads_per_pair + t] * LOG2E
            m = jnp.maximum(jnp.max(jnp.maximum(jnp.maximum(s0, s1), s2), axis=-1, keepdims=True), sk)
            x = jnp.concatenate([s0 - m, s1 - m, s2 - m], axis=1).astype(BF16)
            out.append((x, sk - m))
        return out

    def probs(x_pair):
        return [(jnp.exp2(x), jnp.exp2(d)) for x, d in x_pair]

    def finish(jb, hp, p_pair):
        _, starts = window(jb)
        res = []
        for t in range(heads_per_pair):
            var = heads_per_pair * (hp // group_pairs) + t
            vwin = jnp.concatenate([vx_ref[0, pl.ds(s, blk), var * LANES:(var + 1) * LANES] for s in starts], axis=0)
            res.append(_dot(p_pair[t][0], vwin))
        num = jnp.where(lo, res[0], res[1])
        den = pltpu.roll(jnp.where(lo, res[1], res[0]), HEAD_DIM, 1) + jnp.where(lo, p_pair[0][1], p_pair[1][1])
        o_ref[0, jb * blk:(jb + 1) * blk, hp * LANES:(hp + 1) * LANES] = (num * (1.0 / den)).astype(BF16)

    n_u = len(units)
    st_s, st_x, st_p = {}, {}, {}
    for n in range(n_u + 3):
        if 0 <= n - 3 < n_u:
            finish(*units[n - 3], st_p.pop(n - 3))
        if 0 <= n - 2 < n_u:
            st_p[n - 2] = probs(st_x.pop(n - 2))
        if 0 <= n - 1 < n_u:
            st_x[n - 1] = shifted(*units[n - 1], st_s.pop(n - 1))
        if n < n_u:
            st_s[n] = scores(*units[n])


def _band_attention(q, kx, vx, sink, seq):
    b = q.shape[0]
    tq = 512
    return pl.pallas_call(
        functools.partial(_battn_kernel, seq=seq, tq=tq),
        grid=(b, seq // tq),
        in_specs=[
            pl.BlockSpec(memory_space=pltpu.SMEM),
            pl.BlockSpec((1, tq, ATTN_WIDTH), lambda bi, i: (bi, i, 0)),
            pl.BlockSpec((1,) + kx.shape[1:], lambda bi, i: (bi, 0, 0, 0)),
            pl.BlockSpec((1,) + vx.shape[1:], lambda bi, i: (bi, 0, 0)),
        ],
        out_specs=pl.BlockSpec((1, tq, ATTN_WIDTH), lambda bi, i: (bi, i, 0)),
        out_shape=jax.ShapeDtypeStruct((b, seq, ATTN_WIDTH), BF16),
        compiler_params=_params("parallel", "arbitrary"),
        name="band_attention",
    )(sink, q, kx, vx)


def _filter_kernel(z_ref, t_ref, w1_ref, b1_ref, f1_ref, w2_ref, b2_ref, f2_ref, w3_ref, ad_ref, o_ref, *, tl):
    hp = lax.Precision.HIGHEST
    h = jnp.sin(f1_ref[...] * (jnp.dot(z_ref[...], w1_ref[...], precision=hp, preferred_element_type=F32)
                               + b1_ref[...]))
    h = jnp.sin(f2_ref[...] * (jnp.dot(h, w2_ref[...], precision=hp, preferred_element_type=F32) + b2_ref[...]))
    t = t_ref[...]
    rowid = pl.program_id(0) * tl + lax.broadcasted_iota(jnp.int32, (tl, HY_WIDTH), 0)
    for d in range(2):
        for o in range(HY_ORDER):
            c0 = (d * HY_ORDER + o) * HY_WIDTH
            v = jnp.dot(h, w3_ref[:, c0:c0 + HY_WIDTH], precision=hp, preferred_element_type=F32)
            v = v * jnp.exp(-t * ad_ref[:, o * HY_WIDTH:(o + 1) * HY_WIDTH])
            if d == 1:
                v = jnp.where(rowid == 0, 0.0, v)
            v = v.reshape(tl // FFT_N1, FFT_GROUPS, FFT_J, HY_WIDTH)
            o_ref[d * HY_ORDER + o] = jnp.concatenate([v, jnp.zeros_like(v)], axis=2).astype(BF16)


def _hyena_filters(z_pos, t_pos, w1, b1, f1, w2, b2, f2, w3, absd):
    L, e = z_pos.shape
    tl = 512
    full = lambda a: pl.BlockSpec(a.shape, lambda i: (0,) * a.ndim)
    return pl.pallas_call(
        functools.partial(_filter_kernel, tl=tl),
        grid=(L // tl,),
        in_specs=[
            pl.BlockSpec((tl, e), lambda i: (i, 0)),
            pl.BlockSpec((tl, 1), lambda i: (i, 0)),
            full(w1), full(b1), full(f1), full(w2), full(b2), full(f2), full(w3), full(absd),
        ],
        out_specs=pl.BlockSpec((2 * HY_ORDER, tl // FFT_N1, FFT_GROUPS, 2 * FFT_J, HY_WIDTH),
                               lambda i: (0, i, 0, 0, 0)),
        out_shape=jax.ShapeDtypeStruct((2 * HY_ORDER, L // FFT_N1, FFT_GROUPS, 2 * FFT_J, HY_WIDTH), BF16),
        compiler_params=_params("parallel"),
        name="hyena_filters",
    )(z_pos, t_pos, w1, b1, f1, w2, b2, f2, w3, absd)


def _shortconv_kernel(ua_ref, ub_ref, w_ref, b_ref, o_ref):
    def conv(u_ref):
        u = u_ref[0].astype(F32)
        L, ct = u.shape
        r8 = lax.broadcasted_iota(jnp.int32, (SUBLANES, ct), 0)
        down = pltpu.roll(u, 1, 0)
        up = pltpu.roll(u, L - 1, 0)
        prev = jnp.concatenate([jnp.where(r8 == 0, 0.0, down[:SUBLANES]), down[SUBLANES:]], axis=0)
        nxt = jnp.concatenate([up[:L - SUBLANES], jnp.where(r8 == SUBLANES - 1, 0.0, up[L - SUBLANES:])], axis=0)
        r = prev * w_ref[0:1, :] + u * w_ref[1:2, :] + nxt * w_ref[2:3, :] + b_ref[...]
        return r.reshape(L // FFT_N1, FFT_GROUPS, FFT_J, ct)

    o_ref[0] = jnp.concatenate([conv(ua_ref), conv(ub_ref)], axis=2).astype(BF16)


def _shortconv(u, w, bias):
    b, L, c3 = u.shape
    ct = MXU_DIM
    ncb = b // 2
    return pl.pallas_call(
        _shortconv_kernel,
        grid=(ncb, c3 // ct),
        in_specs=[
            pl.BlockSpec((1, L, ct), lambda bi, ci: (bi, 0, ci)),
            pl.BlockSpec((1, L, ct), lambda bi, ci: (bi + ncb, 0, ci)),
            pl.BlockSpec((3, ct), lambda bi, ci: (0, ci)),
            pl.BlockSpec((1, ct), lambda bi, ci: (0, ci)),
        ],
        out_specs=pl.BlockSpec((1, L // FFT_N1, FFT_GROUPS, 2 * FFT_J, ct), lambda bi, ci: (bi, 0, 0, 0, ci)),
        out_shape=jax.ShapeDtypeStruct((ncb, L // FFT_N1, FFT_GROUPS, 2 * FFT_J, c3), BF16),
        compiler_params=_params("parallel", "parallel"),
        name="shortconv",
    )(u, u, w, bias)


def _real_block(m, n, po, pi, sign, scale=1.0):
    ang = (2.0 * math.pi / n) * (m % n).astype(F32)
    re = jnp.cos(ang) * scale
    im = jnp.sin(ang) * (sign * scale)
    return jnp.where(po == pi, re, jnp.where(po > pi, im, -im))


def _fft_tables():
    nh = FFT_N2 // 2
    j2 = 2 * FFT_J

    def split(idx):
        return idx // j2, (idx // FFT_J) % 2, idx % FFT_J

    def small(n_out, n_in, sign, scale, out_is_freq):
        rows = FFT_GROUPS * n_out * j2
        a = lax.broadcasted_iota(jnp.int32, (n_in * 2, rows), 0)
        b = lax.broadcasted_iota(jnp.int32, (n_in * 2, rows), 1)
        major_out, po, j = split(b % (n_out * j2))
        g = b // (n_out * j2)
        major_in, pi = a // 2, a % 2
        k2, n2 = (major_out, major_in) if out_is_freq else (major_in, major_out)
        m = FFT_N1 * n2 * k2 + (FFT_J * g + j) * k2
        compact = _real_block(m, FFT_N, po, pi, sign, scale).astype(BF16)
        cols = n_in * j2
        rep = (lax.broadcasted_iota(jnp.int32, (n_in * 2, cols), 1) // FFT_J
               == lax.broadcasted_iota(jnp.int32, (n_in * 2, cols), 0)).astype(BF16)
        full = lax.dot_general(compact, rep, (((0,), (0,)), ((), ())), preferred_element_type=F32)
        diag = (lax.broadcasted_iota(jnp.int32, (rows, cols), 0) % FFT_J
                == lax.broadcasted_iota(jnp.int32, (rows, cols), 1) % FFT_J)
        return jnp.where(diag, full, 0.0).astype(BF16).reshape(FFT_GROUPS, n_out * j2, cols)

    g_fwd = small(FFT_N2, nh, -1.0, 1.0, True)
    g_inv = small(nh, FFT_N2, 1.0, 1.0 / FFT_N, False)

    def big(sign):
        r = lax.broadcasted_iota(jnp.int32, (2 * FFT_N1, 2 * FFT_N1), 0)
        c = lax.broadcasted_iota(jnp.int32, (2 * FFT_N1, 2 * FFT_N1), 1)
        gk, po, jk = split(r)
        g, pi, j = split(c)
        m = (FFT_J * gk + jk) * (FFT_J * g + j)
        return _real_block(m, FFT_N1, po, pi, sign).astype(BF16)

    return g_fwd, g_inv, big(-1.0), big(1.0)


def _small_fwd_rows(g_ref, s, tile):
    r = _dot(g_ref[s], tile)
    return r.reshape(FFT_N2, 2 * FFT_J, tile.shape[-1]).astype(BF16)


def _fs_kernel(z_ref, g_ref, o_ref):
    ct = o_ref.shape[-1]
    for s in range(FFT_SUB):
        tile = z_ref[0, :, s, :, :].reshape(FFT_N2 // 2 * 2 * FFT_J, ct)
        o_ref[0, :, s, :, :] = _small_fwd_rows(g_ref, s, tile)


def _fft_small_fwd(zil, g_fwd, *, c_off, n_c):
    ncb, nh, ng, _, _ = zil.shape
    ct = MXU_DIM
    cblk = c_off // ct
    return pl.pallas_call(
        _fs_kernel,
        grid=(ng // FFT_SUB, ncb, n_c // ct),
        in_specs=[
            pl.BlockSpec((1, nh, FFT_SUB, 2 * FFT_J, ct), lambda q, b, ci: (b, 0, q, 0, cblk + ci)),
            pl.BlockSpec((FFT_SUB,) + g_fwd.shape[1:], lambda q, b, ci: (q, 0, 0)),
        ],
        out_specs=pl.BlockSpec((1, FFT_N2, FFT_SUB, 2 * FFT_J, ct), lambda q, b, ci: (b, 0, q, 0, ci)),
        out_shape=jax.ShapeDtypeStruct((ncb, FFT_N2, ng, 2 * FFT_J, n_c), BF16),
        compiler_params=_params("parallel", "parallel", "arbitrary"),
        name="fft_small_fwd",
    )(zil, g_fwd)


def _big_kernel(x_ref, h_ref, fb_ref, fbi_ref, o_ref):
    ct = o_ref.shape[-1]
    h4 = h_ref[0, 0].reshape(FFT_GROUPS, 2, FFT_J, ct)
    hre = h4[:, 0]
    him = h4[:, 1]
    n_b = x_ref.shape[0]

    def forward(b):
        return _dot(fb_ref[...], x_ref[b, 0].reshape(2 * FFT_N1, ct)).reshape(FFT_GROUPS, 2, FFT_J, ct)

    xf = forward(0)
    for b in range(n_b):
        xr = xf[:, 0]
        xi = xf[:, 1]
        if b + 1 < n_b:
            xf = forward(b + 1)
        y = jnp.stack([xr * hre - xi * him, xr * him + xi * hre], axis=1).reshape(2 * FFT_N1, ct).astype(BF16)
        o_ref[b, 0] = _dot(fbi_ref[...], y).reshape(FFT_GROUPS, 2 * FFT_J, ct).astype(BF16)


def _fft_big(xs, spec, order, fb, fbi):
    ncb, n2, ng, _, c = xs.shape
    ct = MXU_DIM
    nbb = ncb
    return pl.pallas_call(
        _big_kernel,
        grid=(c // ct, n2, ncb // nbb),
        in_specs=[
            pl.BlockSpec((nbb, 1, ng, 2 * FFT_J, ct), lambda ci, k, b: (b, k, 0, 0, ci)),
            pl.BlockSpec((1, 1, ng, 2 * FFT_J, ct), lambda ci, k, b: (order, k, 0, 0, ci)),
            pl.BlockSpec(fb.shape, lambda ci, k, b: (0, 0)),
            pl.BlockSpec(fbi.shape, lambda ci, k, b: (0, 0)),
        ],
        out_specs=pl.BlockSpec((nbb, 1, ng, 2 * FFT_J, ct), lambda ci, k, b: (b, k, 0, 0, ci)),
        out_shape=jax.ShapeDtypeStruct(xs.shape, BF16),
        compiler_params=_params("parallel", "parallel", "arbitrary"),
        name="fft_big",
    )(xs, spec, fb, fbi)


def _spectrum_kernel(x_ref, fb_ref, o_ref):
    ct = o_ref.shape[-1]
    xf = [_dot(fb_ref[...], x_ref[i, 0].reshape(2 * FFT_N1, ct)).reshape(FFT_GROUPS, 2, FFT_J, ct)
          for i in range(2 * HY_ORDER)]
    for o in range(HY_ORDER):
        a = xf[o]
        r = xf[HY_ORDER + o]
        spec = jnp.stack([a[:, 0] + r[:, 0], a[:, 1] - r[:, 1]], axis=1)
        o_ref[o, 0] = spec.reshape(FFT_GROUPS, 2 * FFT_J, ct)


def _filter_spectrum(xs, fb):
    nf, n2, ng, _, c = xs.shape
    ct = MXU_DIM
    return pl.pallas_call(
        _spectrum_kernel,
        grid=(c // ct, n2),
        in_specs=[
            pl.BlockSpec((nf, 1, ng, 2 * FFT_J, ct), lambda ci, k: (0, k, 0, 0, ci)),
            pl.BlockSpec(fb.shape, lambda ci, k: (0, 0)),
        ],
        out_specs=pl.BlockSpec((HY_ORDER, 1, ng, 2 * FFT_J, ct), lambda ci, k: (0, k, 0, 0, ci)),
        out_shape=jax.ShapeDtypeStruct((HY_ORDER, n2, ng, 2 * FFT_J, c), F32),
        compiler_params=_params("parallel", "parallel"),
        name="filter_spectrum",
    )(xs, fb)


def _gated_inverse(c_ref, gi_ref, z_ref, gate_ref, skip, s):
    nh = FFT_N2 // 2
    ct = c_ref.shape[-1]
    rows = nh * 2 * FFT_J
    r = _dot(gi_ref[s], c_ref[0, :, s, :, :].reshape(FFT_N2 * 2 * FFT_J, ct))
    z = z_ref[0, :, s, :, :].astype(F32).reshape(rows, ct)
    gate = gate_ref[0, :, s, :, :].astype(F32).reshape(rows, ct)
    return gate * (r + z * skip)


def _is_fs_kernel(c_ref, gi_ref, z_ref, gate_ref, skip_ref, gf_ref, zz_ref, o_ref):
    nh = FFT_N2 // 2
    ct = o_ref.shape[-1]
    skip = skip_ref[...]
    for s in range(FFT_SUB):
        zz = _gated_inverse(c_ref, gi_ref, z_ref, gate_ref, skip, s).astype(BF16)
        zz_ref[0, :, s, :, :] = zz.reshape(nh, 2 * FFT_J, ct)
        o_ref[0, :, s, :, :] = _small_fwd_rows(gf_ref, s, zz)


def _is_last_kernel(c_ref, gi_ref, z_ref, gate_ref, skip_ref, y_ref):
    nh = FFT_N2 // 2
    ct = y_ref.shape[-1]
    skip = skip_ref[...]
    for s in range(FFT_SUB):
        y = _gated_inverse(c_ref, gi_ref, z_ref, gate_ref, skip, s).reshape(nh, 2, FFT_J, ct)
        for p in range(2):
            y_ref[p, 0, :, FFT_J * s:FFT_J * (s + 1), :] = y[:, p]


def _fft_small_inv(cs, g_inv, zil, z_off, gil, gate_off, skip, g_fwd=None):
    ncb, n2, ng, _, c = cs.shape
    nh = n2 // 2
    ct = MXU_DIM
    zb = z_off // ct
    gb = gate_off // ct
    til = lambda off: pl.BlockSpec((1, nh, FFT_SUB, 2 * FFT_J, ct), lambda q, b, ci: (b, 0, q, 0, off + ci))
    freq = pl.BlockSpec((1, n2, FFT_SUB, 2 * FFT_J, ct), lambda q, b, ci: (b, 0, q, 0, ci))
    mat = lambda m: pl.BlockSpec((FFT_SUB,) + m.shape[1:], lambda q, b, ci: (q, 0, 0))
    ins = [freq, mat(g_inv), til(zb), til(gb), pl.BlockSpec((1, ct), lambda q, b, ci: (0, ci))]
    args = [cs, g_inv, zil, gil, skip]
    if g_fwd is not None:
        body = _is_fs_kernel
        ins.append(mat(g_fwd))
        args.append(g_fwd)
        out_specs = [til(0), freq]
        out_shape = [jax.ShapeDtypeStruct((ncb, nh, ng, 2 * FFT_J, c), BF16), jax.ShapeDtypeStruct(cs.shape, BF16)]
    else:
        body = _is_last_kernel
        out_specs = pl.BlockSpec((2, 1, nh, FFT_CHUNK, ct), lambda q, b, ci: (0, b, 0, q, ci))
        out_shape = jax.ShapeDtypeStruct((2, ncb, nh, ng * FFT_J, c), F32)
    return pl.pallas_call(
        body,
        grid=(ng // FFT_SUB, ncb, c // ct),
        in_specs=ins,
        out_specs=out_specs,
        out_shape=out_shape,
        compiler_params=_params("parallel", "parallel", "arbitrary"),
        name="fft_small_inv",
    )(*args)


def _norm_mm_kernel(x_ref, g_ref, w_ref, o_ref):
    o_ref[...] = _dot(_rms(x_ref[...], g_ref[...]).astype(BF16), w_ref[...]).astype(o_ref.dtype)


def _norm_mm(x2d, g, w):
    n, d = x2d.shape
    tm = 512
    return pl.pallas_call(
        _norm_mm_kernel,
        grid=(n // tm,),
        in_specs=[
            pl.BlockSpec((tm, d), lambda i: (i, 0)),
            pl.BlockSpec((1, d), lambda i: (0, 0)),
            pl.BlockSpec(w.shape, lambda i: (0, 0)),
        ],
        out_specs=pl.BlockSpec((tm, w.shape[1]), lambda i: (i, 0)),
        out_shape=jax.ShapeDtypeStruct((n, w.shape[1]), BF16),
        compiler_params=_params("parallel"),
        name="memory_kv",
    )(x2d, g, w)


def _route(xf, g_ref, wr_ref, h_ref, r_ref, rt_ref):
    hf = _rms(xf, g_ref[...])
    hb = hf.astype(BF16)
    h_ref[...] = hb
    h_lo = (hf - hb.astype(F32)).astype(BF16)
    parts = _dot(hb, wr_ref[...]) + _dot(h_lo, wr_ref[...])
    logits = parts[:, :LANES] + parts[:, LANES:]
    lane = lax.broadcasted_iota(jnp.int32, logits.shape, 1)
    logits = jnp.where(lane < N_EXPERTS, logits, NEG)
    lanef = lane.astype(F32)
    big = float(LANES)
    m1 = jnp.max(logits, axis=-1, keepdims=True)
    i1 = jnp.min(jnp.where(logits == m1, lanef, big), axis=-1, keepdims=True)
    rest = jnp.where(lanef == i1, NEG, logits)
    m2 = jnp.max(rest, axis=-1, keepdims=True)
    i2 = jnp.min(jnp.where(rest == m2, lanef, big), axis=-1, keepdims=True)
    e2 = jnp.exp(m2 - m1)
    w1 = 1.0 / (1.0 + e2)
    w2 = e2 / (1.0 + e2)
    rec = jnp.where(lane == 0, i1, jnp.where(lane == 1, i2, jnp.where(lane == 2, w1, jnp.where(lane == 3, w2, 0.0))))
    r_ref[...] = rec
    rt_ref[...] = rec.T[:SUBLANES, :]


def _mix_xattn_kernel(x_ref, a_ref, y_ref, ga_ref, gy_ref, wo_ref, gx_ref, wq_ref, kv_ref, wxo_ref, o_ref):
    an = _rms(a_ref[0].astype(F32), ga_ref[...]).astype(BF16)
    yn = _rms(y_ref[0], gy_ref[...]).astype(BF16)
    x1 = x_ref[0] + _dot(an, wo_ref[:ATTN_WIDTH, :]) + _dot(yn, wo_ref[ATTN_WIDTH:, :])
    h = _rms(x1, gx_ref[...]).astype(BF16)
    q = (_dot(h, wq_ref[...]) * (X_HEAD_DIM ** -0.5)).astype(BF16)
    outs = []
    for hh in range(X_HEADS):
        cols = slice(hh * X_HEAD_DIM, (hh + 1) * X_HEAD_DIM)
        kh = kv_ref[0, :, cols]
        vh = kv_ref[0, :, X_WIDTH + hh * X_HEAD_DIM:X_WIDTH + (hh + 1) * X_HEAD_DIM]
        s = _dot_t(q[:, cols], kh)
        p = jnp.exp(s - jnp.max(s, axis=-1, keepdims=True))
        den = jnp.sum(p, axis=-1, keepdims=True)
        outs.append((_dot(p.astype(BF16), vh) / den).astype(BF16))
    o_ref[0] = x1 + _dot(jnp.concatenate(outs, axis=1), wxo_ref[...])


def _mix_xattn(x, a, y, ga, gy, wo, gx, wq, kv, wxo):
    b, s, d = x.shape
    tq = 512
    m = kv.shape[1]
    full = lambda arr: pl.BlockSpec(arr.shape, lambda bi, i: (0,) * arr.ndim)
    tok = lambda w: pl.BlockSpec((1, tq, w), lambda bi, i: (bi, i, 0))
    return pl.pallas_call(
        _mix_xattn_kernel,
        grid=(b, s // tq),
        in_specs=[tok(d), tok(ATTN_WIDTH), tok(HY_WIDTH), full(ga), full(gy), full(wo), full(gx), full(wq),
                  pl.BlockSpec((1, m, 2 * X_WIDTH), lambda bi, i: (bi, 0, 0)), full(wxo)],
        out_specs=tok(d),
        out_shape=jax.ShapeDtypeStruct((b, s, d), F32),
        compiler_params=_params("parallel", "arbitrary"),
        name="mix_xattn",
    )(x, a, y, ga, gy, wo, gx, wq, kv, wxo)


def _swiglu_chunks(h, wg, wu, wd, width, acc):
    for c in range(width // MXU_DIM):
        cols = slice(c * MXU_DIM, (c + 1) * MXU_DIM)
        g = _dot(h, wg(cols))
        u = _dot(h, wu(cols))
        a = (g * (1.0 / (1.0 + jnp.exp(-g))) * u).astype(BF16)
        acc = acc + _dot(a, wd(cols))
    return acc


def _ffn_kernel(x_ref, g_ref, wg_ref, wu_ref, wd_ref, o_ref):
    x = x_ref[...]
    h = _rms(x, g_ref[...]).astype(BF16)
    o_ref[...] = _swiglu_chunks(h, lambda c: wg_ref[:, c], lambda c: wu_ref[:, c], lambda c: wd_ref[c, :],
                                wg_ref.shape[1], x)


def _dense_ffn(x2d, g, wg, wu, wd):
    n, d = x2d.shape
    tm = 512
    resident = lambda w: pl.BlockSpec(w.shape, lambda i: (0, 0), pipeline_mode=pl.Buffered(1))
    return pl.pallas_call(
        _ffn_kernel,
        grid=(n // tm,),
        in_specs=[
            pl.BlockSpec((tm, d), lambda i: (i, 0)),
            pl.BlockSpec((1, d), lambda i: (0, 0)),
            resident(wg), resident(wu), resident(wd),
        ],
        out_specs=pl.BlockSpec((tm, d), lambda i: (i, 0)),
        out_shape=jax.ShapeDtypeStruct((n, d), F32),
        compiler_params=_params("parallel"),
        name="dense_ffn",
    )(x2d, g, wg, wu, wd)


def _router_kernel(x_ref, g_ref, wr_ref, h_ref, r_ref, rt_ref, z_ref):
    _route(x_ref[...], g_ref, wr_ref, h_ref, r_ref, rt_ref)
    z_ref[...] = jnp.zeros(z_ref.shape, z_ref.dtype)


def _router(x2d, g, wr, buf_rows):
    n, d = x2d.shape
    tm = 512
    steps = n // tm
    zr = buf_rows // steps
    assert zr * steps == buf_rows and zr % (2 * SUBLANES) == 0
    return pl.pallas_call(
        _router_kernel,
        grid=(steps,),
        in_specs=[
            pl.BlockSpec((tm, d), lambda i: (i, 0)),
            pl.BlockSpec((1, d), lambda i: (0, 0)),
            pl.BlockSpec(wr.shape, lambda i: (0, 0)),
        ],
        out_specs=[pl.BlockSpec((tm, d), lambda i: (i, 0)), pl.BlockSpec((tm, LANES), lambda i: (i, 0)),
                   pl.BlockSpec((SUBLANES, tm), lambda i: (0, i)), pl.BlockSpec((zr, d), lambda i: (i, 0))],
        out_shape=[jax.ShapeDtypeStruct((n, d), BF16), jax.ShapeDtypeStruct((n, LANES), F32),
                   jax.ShapeDtypeStruct((SUBLANES, n), F32), jax.ShapeDtypeStruct((buf_rows, d), BF16)],
        compiler_params=_params("parallel"),
        name="router",
    )(x2d, g, wr)


def _moe_kernel(be_ref, x_ref, wg_ref, wu_ref, wd_ref, y_prev_ref, o_ref, acc_ref, *, n_blk):
    del y_prev_ref
    f = pl.program_id(1)

    @pl.when(f == 0)
    def _():
        acc_ref[...] = jnp.zeros_like(acc_ref)

    @pl.when(pl.program_id(0) < be_ref[n_blk])
    def _():
        acc_ref[...] = _swiglu_chunks(x_ref[...], lambda c: wg_ref[0, :, c], lambda c: wu_ref[0, :, c],
                                      lambda c: wd_ref[0, c, :], wg_ref.shape[2], acc_ref[...])

    @pl.when(f == pl.num_programs(1) - 1)
    def _():
        o_ref[...] = acc_ref[...].astype(BF16)


def _moe_experts(blk_meta, xb, wg, wu, wd, y_prev, blk_off):
    rows, d = xb.shape
    ff = wg.shape[2]
    tf = ff // 2
    nf = ff // tf
    n_blk = rows // MOE_ROWS
    ftile = lambda i, f, be: jnp.where(i < be[n_blk], f, nf - 1)
    grid_spec = pltpu.PrefetchScalarGridSpec(
        num_scalar_prefetch=1,
        grid=(n_blk, nf),
        in_specs=[
            pl.BlockSpec((MOE_ROWS, d), lambda i, f, be: (i, 0)),
            pl.BlockSpec((1, d, tf), lambda i, f, be: (be[i], 0, ftile(i, f, be))),
            pl.BlockSpec((1, d, tf), lambda i, f, be: (be[i], 0, ftile(i, f, be))),
            pl.BlockSpec((1, tf, d), lambda i, f, be: (be[i], ftile(i, f, be), 0)),
            pl.BlockSpec(memory_space=pl.ANY),
        ],
        out_specs=pl.BlockSpec((MOE_ROWS, d), lambda i, f, be: (blk_off + i, 0)),
        scratch_shapes=[pltpu.VMEM((MOE_ROWS, d), F32)],
    )
    return pl.pallas_call(
        functools.partial(_moe_kernel, n_blk=n_blk),
        grid_spec=grid_spec,
        out_shape=jax.ShapeDtypeStruct(y_prev.shape, BF16),
        input_output_aliases={5: 0},
        compiler_params=_params("parallel", "arbitrary"),
        name="moe_experts",
    )(blk_meta, xb, wg, wu, wd, y_prev)


def _combine_kernel(*refs, normed):
    if normed:
        x_ref, y1_ref, y2_ref, r_ref, g_ref, o_ref = refs
    else:
        x_ref, y1_ref, y2_ref, r_ref, o_ref = refs
    w1 = r_ref[:, 2:3]
    w2 = r_ref[:, 3:4]
    x = x_ref[...] + y1_ref[...].astype(F32) * w1 + y2_ref[...].astype(F32) * w2
    o_ref[...] = _rms(x, g_ref[...]) if normed else x


def _combine(x2d, y1, y2, route, gain):
    n, d = x2d.shape
    tm = 512
    normed = gain is not None
    tok = lambda w: pl.BlockSpec((tm, w), lambda i: (i, 0))
    ins = [tok(d), tok(d), tok(d), tok(LANES)]
    args = [x2d, y1, y2, route]
    if normed:
        ins.append(pl.BlockSpec((1, d), lambda i: (0, 0)))
        args.append(gain)
    return pl.pallas_call(
        functools.partial(_combine_kernel, normed=normed),
        grid=(n // tm,),
        in_specs=ins,
        out_specs=tok(d),
        out_shape=jax.ShapeDtypeStruct((n, d), F32),
        compiler_params=_params("parallel"),
        name="moe_combine",
    )(*args)


def _norm_kernel(x_ref, g_ref, o_ref):
    o_ref[...] = _rms(x_ref[...], g_ref[...])


def _final_norm(x2d, g):
    n, d = x2d.shape
    tm = 512
    return pl.pallas_call(
        _norm_kernel,
        grid=(n // tm,),
        in_specs=[pl.BlockSpec((tm, d), lambda i: (i, 0)), pl.BlockSpec((1, d), lambda i: (0, 0))],
        out_specs=pl.BlockSpec((tm, d), lambda i: (i, 0)),
        out_shape=jax.ShapeDtypeStruct((n, d), F32),
        compiler_params=_params("parallel"),
        name="final_norm",
    )(x2d, g)


def _hyena(u, conv_w, conv_b, filt, skip, tables):
    g_fwd, g_inv, fb, fbi = tables
    b, L, c3 = u.shape
    c = c3 // (HY_ORDER + 1)
    uc = _shortconv(u, conv_w, conv_b.reshape(1, c3))
    spec = _filter_spectrum(_fft_small_fwd(filt, g_fwd, c_off=0, n_c=c), fb)
    t = _fft_small_fwd(uc, g_fwd, c_off=2 * c, n_c=c)
    t = _fft_big(t, spec, 0, fb, fbi)
    zz, t = _fft_small_inv(t, g_inv, uc, 2 * c, uc, 0, skip[0:1], g_fwd)
    t = _fft_big(t, spec, 1, fb, fbi)
    y5 = _fft_small_inv(t, g_inv, zz, 0, uc, c, skip[1:2])
    return y5.reshape(b, L, c)


def _moe_dispatch(route_t, n_tok):
    n_asg = n_tok * TOP_K
    flat_e = route_t[:TOP_K].astype(jnp.int32).reshape(n_asg)
    experts = jnp.arange(N_EXPERTS, dtype=jnp.int32)[:, None]
    onehot = (flat_e[None, :] == experts).astype(jnp.int32)
    csum = jnp.cumsum(onehot, axis=1)
    counts = csum[:, -1]
    rank = jnp.sum(onehot * (csum - 1), axis=0)
    padded = (counts + MOE_ROWS - 1) // MOE_ROWS * MOE_ROWS
    pad_end = jnp.cumsum(padded)
    pad_start = pad_end - padded
    dest = jnp.sum(onehot * pad_start[:, None], axis=0) + rank
    seg_start = jnp.cumsum(counts) - counts
    n_blk = -(-(n_asg + N_EXPERTS * (MOE_ROWS - 1)) // MOE_ROWS)
    blk_start = jnp.arange(n_blk, dtype=jnp.int32) * MOE_ROWS
    blk_e = jnp.minimum(jnp.sum((blk_start[:, None] >= pad_end[None, :]).astype(jnp.int32), axis=1), N_EXPERTS - 1)
    order = jnp.argsort(flat_e, stable=True).astype(jnp.int32)
    e_row = jnp.repeat(blk_e, MOE_ROWS)
    r = jnp.arange(n_blk * MOE_ROWS, dtype=jnp.int32) - pad_start[e_row]
    src = order[jnp.clip(seg_start[e_row] + r, 0, n_asg - 1)]
    buf_tok = jnp.where(r < counts[e_row], src % n_tok, 0)
    n_used = (pad_end[-1] // MOE_ROWS).astype(jnp.int32)
    return buf_tok, blk_e.astype(jnp.int32), n_used, dest.reshape(TOP_K, n_tok)


def kernel(x, mem, mem_norm, mix_norm, w_in, attn_sink, hy_conv_w, hy_conv_b, hy_f_w1, hy_f_b1, hy_f_freq1,
           hy_f_w2, hy_f_b2, hy_f_freq2, hy_f_w3, hy_skip, attn_out_norm, hy_out_norm, w_out, xattn_norm,
           xw_q, xw_k, xw_v, xw_o, ffn_norm, ffn_w_gate, ffn_w_up, ffn_w_down,
           moe_router, moe_w_gate, moe_w_up, moe_w_down, final_norm):
    b, s, d = x.shape
    depth = w_in.shape[0]
    n_tok = b * s
    assert s == FFT_N // 2 and b % 2 == 0
    row = lambda v: v.reshape(1, -1).astype(F32)

    pos = jnp.arange(s, dtype=F32)
    inv = ROPE_THETA ** (-jnp.arange(0, HEAD_DIM, 2, dtype=F32) / HEAD_DIM)
    ang = pos[:, None] * inv[None, :]
    cosf = jnp.tile(jnp.cos(ang), (1, LANES // (HEAD_DIM // 2)))
    sins = jnp.tile(jnp.concatenate([-jnp.sin(ang), jnp.sin(ang)], axis=1), (1, LANES // HEAD_DIM))
    t_pos = jnp.linspace(0.0, 1.0, s, dtype=F32)[:, None]
    wv = 2.0 * math.pi * jnp.arange(s, dtype=F32)[:, None] / s
    fr = jnp.linspace(1e-4, HY_BANDS - 1, HY_BANDS, dtype=F32)[None, :]
    z_pos = jnp.concatenate([t_pos, jnp.cos(fr * wv), -jnp.sin(fr * wv)], axis=-1)
    emb_pad = 64
    z_pos = jnp.pad(z_pos, ((0, 0), (0, emb_pad - HY_EMB)))
    absd = jnp.abs(jnp.linspace(HY_MIN_DECAY, HY_MAX_DECAY, HY_ORDER * HY_WIDTH, dtype=F32)).reshape(1, -1)
    tables = _fft_tables()

    mem2d = mem.reshape(-1, d)
    x2d = x.reshape(n_tok, d)
    out = None
    for l in range(depth):
        q, kx, vx, u = _inproj(x2d, row(mix_norm[l]), w_in[l].astype(BF16), cosf, sins, s)
        a = _band_attention(q.reshape(b, s, ATTN_WIDTH), kx, vx.reshape(b, s, -1), attn_sink[l].astype(F32), s)
        filt = _hyena_filters(
            z_pos, t_pos, jnp.pad(hy_f_w1[l], ((0, emb_pad - HY_EMB), (0, 0))), row(hy_f_b1[l]),
            row(hy_f_freq1[l]), hy_f_w2[l], row(hy_f_b2[l]), row(hy_f_freq2[l]), hy_f_w3[l], absd)
        y = _hyena(u.reshape(b, s, -1), hy_conv_w[l], hy_conv_b[l], filt, hy_skip[l], tables)
        mkv = _norm_mm(mem2d, row(mem_norm), jnp.concatenate([xw_k[l], xw_v[l]], axis=1).astype(BF16))
        j = l // 2
        x3 = _mix_xattn(x2d.reshape(b, s, d), a, y, row(attn_out_norm[l]), row(hy_out_norm[l]),
                        w_out[l].astype(BF16), row(xattn_norm[l]), xw_q[l].astype(BF16),
                        mkv.reshape(b, -1, 2 * X_WIDTH), xw_o[l].astype(BF16))
        x2d = x3.reshape(n_tok, d)
        last = l == depth - 1
        if l % 2 == 0:
            x2d = _dense_ffn(x2d, row(ffn_norm[l]), ffn_w_gate[j].astype(BF16), ffn_w_up[j].astype(BF16),
                             ffn_w_down[j].astype(BF16))
            if last:
                out = _final_norm(x2d, row(final_norm))
        else:
            wr = jnp.pad(moe_router[j].astype(F32), ((0, 0), (0, LANES - N_EXPERTS)))
            wr_hi = wr.astype(BF16)
            wr = jnp.concatenate([wr_hi, (wr - wr_hi.astype(F32)).astype(BF16)], axis=1)
            n_blk = -(-(n_tok * TOP_K + N_EXPERTS * (MOE_ROWS - 1)) // MOE_ROWS)
            hb, route, route_t, yb = _router(x2d, row(ffn_norm[l]), wr, n_blk * MOE_ROWS)
            buf_tok, blk_e, n_used, dest = _moe_dispatch(route_t, n_tok)
            assert blk_e.shape[0] == n_blk
            wg, wu, wd = (w[j].astype(BF16) for w in (moe_w_gate, moe_w_up, moe_w_down))
            lo, per = 0, MOE_FIRST_SLICE
            while lo < n_blk:
                per = min(per, n_blk - lo)
                meta = jnp.concatenate([blk_e[lo:lo + per], (n_used - lo)[None]])
                xb = hb[buf_tok[lo * MOE_ROWS:(lo + per) * MOE_ROWS]]
                yb = _moe_experts(meta, xb, wg, wu, wd, yb, lo)
                lo, per = lo + per, per * MOE_SLICE_GROWTH
            res = _combine(x2d, yb[dest[0]], yb[dest[1]], route, row(final_norm) if last else None)
            if last:
                out = res
            else:
                x2d = res
    return out.reshape(b, s, d)
```

```python
import functools
import math

import jax
import jax.numpy as jnp
from jax import lax
from jax.experimental import pallas as pl
from jax.experimental.pallas import tpu as pltpu

F32 = jnp.float32
BF16 = jnp.bfloat16

EPS = 1e-6
N_HEADS = 8
N_KV_HEADS = 2
HEAD_DIM = 64
ATTN_WIDTH = N_HEADS * HEAD_DIM
KV_WIDTH = N_KV_HEADS * HEAD_DIM
WINDOW = 128
ROPE_THETA = 10000.0
HY_WIDTH = 512
HY_ORDER = 2
HY_EMB = 33
HY_BANDS = (HY_EMB - 1) // 2
HY_FILTER_HIDDEN = 64
HY_TARGET = 1e-2
HY_FAST_DECAY = 0.3
HY_SLOW_DECAY = 1.5
HY_MIN_DECAY = math.log(HY_TARGET) / HY_SLOW_DECAY
HY_MAX_DECAY = math.log(HY_TARGET) / HY_FAST_DECAY
Q_END = ATTN_WIDTH
K_END = Q_END + KV_WIDTH
V_END = K_END + KV_WIDTH
X_HEADS = 4
X_HEAD_DIM = 128
X_WIDTH = X_HEADS * X_HEAD_DIM
N_EXPERTS = 8
TOP_K = 2

LANES = 128
SUBLANES = 8
MXU_DIM = 256
VMEM_LIMIT = 56 * 1024 * 1024

FFT_N1 = 256
FFT_N2 = 32
FFT_N = FFT_N1 * FFT_N2
FFT_J = SUBLANES
FFT_GROUPS = FFT_N1 // FFT_J
FFT_CHUNK = 64
FFT_SUB = FFT_CHUNK // FFT_J

MOE_ROWS = 1024
MOE_SLICES = 8
NEG = float(jnp.finfo(jnp.float32).min)
MASKED = -1e30
LOG2E = math.log2(math.e)


def _params(*sem):
    return pltpu.CompilerParams(dimension_semantics=sem, vmem_limit_bytes=VMEM_LIMIT)


def _dot(a, b):
    return jnp.dot(a, b, preferred_element_type=F32)


def _dot_t(a, b):
    return lax.dot_general(a, b, (((1,), (1,)), ((), ())), preferred_element_type=F32)


def _rms(xf, g):
    ms = jnp.mean(xf * xf, axis=-1, keepdims=True)
    return xf * lax.rsqrt(ms + EPS) * g


def _inproj_kernel(x_ref, g_ref, w_ref, cos_ref, sin_ref, q_ref, kx_ref, vx_ref, u_ref):
    h = _rms(x_ref[...], g_ref[...]).astype(BF16)
    cosf = cos_ref[...]
    sins = sin_ref[...]
    lane = lax.broadcasted_iota(jnp.int32, cosf.shape, 1)
    low = (lane % HEAD_DIM) < (HEAD_DIM // 2)

    def rope(c):
        rot = jnp.where(low, pltpu.roll(c, LANES - HEAD_DIM // 2, 1), pltpu.roll(c, HEAD_DIM // 2, 1))
        return c * cosf + rot * sins

    scale = HEAD_DIM ** -0.5 * LOG2E
    for j in range(ATTN_WIDTH // MXU_DIM):
        qc = _dot(h, w_ref[:, j * MXU_DIM:(j + 1) * MXU_DIM])
        for t in range(MXU_DIM // LANES):
            c0 = j * MXU_DIM + t * LANES
            q_ref[:, c0:c0 + LANES] = (rope(qc[:, t * LANES:(t + 1) * LANES]) * scale).astype(BF16)
    kvc = _dot(h, w_ref[:, Q_END:V_END])
    kt = rope(kvc[:, :KV_WIDTH]).T
    trow = lax.broadcasted_iota(jnp.int32, kt.shape, 0)
    top = jnp.where(trow < HEAD_DIM, kt, 0.0)
    bot = jnp.where(trow >= HEAD_DIM, kt, 0.0)
    kx_ref[0, 0] = top.astype(BF16)
    kx_ref[0, 1] = pltpu.roll(top, HEAD_DIM, 0).astype(BF16)
    kx_ref[0, 2] = pltpu.roll(bot, HEAD_DIM, 0).astype(BF16)
    kx_ref[0, 3] = bot.astype(BF16)
    vc = kvc[:, KV_WIDTH:]
    vsw = pltpu.roll(vc, HEAD_DIM, 1)
    lo = lane < HEAD_DIM
    vx_ref[:, 0 * LANES:1 * LANES] = jnp.where(lo, vc, 1.0).astype(BF16)
    vx_ref[:, 1 * LANES:2 * LANES] = jnp.where(lo, 1.0, vsw).astype(BF16)
    vx_ref[:, 2 * LANES:3 * LANES] = jnp.where(lo, vsw, 1.0).astype(BF16)
    vx_ref[:, 3 * LANES:4 * LANES] = jnp.where(lo, 1.0, vc).astype(BF16)
    n_u = u_ref.shape[1]
    for j in range(n_u // 512):
        u_ref[:, j * 512:(j + 1) * 512] = _dot(h, w_ref[:, V_END + j * 512:V_END + (j + 1) * 512]).astype(BF16)


def _inproj(x2d, g, w, cosf, sins, seq):
    n, d = x2d.shape
    tm = 512
    n_u = w.shape[1] - V_END
    spb = seq // tm
    return pl.pallas_call(
        _inproj_kernel,
        grid=(n // tm,),
        in_specs=[
            pl.BlockSpec((tm, d), lambda i: (i, 0)),
            pl.BlockSpec((1, d), lambda i: (0, 0)),
            pl.BlockSpec(w.shape, lambda i: (0, 0)),
            pl.BlockSpec((tm, LANES), lambda i: (i % spb, 0)),
            pl.BlockSpec((tm, LANES), lambda i: (i % spb, 0)),
        ],
        out_specs=[
            pl.BlockSpec((tm, ATTN_WIDTH), lambda i: (i, 0)),
            pl.BlockSpec((1, 2 * N_KV_HEADS, LANES, tm), lambda i: (i // spb, 0, 0, i % spb)),
            pl.BlockSpec((tm, 2 * N_KV_HEADS * LANES), lambda i: (i, 0)),
            pl.BlockSpec((tm, n_u), lambda i: (i, 0)),
        ],
        out_shape=[
            jax.ShapeDtypeStruct((n, ATTN_WIDTH), BF16),
            jax.ShapeDtypeStruct((n // seq, 2 * N_KV_HEADS, LANES, seq), BF16),
            jax.ShapeDtypeStruct((n, 2 * N_KV_HEADS * LANES), BF16),
            jax.ShapeDtypeStruct((n, n_u), BF16),
        ],
        compiler_params=_params("parallel"),
        name="inproj",
    )(x2d, g, w, cosf, sins)


def _battn_kernel(sink_ref, q_ref, kx_ref, vx_ref, o_ref, *, seq, tq):
    i = pl.program_id(1)
    blk = WINDOW
    n_blk = seq // blk
    row = lax.broadcasted_iota(jnp.int32, (blk, blk), 0)
    col = lax.broadcasted_iota(jnp.int32, (blk, blk), 1)
    tri_prev = jnp.where(col >= row, 0.0, MASKED)
    tri_next = jnp.where(col <= row, 0.0, MASKED)
    lo = col < HEAD_DIM
    heads_per_pair = LANES // HEAD_DIM
    group_pairs = N_HEADS // N_KV_HEADS // heads_per_pair
    units = [(jb, hp) for jb in range(tq // blk) for hp in range(N_HEADS // heads_per_pair)]

    def window(jb):
        bi = i * (tq // blk) + jb
        starts = [jnp.maximum(bi - 1, 0), bi, jnp.minimum(bi + 1, n_blk - 1)]
        return bi, [pl.multiple_of(s * blk, blk) for s in starts]

    def scores(jb, hp):
        _, starts = window(jb)
        qp = q_ref[0, jb * blk:(jb + 1) * blk, hp * LANES:(hp + 1) * LANES]
        out = []
        for t in range(heads_per_pair):
            var = heads_per_pair * (hp // group_pairs) + t
            kwin = jnp.concatenate([kx_ref[0, var, :, pl.ds(s, blk)] for s in starts], axis=1)
            out.append(_dot(qp, kwin))
        return out

    def shifted(jb, hp, s_pair):
        bi, _ = window(jb)
        b_prev = tri_prev + jnp.where(bi == 0, MASKED, 0.0)
        b_next = tri_next + jnp.where(bi == n_blk - 1, MASKED, 0.0)
        out = []
        for t in range(heads_per_pair):
            s = s_pair[t]
            s0 = s[:, :blk] + b_prev
            s1 = s[:, blk:2 * blk]
            s2 = s[:, 2 * blk:] + b_next
            sk = sink_ref[hp * heads_per_pair + t] * LOG2E
            m = jnp.maximum(jnp.max(jnp.maximum(jnp.maximum(s0, s1), s2), axis=-1, keepdims=True), sk)
            x = jnp.concatenate([s0 - m, s1 - m, s2 - m], axis=1).astype(BF16)
            out.append((x, sk - m))
        return out

    def probs(x_pair):
        return [(jnp.exp2(x), jnp.exp2(d)) for x, d in x_pair]

    def finish(jb, hp, p_pair):
        _, starts = window(jb)
        res = []
        for t in range(heads_per_pair):
            var = heads_per_pair * (hp // group_pairs) + t
            vwin = jnp.concatenate([vx_ref[0, pl.ds(s, blk), var * LANES:(var + 1) * LANES] for s in starts], axis=0)
            res.append(_dot(p_pair[t][0], vwin))
        num = jnp.where(lo, res[0], res[1])
        den = pltpu.roll(jnp.where(lo, res[1], res[0]), HEAD_DIM, 1) + jnp.where(lo, p_pair[0][1], p_pair[1][1])
        o_ref[0, jb * blk:(jb + 1) * blk, hp * LANES:(hp + 1) * LANES] = (num * (1.0 / den)).astype(BF16)

    n_u = len(units)
    st_s, st_x, st_p = {}, {}, {}
    for n in range(n_u + 3):
        if 0 <= n - 3 < n_u:
            finish(*units[n - 3], st_p.pop(n - 3))
        if 0 <= n - 2 < n_u:
            st_p[n - 2] = probs(st_x.pop(n - 2))
        if 0 <= n - 1 < n_u:
            st_x[n - 1] = shifted(*units[n - 1], st_s.pop(n - 1))
        if n < n_u:
            st_s[n] = scores(*units[n])


def _band_attention(q, kx, vx, sink, seq):
    b = q.shape[0]
    tq = 512
    return pl.pallas_call(
        functools.partial(_battn_kernel, seq=seq, tq=tq),
        grid=(b, seq // tq),
        in_specs=[
            pl.BlockSpec(memory_space=pltpu.SMEM),
            pl.BlockSpec((1, tq, ATTN_WIDTH), lambda bi, i: (bi, i, 0)),
            pl.BlockSpec((1,) + kx.shape[1:], lambda bi, i: (bi, 0, 0, 0)),
            pl.BlockSpec((1,) + vx.shape[1:], lambda bi, i: (bi, 0, 0)),
        ],
        out_specs=pl.BlockSpec((1, tq, ATTN_WIDTH), lambda bi, i: (bi, i, 0)),
        out_shape=jax.ShapeDtypeStruct((b, seq, ATTN_WIDTH), BF16),
        compiler_params=_params("parallel", "arbitrary"),
        name="band_attention",
    )(sink, q, kx, vx)


def _filter_kernel(z_ref, t_ref, w1_ref, b1_ref, f1_ref, w2_ref, b2_ref, f2_ref, w3_ref, ad_ref, o_ref, *, tl):
    hp = lax.Precision.HIGHEST
    h = jnp.sin(f1_ref[...] * (jnp.dot(z_ref[...], w1_ref[...], precision=hp, preferred_element_type=F32)
                               + b1_ref[...]))
    h = jnp.sin(f2_ref[...] * (jnp.dot(h, w2_ref[...], precision=hp, preferred_element_type=F32) + b2_ref[...]))
    t = t_ref[...]
    rowid = pl.program_id(0) * tl + lax.broadcasted_iota(jnp.int32, (tl, HY_WIDTH), 0)
    for d in range(2):
        for o in range(HY_ORDER):
            c0 = (d * HY_ORDER + o) * HY_WIDTH
            v = jnp.dot(h, w3_ref[:, c0:c0 + HY_WIDTH], precision=hp, preferred_element_type=F32)
            v = v * jnp.exp(-t * ad_ref[:, o * HY_WIDTH:(o + 1) * HY_WIDTH])
            if d == 1:
                v = jnp.where(rowid == 0, 0.0, v)
            v = v.reshape(tl // FFT_N1, FFT_GROUPS, FFT_J, HY_WIDTH)
            o_ref[d * HY_ORDER + o] = jnp.concatenate([v, jnp.zeros_like(v)], axis=2).astype(BF16)


def _hyena_filters(z_pos, t_pos, w1, b1, f1, w2, b2, f2, w3, absd):
    L, e = z_pos.shape
    tl = 512
    full = lambda a: pl.BlockSpec(a.shape, lambda i: (0,) * a.ndim)
    return pl.pallas_call(
        functools.partial(_filter_kernel, tl=tl),
        grid=(L // tl,),
        in_specs=[
            pl.BlockSpec((tl, e), lambda i: (i, 0)),
            pl.BlockSpec((tl, 1), lambda i: (i, 0)),
            full(w1), full(b1), full(f1), full(w2), full(b2), full(f2), full(w3), full(absd),
        ],
        out_specs=pl.BlockSpec((2 * HY_ORDER, tl // FFT_N1, FFT_GROUPS, 2 * FFT_J, HY_WIDTH),
                               lambda i: (0, i, 0, 0, 0)),
        out_shape=jax.ShapeDtypeStruct((2 * HY_ORDER, L // FFT_N1, FFT_GROUPS, 2 * FFT_J, HY_WIDTH), BF16),
        compiler_params=_params("parallel"),
        name="hyena_filters",
    )(z_pos, t_pos, w1, b1, f1, w2, b2, f2, w3, absd)


def _shortconv_kernel(ua_ref, ub_ref, w_ref, b_ref, o_ref):
    def conv(u_ref):
        u = u_ref[0].astype(F32)
        L, ct = u.shape
        r8 = lax.broadcasted_iota(jnp.int32, (SUBLANES, ct), 0)
        down = pltpu.roll(u, 1, 0)
        up = pltpu.roll(u, L - 1, 0)
        prev = jnp.concatenate([jnp.where(r8 == 0, 0.0, down[:SUBLANES]), down[SUBLANES:]], axis=0)
        nxt = jnp.concatenate([up[:L - SUBLANES], jnp.where(r8 == SUBLANES - 1, 0.0, up[L - SUBLANES:])], axis=0)
        r = prev * w_ref[0:1, :] + u * w_ref[1:2, :] + nxt * w_ref[2:3, :] + b_ref[...]
        return r.reshape(L // FFT_N1, FFT_GROUPS, FFT_J, ct)

    o_ref[0] = jnp.concatenate([conv(ua_ref), conv(ub_ref)], axis=2).astype(BF16)


def _shortconv(u, w, bias):
    b, L, c3 = u.shape
    ct = MXU_DIM
    ncb = b // 2
    return pl.pallas_call(
        _shortconv_kernel,
        grid=(ncb, c3 // ct),
        in_specs=[
            pl.BlockSpec((1, L, ct), lambda bi, ci: (bi, 0, ci)),
            pl.BlockSpec((1, L, ct), lambda bi, ci: (bi + ncb, 0, ci)),
            pl.BlockSpec((3, ct), lambda bi, ci: (0, ci)),
            pl.BlockSpec((1, ct), lambda bi, ci: (0, ci)),
        ],
        out_specs=pl.BlockSpec((1, L // FFT_N1, FFT_GROUPS, 2 * FFT_J, ct), lambda bi, ci: (bi, 0, 0, 0, ci)),
        out_shape=jax.ShapeDtypeStruct((ncb, L // FFT_N1, FFT_GROUPS, 2 * FFT_J, c3), BF16),
        compiler_params=_params("parallel", "parallel"),
        name="shortconv",
    )(u, u, w, bias)


def _real_block(m, n, po, pi, sign, scale=1.0):
    ang = (2.0 * math.pi / n) * (m % n).astype(F32)
    re = jnp.cos(ang) * scale
    im = jnp.sin(ang) * (sign * scale)
    return jnp.where(po == pi, re, jnp.where(po > pi, im, -im))


def _fft_tables():
    nh = FFT_N2 // 2
    j2 = 2 * FFT_J

    def split(idx):
        return idx // j2, (idx // FFT_J) % 2, idx % FFT_J

    def small(n_out, n_in, sign, scale, out_is_freq):
        rows = FFT_GROUPS * n_out * j2
        a = lax.broadcasted_iota(jnp.int32, (n_in * 2, rows), 0)
        b = lax.broadcasted_iota(jnp.int32, (n_in * 2, rows), 1)
        major_out, po, j = split(b % (n_out * j2))
        g = b // (n_out * j2)
        major_in, pi = a // 2, a % 2
        k2, n2 = (major_out, major_in) if out_is_freq else (major_in, major_out)
        m = FFT_N1 * n2 * k2 + (FFT_J * g + j) * k2
        compact = _real_block(m, FFT_N, po, pi, sign, scale).astype(BF16)
        cols = n_in * j2
        rep = (lax.broadcasted_iota(jnp.int32, (n_in * 2, cols), 1) // FFT_J
               == lax.broadcasted_iota(jnp.int32, (n_in * 2, cols), 0)).astype(BF16)
        full = lax.dot_general(compact, rep, (((0,), (0,)), ((), ())), preferred_element_type=F32)
        diag = (lax.broadcasted_iota(jnp.int32, (rows, cols), 0) % FFT_J
                == lax.broadcasted_iota(jnp.int32, (rows, cols), 1) % FFT_J)
        return jnp.where(diag, full, 0.0).astype(BF16).reshape(FFT_GROUPS, n_out * j2, cols)

    g_fwd = small(FFT_N2, nh, -1.0, 1.0, True)
    g_inv = small(nh, FFT_N2, 1.0, 1.0 / FFT_N, False)

    def big(sign):
        r = lax.broadcasted_iota(jnp.int32, (2 * FFT_N1, 2 * FFT_N1), 0)
        c = lax.broadcasted_iota(jnp.int32, (2 * FFT_N1, 2 * FFT_N1), 1)
        gk, po, jk = split(r)
        g, pi, j = split(c)
        m = (FFT_J * gk + jk) * (FFT_J * g + j)
        return _real_block(m, FFT_N1, po, pi, sign).astype(BF16)

    return g_fwd, g_inv, big(-1.0), big(1.0)


def _small_fwd_rows(g_ref, s, tile):
    r = _dot(g_ref[s], tile)
    return r.reshape(FFT_N2, 2 * FFT_J, tile.shape[-1]).astype(BF16)


def _fs_kernel(z_ref, g_ref, o_ref):
    ct = o_ref.shape[-1]
    for s in range(FFT_SUB):
        tile = z_ref[0, :, s, :, :].reshape(FFT_N2 // 2 * 2 * FFT_J, ct)
        o_ref[0, :, s, :, :] = _small_fwd_rows(g_ref, s, tile)


def _fft_small_fwd(zil, g_fwd, *, c_off, n_c):
    ncb, nh, ng, _, _ = zil.shape
    ct = MXU_DIM
    cblk = c_off // ct
    return pl.pallas_call(
        _fs_kernel,
        grid=(ng // FFT_SUB, ncb, n_c // ct),
        in_specs=[
            pl.BlockSpec((1, nh, FFT_SUB, 2 * FFT_J, ct), lambda q, b, ci: (b, 0, q, 0, cblk + ci)),
            pl.BlockSpec((FFT_SUB,) + g_fwd.shape[1:], lambda q, b, ci: (q, 0, 0)),
        ],
        out_specs=pl.BlockSpec((1, FFT_N2, FFT_SUB, 2 * FFT_J, ct), lambda q, b, ci: (b, 0, q, 0, ci)),
        out_shape=jax.ShapeDtypeStruct((ncb, FFT_N2, ng, 2 * FFT_J, n_c), BF16),
        compiler_params=_params("parallel", "parallel", "arbitrary"),
        name="fft_small_fwd",
    )(zil, g_fwd)


def _big_kernel(x_ref, h_ref, fb_ref, fbi_ref, o_ref):
    ct = o_ref.shape[-1]
    h4 = h_ref[0, 0].astype(F32).reshape(FFT_GROUPS, 2, FFT_J, ct)
    hre = h4[:, 0]
    him = h4[:, 1]
    n_b = x_ref.shape[0]

    def forward(b):
        return _dot(fb_ref[...], x_ref[b, 0].reshape(2 * FFT_N1, ct)).reshape(FFT_GROUPS, 2, FFT_J, ct)

    xf = forward(0)
    for b in range(n_b):
        xr = xf[:, 0]
        xi = xf[:, 1]
        if b + 1 < n_b:
            xf = forward(b + 1)
        y = jnp.stack([xr * hre - xi * him, xr * him + xi * hre], axis=1).reshape(2 * FFT_N1, ct).astype(BF16)
        o_ref[b, 0] = _dot(fbi_ref[...], y).reshape(FFT_GROUPS, 2 * FFT_J, ct).astype(BF16)


def _fft_big(xs, spec, order, fb, fbi):
    ncb, n2, ng, _, c = xs.shape
    ct = MXU_DIM
    nbb = ncb
    return pl.pallas_call(
        _big_kernel,
        grid=(c // ct, n2, ncb // nbb),
        in_specs=[
            pl.BlockSpec((nbb, 1, ng, 2 * FFT_J, ct), lambda ci, k, b: (b, k, 0, 0, ci)),
            pl.BlockSpec((1, 1, ng, 2 * FFT_J, ct), lambda ci, k, b: (order, k, 0, 0, ci)),
            pl.BlockSpec(fb.shape, lambda ci, k, b: (0, 0)),
            pl.BlockSpec(fbi.shape, lambda ci, k, b: (0, 0)),
        ],
        out_specs=pl.BlockSpec((nbb, 1, ng, 2 * FFT_J, ct), lambda ci, k, b: (b, k, 0, 0, ci)),
        out_shape=jax.ShapeDtypeStruct(xs.shape, BF16),
        compiler_params=_params("parallel", "parallel", "arbitrary"),
        name="fft_big",
    )(xs, spec, fb, fbi)


def _spectrum_kernel(x_ref, fb_ref, o_ref):
    ct = o_ref.shape[-1]
    xf = [_dot(fb_ref[...], x_ref[i, 0].reshape(2 * FFT_N1, ct)).reshape(FFT_GROUPS, 2, FFT_J, ct)
          for i in range(2 * HY_ORDER)]
    for o in range(HY_ORDER):
        a = xf[o]
        r = xf[HY_ORDER + o]
        spec = jnp.stack([a[:, 0] + r[:, 0], a[:, 1] - r[:, 1]], axis=1)
        o_ref[o, 0] = spec.reshape(FFT_GROUPS, 2 * FFT_J, ct).astype(BF16)


def _filter_spectrum(xs, fb):
    nf, n2, ng, _, c = xs.shape
    ct = MXU_DIM
    return pl.pallas_call(
        _spectrum_kernel,
        grid=(c // ct, n2),
        in_specs=[
            pl.BlockSpec((nf, 1, ng, 2 * FFT_J, ct), lambda ci, k: (0, k, 0, 0, ci)),
            pl.BlockSpec(fb.shape, lambda ci, k: (0, 0)),
        ],
        out_specs=pl.BlockSpec((HY_ORDER, 1, ng, 2 * FFT_J, ct), lambda ci, k: (0, k, 0, 0, ci)),
        out_shape=jax.ShapeDtypeStruct((HY_ORDER, n2, ng, 2 * FFT_J, c), BF16),
        compiler_params=_params("parallel", "parallel"),
        name="filter_spectrum",
    )(xs, fb)


def _gated_inverse(c_ref, gi_ref, z_ref, gate_ref, skip, s):
    nh = FFT_N2 // 2
    ct = c_ref.shape[-1]
    rows = nh * 2 * FFT_J
    r = _dot(gi_ref[s], c_ref[0, :, s, :, :].reshape(FFT_N2 * 2 * FFT_J, ct))
    z = z_ref[0, :, s, :, :].astype(F32).reshape(rows, ct)
    gate = gate_ref[0, :, s, :, :].astype(F32).reshape(rows, ct)
    return gate * (r + z * skip)


def _is_fs_kernel(c_ref, gi_ref, z_ref, gate_ref, skip_ref, gf_ref, zz_ref, o_ref):
    nh = FFT_N2 // 2
    ct = o_ref.shape[-1]
    skip = skip_ref[...]
    for s in range(FFT_SUB):
        zz = _gated_inverse(c_ref, gi_ref, z_ref, gate_ref, skip, s).astype(BF16)
        zz_ref[0, :, s, :, :] = zz.reshape(nh, 2 * FFT_J, ct)
        o_ref[0, :, s, :, :] = _small_fwd_rows(gf_ref, s, zz)


def _is_last_kernel(c_ref, gi_ref, z_ref, gate_ref, skip_ref, y_ref):
    nh = FFT_N2 // 2
    ct = y_ref.shape[-1]
    skip = skip_ref[...]
    for s in range(FFT_SUB):
        y = _gated_inverse(c_ref, gi_ref, z_ref, gate_ref, skip, s).reshape(nh, 2, FFT_J, ct)
        for p in range(2):
            y_ref[p, 0, :, FFT_J * s:FFT_J * (s + 1), :] = y[:, p]


def _fft_small_inv(cs, g_inv, zil, z_off, gil, gate_off, skip, g_fwd=None):
    ncb, n2, ng, _, c = cs.shape
    nh = n2 // 2
    ct = MXU_DIM
    zb = z_off // ct
    gb = gate_off // ct
    til = lambda off: pl.BlockSpec((1, nh, FFT_SUB, 2 * FFT_J, ct), lambda q, b, ci: (b, 0, q, 0, off + ci))
    freq = pl.BlockSpec((1, n2, FFT_SUB, 2 * FFT_J, ct), lambda q, b, ci: (b, 0, q, 0, ci))
    mat = lambda m: pl.BlockSpec((FFT_SUB,) + m.shape[1:], lambda q, b, ci: (q, 0, 0))
    ins = [freq, mat(g_inv), til(zb), til(gb), pl.BlockSpec((1, ct), lambda q, b, ci: (0, ci))]
    args = [cs, g_inv, zil, gil, skip]
    if g_fwd is not None:
        body = _is_fs_kernel
        ins.append(mat(g_fwd))
        args.append(g_fwd)
        out_specs = [til(0), freq]
        out_shape = [jax.ShapeDtypeStruct((ncb, nh, ng, 2 * FFT_J, c), BF16), jax.ShapeDtypeStruct(cs.shape, BF16)]
    else:
        body = _is_last_kernel
        out_specs = pl.BlockSpec((2, 1, nh, FFT_CHUNK, ct), lambda q, b, ci: (0, b, 0, q, ci))
        out_shape = jax.ShapeDtypeStruct((2, ncb, nh, ng * FFT_J, c), F32)
    return pl.pallas_call(
        body,
        grid=(ng // FFT_SUB, ncb, c // ct),
        in_specs=ins,
        out_specs=out_specs,
        out_shape=out_shape,
        compiler_params=_params("parallel", "parallel", "arbitrary"),
        name="fft_small_inv",
    )(*args)


def _norm_mm_kernel(x_ref, g_ref, w_ref, o_ref):
    o_ref[...] = _dot(_rms(x_ref[...], g_ref[...]).astype(BF16), w_ref[...]).astype(o_ref.dtype)


def _norm_mm(x2d, g, w):
    n, d = x2d.shape
    tm = 512
    return pl.pallas_call(
        _norm_mm_kernel,
        grid=(n // tm,),
        in_specs=[
            pl.BlockSpec((tm, d), lambda i: (i, 0)),
            pl.BlockSpec((1, d), lambda i: (0, 0)),
            pl.BlockSpec(w.shape, lambda i: (0, 0)),
        ],
        out_specs=pl.BlockSpec((tm, w.shape[1]), lambda i: (i, 0)),
        out_shape=jax.ShapeDtypeStruct((n, w.shape[1]), BF16),
        compiler_params=_params("parallel"),
        name="memory_kv",
    )(x2d, g, w)


def _route(xf, g_ref, wr_ref, h_ref, r_ref, rt_ref):
    hf = _rms(xf, g_ref[...])
    hb = hf.astype(BF16)
    h_ref[...] = hb
    h_lo = (hf - hb.astype(F32)).astype(BF16)
    parts = _dot(hb, wr_ref[...]) + _dot(h_lo, wr_ref[...])
    logits = parts[:, :LANES] + parts[:, LANES:]
    lane = lax.broadcasted_iota(jnp.int32, logits.shape, 1)
    logits = jnp.where(lane < N_EXPERTS, logits, NEG)
    lanef = lane.astype(F32)
    big = float(LANES)
    m1 = jnp.max(logits, axis=-1, keepdims=True)
    i1 = jnp.min(jnp.where(logits == m1, lanef, big), axis=-1, keepdims=True)
    rest = jnp.where(lanef == i1, NEG, logits)
    m2 = jnp.max(rest, axis=-1, keepdims=True)
    i2 = jnp.min(jnp.where(rest == m2, lanef, big), axis=-1, keepdims=True)
    e2 = jnp.exp(m2 - m1)
    w1 = 1.0 / (1.0 + e2)
    w2 = e2 / (1.0 + e2)
    rec = jnp.where(lane == 0, i1, jnp.where(lane == 1, i2, jnp.where(lane == 2, w1, jnp.where(lane == 3, w2, 0.0))))
    r_ref[...] = rec
    rt_ref[...] = rec.T[:SUBLANES, :]


def _mix_xattn_kernel(x_ref, a_ref, y_ref, ga_ref, gy_ref, wo_ref, gx_ref, wq_ref, kv_ref, wxo_ref, o_ref):
    an = _rms(a_ref[0].astype(F32), ga_ref[...]).astype(BF16)
    yn = _rms(y_ref[0], gy_ref[...]).astype(BF16)
    x1 = x_ref[0] + _dot(an, wo_ref[:ATTN_WIDTH, :]) + _dot(yn, wo_ref[ATTN_WIDTH:, :])
    h = _rms(x1, gx_ref[...]).astype(BF16)
    q = (_dot(h, wq_ref[...]) * (X_HEAD_DIM ** -0.5)).astype(BF16)
    outs = []
    for hh in range(X_HEADS):
        cols = slice(hh * X_HEAD_DIM, (hh + 1) * X_HEAD_DIM)
        kh = kv_ref[0, :, cols]
        vh = kv_ref[0, :, X_WIDTH + hh * X_HEAD_DIM:X_WIDTH + (hh + 1) * X_HEAD_DIM]
        s = _dot_t(q[:, cols], kh)
        p = jnp.exp(s - jnp.max(s, axis=-1, keepdims=True))
        den = jnp.sum(p, axis=-1, keepdims=True)
        outs.append((_dot(p.astype(BF16), vh) / den).astype(BF16))
    o_ref[0] = x1 + _dot(jnp.concatenate(outs, axis=1), wxo_ref[...])


def _mix_xattn(x, a, y, ga, gy, wo, gx, wq, kv, wxo):
    b, s, d = x.shape
    tq = 512
    m = kv.shape[1]
    full = lambda arr: pl.BlockSpec(arr.shape, lambda bi, i: (0,) * arr.ndim)
    tok = lambda w: pl.BlockSpec((1, tq, w), lambda bi, i: (bi, i, 0))
    return pl.pallas_call(
        _mix_xattn_kernel,
        grid=(b, s // tq),
        in_specs=[tok(d), tok(ATTN_WIDTH), tok(HY_WIDTH), full(ga), full(gy), full(wo), full(gx), full(wq),
                  pl.BlockSpec((1, m, 2 * X_WIDTH), lambda bi, i: (bi, 0, 0)), full(wxo)],
        out_specs=tok(d),
        out_shape=jax.ShapeDtypeStruct((b, s, d), F32),
        compiler_params=_params("parallel", "arbitrary"),
        name="mix_xattn",
    )(x, a, y, ga, gy, wo, gx, wq, kv, wxo)


def _swiglu_chunks(h, wg, wu, wd, width, acc):
    for c in range(width // MXU_DIM):
        cols = slice(c * MXU_DIM, (c + 1) * MXU_DIM)
        g = _dot(h, wg(cols))
        u = _dot(h, wu(cols))
        a = (g * (1.0 / (1.0 + jnp.exp(-g))) * u).astype(BF16)
        acc = acc + _dot(a, wd(cols))
    return acc


def _ffn_kernel(x_ref, g_ref, wg_ref, wu_ref, wd_ref, o_ref):
    x = x_ref[...]
    h = _rms(x, g_ref[...]).astype(BF16)
    o_ref[...] = _swiglu_chunks(h, lambda c: wg_ref[:, c], lambda c: wu_ref[:, c], lambda c: wd_ref[c, :],
                                wg_ref.shape[1], x)


def _dense_ffn(x2d, g, wg, wu, wd):
    n, d = x2d.shape
    tm = 512
    resident = lambda w: pl.BlockSpec(w.shape, lambda i: (0, 0), pipeline_mode=pl.Buffered(1))
    return pl.pallas_call(
        _ffn_kernel,
        grid=(n // tm,),
        in_specs=[
            pl.BlockSpec((tm, d), lambda i: (i, 0)),
            pl.BlockSpec((1, d), lambda i: (0, 0)),
            resident(wg), resident(wu), resident(wd),
        ],
        out_specs=pl.BlockSpec((tm, d), lambda i: (i, 0)),
        out_shape=jax.ShapeDtypeStruct((n, d), F32),
        compiler_params=_params("parallel"),
        name="dense_ffn",
    )(x2d, g, wg, wu, wd)


def _router_kernel(x_ref, g_ref, wr_ref, h_ref, r_ref, rt_ref, z_ref):
    _route(x_ref[...], g_ref, wr_ref, h_ref, r_ref, rt_ref)
    z_ref[...] = jnp.zeros(z_ref.shape, z_ref.dtype)


def _router(x2d, g, wr, buf_rows):
    n, d = x2d.shape
    tm = 512
    steps = n // tm
    zr = buf_rows // steps
    assert zr * steps == buf_rows and zr % (2 * SUBLANES) == 0
    return pl.pallas_call(
        _router_kernel,
        grid=(steps,),
        in_specs=[
            pl.BlockSpec((tm, d), lambda i: (i, 0)),
            pl.BlockSpec((1, d), lambda i: (0, 0)),
            pl.BlockSpec(wr.shape, lambda i: (0, 0)),
        ],
        out_specs=[pl.BlockSpec((tm, d), lambda i: (i, 0)), pl.BlockSpec((tm, LANES), lambda i: (i, 0)),
                   pl.BlockSpec((SUBLANES, tm), lambda i: (0, i)), pl.BlockSpec((zr, d), lambda i: (i, 0))],
        out_shape=[jax.ShapeDtypeStruct((n, d), BF16), jax.ShapeDtypeStruct((n, LANES), F32),
                   jax.ShapeDtypeStruct((SUBLANES, n), F32), jax.ShapeDtypeStruct((buf_rows, d), BF16)],
        compiler_params=_params("parallel"),
        name="router",
    )(x2d, g, wr)


def _moe_kernel(be_ref, x_ref, wg_ref, wu_ref, wd_ref, y_prev_ref, o_ref, acc_ref, *, n_blk):
    del y_prev_ref
    f = pl.program_id(1)

    @pl.when(f == 0)
    def _():
        acc_ref[...] = jnp.zeros_like(acc_ref)

    @pl.when(pl.program_id(0) < be_ref[n_blk])
    def _():
        acc_ref[...] = _swiglu_chunks(x_ref[...], lambda c: wg_ref[0, :, c], lambda c: wu_ref[0, :, c],
                                      lambda c: wd_ref[0, c, :], wg_ref.shape[2], acc_ref[...])

    @pl.when(f == pl.num_programs(1) - 1)
    def _():
        o_ref[...] = acc_ref[...].astype(BF16)


def _moe_experts(blk_meta, xb, wg, wu, wd, y_prev, blk_off):
    rows, d = xb.shape
    ff = wg.shape[2]
    tf = ff // 2
    nf = ff // tf
    n_blk = rows // MOE_ROWS
    ftile = lambda i, f, be: jnp.where(i < be[n_blk], f, nf - 1)
    grid_spec = pltpu.PrefetchScalarGridSpec(
        num_scalar_prefetch=1,
        grid=(n_blk, nf),
        in_specs=[
            pl.BlockSpec((MOE_ROWS, d), lambda i, f, be: (i, 0)),
            pl.BlockSpec((1, d, tf), lambda i, f, be: (be[i], 0, ftile(i, f, be))),
            pl.BlockSpec((1, d, tf), lambda i, f, be: (be[i], 0, ftile(i, f, be))),
            pl.BlockSpec((1, tf, d), lambda i, f, be: (be[i], ftile(i, f, be), 0)),
            pl.BlockSpec(memory_space=pl.ANY),
        ],
        out_specs=pl.BlockSpec((MOE_ROWS, d), lambda i, f, be: (blk_off + i, 0)),
        scratch_shapes=[pltpu.VMEM((MOE_ROWS, d), F32)],
    )
    return pl.pallas_call(
        functools.partial(_moe_kernel, n_blk=n_blk),
        grid_spec=grid_spec,
        out_shape=jax.ShapeDtypeStruct(y_prev.shape, BF16),
        input_output_aliases={5: 0},
        compiler_params=_params("parallel", "arbitrary"),
        name="moe_experts",
    )(blk_meta, xb, wg, wu, wd, y_prev)


def _combine_kernel(*refs, normed):
    if normed:
        x_ref, y1_ref, y2_ref, r_ref, g_ref, o_ref = refs
    else:
        x_ref, y1_ref, y2_ref, r_ref, o_ref = refs
    w1 = r_ref[:, 2:3]
    w2 = r_ref[:, 3:4]
    x = x_ref[...] + y1_ref[...].astype(F32) * w1 + y2_ref[...].astype(F32) * w2
    o_ref[...] = _rms(x, g_ref[...]) if normed else x


def _combine(x2d, y1, y2, route, gain):
    n, d = x2d.shape
    tm = 512
    normed = gain is not None
    tok = lambda w: pl.BlockSpec((tm, w), lambda i: (i, 0))
    ins = [tok(d), tok(d), tok(d), tok(LANES)]
    args = [x2d, y1, y2, route]
    if normed:
        ins.append(pl.BlockSpec((1, d), lambda i: (0, 0)))
        args.append(gain)
    return pl.pallas_call(
        functools.partial(_combine_kernel, normed=normed),
        grid=(n // tm,),
        in_specs=ins,
        out_specs=tok(d),
        out_shape=jax.ShapeDtypeStruct((n, d), F32),
        compiler_params=_params("parallel"),
        name="moe_combine",
    )(*args)


def _norm_kernel(x_ref, g_ref, o_ref):
    o_ref[...] = _rms(x_ref[...], g_ref[...])


def _final_norm(x2d, g):
    n, d = x2d.shape
    tm = 512
    return pl.pallas_call(
        _norm_kernel,
        grid=(n // tm,),
        in_specs=[pl.BlockSpec((tm, d), lambda i: (i, 0)), pl.BlockSpec((1, d), lambda i: (0, 0))],
        out_specs=pl.BlockSpec((tm, d), lambda i: (i, 0)),
        out_shape=jax.ShapeDtypeStruct((n, d), F32),
        compiler_params=_params("parallel"),
        name="final_norm",
    )(x2d, g)


def _hyena(u, conv_w, conv_b, filt, skip, tables):
    g_fwd, g_inv, fb, fbi = tables
    b, L, c3 = u.shape
    c = c3 // (HY_ORDER + 1)
    uc = _shortconv(u, conv_w, conv_b.reshape(1, c3))
    spec = _filter_spectrum(_fft_small_fwd(filt, g_fwd, c_off=0, n_c=c), fb)
    t = _fft_small_fwd(uc, g_fwd, c_off=2 * c, n_c=c)
    t = _fft_big(t, spec, 0, fb, fbi)
    zz, t = _fft_small_inv(t, g_inv, uc, 2 * c, uc, 0, skip[0:1], g_fwd)
    t = _fft_big(t, spec, 1, fb, fbi)
    y5 = _fft_small_inv(t, g_inv, zz, 0, uc, c, skip[1:2])
    return y5.reshape(b, L, c)


def _moe_dispatch(route_t, n_tok):
    n_asg = n_tok * TOP_K
    flat_e = route_t[:TOP_K].astype(jnp.int32).reshape(n_asg)
    experts = jnp.arange(N_EXPERTS, dtype=jnp.int32)[:, None]
    onehot = (flat_e[None, :] == experts).astype(jnp.int32)
    csum = jnp.cumsum(onehot, axis=1)
    counts = csum[:, -1]
    rank = jnp.sum(onehot * (csum - 1), axis=0)
    padded = (counts + MOE_ROWS - 1) // MOE_ROWS * MOE_ROWS
    pad_end = jnp.cumsum(padded)
    pad_start = pad_end - padded
    dest = jnp.sum(onehot * pad_start[:, None], axis=0) + rank
    seg_start = jnp.cumsum(counts) - counts
    n_blk = -(-(n_asg + N_EXPERTS * (MOE_ROWS - 1)) // MOE_ROWS)
    blk_start = jnp.arange(n_blk, dtype=jnp.int32) * MOE_ROWS
    blk_e = jnp.minimum(jnp.sum((blk_start[:, None] >= pad_end[None, :]).astype(jnp.int32), axis=1), N_EXPERTS - 1)
    order = jnp.argsort(flat_e, stable=True).astype(jnp.int32)
    e_row = jnp.repeat(blk_e, MOE_ROWS)
    r = jnp.arange(n_blk * MOE_ROWS, dtype=jnp.int32) - pad_start[e_row]
    src = order[jnp.clip(seg_start[e_row] + r, 0, n_asg - 1)]
    buf_tok = jnp.where(r < counts[e_row], src % n_tok, 0)
    n_used = (pad_end[-1] // MOE_ROWS).astype(jnp.int32)
    return buf_tok, blk_e.astype(jnp.int32), n_used, dest.reshape(TOP_K, n_tok)


def kernel(x, mem, mem_norm, mix_norm, w_in, attn_sink, hy_conv_w, hy_conv_b, hy_f_w1, hy_f_b1, hy_f_freq1,
           hy_f_w2, hy_f_b2, hy_f_freq2, hy_f_w3, hy_skip, attn_out_norm, hy_out_norm, w_out, xattn_norm,
           xw_q, xw_k, xw_v, xw_o, ffn_norm, ffn_w_gate, ffn_w_up, ffn_w_down,
           moe_router, moe_w_gate, moe_w_up, moe_w_down, final_norm):
    b, s, d = x.shape
    depth = w_in.shape[0]
    n_tok = b * s
    assert s == FFT_N // 2 and b % 2 == 0
    row = lambda v: v.reshape(1, -1).astype(F32)

    pos = jnp.arange(s, dtype=F32)
    inv = ROPE_THETA ** (-jnp.arange(0, HEAD_DIM, 2, dtype=F32) / HEAD_DIM)
    ang = pos[:, None] * inv[None, :]
    cosf = jnp.tile(jnp.cos(ang), (1, LANES // (HEAD_DIM // 2)))
    sins = jnp.tile(jnp.concatenate([-jnp.sin(ang), jnp.sin(ang)], axis=1), (1, LANES // HEAD_DIM))
    t_pos = jnp.linspace(0.0, 1.0, s, dtype=F32)[:, None]
    wv = 2.0 * math.pi * jnp.arange(s, dtype=F32)[:, None] / s
    fr = jnp.linspace(1e-4, HY_BANDS - 1, HY_BANDS, dtype=F32)[None, :]
    z_pos = jnp.concatenate([t_pos, jnp.cos(fr * wv), -jnp.sin(fr * wv)], axis=-1)
    emb_pad = 64
    z_pos = jnp.pad(z_pos, ((0, 0), (0, emb_pad - HY_EMB)))
    absd = jnp.abs(jnp.linspace(HY_MIN_DECAY, HY_MAX_DECAY, HY_ORDER * HY_WIDTH, dtype=F32)).reshape(1, -1)
    tables = _fft_tables()

    mem2d = mem.reshape(-1, d)
    x2d = x.reshape(n_tok, d)
    out = None
    for l in range(depth):
        q, kx, vx, u = _inproj(x2d, row(mix_norm[l]), w_in[l].astype(BF16), cosf, sins, s)
        a = _band_attention(q.reshape(b, s, ATTN_WIDTH), kx, vx.reshape(b, s, -1), attn_sink[l].astype(F32), s)
        filt = _hyena_filters(
            z_pos, t_pos, jnp.pad(hy_f_w1[l], ((0, emb_pad - HY_EMB), (0, 0))), row(hy_f_b1[l]),
            row(hy_f_freq1[l]), hy_f_w2[l], row(hy_f_b2[l]), row(hy_f_freq2[l]), hy_f_w3[l], absd)
        y = _hyena(u.reshape(b, s, -1), hy_conv_w[l], hy_conv_b[l], filt, hy_skip[l], tables)
        mkv = _norm_mm(mem2d, row(mem_norm), jnp.concatenate([xw_k[l], xw_v[l]], axis=1).astype(BF16))
        j = l // 2
        x3 = _mix_xattn(x2d.reshape(b, s, d), a, y, row(attn_out_norm[l]), row(hy_out_norm[l]),
                        w_out[l].astype(BF16), row(xattn_norm[l]), xw_q[l].astype(BF16),
                        mkv.reshape(b, -1, 2 * X_WIDTH), xw_o[l].astype(BF16))
        x2d = x3.reshape(n_tok, d)
        last = l == depth - 1
        if l % 2 == 0:
            x2d = _dense_ffn(x2d, row(ffn_norm[l]), ffn_w_gate[j].astype(BF16), ffn_w_up[j].astype(BF16),
                             ffn_w_down[j].astype(BF16))
            if last:
                out = _final_norm(x2d, row(final_norm))
        else:
            wr = jnp.pad(moe_router[j].astype(F32), ((0, 0), (0, LANES - N_EXPERTS)))
            wr_hi = wr.astype(BF16)
            wr = jnp.concatenate([wr_hi, (wr - wr_hi.astype(F32)).astype(BF16)], axis=1)
            n_blk = -(-(n_tok * TOP_K + N_EXPERTS * (MOE_ROWS - 1)) // MOE_ROWS)
            hb, route, route_t, yb = _router(x2d, row(ffn_norm[l]), wr, n_blk * MOE_ROWS)
            buf_tok, blk_e, n_used, dest = _moe_dispatch(route_t, n_tok)
            assert blk_e.shape[0] == n_blk
            wg, wu, wd = (w[j].astype(BF16) for w in (moe_w_gate, moe_w_up, moe_w_down))
            per = n_blk // MOE_SLICES
            assert per * MOE_SLICES == n_blk
            for k in range(MOE_SLICES):
                lo = k * per
                meta = jnp.concatenate([blk_e[lo:lo + per], (n_used - lo)[None]])
                xb = hb[buf_tok[lo * MOE_ROWS:(lo + per) * MOE_ROWS]]
                yb = _moe_experts(meta, xb, wg, wu, wd, yb, lo)
            res = _combine(x2d, yb[dest[0]], yb[dest[1]], route, row(final_norm) if last else None)
            if last:
                out = res
            else:
                x2d = res
    return out.reshape(b, s, d)
```
